```python
import jax, jax.numpy as jnp
from jax import lax
import numpy as np

D_MODEL = 1024
BATCH = 32
SEQ = 2048
DEPTH = 1
DEC_BATCH = 4
DEC_SEQ = 4096
PAST_LEN = 128

GRID_W = 64
EPS = 1e-5
GLA_HEADS = 4
GLA_DK = 64
GLA_DV = 128
GLA_RANK = 16
GLA_TAU = 16.0
GLA_CHUNK = 64
GLA_QK = GLA_HEADS * GLA_DK
GLA_V = GLA_HEADS * GLA_DV
NAT_HEADS = 16
NAT_HD = 32
NAT_W = NAT_HEADS * NAT_HD
NAT_KH_MAX = 8
NAT_KW = 16
NAT_QCOLS = 16
NAT_KCOLS = 32
MIX_W = GLA_V + NAT_W
IN_SIZES = (GLA_QK, GLA_QK, GLA_V, GLA_V, GLA_RANK, GLA_RANK, NAT_W, NAT_W, NAT_W)
IN_COLS = sum(IN_SIZES)
N_EXPERTS = 32
TOP_K = 4
D_FF = 1024
SWIGLU_LIMIT = 7.0
SWIGLU_ALPHA = 1.702
MOE_BLOCK = 256

kernel_name = 'hymba_gla_natten_moe_encoder'


def rms_norm(x, g):
    xf = x.astype(jnp.float32)
    y = xf * lax.rsqrt(jnp.mean(xf * xf, axis=-1, keepdims=True) + EPS)
    return (y * g.astype(jnp.float32)).astype(x.dtype)


def gla_one_direction(q, k, v, log_a):
    B, T, H, DK = q.shape
    DV = v.shape[-1]
    C = GLA_CHUNK
    N = T // C
    to_chunks = lambda a: a.reshape(B, N, C, H, a.shape[-1]).transpose(1, 0, 3, 2, 4)
    q, k, v, log_a = map(to_chunks, (q, k, v, log_a))
    b = jnp.cumsum(log_a, axis=3)
    b_last = b[:, :, :, -1:, :]
    q_t = q * jnp.exp(b)
    k_t = k * jnp.exp(-b)
    k_end = k * jnp.exp(b_last - b)
    causal_in_chunk = jnp.tril(jnp.ones((C, C), dtype=bool))
    A = jnp.einsum('nbhqd,nbhkd->nbhqk', q_t, k_t)
    A = jnp.where(causal_in_chunk, A, 0.0)
    o_intra = jnp.einsum('nbhqk,nbhkv->nbhqv', A, v)

    def step(S, inp):
        qc, kc, vc, bl = inp
        o = jnp.einsum('bhqd,bhdv->bhqv', qc, S)
        S = S * jnp.exp(bl)[:, :, 0, :, None] + jnp.einsum('bhkd,bhkv->bhdv', kc, vc)
        return S, o

    S0 = jnp.zeros((B, H, DK, DV), jnp.float32)
    _, o_inter = lax.scan(step, S0, (q_t, k_end, v, b_last))
    o = o_intra + o_inter
    return o.transpose(1, 0, 3, 2, 4).reshape(B, T, H, DV)


def gla_mixer(q, k, v, g, lr_f, lr_b, w2_f, b2_f, w2_b, b2_b, norm_g):
    B, T, _ = q.shape
    heads = lambda a, d: a.astype(jnp.float32).reshape(B, T, GLA_HEADS, d)
    qh = heads(q, GLA_DK) * (GLA_DK ** -0.5)
    kh = heads(k, GLA_DK)
    vh = heads(v, GLA_DV)
    la_f = heads(jax.nn.log_sigmoid((lr_f @ w2_f + b2_f).astype(jnp.float32)) / GLA_TAU, GLA_DK)
    la_b = heads(jax.nn.log_sigmoid((lr_b @ w2_b + b2_b).astype(jnp.float32)) / GLA_TAU, GLA_DK)
    fl = lambda a: jnp.flip(a, axis=1)
    o = gla_one_direction(qh, kh, vh, la_f) + fl(gla_one_direction(fl(qh), fl(kh), fl(vh), fl(la_b)))
    o = rms_norm(o, norm_g).reshape(B, T, GLA_V)
    o = o * jax.nn.silu(g.astype(jnp.float32))
    return o.astype(q.dtype)


def nat_mixer(q, k, v, rpb):
    B, T, _ = q.shape
    rows = T // GRID_W
    kh = min(NAT_KH_MAX, rows)
    n_cb = GRID_W // NAT_QCOLS
    grid = lambda a: a.reshape(B, rows, GRID_W, NAT_HEADS, NAT_HD)
    q = grid(q) * (NAT_HD ** -0.5)
    k = grid(k)
    v = grid(v)
    qcol = np.arange(GRID_W)
    c0 = np.clip(qcol - NAT_KW // 2, 0, GRID_W - NAT_KW).reshape(n_cb, NAT_QCOLS)
    cs = np.clip(np.arange(n_cb) * NAT_QCOLS - NAT_KW // 2, 0, GRID_W - NAT_KCOLS)
    kcol = cs[:, None] + np.arange(NAT_KCOLS)[None, :]
    qcol_b = qcol.reshape(n_cb, NAT_QCOLS)
    colmask = (kcol[:, None, :] >= c0[:, :, None]) & (kcol[:, None, :] < c0[:, :, None] + NAT_KW)
    dcol = np.clip(kcol[:, None, :] - qcol_b[:, :, None] + NAT_KW - 1, 0, 2 * NAT_KW - 2)
    rpb_col = rpb[:, :, dcol]

    def row_block(i):
        r0 = jnp.clip(i - kh // 2, 0, rows - kh)
        kb = lax.dynamic_slice_in_dim(k, r0, kh, axis=1)[:, :, kcol]
        vb = lax.dynamic_slice_in_dim(v, r0, kh, axis=1)[:, :, kcol]
        qi = lax.dynamic_index_in_dim(q, i, axis=1, keepdims=False)
        qi = qi.reshape(B, n_cb, NAT_QCOLS, NAT_HEADS, NAT_HD)
        s = jnp.einsum('bmqhd,bamkhd->bhmqak', qi, kb).astype(jnp.float32)
        drow = r0 + jnp.arange(kh) - i + NAT_KH_MAX - 1
        bias = jnp.take(rpb_col, drow, axis=1).transpose(0, 2, 3, 1, 4)
        s = s + bias[None].astype(jnp.float32)
        s = jnp.where(colmask[None, None, :, :, None, :], s, -jnp.inf)
        p = jax.nn.softmax(s.reshape(B, NAT_HEADS, n_cb, NAT_QCOLS, kh * NAT_KCOLS), axis=-1)
        p = p.reshape(s.shape).astype(vb.dtype)
        o = jnp.einsum('bhmqak,bamkhd->bmqhd', p, vb)
        return o.reshape(B, GRID_W, NAT_W)

    out = lax.map(row_block, jnp.arange(rows))
    return out.transpose(1, 0, 2, 3).reshape(B, T, NAT_W)


def moe(h, router_w, router_b, w_gate_up, b_gate_up, w_down, b_down):
    N, D = h.shape
    NK = N * TOP_K
    logits = (h @ router_w).astype(jnp.float32) + router_b.astype(jnp.float32)
    top_v, top_i = lax.top_k(logits, TOP_K)
    gates = jax.nn.softmax(top_v, axis=-1)
    flat_e = top_i.reshape(-1).astype(jnp.int32)
    flat_tok = jnp.arange(NK, dtype=jnp.int32) // TOP_K
    flat_g = gates.reshape(-1)
    order = jnp.argsort(flat_e, stable=True)
    se = flat_e[order]
    counts = jnp.bincount(flat_e, length=N_EXPERTS).astype(jnp.int32)
    padded = (counts + MOE_BLOCK - 1) // MOE_BLOCK * MOE_BLOCK
    pad_end = jnp.cumsum(padded)
    pad_start = pad_end - padded
    start = jnp.cumsum(counts) - counts
    dest = pad_start[se] + jnp.arange(NK, dtype=jnp.int32) - start[se]
    n_blocks = -(-NK // MOE_BLOCK) + N_EXPERTS
    cap = n_blocks * MOE_BLOCK
    buf_tok = jnp.zeros((cap,), jnp.int32).at[dest].set(flat_tok[order])
    buf_g = jnp.zeros((cap,), jnp.float32).at[dest].set(flat_g[order])
    blk_e = jnp.minimum(jnp.searchsorted(pad_end, jnp.arange(n_blocks, dtype=jnp.int32) * MOE_BLOCK, side='right'),
                        N_EXPERTS - 1).astype(jnp.int32)

    def expert_block(args):
        tok, g, e = args
        xb = h[tok]
        gu = xb @ w_gate_up[e] + b_gate_up[e]
        gate = jnp.minimum(gu[:, 0::2], SWIGLU_LIMIT)
        up = jnp.clip(gu[:, 1::2], -SWIGLU_LIMIT, SWIGLU_LIMIT)
        act = (up + 1.0) * gate * jax.nn.sigmoid(SWIGLU_ALPHA * gate)
        y = act @ w_down[e] + b_down[e]
        return y * g[:, None].astype(y.dtype)

    ys = lax.map(expert_block, (buf_tok.reshape(n_blocks, MOE_BLOCK), buf_g.reshape(n_blocks, MOE_BLOCK), blk_e))
    return jnp.zeros_like(h).at[buf_tok].add(ys.reshape(cap, D))


def encoder_layer(x, c, norm1_g, w_ada, b_ada, w_in, gla_w2_fwd, gla_b2_fwd, gla_w2_bwd, gla_b2_bwd,
                  gla_norm_g, nat_rpb, w_out, norm2_g, router_w, router_b, w_gate_up, b_gate_up, w_down, b_down):
    B, T, D = x.shape
    mod = jax.nn.silu(c) @ w_ada + b_ada
    sh1, sc1, g1, sh2, sc2, g2 = jnp.split(mod[:, None, :], 6, axis=-1)
    h = rms_norm(x, norm1_g) * (1.0 + sc1) + sh1
    proj = h @ w_in
    gq, gk, gv, gg, lr_f, lr_b, nq, nk, nv = jnp.split(proj, list(np.cumsum(IN_SIZES)[:-1]), axis=-1)
    o_gla = gla_mixer(gq, gk, gv, gg, lr_f, lr_b, gla_w2_fwd, gla_b2_fwd, gla_w2_bwd, gla_b2_bwd, gla_norm_g)
    o_nat = nat_mixer(nq, nk, nv, nat_rpb)
    mix = jnp.concatenate([o_gla, o_nat], axis=-1) @ w_out
    x = x + g1 * mix
    h = rms_norm(x, norm2_g) * (1.0 + sc2) + sh2
    y = moe(h.reshape(B * T, D), router_w, router_b, w_gate_up, b_gate_up, w_down, b_down)
    return x + g2 * y.reshape(B, T, D)


def encoder(x, c, norm1_g, w_ada, b_ada, w_in, gla_w2_fwd, gla_b2_fwd, gla_w2_bwd, gla_b2_bwd,
            gla_norm_g, nat_rpb, w_out, norm2_g, router_w, router_b, w_gate_up, b_gate_up, w_down, b_down,
            final_norm_g):
    for l in range(DEPTH):
        x = encoder_layer(x, c, norm1_g[l], w_ada[l], b_ada[l], w_in[l], gla_w2_fwd[l], gla_b2_fwd[l],
                          gla_w2_bwd[l], gla_b2_bwd[l], gla_norm_g[l], nat_rpb[l], w_out[l], norm2_g[l],
                          router_w[l], router_b[l], w_gate_up[l], b_gate_up[l], w_down[l], b_down[l])
    return rms_norm(x, final_norm_g)


def setup_inputs(seed: int = 0) -> dict:
    key = jax.random.key(seed)
    ks = jax.random.split(key, 32)
    nrm = lambda k, shape, scale: jax.random.normal(k, shape, jnp.float32) * scale
    L, D = DEPTH, D_MODEL
    return {
        'x_prompt': nrm(ks[0], (BATCH, SEQ, D), 1.0),
        'x_sample': nrm(ks[1], (DEC_BATCH, DEC_SEQ, D), 1.0),
        'c_prompt': nrm(ks[2], (BATCH, D), 1.0),
        'c_sample': nrm(ks[3], (DEC_BATCH, D), 1.0),
        'norm1_g': 1.0 + nrm(ks[4], (L, D), 0.02),
        'w_ada': nrm(ks[5], (L, D, 6 * D), 0.5 * D ** -0.5),
        'b_ada': nrm(ks[6], (L, 6 * D), 0.02),
        'w_in': nrm(ks[7], (L, D, IN_COLS), D ** -0.5),
        'gla_w2_fwd': nrm(ks[8], (L, GLA_RANK, GLA_QK), GLA_RANK ** -0.5),
        'gla_b2_fwd': nrm(ks[9], (L, GLA_QK), 0.1),
        'gla_w2_bwd': nrm(ks[10], (L, GLA_RANK, GLA_QK), GLA_RANK ** -0.5),
        'gla_b2_bwd': nrm(ks[11], (L, GLA_QK), 0.1),
        'gla_norm_g': 1.0 + nrm(ks[12], (L, GLA_DV), 0.02),
        'nat_rpb': nrm(ks[13], (L, NAT_HEADS, 2 * NAT_KH_MAX - 1, 2 * NAT_KW - 1), 0.02),
        'w_out': nrm(ks[14], (L, MIX_W, D), MIX_W ** -0.5),
        'norm2_g': 1.0 + nrm(ks[15], (L, D), 0.02),
        'router_w': nrm(ks[16], (L, D, N_EXPERTS), D ** -0.5),
        'router_b': nrm(ks[17], (L, N_EXPERTS), 0.01),
        'w_gate_up': nrm(ks[18], (L, N_EXPERTS, D, 2 * D_FF), D ** -0.5),
        'b_gate_up': nrm(ks[19], (L, N_EXPERTS, 2 * D_FF), 0.01),
        'w_down': nrm(ks[20], (L, N_EXPERTS, D_FF, D), D_FF ** -0.5),
        'b_down': nrm(ks[21], (L, N_EXPERTS, D), 0.01),
        'final_norm_g': 1.0 + nrm(ks[22], (D,), 0.02),
    }


def reference(x_prompt, x_sample, c_prompt, c_sample, norm1_g, w_ada, b_ada, w_in, gla_w2_fwd, gla_b2_fwd,
              gla_w2_bwd, gla_b2_bwd, gla_norm_g, nat_rpb, w_out, norm2_g, router_w, router_b,
              w_gate_up, b_gate_up, w_down, b_down, final_norm_g):
    y_prompt = encoder(x_prompt, c_prompt, norm1_g, w_ada, b_ada, w_in, gla_w2_fwd, gla_b2_fwd, gla_w2_bwd,
                       gla_b2_bwd, gla_norm_g, nat_rpb, w_out, norm2_g, router_w, router_b, w_gate_up,
                       b_gate_up, w_down, b_down, final_norm_g)
    y_sample = encoder(x_sample, c_sample, norm1_g, w_ada, b_ada, w_in, gla_w2_fwd, gla_b2_fwd, gla_w2_bwd,
                       gla_b2_bwd, gla_norm_g, nat_rpb, w_out, norm2_g, router_w, router_b, w_gate_up,
                       b_gate_up, w_down, b_down, final_norm_g)
    return (y_prompt, y_sample)
```

```python
import functools

import numpy as np
import jax
import jax.numpy as jnp
from jax import lax
from jax.experimental import pallas as pl
from jax.experimental.pallas import tpu as pltpu

F32 = jnp.float32
BF16 = jnp.bfloat16
HIGHEST = lax.Precision.HIGHEST

EPS = 1e-5
GRID_W = 64
GLA_HEADS = 4
GLA_DK = 64
GLA_DV = 128
GLA_RANK = 16
GLA_TAU = 16.0
GLA_CHUNK = 64
GLA_QK = GLA_HEADS * GLA_DK
GLA_V = GLA_HEADS * GLA_DV
NAT_HEADS = 16
NAT_HD = 32
NAT_W = NAT_HEADS * NAT_HD
NAT_KH = 8
NAT_KW = 16
NAT_GROUP = 4
N_EXPERTS = 32
TOP_K = 4
SWIGLU_LIMIT = 7.0
SWIGLU_ALPHA = 1.702

LANES = 128
SUBLANES = 8
TOKEN_TILE = 512
MOE_ROWS = 512
GATHER_TILE = 256
VMEM_LIMIT = 56 * 1024 * 1024


def _cparams(sem, vmem=VMEM_LIMIT):
    return pltpu.CompilerParams(dimension_semantics=sem, vmem_limit_bytes=vmem)


def _dot(a, b):
    return jnp.dot(a, b, preferred_element_type=F32)


def _dot_nt(a, b, precision=None):
    return lax.dot_general(a, b, (((1,), (1,)), ((), ())), preferred_element_type=F32, precision=precision)


def _dot_tn(a, b):
    return lax.dot_general(a, b, (((0,), (0,)), ((), ())), preferred_element_type=F32)


def _ada_kernel(c_ref, w_ref, b_ref, o_ref):
    c = c_ref[...]
    s = c * jax.nn.sigmoid(c)
    o_ref[...] = jnp.dot(s, w_ref[...], preferred_element_type=F32, precision=HIGHEST) + b_ref[...]


def _ada(c, w, b):
    nb, d = c.shape
    cols = w.shape[1]
    blk = 1024
    return pl.pallas_call(
        _ada_kernel,
        grid=(cols // blk,),
        in_specs=[pl.BlockSpec((nb, d), lambda j: (0, 0)),
                  pl.BlockSpec((d, blk), lambda j: (0, j)),
                  pl.BlockSpec((1, blk), lambda j: (0, j))],
        out_specs=pl.BlockSpec((nb, blk), lambda j: (0, j)),
        out_shape=jax.ShapeDtypeStruct((nb, cols), F32),
        compiler_params=_cparams(("arbitrary",)),
    )(c, w, b.reshape(1, cols))


def _in_kernel(x_ref, mod_ref, g_ref, w_ref, pg_ref, pn_ref, plr_ref):
    x = x_ref[...]
    ms = jnp.mean(x * x, axis=-1, keepdims=True)
    y = x * lax.rsqrt(ms + EPS) * g_ref[...]
    h = y * (1.0 + mod_ref[0, 1:2, :]) + mod_ref[0, 0:1, :]
    hb = h.astype(BF16)
    wg = GLA_QK * 2 + GLA_V * 2
    wn = 3 * NAT_W
    pg_ref[...] = _dot(hb, w_ref[:, 0:wg]).astype(BF16)
    pn_ref[...] = _dot(hb, w_ref[:, wg:wg + wn]).astype(BF16)
    plr_ref[...] = _dot(hb, w_ref[:, wg + wn:wg + wn + LANES])


def _in_proj(x, mod3, norm_g, w_all, batch_of_tile):
    n, d = x.shape
    wg = GLA_QK * 2 + GLA_V * 2
    wn = 3 * NAT_W
    tm = TOKEN_TILE
    return pl.pallas_call(
        _in_kernel,
        grid=(n // tm,),
        in_specs=[pl.BlockSpec((tm, d), lambda i: (i, 0)),
                  pl.BlockSpec((1, 6, d), lambda i: (batch_of_tile(i), 0, 0)),
                  pl.BlockSpec((1, d), lambda i: (0, 0)),
                  pl.BlockSpec(w_all.shape, lambda i: (0, 0))],
        out_specs=[pl.BlockSpec((tm, wg), lambda i: (i, 0)),
                   pl.BlockSpec((tm, wn), lambda i: (i, 0)),
                   pl.BlockSpec((tm, LANES), lambda i: (i, 0))],
        out_shape=[jax.ShapeDtypeStruct((n, wg), BF16),
                   jax.ShapeDtypeStruct((n, wn), BF16),
                   jax.ShapeDtypeStruct((n, LANES), F32)],
        compiler_params=_cparams(("parallel",)),
    )(x, mod3, norm_g.reshape(1, d), w_all)


def _gla_kernel(q_ref, k_ref, v_ref, g_ref, lr_ref, w2f_ref, b2f_ref, w2b_ref, b2b_ref, ng_ref,
                o_ref, acc_ref, sf_ref, sb_ref, *, seq):
    c_len = GLA_CHUNK
    n = seq // c_len
    row = lax.broadcasted_iota(jnp.int32, (c_len, c_len), 0)
    col = lax.broadcasted_iota(jnp.int32, (c_len, c_len), 1)
    tri_f = col <= row
    tri_b = col >= row
    cum_f = tri_f.astype(F32)
    cum_b = tri_b.astype(F32)
    mask_f = jnp.concatenate([tri_f, tri_f], axis=0)
    mask_b = jnp.concatenate([tri_b, tri_b], axis=0)
    head_a = lax.broadcasted_iota(jnp.int32, (c_len, LANES), 1) < GLA_DK
    st_row = lax.broadcasted_iota(jnp.int32, (2 * GLA_DV, LANES), 0)
    st_col = lax.broadcasted_iota(jnp.int32, (2 * GLA_DV, LANES), 1)
    blockdiag = (st_row < GLA_DV) == (st_col < GLA_DK)

    sf_ref[...] = jnp.zeros_like(sf_ref)
    sb_ref[...] = jnp.zeros_like(sb_ref)

    def direction(c, s_ref, w2_ref, b2_ref, cum, mask):
        rows = pl.ds(pl.multiple_of(c * c_len, c_len), c_len)
        q = q_ref[rows, :].astype(F32) * (GLA_DK ** -0.5)
        k = k_ref[rows, :].astype(F32)
        vb = v_ref[rows, :]
        z = jnp.dot(lr_ref[rows, :], w2_ref[...], preferred_element_type=F32, precision=HIGHEST) + b2_ref[...]
        la = (jnp.minimum(z, 0.0) - jnp.log(1.0 + jnp.exp(-jnp.abs(z)))) * (1.0 / GLA_TAU)
        b = jnp.dot(cum, la, preferred_element_type=F32, precision=HIGHEST)
        btot = jnp.sum(la, axis=0, keepdims=True)
        q_t = q * jnp.exp(b)
        k_t = (k * jnp.exp(-b)).astype(BF16)
        k_end = (k * jnp.exp(btot - b)).astype(BF16)
        zero = jnp.zeros_like(q_t)
        q_stack = jnp.concatenate([jnp.where(head_a, q_t, zero), jnp.where(head_a, zero, q_t)], axis=0).astype(BF16)
        a = jnp.where(mask, _dot_nt(q_stack, k_t), 0.0)
        oi = _dot(a.astype(BF16), vb)
        o_intra = jnp.concatenate([oi[0:c_len, 0:GLA_DV], oi[c_len:, GLA_DV:]], axis=1)
        st = s_ref[...]
        o_inter = _dot_nt(q_t.astype(BF16), st.astype(BF16))
        upd = _dot_tn(vb, k_end)
        s_ref[...] = st * jnp.exp(btot) + jnp.where(blockdiag, upd, 0.0)
        return rows, o_intra + o_inter

    def finalize(rows, o):
        halves = []
        for hh in range(2):
            oh = o[:, hh * GLA_DV:(hh + 1) * GLA_DV]
            ms = jnp.mean(oh * oh, axis=-1, keepdims=True)
            halves.append(oh * lax.rsqrt(ms + EPS) * ng_ref[...])
        y = jnp.concatenate(halves, axis=1)
        g = g_ref[rows, :].astype(F32)
        o_ref[rows, :] = (y * (g * jax.nn.sigmoid(g))).astype(o_ref.dtype)

    def first_half(c, carry):
        rows, o = direction(c, sf_ref, w2f_ref, b2f_ref, cum_f, mask_f)
        acc_ref[rows, :] = o
        rows, o = direction(n - 1 - c, sb_ref, w2b_ref, b2b_ref, cum_b, mask_b)
        acc_ref[rows, :] = o
        return carry

    def second_half(c, carry):
        rows, o = direction(c, sf_ref, w2f_ref, b2f_ref, cum_f, mask_f)
        finalize(rows, acc_ref[rows, :] + o)
        rows, o = direction(n - 1 - c, sb_ref, w2b_ref, b2b_ref, cum_b, mask_b)
        finalize(rows, acc_ref[rows, :] + o)
        return carry

    lax.fori_loop(0, n // 2, first_half, 0)
    lax.fori_loop(n // 2, n, second_half, 0)


def _gla(pg, plr, w2f, b2f, w2b, b2b, norm_g, *, seq, batches, block0):
    pair_w = 2 * GLA_DK
    pair_v = 2 * GLA_DV
    k0 = GLA_QK // pair_w
    v0 = 2 * GLA_QK // pair_v
    g0 = (2 * GLA_QK + GLA_V) // pair_v
    in_specs = [pl.BlockSpec((seq, pair_w), lambda b, p: (block0 + b, p)),
                pl.BlockSpec((seq, pair_w), lambda b, p: (block0 + b, k0 + p)),
                pl.BlockSpec((seq, pair_v), lambda b, p: (block0 + b, v0 + p)),
                pl.BlockSpec((seq, pair_v), lambda b, p: (block0 + b, g0 + p)),
                pl.BlockSpec((seq, LANES), lambda b, p: (block0 + b, 0)),
                pl.BlockSpec((LANES, pair_w), lambda b, p: (0, p)),
                pl.BlockSpec((1, pair_w), lambda b, p: (0, p)),
                pl.BlockSpec((LANES, pair_w), lambda b, p: (0, p)),
                pl.BlockSpec((1, pair_w), lambda b, p: (0, p)),
                pl.BlockSpec((1, GLA_DV), lambda b, p: (0, 0))]
    return pl.pallas_call(
        functools.partial(_gla_kernel, seq=seq),
        grid=(batches, GLA_HEADS // 2),
        in_specs=in_specs,
        out_specs=pl.BlockSpec((seq, pair_v), lambda b, p: (b, p)),
        out_shape=jax.ShapeDtypeStruct((batches * seq, GLA_V), BF16),
        scratch_shapes=[pltpu.VMEM((seq, pair_v), F32),
                        pltpu.VMEM((pair_v, pair_w), F32),
                        pltpu.VMEM((pair_v, pair_w), F32)],
        compiler_params=_cparams(("parallel", "parallel")),
    )(pg, pg, pg, pg, plr, w2f, b2f, w2b, b2b, norm_g)


def _nat_kernel(q_ref, k_ref, v_ref, tab_ref, o_ref, *, rows):
    w = GRID_W
    head = lax.broadcasted_iota(jnp.int32, (w, LANES), 1) // NAT_HD
    win = NAT_KH * w

    def body(i, carry):
        r0 = jnp.clip(i - NAT_KH // 2, 0, rows - NAT_KH)
        base = r0 - i + NAT_KH - 1
        qrows = pl.ds(pl.multiple_of(i * w, w), w)
        krows = pl.ds(pl.multiple_of(r0 * w, w), win)
        q = q_ref[qrows, :]
        zero = jnp.zeros_like(q)
        q_stack = jnp.concatenate([jnp.where(head == h, q, zero) for h in range(NAT_GROUP)], axis=0)
        s = _dot_nt(q_stack, k_ref[krows, :]) * (NAT_HD ** -0.5) + tab_ref[0, base]
        m = jnp.max(s, axis=-1, keepdims=True)
        e = jnp.exp(s - m)
        l = jnp.sum(e, axis=-1, keepdims=True)
        o = _dot(e.astype(BF16), v_ref[krows, :]) / l
        out = jnp.zeros((w, LANES), F32)
        for h in range(NAT_GROUP):
            out = out + jnp.where(head == h, o[h * w:(h + 1) * w, :], 0.0)
        o_ref[qrows, :] = out.astype(o_ref.dtype)
        return carry

    lax.fori_loop(0, rows, body, 0)


def _nat(pn, tab, *, seq, batches, block0):
    rows = seq // GRID_W
    groups = NAT_HEADS // NAT_GROUP
    in_specs = [pl.BlockSpec((seq, LANES), lambda g, b: (block0 + b, g)),
                pl.BlockSpec((seq, LANES), lambda g, b: (block0 + b, groups + g)),
                pl.BlockSpec((seq, LANES), lambda g, b: (block0 + b, 2 * groups + g)),
                pl.BlockSpec((1,) + tab.shape[1:], lambda g, b: (g, 0, 0, 0))]
    return pl.pallas_call(
        functools.partial(_nat_kernel, rows=rows),
        grid=(groups, batches),
        in_specs=in_specs,
        out_specs=pl.BlockSpec((seq, LANES), lambda g, b: (b, g)),
        out_shape=jax.ShapeDtypeStruct((batches * seq, NAT_W), BF16),
        compiler_params=_cparams(("parallel", "parallel")),
    )(pn, pn, pn, tab)


def _nat_bias_table(rpb):
    w = GRID_W
    jq = np.arange(w)[:, None]
    jk = np.arange(w)[None, :]
    c0 = np.clip(jq - NAT_KW // 2, 0, w - NAT_KW)
    valid = (jk >= c0) & (jk < c0 + NAT_KW)
    dcol = np.clip(jk - jq + NAT_KW - 1, 0, 2 * NAT_KW - 2)
    drow = np.arange(NAT_KH)[:, None] + np.arange(NAT_KH)[None, :]
    t = rpb.astype(F32)[:, drow][:, :, :, dcol]
    t = jnp.where(valid[None, None, None], t, -jnp.inf)
    t = t.transpose(0, 1, 3, 2, 4).reshape(NAT_HEADS, NAT_KH, w, NAT_KH * w)
    t = t.reshape(NAT_HEADS // NAT_GROUP, NAT_GROUP, NAT_KH, w, NAT_KH * w).transpose(0, 2, 1, 3, 4)
    return t.reshape(NAT_HEADS // NAT_GROUP, NAT_KH, NAT_GROUP * w, NAT_KH * w)


def _out_kernel(og_ref, on_ref, x_ref, mod_ref, wg_ref, wn_ref, n2_ref, rw_ref, rb_ref,
                x1_ref, h2_ref, ti_ref, gt_ref, rk_ref, cnt_ref, carry_ref):
    tm = x_ref.shape[0]

    @pl.when(pl.program_id(0) == 0)
    def _():
        carry_ref[...] = jnp.zeros_like(carry_ref)

    mix = _dot(og_ref[...], wg_ref[...]) + _dot(on_ref[...], wn_ref[...])
    x1 = x_ref[...] + mod_ref[0, 2:3, :] * mix
    x1_ref[...] = x1
    ms = jnp.mean(x1 * x1, axis=-1, keepdims=True)
    h2 = x1 * lax.rsqrt(ms + EPS) * n2_ref[...]
    h2 = h2 * (1.0 + mod_ref[0, 4:5, :]) + mod_ref[0, 3:4, :]
    for s in range(SUBLANES):
        h2_ref[pl.ds(s, tm, stride=SUBLANES), :] = h2[:, s * LANES:(s + 1) * LANES]

    logits = _dot_nt(rw_ref[...], h2, precision=HIGHEST) + rb_ref[...]
    eidx = lax.broadcasted_iota(jnp.int32, logits.shape, 0)
    vals = logits
    top_v, top_i = [], []
    for _ in range(TOP_K):
        m = jnp.max(vals, axis=0, keepdims=True)
        idx = jnp.min(jnp.where(vals == m, eidx, N_EXPERTS), axis=0, keepdims=True)
        top_v.append(m)
        top_i.append(idx)
        vals = jnp.where(eidx == idx, -jnp.inf, vals)
    ev = [jnp.exp(v - top_v[0]) for v in top_v]
    den = ev[0] + ev[1] + ev[2] + ev[3]
    ti_ref[...] = jnp.concatenate(top_i, axis=0)
    gt_ref[...] = jnp.concatenate([e / den for e in ev], axis=0)

    onehot = jnp.zeros(logits.shape, F32)
    for idx in top_i:
        onehot = onehot + (eidx == idx).astype(F32)
    r = lax.broadcasted_iota(jnp.int32, (tm, tm), 0)
    c = lax.broadcasted_iota(jnp.int32, (tm, tm), 1)
    earlier = (r < c).astype(BF16)
    before = carry_ref[...] + _dot(onehot.astype(BF16), earlier)
    rk_ref[...] = jnp.concatenate(
        [jnp.sum(jnp.where(eidx == idx, before, 0.0), axis=0, keepdims=True) for idx in top_i], axis=0).astype(jnp.int32)
    carry_ref[...] = carry_ref[...] + jnp.sum(onehot, axis=1, keepdims=True)
    cnt_ref[...] = jnp.broadcast_to(carry_ref[...], cnt_ref.shape)


def _out_proj(og, on, x, mod3, w_g, w_n, norm_g, rw_t, rb, batch_of_tile):
    n, d = x.shape
    tm = TOKEN_TILE
    tok = lambda i: (i, 0)
    const = lambda i: (0, 0)
    return pl.pallas_call(
        _out_kernel,
        grid=(n // tm,),
        in_specs=[pl.BlockSpec((tm, GLA_V), tok),
                  pl.BlockSpec((tm, NAT_W), tok),
                  pl.BlockSpec((tm, d), tok),
                  pl.BlockSpec((1, 6, d), lambda i: (batch_of_tile(i), 0, 0)),
                  pl.BlockSpec(w_g.shape, const),
                  pl.BlockSpec(w_n.shape, const),
                  pl.BlockSpec((1, d), const),
                  pl.BlockSpec(rw_t.shape, const),
                  pl.BlockSpec((N_EXPERTS, 1), const)],
        out_specs=[pl.BlockSpec((tm, d), tok),
                   pl.BlockSpec((tm * SUBLANES, LANES), tok),
                   pl.BlockSpec((TOP_K, tm), lambda i: (0, i)),
                   pl.BlockSpec((TOP_K, tm), lambda i: (0, i)),
                   pl.BlockSpec((TOP_K, tm), lambda i: (0, i)),
                   pl.BlockSpec((N_EXPERTS, LANES), const)],
        out_shape=[jax.ShapeDtypeStruct((n, d), F32),
                   jax.ShapeDtypeStruct((n * SUBLANES, LANES), F32),
                   jax.ShapeDtypeStruct((TOP_K, n), jnp.int32),
                   jax.ShapeDtypeStruct((TOP_K, n), F32),
                   jax.ShapeDtypeStruct((TOP_K, n), jnp.int32),
                   jax.ShapeDtypeStruct((N_EXPERTS, LANES), F32)],
        scratch_shapes=[pltpu.VMEM((N_EXPERTS, 1), F32)],
        compiler_params=_cparams(("arbitrary",)),
    )(og, on, x, mod3, w_g, w_n, norm_g.reshape(1, d), rw_t, rb.reshape(N_EXPERTS, 1))


def _row_copy(src, dst, sem):
    return pltpu.make_async_copy(src, dst, sem)


def _dispatch_kernel(dest_ref, h_hbm, xs_in, xs_hbm, sem):
    del xs_in
    tt = dest_ref.shape[1]
    t0 = pl.program_id(0) * tt

    def issue(t, carry):
        for k in range(TOP_K):
            _row_copy(h_hbm.at[t0 + t], xs_hbm.at[dest_ref[k, t]], sem).start()
        return carry

    def drain(t, carry):
        for k in range(TOP_K):
            _row_copy(h_hbm.at[t0 + t], xs_hbm.at[dest_ref[k, t]], sem).wait()
        return carry

    lax.fori_loop(0, tt, issue, 0)
    lax.fori_loop(0, tt, drain, 0)


def _dispatch(dest_t, h3, xs_init):
    n = h3.shape[0]
    tt = GATHER_TILE
    return pl.pallas_call(
        _dispatch_kernel,
        grid=(n // tt,),
        in_specs=[pl.BlockSpec((TOP_K, tt), lambda i: (0, i), memory_space=pltpu.SMEM),
                  pl.BlockSpec(memory_space=pl.ANY),
                  pl.BlockSpec(memory_space=pl.ANY)],
        out_specs=pl.BlockSpec(memory_space=pl.ANY),
        out_shape=jax.ShapeDtypeStruct(xs_init.shape, xs_init.dtype),
        scratch_shapes=[pltpu.SemaphoreType.DMA(())],
        input_output_aliases={2: 0},
        compiler_params=_cparams(("arbitrary",)),
    )(dest_t, h3, xs_init)


def _expert_kernel(be_ref, nu_ref, xs_ref, wg_ref, wu_ref, bg_ref, bu_ref, wd_ref, bd_ref, ys_ref):
    rows = xs_ref.shape[0] // SUBLANES
    used = pl.program_id(0) < nu_ref[0]

    @pl.when(used)
    def _():
        x = jnp.concatenate([xs_ref[pl.ds(s, rows, stride=SUBLANES), :] for s in range(SUBLANES)], axis=1).astype(BF16)
        gate = jnp.minimum(_dot(x, wg_ref[0]) + bg_ref[0], SWIGLU_LIMIT)
        up = jnp.clip(_dot(x, wu_ref[0]) + bu_ref[0], -SWIGLU_LIMIT, SWIGLU_LIMIT)
        act = (up + 1.0) * gate * jax.nn.sigmoid(SWIGLU_ALPHA * gate)
        y = _dot(act.astype(BF16), wd_ref[0]) + bd_ref[0]
        for s in range(SUBLANES):
            ys_ref[pl.ds(s, rows, stride=SUBLANES), :] = y[:, s * LANES:(s + 1) * LANES]

    @pl.when(jnp.logical_not(used))
    def _():
        ys_ref[...] = jnp.zeros_like(ys_ref)


def _experts(blk_e, n_used, xs2, w_gate, w_up, b_gate, b_up, w_down, b_down):
    cap8 = xs2.shape[0]
    rows8 = MOE_ROWS * SUBLANES
    d, f = w_gate.shape[1], w_gate.shape[2]
    wmap = lambda j, be, nu: (be[j], 0, 0)
    grid_spec = pltpu.PrefetchScalarGridSpec(
        num_scalar_prefetch=2,
        grid=(cap8 // rows8,),
        in_specs=[pl.BlockSpec((rows8, LANES), lambda j, be, nu: (j, 0)),
                  pl.BlockSpec((1, d, f), wmap),
                  pl.BlockSpec((1, d, f), wmap),
                  pl.BlockSpec((1, 1, f), wmap),
                  pl.BlockSpec((1, 1, f), wmap),
                  pl.BlockSpec((1, f, d), wmap),
                  pl.BlockSpec((1, 1, d), wmap)],
        out_specs=pl.BlockSpec((rows8, LANES), lambda j, be, nu: (j, 0)),
    )
    return pl.pallas_call(
        _expert_kernel,
        grid_spec=grid_spec,
        out_shape=jax.ShapeDtypeStruct((cap8, LANES), F32),
        compiler_params=_cparams(("arbitrary",)),
    )(blk_e, n_used, xs2, w_gate, w_up, b_gate, b_up, w_down, b_down)


def _combine_kernel(dest_ref, gates_ref, x1_ref, mod_ref, fg_ref, ys_hbm, o_ref, rows_ref, sem):
    tt = x1_ref.shape[0]

    def slot(k, t):
        return rows_ref.at[pl.ds(pl.multiple_of((k * tt + t) * SUBLANES, SUBLANES), SUBLANES), :]

    def issue(t, carry):
        for k in range(TOP_K):
            _row_copy(ys_hbm.at[dest_ref[k, t]], slot(k, t), sem).start()
        return carry

    def drain(t, carry):
        for k in range(TOP_K):
            _row_copy(ys_hbm.at[dest_ref[k, t]], slot(k, t), sem).wait()
        return carry

    lax.fori_loop(0, tt, issue, 0)
    lax.fori_loop(0, tt, drain, 0)

    gates = gates_ref[...]
    y = jnp.zeros(x1_ref.shape, F32)
    for k in range(TOP_K):
        yk = jnp.concatenate(
            [rows_ref[pl.ds(k * tt * SUBLANES + s, tt, stride=SUBLANES), :] for s in range(SUBLANES)], axis=1)
        y = y + gates[:, k:k + 1] * yk
    x2 = x1_ref[...] + mod_ref[0, 5:6, :] * y
    ms = jnp.mean(x2 * x2, axis=-1, keepdims=True)
    o_ref[...] = x2 * lax.rsqrt(ms + EPS) * fg_ref[...]


def _combine(dest_t, gates, x1, mod3, final_g, ys3, batch_of_tile):
    n, d = x1.shape
    tt = GATHER_TILE
    return pl.pallas_call(
        _combine_kernel,
        grid=(n // tt,),
        in_specs=[pl.BlockSpec((TOP_K, tt), lambda i: (0, i), memory_space=pltpu.SMEM),
                  pl.BlockSpec((tt, TOP_K), lambda i: (i, 0)),
                  pl.BlockSpec((tt, d), lambda i: (i, 0)),
                  pl.BlockSpec((1, 6, d), lambda i: (batch_of_tile(i), 0, 0)),
                  pl.BlockSpec((1, d), lambda i: (0, 0)),
                  pl.BlockSpec(memory_space=pl.ANY)],
        out_specs=pl.BlockSpec((tt, d), lambda i: (i, 0)),
        out_shape=jax.ShapeDtypeStruct((n, d), F32),
        scratch_shapes=[pltpu.VMEM((TOP_K * tt * SUBLANES, LANES), F32),
                        pltpu.SemaphoreType.DMA(())],
        compiler_params=_cparams(("arbitrary",)),
    )(dest_t, gates, x1, mod3, final_g.reshape(1, d), ys3)


def _batch_of_tile_fn(tile, bp, tp, ts):
    n_prompt = bp * tp

    def batch_of_tile(i):
        t = i * tile
        return jnp.where(t < n_prompt, t // tp, bp + (t - n_prompt) // ts)

    return batch_of_tile


def kernel(x_prompt, x_sample, c_prompt, c_sample, norm1_g, w_ada, b_ada, w_in, gla_w2_fwd, gla_b2_fwd, gla_w2_bwd, gla_b2_bwd, gla_norm_g, nat_rpb, w_out, norm2_g, router_w, router_b, w_gate_up, b_gate_up, w_down, b_down, final_norm_g):
    assert w_ada.shape[0] == 1, "single-layer encoder"
    bp, tp, d = x_prompt.shape
    bs, ts, _ = x_sample.shape
    n_prompt, n_sample = bp * tp, bs * ts
    n = n_prompt + n_sample
    assert tp % TOKEN_TILE == 0 and ts % TOKEN_TILE == 0 and n_prompt % ts == 0
    assert d == SUBLANES * LANES

    x = jnp.concatenate([x_prompt.reshape(n_prompt, d), x_sample.reshape(n_sample, d)], axis=0)
    c = jnp.concatenate([c_prompt, c_sample], axis=0)

    sizes = (GLA_QK, GLA_QK, GLA_V, GLA_V, GLA_RANK, GLA_RANK, NAT_W, NAT_W, NAT_W)
    offs = np.concatenate([[0], np.cumsum(sizes)])
    w_in0 = w_in[0]
    seg = lambda j: w_in0[:, offs[j]:offs[j + 1]]
    w_all = jnp.concatenate(
        [seg(0), seg(1), seg(2), seg(3), seg(6), seg(7), seg(8), seg(4), seg(5),
         jnp.zeros((d, LANES - 2 * GLA_RANK), F32)], axis=1).astype(BF16)
    w2f = jnp.zeros((LANES, GLA_QK), F32).at[0:GLA_RANK].set(gla_w2_fwd[0])
    w2b = jnp.zeros((LANES, GLA_QK), F32).at[GLA_RANK:2 * GLA_RANK].set(gla_w2_bwd[0])
    tab = _nat_bias_table(nat_rpb[0])
    w_og = w_out[0, :GLA_V].astype(BF16)
    w_on = w_out[0, GLA_V:].astype(BF16)
    rw_t = router_w[0].T
    w_gate = w_gate_up[0, :, :, 0::2].astype(BF16)
    w_up = w_gate_up[0, :, :, 1::2].astype(BF16)
    b_gate = b_gate_up[0, :, None, 0::2]
    b_up = b_gate_up[0, :, None, 1::2]
    w_dn = w_down[0].astype(BF16)
    b_dn = b_down[0, :, None, :]

    mod3 = _ada(c, w_ada[0], b_ada[0]).reshape(bp + bs, 6, d)
    bot_tok = _batch_of_tile_fn(TOKEN_TILE, bp, tp, ts)
    bot_gat = _batch_of_tile_fn(GATHER_TILE, bp, tp, ts)

    pg, pn, plr = _in_proj(x, mod3, norm1_g[0], w_all, bot_tok)

    gla_args = (pg, plr, w2f, gla_b2_fwd[0].reshape(1, -1), w2b, gla_b2_bwd[0].reshape(1, -1),
                gla_norm_g[0].reshape(1, -1))
    og = jnp.concatenate([_gla(*gla_args, seq=tp, batches=bp, block0=0),
                          _gla(*gla_args, seq=ts, batches=bs, block0=n_prompt // ts)], axis=0)
    on = jnp.concatenate([_nat(pn, tab, seq=tp, batches=bp, block0=0),
                          _nat(pn, tab, seq=ts, batches=bs, block0=n_prompt // ts)], axis=0)

    x1, h2, top_i, gates_t, rank_t, cnt = _out_proj(og, on, x, mod3, w_og, w_on, norm2_g[0], rw_t, router_b[0], bot_tok)

    counts = cnt[:, 0].astype(jnp.int32)
    padded = (counts + MOE_ROWS - 1) // MOE_ROWS * MOE_ROWS
    pad_end = jnp.cumsum(padded)
    pad_start = pad_end - padded
    n_blocks = -(-(n * TOP_K) // MOE_ROWS) + N_EXPERTS
    cap = n_blocks * MOE_ROWS
    dest_t = pad_start[top_i] + rank_t
    blk_e = jnp.minimum(jnp.searchsorted(pad_end, jnp.arange(n_blocks, dtype=jnp.int32) * MOE_ROWS, side='right'),
                        N_EXPERTS - 1).astype(jnp.int32)
    n_used = (pad_end[-1:] // MOE_ROWS).astype(jnp.int32)

    xs = _dispatch(dest_t, h2.reshape(n, SUBLANES, LANES), jnp.zeros((cap, SUBLANES, LANES), F32))
    ys = _experts(blk_e, n_used, xs.reshape(cap * SUBLANES, LANES), w_gate, w_up, b_gate, b_up, w_dn, b_dn)
    y = _combine(dest_t, gates_t.T, x1, mod3, final_norm_g, ys.reshape(cap, SUBLANES, LANES), bot_gat)

    return (y[:n_prompt].reshape(bp, tp, d), y[n_prompt:].reshape(bs, ts, d))
```

```python
import functools

import numpy as np
import jax
import jax.numpy as jnp
from jax import lax
from jax.experimental import pallas as pl
from jax.experimental.pallas import tpu as pltpu

F32 = jnp.float32
BF16 = jnp.bfloat16
HIGHEST = lax.Precision.HIGHEST

EPS = 1e-5
GRID_W = 64
GLA_HEADS = 4
GLA_DK = 64
GLA_DV = 128
GLA_RANK = 16
GLA_TAU = 16.0
GLA_CHUNK = 64
GLA_QK = GLA_HEADS * GLA_DK
GLA_V = GLA_HEADS * GLA_DV
NAT_HEADS = 16
NAT_HD = 32
NAT_W = NAT_HEADS * NAT_HD
NAT_KH = 8
NAT_KW = 16
NAT_GROUP = 4
N_EXPERTS = 32
TOP_K = 4
SWIGLU_LIMIT = 7.0
SWIGLU_ALPHA = 1.702

LANES = 128
SUBLANES = 8
TOKEN_TILE = 512
MOE_ROWS = 512
GATHER_TILE = 256
VMEM_LIMIT = 56 * 1024 * 1024


def _cparams(sem, vmem=VMEM_LIMIT):
    return pltpu.CompilerParams(dimension_semantics=sem, vmem_limit_bytes=vmem)


def _dot(a, b):
    return jnp.dot(a, b, preferred_element_type=F32)


def _dot_nt(a, b, precision=None):
    return lax.dot_general(a, b, (((1,), (1,)), ((), ())), preferred_element_type=F32, precision=precision)


def _dot_tn(a, b):
    return lax.dot_general(a, b, (((0,), (0,)), ((), ())), preferred_element_type=F32)


def _ada_kernel(c_ref, w_ref, b_ref, o_ref):
    c = c_ref[...]
    s = c * jax.nn.sigmoid(c)
    o_ref[...] = jnp.dot(s, w_ref[...], preferred_element_type=F32, precision=HIGHEST) + b_ref[...]


def _ada(c, w, b):
    nb, d = c.shape
    cols = w.shape[1]
    blk = 1024
    return pl.pallas_call(
        _ada_kernel,
        grid=(cols // blk,),
        in_specs=[pl.BlockSpec((nb, d), lambda j: (0, 0)),
                  pl.BlockSpec((d, blk), lambda j: (0, j)),
                  pl.BlockSpec((1, blk), lambda j: (0, j))],
        out_specs=pl.BlockSpec((nb, blk), lambda j: (0, j)),
        out_shape=jax.ShapeDtypeStruct((nb, cols), F32),
        compiler_params=_cparams(("arbitrary",)),
    )(c, w, b.reshape(1, cols))


def _in_kernel(x_ref, mod_ref, g_ref, w_ref, pg_ref, pn_ref, plr_ref):
    x = x_ref[...]
    ms = jnp.mean(x * x, axis=-1, keepdims=True)
    y = x * lax.rsqrt(ms + EPS) * g_ref[...]
    h = y * (1.0 + mod_ref[0, 1:2, :]) + mod_ref[0, 0:1, :]
    hb = h.astype(BF16)
    wg = GLA_QK * 2 + GLA_V * 2
    wn = 3 * NAT_W
    pg_ref[...] = _dot(hb, w_ref[:, 0:wg]).astype(BF16)
    pn_ref[...] = _dot(hb, w_ref[:, wg:wg + wn]).astype(BF16)
    plr_ref[...] = _dot(hb, w_ref[:, wg + wn:wg + wn + LANES])


def _in_proj(x, mod3, norm_g, w_all, batch_of_tile):
    n, d = x.shape
    wg = GLA_QK * 2 + GLA_V * 2
    wn = 3 * NAT_W
    tm = TOKEN_TILE
    return pl.pallas_call(
        _in_kernel,
        grid=(n // tm,),
        in_specs=[pl.BlockSpec((tm, d), lambda i: (i, 0)),
                  pl.BlockSpec((1, 6, d), lambda i: (batch_of_tile(i), 0, 0)),
                  pl.BlockSpec((1, d), lambda i: (0, 0)),
                  pl.BlockSpec(w_all.shape, lambda i: (0, 0))],
        out_specs=[pl.BlockSpec((tm, wg), lambda i: (i, 0)),
                   pl.BlockSpec((tm, wn), lambda i: (i, 0)),
                   pl.BlockSpec((tm, LANES), lambda i: (i, 0))],
        out_shape=[jax.ShapeDtypeStruct((n, wg), BF16),
                   jax.ShapeDtypeStruct((n, wn), BF16),
                   jax.ShapeDtypeStruct((n, LANES), F32)],
        compiler_params=_cparams(("parallel",)),
    )(x, mod3, norm_g.reshape(1, d), w_all)


def _gla_kernel(q_ref, k_ref, v_ref, g_ref, lr_ref, w2f_ref, b2f_ref, w2b_ref, b2b_ref, ng_ref,
                o_ref, acc_ref, sf_ref, sb_ref, *, seq):
    c_len = GLA_CHUNK
    n = seq // c_len
    row = lax.broadcasted_iota(jnp.int32, (c_len, c_len), 0)
    col = lax.broadcasted_iota(jnp.int32, (c_len, c_len), 1)
    tri_f = col <= row
    tri_b = col >= row
    cum_f = tri_f.astype(F32)
    cum_b = tri_b.astype(F32)
    mask_f = jnp.concatenate([tri_f, tri_f], axis=0)
    mask_b = jnp.concatenate([tri_b, tri_b], axis=0)
    head_a = lax.broadcasted_iota(jnp.int32, (c_len, LANES), 1) < GLA_DK
    st_row = lax.broadcasted_iota(jnp.int32, (2 * GLA_DV, LANES), 0)
    st_col = lax.broadcasted_iota(jnp.int32, (2 * GLA_DV, LANES), 1)
    blockdiag = (st_row < GLA_DV) == (st_col < GLA_DK)

    sf_ref[...] = jnp.zeros_like(sf_ref)
    sb_ref[...] = jnp.zeros_like(sb_ref)

    def direction(c, s_ref, w2_ref, b2_ref, cum, mask):
        rows = pl.ds(pl.multiple_of(c * c_len, c_len), c_len)
        q = q_ref[rows, :].astype(F32) * (GLA_DK ** -0.5)
        k = k_ref[rows, :].astype(F32)
        vb = v_ref[rows, :]
        z = jnp.dot(lr_ref[rows, :], w2_ref[...], preferred_element_type=F32, precision=HIGHEST) + b2_ref[...]
        la = (jnp.minimum(z, 0.0) - jnp.log(1.0 + jnp.exp(-jnp.abs(z)))) * (1.0 / GLA_TAU)
        b = jnp.dot(cum, la, preferred_element_type=F32, precision=HIGHEST)
        btot = jnp.sum(la, axis=0, keepdims=True)
        q_t = q * jnp.exp(b)
        k_t = (k * jnp.exp(-b)).astype(BF16)
        k_end = (k * jnp.exp(btot - b)).astype(BF16)
        zero = jnp.zeros_like(q_t)
        q_stack = jnp.concatenate([jnp.where(head_a, q_t, zero), jnp.where(head_a, zero, q_t)], axis=0).astype(BF16)
        a = jnp.where(mask, _dot_nt(q_stack, k_t), 0.0)
        oi = _dot(a.astype(BF16), vb)
        o_intra = jnp.concatenate([oi[0:c_len, 0:GLA_DV], oi[c_len:, GLA_DV:]], axis=1)
        st = s_ref[...]
        o_inter = _dot_nt(q_t.astype(BF16), st.astype(BF16))
        upd = _dot_tn(vb, k_end)
        s_ref[...] = st * jnp.exp(btot) + jnp.where(blockdiag, upd, 0.0)
        return rows, o_intra + o_inter

    def finalize(rows, o):
        halves = []
        for hh in range(2):
            oh = o[:, hh * GLA_DV:(hh + 1) * GLA_DV]
            ms = jnp.mean(oh * oh, axis=-1, keepdims=True)
            halves.append(oh * lax.rsqrt(ms + EPS) * ng_ref[...])
        y = jnp.concatenate(halves, axis=1)
        g = g_ref[rows, :].astype(F32)
        o_ref[rows, :] = (y * (g * jax.nn.sigmoid(g))).astype(o_ref.dtype)

    def first_half(c, carry):
        rows, o = direction(c, sf_ref, w2f_ref, b2f_ref, cum_f, mask_f)
        acc_ref[rows, :] = o
        rows, o = direction(n - 1 - c, sb_ref, w2b_ref, b2b_ref, cum_b, mask_b)
        acc_ref[rows, :] = o
        return carry

    def second_half(c, carry):
        rows, o = direction(c, sf_ref, w2f_ref, b2f_ref, cum_f, mask_f)
        finalize(rows, acc_ref[rows, :] + o)
        rows, o = direction(n - 1 - c, sb_ref, w2b_ref, b2b_ref, cum_b, mask_b)
        finalize(rows, acc_ref[rows, :] + o)
        return carry

    lax.fori_loop(0, n // 2, first_half, 0)
    lax.fori_loop(n // 2, n, second_half, 0)


def _gla(pg, plr, w2f, b2f, w2b, b2b, norm_g, *, seq, batches, block0):
    pair_w = 2 * GLA_DK
    pair_v = 2 * GLA_DV
    k0 = GLA_QK // pair_w
    v0 = 2 * GLA_QK // pair_v
    g0 = (2 * GLA_QK + GLA_V) // pair_v
    in_specs = [pl.BlockSpec((seq, pair_w), lambda b, p: (block0 + b, p)),
                pl.BlockSpec((seq, pair_w), lambda b, p: (block0 + b, k0 + p)),
                pl.BlockSpec((seq, pair_v), lambda b, p: (block0 + b, v0 + p)),
                pl.BlockSpec((seq, pair_v), lambda b, p: (block0 + b, g0 + p)),
                pl.BlockSpec((seq, LANES), lambda b, p: (block0 + b, 0)),
                pl.BlockSpec((LANES, pair_w), lambda b, p: (0, p)),
                pl.BlockSpec((1, pair_w), lambda b, p: (0, p)),
                pl.BlockSpec((LANES, pair_w), lambda b, p: (0, p)),
                pl.BlockSpec((1, pair_w), lambda b, p: (0, p)),
                pl.BlockSpec((1, GLA_DV), lambda b, p: (0, 0))]
    return pl.pallas_call(
        functools.partial(_gla_kernel, seq=seq),
        grid=(batches, GLA_HEADS // 2),
        in_specs=in_specs,
        out_specs=pl.BlockSpec((seq, pair_v), lambda b, p: (b, p)),
        out_shape=jax.ShapeDtypeStruct((batches * seq, GLA_V), BF16),
        scratch_shapes=[pltpu.VMEM((seq, pair_v), F32),
                        pltpu.VMEM((pair_v, pair_w), F32),
                        pltpu.VMEM((pair_v, pair_w), F32)],
        compiler_params=_cparams(("parallel", "parallel")),
    )(pg, pg, pg, pg, plr, w2f, b2f, w2b, b2b, norm_g)


def _nat_kernel(q_ref, k_ref, v_ref, tab_ref, o_ref, *, rows):
    w = GRID_W
    head = lax.broadcasted_iota(jnp.int32, (w, LANES), 1) // NAT_HD
    win = NAT_KH * w

    def body(i, carry):
        r0 = jnp.clip(i - NAT_KH // 2, 0, rows - NAT_KH)
        base = r0 - i + NAT_KH - 1
        qrows = pl.ds(pl.multiple_of(i * w, w), w)
        krows = pl.ds(pl.multiple_of(r0 * w, w), win)
        q = q_ref[qrows, :]
        zero = jnp.zeros_like(q)
        q_stack = jnp.concatenate([jnp.where(head == h, q, zero) for h in range(NAT_GROUP)], axis=0)
        s = _dot_nt(q_stack, k_ref[krows, :]) * (NAT_HD ** -0.5) + tab_ref[0, base]
        m = jnp.max(s, axis=-1, keepdims=True)
        e = jnp.exp(s - m)
        l = jnp.sum(e, axis=-1, keepdims=True)
        o = _dot(e.astype(BF16), v_ref[krows, :]) / l
        out = jnp.zeros((w, LANES), F32)
        for h in range(NAT_GROUP):
            out = out + jnp.where(head == h, o[h * w:(h + 1) * w, :], 0.0)
        o_ref[qrows, :] = out.astype(o_ref.dtype)
        return carry

    lax.fori_loop(0, rows, body, 0)


def _nat(pn, tab, *, seq, batches, block0):
    rows = seq // GRID_W
    groups = NAT_HEADS // NAT_GROUP
    in_specs = [pl.BlockSpec((seq, LANES), lambda g, b: (block0 + b, g)),
                pl.BlockSpec((seq, LANES), lambda g, b: (block0 + b, groups + g)),
                pl.BlockSpec((seq, LANES), lambda g, b: (block0 + b, 2 * groups + g)),
                pl.BlockSpec((1,) + tab.shape[1:], lambda g, b: (g, 0, 0, 0))]
    return pl.pallas_call(
        functools.partial(_nat_kernel, rows=rows),
        grid=(groups, batches),
        in_specs=in_specs,
        out_specs=pl.BlockSpec((seq, LANES), lambda g, b: (b, g)),
        out_shape=jax.ShapeDtypeStruct((batches * seq, NAT_W), BF16),
        compiler_params=_cparams(("parallel", "parallel")),
    )(pn, pn, pn, tab)


def _nat_bias_table(rpb):
    w = GRID_W
    jq = np.arange(w)[:, None]
    jk = np.arange(w)[None, :]
    c0 = np.clip(jq - NAT_KW // 2, 0, w - NAT_KW)
    valid = (jk >= c0) & (jk < c0 + NAT_KW)
    dcol = np.clip(jk - jq + NAT_KW - 1, 0, 2 * NAT_KW - 2)
    drow = np.arange(NAT_KH)[:, None] + np.arange(NAT_KH)[None, :]
    t = rpb.astype(F32)[:, drow][:, :, :, dcol]
    t = jnp.where(valid[None, None, None], t, -jnp.inf)
    t = t.transpose(0, 1, 3, 2, 4).reshape(NAT_HEADS, NAT_KH, w, NAT_KH * w)
    t = t.reshape(NAT_HEADS // NAT_GROUP, NAT_GROUP, NAT_KH, w, NAT_KH * w).transpose(0, 2, 1, 3, 4)
    return t.reshape(NAT_HEADS // NAT_GROUP, NAT_KH, NAT_GROUP * w, NAT_KH * w)


def _out_kernel(og_ref, on_ref, x_ref, mod_ref, wg_ref, wn_ref, n2_ref, rw_ref, rb_ref,
                x1_ref, h2_ref, ti_ref, gt_ref, rk_ref, cnt_ref, carry_ref):
    tm = x_ref.shape[0]

    @pl.when(pl.program_id(0) == 0)
    def _():
        carry_ref[...] = jnp.zeros_like(carry_ref)

    mix = _dot(og_ref[...], wg_ref[...]) + _dot(on_ref[...], wn_ref[...])
    x1 = x_ref[...] + mod_ref[0, 2:3, :] * mix
    x1_ref[...] = x1
    ms = jnp.mean(x1 * x1, axis=-1, keepdims=True)
    h2 = x1 * lax.rsqrt(ms + EPS) * n2_ref[...]
    h2 = h2 * (1.0 + mod_ref[0, 4:5, :]) + mod_ref[0, 3:4, :]
    for s in range(SUBLANES):
        h2_ref[pl.ds(s, tm, stride=SUBLANES), :] = h2[:, s * LANES:(s + 1) * LANES]

    logits = _dot_nt(rw_ref[...], h2, precision=HIGHEST) + rb_ref[...]
    eidx = lax.broadcasted_iota(jnp.int32, logits.shape, 0)
    vals = logits
    top_v, top_i = [], []
    for _ in range(TOP_K):
        m = jnp.max(vals, axis=0, keepdims=True)
        idx = jnp.min(jnp.where(vals == m, eidx, N_EXPERTS), axis=0, keepdims=True)
        top_v.append(m)
        top_i.append(idx)
        vals = jnp.where(eidx == idx, -jnp.inf, vals)
    ev = [jnp.exp(v - top_v[0]) for v in top_v]
    den = ev[0] + ev[1] + ev[2] + ev[3]
    ti_ref[...] = jnp.concatenate(top_i, axis=0)
    gt_ref[...] = jnp.concatenate([e / den for e in ev], axis=0)

    onehot = jnp.zeros(logits.shape, F32)
    for idx in top_i:
        onehot = onehot + (eidx == idx).astype(F32)
    r = lax.broadcasted_iota(jnp.int32, (tm, tm), 0)
    c = lax.broadcasted_iota(jnp.int32, (tm, tm), 1)
    earlier = (r < c).astype(BF16)
    before = carry_ref[...] + _dot(onehot.astype(BF16), earlier)
    rk_ref[...] = jnp.concatenate(
        [jnp.sum(jnp.where(eidx == idx, before, 0.0), axis=0, keepdims=True) for idx in top_i], axis=0).astype(jnp.int32)
    carry_ref[...] = carry_ref[...] + jnp.sum(onehot, axis=1, keepdims=True)
    cnt_ref[...] = jnp.broadcast_to(carry_ref[...], cnt_ref.shape)


def _out_proj(og, on, x, mod3, w_g, w_n, norm_g, rw_t, rb, batch_of_tile):
    n, d = x.shape
    tm = TOKEN_TILE
    tok = lambda i: (i, 0)
    const = lambda i: (0, 0)
    return pl.pallas_call(
        _out_kernel,
        grid=(n // tm,),
        in_specs=[pl.BlockSpec((tm, GLA_V), tok),
                  pl.BlockSpec((tm, NAT_W), tok),
                  pl.BlockSpec((tm, d), tok),
                  pl.BlockSpec((1, 6, d), lambda i: (batch_of_tile(i), 0, 0)),
                  pl.BlockSpec(w_g.shape, const),
                  pl.BlockSpec(w_n.shape, const),
                  pl.BlockSpec((1, d), const),
                  pl.BlockSpec(rw_t.shape, const),
                  pl.BlockSpec((N_EXPERTS, 1), const)],
        out_specs=[pl.BlockSpec((tm, d), tok),
                   pl.BlockSpec((tm * SUBLANES, LANES), tok),
                   pl.BlockSpec((TOP_K, tm), lambda i: (0, i)),
                   pl.BlockSpec((TOP_K, tm), lambda i: (0, i)),
                   pl.BlockSpec((TOP_K, tm), lambda i: (0, i)),
                   pl.BlockSpec((N_EXPERTS, LANES), const)],
        out_shape=[jax.ShapeDtypeStruct((n, d), F32),
                   jax.ShapeDtypeStruct((n * SUBLANES, LANES), F32),
                   jax.ShapeDtypeStruct((TOP_K, n), jnp.int32),
                   jax.ShapeDtypeStruct((TOP_K, n), F32),
                   jax.ShapeDtypeStruct((TOP_K, n), jnp.int32),
                   jax.ShapeDtypeStruct((N_EXPERTS, LANES), F32)],
        scratch_shapes=[pltpu.VMEM((N_EXPERTS, 1), F32)],
        compiler_params=_cparams(("arbitrary",)),
    )(og, on, x, mod3, w_g, w_n, norm_g.reshape(1, d), rw_t, rb.reshape(N_EXPERTS, 1))


def _row_copy(src, dst, sem):
    return pltpu.make_async_copy(src, dst, sem)


def _dispatch_kernel(dest_ref, h_ref, xs_hbm, sem):
    tt = dest_ref.shape[1]

    def src(t):
        return h_ref.at[pl.ds(pl.multiple_of(t * SUBLANES, SUBLANES), SUBLANES), :]

    def issue(t, carry):
        for k in range(TOP_K):
            _row_copy(src(t), xs_hbm.at[dest_ref[k, t]], sem).start()
        return carry

    def drain(t, carry):
        for k in range(TOP_K):
            _row_copy(src(t), xs_hbm.at[dest_ref[k, t]], sem).wait()
        return carry

    lax.fori_loop(0, tt, issue, 0)
    lax.fori_loop(0, tt, drain, 0)


def _dispatch(dest_t, h2):
    n = h2.shape[0] // SUBLANES
    tt = GATHER_TILE
    return pl.pallas_call(
        _dispatch_kernel,
        grid=(n // tt,),
        in_specs=[pl.BlockSpec((TOP_K, tt), lambda i: (0, i), memory_space=pltpu.SMEM),
                  pl.BlockSpec((tt * SUBLANES, LANES), lambda i: (i, 0))],
        out_specs=pl.BlockSpec(memory_space=pl.ANY),
        out_shape=jax.ShapeDtypeStruct((n * TOP_K, SUBLANES, LANES), F32),
        scratch_shapes=[pltpu.SemaphoreType.DMA(())],
        compiler_params=_cparams(("arbitrary",)),
    )(dest_t, h2)


def _deinterleave_kernel(w_ref, o_ref):
    f2 = w_ref.shape[2]
    slab = 2 * LANES
    r = lax.broadcasted_iota(jnp.int32, (slab, slab), 0)
    c = lax.broadcasted_iota(jnp.int32, (slab, slab), 1)
    perm = (r == jnp.where(c < LANES, 2 * c, 2 * (c - LANES) + 1)).astype(BF16)
    for j in range(f2 // slab):
        y = _dot(w_ref[0, :, j * slab:(j + 1) * slab].astype(BF16), perm)
        o_ref[0, :, j * LANES:(j + 1) * LANES] = y[:, :LANES].astype(BF16)
        o_ref[0, :, f2 // 2 + j * LANES:f2 // 2 + (j + 1) * LANES] = y[:, LANES:].astype(BF16)


def _deinterleave(w):
    e, d, f2 = w.shape
    return pl.pallas_call(
        _deinterleave_kernel,
        grid=(e,),
        in_specs=[pl.BlockSpec((1, d, f2), lambda i: (i, 0, 0))],
        out_specs=pl.BlockSpec((1, d, f2), lambda i: (i, 0, 0)),
        out_shape=jax.ShapeDtypeStruct((e, d, f2), BF16),
        compiler_params=_cparams(("parallel",)),
    )(w)


def _expert_kernel(blk_ref, exp_ref, lo_ref, hi_ref, xs_ref, wgu_ref, bgu_ref, wd_ref, bd_ref, ys_ref):
    del blk_ref, exp_ref
    rows = xs_ref.shape[0] // SUBLANES
    f = wd_ref.shape[1]
    i = pl.program_id(0)
    lo = lo_ref[i]
    hi = hi_ref[i]

    @pl.when(hi > lo)
    def _():
        x = jnp.concatenate([xs_ref[pl.ds(s, rows, stride=SUBLANES), :] for s in range(SUBLANES)], axis=1).astype(BF16)
        gu = _dot(x, wgu_ref[0]) + bgu_ref[0]
        gate = jnp.minimum(gu[:, :f], SWIGLU_LIMIT)
        up = jnp.clip(gu[:, f:], -SWIGLU_LIMIT, SWIGLU_LIMIT)
        act = (up + 1.0) * gate * jax.nn.sigmoid(SWIGLU_ALPHA * gate)
        y = _dot(act.astype(BF16), wd_ref[0]) + bd_ref[0]
        r = lax.broadcasted_iota(jnp.int32, (rows, LANES), 0)
        mine = (r >= lo) & (r < hi)

        @pl.when(lo == 0)
        def _():
            for s in range(SUBLANES):
                ys_ref[pl.ds(s, rows, stride=SUBLANES), :] = jnp.where(mine, y[:, s * LANES:(s + 1) * LANES], 0.0)

        @pl.when(lo > 0)
        def _():
            for s in range(SUBLANES):
                sl = pl.ds(s, rows, stride=SUBLANES)
                ys_ref[sl, :] = jnp.where(mine, y[:, s * LANES:(s + 1) * LANES], ys_ref[sl, :])


def _experts(items, xs2, w_gu, b_gu, w_down, b_down):
    item_blk, item_exp, item_lo, item_hi = items
    rows8 = MOE_ROWS * SUBLANES
    d, f2 = w_gu.shape[1], w_gu.shape[2]
    wmap = lambda i, blk, exp, lo, hi: (exp[i], 0, 0)
    xmap = lambda i, blk, exp, lo, hi: (blk[i], 0)
    grid_spec = pltpu.PrefetchScalarGridSpec(
        num_scalar_prefetch=4,
        grid=(item_blk.shape[0],),
        in_specs=[pl.BlockSpec((rows8, LANES), xmap),
                  pl.BlockSpec((1, d, f2), wmap),
                  pl.BlockSpec((1, 1, f2), wmap),
                  pl.BlockSpec((1, f2 // 2, d), wmap),
                  pl.BlockSpec((1, 1, d), wmap)],
        out_specs=pl.BlockSpec((rows8, LANES), xmap),
    )
    return pl.pallas_call(
        _expert_kernel,
        grid_spec=grid_spec,
        out_shape=jax.ShapeDtypeStruct(xs2.shape, F32),
        compiler_params=_cparams(("arbitrary",)),
    )(item_blk, item_exp, item_lo, item_hi, xs2, w_gu, b_gu, w_down, b_down)


def _expert_work_items(counts, n_rows):
    ends = jnp.cumsum(counts)
    starts = ends - counts
    n_blk = n_rows // MOE_ROWS
    cuts = jnp.sort(jnp.concatenate([jnp.arange(n_blk, dtype=jnp.int32) * MOE_ROWS, starts[1:]]))
    nxt = jnp.concatenate([cuts[1:], jnp.full((1,), n_rows, jnp.int32)])
    blk = jnp.minimum(cuts // MOE_ROWS, n_blk - 1)
    exp = jnp.minimum(jnp.searchsorted(ends, cuts, side='right'), N_EXPERTS - 1).astype(jnp.int32)
    lo = cuts - blk * MOE_ROWS
    hi = nxt - blk * MOE_ROWS
    return (blk.astype(jnp.int32), exp, lo.astype(jnp.int32), hi.astype(jnp.int32)), starts


def _combine_kernel(dest_ref, gates_ref, x1_ref, mod_ref, fg_ref, ys_hbm, o_ref, rows_ref, sem):
    tt = x1_ref.shape[0]

    def slot(k, t):
        return rows_ref.at[pl.ds(pl.multiple_of((k * tt + t) * SUBLANES, SUBLANES), SUBLANES), :]

    def issue(t, carry):
        for k in range(TOP_K):
            _row_copy(ys_hbm.at[dest_ref[k, t]], slot(k, t), sem).start()
        return carry

    def drain(t, carry):
        for k in range(TOP_K):
            _row_copy(ys_hbm.at[dest_ref[k, t]], slot(k, t), sem).wait()
        return carry

    lax.fori_loop(0, tt, issue, 0)
    lax.fori_loop(0, tt, drain, 0)

    gates = gates_ref[...]
    y = jnp.zeros(x1_ref.shape, F32)
    for k in range(TOP_K):
        yk = jnp.concatenate(
            [rows_ref[pl.ds(k * tt * SUBLANES + s, tt, stride=SUBLANES), :] for s in range(SUBLANES)], axis=1)
        y = y + gates[:, k:k + 1] * yk
    x2 = x1_ref[...] + mod_ref[0, 5:6, :] * y
    ms = jnp.mean(x2 * x2, axis=-1, keepdims=True)
    o_ref[...] = x2 * lax.rsqrt(ms + EPS) * fg_ref[...]


def _combine(dest_t, gates, x1, mod3, final_g, ys3, batch_of_tile):
    n, d = x1.shape
    tt = GATHER_TILE
    return pl.pallas_call(
        _combine_kernel,
        grid=(n // tt,),
        in_specs=[pl.BlockSpec((TOP_K, tt), lambda i: (0, i), memory_space=pltpu.SMEM),
                  pl.BlockSpec((tt, TOP_K), lambda i: (i, 0)),
                  pl.BlockSpec((tt, d), lambda i: (i, 0)),
                  pl.BlockSpec((1, 6, d), lambda i: (batch_of_tile(i), 0, 0)),
                  pl.BlockSpec((1, d), lambda i: (0, 0)),
                  pl.BlockSpec(memory_space=pl.ANY)],
        out_specs=pl.BlockSpec((tt, d), lambda i: (i, 0)),
        out_shape=jax.ShapeDtypeStruct((n, d), F32),
        scratch_shapes=[pltpu.VMEM((TOP_K * tt * SUBLANES, LANES), F32),
                        pltpu.SemaphoreType.DMA(())],
        compiler_params=_cparams(("arbitrary",)),
    )(dest_t, gates, x1, mod3, final_g.reshape(1, d), ys3)


def _batch_of_tile_fn(tile, bp, tp, ts):
    n_prompt = bp * tp

    def batch_of_tile(i):
        t = i * tile
        return jnp.where(t < n_prompt, t // tp, bp + (t - n_prompt) // ts)

    return batch_of_tile


def kernel(x_prompt, x_sample, c_prompt, c_sample, norm1_g, w_ada, b_ada, w_in, gla_w2_fwd, gla_b2_fwd, gla_w2_bwd, gla_b2_bwd, gla_norm_g, nat_rpb, w_out, norm2_g, router_w, router_b, w_gate_up, b_gate_up, w_down, b_down, final_norm_g):
    assert w_ada.shape[0] == 1, "single-layer encoder"
    bp, tp, d = x_prompt.shape
    bs, ts, _ = x_sample.shape
    n_prompt, n_sample = bp * tp, bs * ts
    n = n_prompt + n_sample
    assert tp % TOKEN_TILE == 0 and ts % TOKEN_TILE == 0 and n_prompt % ts == 0
    assert d == SUBLANES * LANES

    x = jnp.concatenate([x_prompt.reshape(n_prompt, d), x_sample.reshape(n_sample, d)], axis=0)
    c = jnp.concatenate([c_prompt, c_sample], axis=0)

    sizes = (GLA_QK, GLA_QK, GLA_V, GLA_V, GLA_RANK, GLA_RANK, NAT_W, NAT_W, NAT_W)
    offs = np.concatenate([[0], np.cumsum(sizes)])
    w_in0 = w_in[0]
    seg = lambda j: w_in0[:, offs[j]:offs[j + 1]]
    w_all = jnp.concatenate(
        [seg(0), seg(1), seg(2), seg(3), seg(6), seg(7), seg(8), seg(4), seg(5),
         jnp.zeros((d, LANES - 2 * GLA_RANK), F32)], axis=1).astype(BF16)
    w2f = jnp.zeros((LANES, GLA_QK), F32).at[0:GLA_RANK].set(gla_w2_fwd[0])
    w2b = jnp.zeros((LANES, GLA_QK), F32).at[GLA_RANK:2 * GLA_RANK].set(gla_w2_bwd[0])
    tab = _nat_bias_table(nat_rpb[0])
    w_og = w_out[0, :GLA_V].astype(BF16)
    w_on = w_out[0, GLA_V:].astype(BF16)
    rw_t = router_w[0].T
    w_gu = _deinterleave(w_gate_up[0])
    b_gu = jnp.concatenate([b_gate_up[0, :, None, 0::2], b_gate_up[0, :, None, 1::2]], axis=-1)
    w_dn = w_down[0].astype(BF16)
    b_dn = b_down[0, :, None, :]

    mod3 = _ada(c, w_ada[0], b_ada[0]).reshape(bp + bs, 6, d)
    bot_tok = _batch_of_tile_fn(TOKEN_TILE, bp, tp, ts)
    bot_gat = _batch_of_tile_fn(GATHER_TILE, bp, tp, ts)

    pg, pn, plr = _in_proj(x, mod3, norm1_g[0], w_all, bot_tok)

    gla_args = (pg, plr, w2f, gla_b2_fwd[0].reshape(1, -1), w2b, gla_b2_bwd[0].reshape(1, -1),
                gla_norm_g[0].reshape(1, -1))
    og = jnp.concatenate([_gla(*gla_args, seq=tp, batches=bp, block0=0),
                          _gla(*gla_args, seq=ts, batches=bs, block0=n_prompt // ts)], axis=0)
    on = jnp.concatenate([_nat(pn, tab, seq=tp, batches=bp, block0=0),
                          _nat(pn, tab, seq=ts, batches=bs, block0=n_prompt // ts)], axis=0)

    x1, h2, top_i, gates_t, rank_t, cnt = _out_proj(og, on, x, mod3, w_og, w_on, norm2_g[0], rw_t, router_b[0], bot_tok)

    n_rows = n * TOP_K
    assert n_rows % MOE_ROWS == 0
    counts = cnt[:, 0].astype(jnp.int32)
    items, starts = _expert_work_items(counts, n_rows)
    chosen = top_i[None] == jnp.arange(N_EXPERTS, dtype=jnp.int32)[:, None, None]
    dest_t = rank_t + jnp.sum(jnp.where(chosen, starts[:, None, None], 0), axis=0)

    xs = _dispatch(dest_t, h2)
    ys = _experts(items, xs.reshape(n_rows * SUBLANES, LANES), w_gu, b_gu, w_dn, b_dn)
    y = _combine(dest_t, gates_t.T, x1, mod3, final_norm_g, ys.reshape(n_rows, SUBLANES, LANES), bot_gat)

    return (y[:n_prompt].reshape(bp, tp, d), y[n_prompt:].reshape(bs, ts, d))
```

```python
import functools

import numpy as np
import jax
import jax.numpy as jnp
from jax import lax
from jax.experimental import pallas as pl
from jax.experimental.pallas import tpu as pltpu

F32 = jnp.float32
BF16 = jnp.bfloat16
HIGHEST = lax.Precision.HIGHEST

EPS = 1e-5
GRID_W = 64
GLA_HEADS = 4
GLA_DK = 64
GLA_DV = 128
GLA_RANK = 16
GLA_TAU = 16.0
GLA_CHUNK = 64
GLA_QK = GLA_HEADS * GLA_DK
GLA_V = GLA_HEADS * GLA_DV
NAT_HEADS = 16
NAT_HD = 32
NAT_W = NAT_HEADS * NAT_HD
NAT_KH = 8
NAT_KW = 16
NAT_GROUP = 4
N_EXPERTS = 32
TOP_K = 4
SWIGLU_LIMIT = 7.0
SWIGLU_ALPHA = 1.702

LANES = 128
SUBLANES = 8
TOKEN_TILE = 512
MOE_ROWS = 512
GATHER_TILE = 256
VMEM_LIMIT = 56 * 1024 * 1024


def _cparams(sem, vmem=VMEM_LIMIT):
    return pltpu.CompilerParams(dimension_semantics=sem, vmem_limit_bytes=vmem)


def _dot(a, b):
    return jnp.dot(a, b, preferred_element_type=F32)


def _dot_nt(a, b, precision=None):
    return lax.dot_general(a, b, (((1,), (1,)), ((), ())), preferred_element_type=F32, precision=precision)


def _dot_tn(a, b):
    return lax.dot_general(a, b, (((0,), (0,)), ((), ())), preferred_element_type=F32)


def _ada_kernel(c_ref, w_ref, b_ref, o_ref):
    c = c_ref[...]
    s = c * jax.nn.sigmoid(c)
    o_ref[...] = jnp.dot(s, w_ref[...], preferred_element_type=F32, precision=HIGHEST) + b_ref[...]


def _ada(c, w, b):
    nb, d = c.shape
    cols = w.shape[1]
    blk = 1024
    return pl.pallas_call(
        _ada_kernel,
        grid=(cols // blk,),
        in_specs=[pl.BlockSpec((nb, d), lambda j: (0, 0)),
                  pl.BlockSpec((d, blk), lambda j: (0, j)),
                  pl.BlockSpec((1, blk), lambda j: (0, j))],
        out_specs=pl.BlockSpec((nb, blk), lambda j: (0, j)),
        out_shape=jax.ShapeDtypeStruct((nb, cols), F32),
        compiler_params=_cparams(("arbitrary",)),
    )(c, w, b.reshape(1, cols))


def _stream_specs(block, n_prompt_tiles):
    return (pl.BlockSpec(block, lambda i: (jnp.minimum(i, n_prompt_tiles - 1), 0)),
            pl.BlockSpec(block, lambda i: (jnp.maximum(i - n_prompt_tiles, 0), 0)))


def _per_stream(n_prompt_tiles, body):
    i = pl.program_id(0)
    pl.when(i < n_prompt_tiles)(functools.partial(body, 0))
    pl.when(i >= n_prompt_tiles)(functools.partial(body, 1))


def _in_kernel(xp_ref, xs_ref, mod_ref, g_ref, w_ref, pg_ref, pn_ref, plr_ref, *, n_prompt_tiles):
    def body(stream):
        x = (xp_ref, xs_ref)[stream][...]
        ms = jnp.mean(x * x, axis=-1, keepdims=True)
        y = x * lax.rsqrt(ms + EPS) * g_ref[...]
        h = y * (1.0 + mod_ref[0, 1:2, :]) + mod_ref[0, 0:1, :]
        hb = h.astype(BF16)
        wg = GLA_QK * 2 + GLA_V * 2
        wn = 3 * NAT_W
        pg_ref[...] = _dot(hb, w_ref[:, 0:wg]).astype(BF16)
        pn_ref[...] = _dot(hb, w_ref[:, wg:wg + wn]).astype(BF16)
        plr_ref[...] = _dot(hb, w_ref[:, wg + wn:wg + wn + LANES])

    _per_stream(n_prompt_tiles, body)


def _in_proj(xp, xs, mod3, norm_g, w_all, batch_of_tile):
    d = xp.shape[1]
    n = xp.shape[0] + xs.shape[0]
    wg = GLA_QK * 2 + GLA_V * 2
    wn = 3 * NAT_W
    tm = TOKEN_TILE
    n_prompt_tiles = xp.shape[0] // tm
    return pl.pallas_call(
        functools.partial(_in_kernel, n_prompt_tiles=n_prompt_tiles),
        grid=(n // tm,),
        in_specs=[*_stream_specs((tm, d), n_prompt_tiles),
                  pl.BlockSpec((1, 6, d), lambda i: (batch_of_tile(i), 0, 0)),
                  pl.BlockSpec((1, d), lambda i: (0, 0)),
                  pl.BlockSpec(w_all.shape, lambda i: (0, 0))],
        out_specs=[pl.BlockSpec((tm, wg), lambda i: (i, 0)),
                   pl.BlockSpec((tm, wn), lambda i: (i, 0)),
                   pl.BlockSpec((tm, LANES), lambda i: (i, 0))],
        out_shape=[jax.ShapeDtypeStruct((n, wg), BF16),
                   jax.ShapeDtypeStruct((n, wn), BF16),
                   jax.ShapeDtypeStruct((n, LANES), F32)],
        compiler_params=_cparams(("arbitrary",)),
    )(xp, xs, mod3, norm_g.reshape(1, d), w_all)


def _gla_kernel(q_ref, k_ref, v_ref, g_ref, lr_ref, w2f_ref, b2f_ref, w2b_ref, b2b_ref, ng_ref,
                o_ref, acc_ref, sf_ref, sb_ref, *, seq):
    c_len = GLA_CHUNK
    n = seq // c_len
    row = lax.broadcasted_iota(jnp.int32, (c_len, c_len), 0)
    col = lax.broadcasted_iota(jnp.int32, (c_len, c_len), 1)
    tri_f = col <= row
    tri_b = col >= row
    cum_f = tri_f.astype(F32)
    cum_b = tri_b.astype(F32)
    mask_f = jnp.concatenate([tri_f, tri_f], axis=0)
    mask_b = jnp.concatenate([tri_b, tri_b], axis=0)
    head_a = lax.broadcasted_iota(jnp.int32, (c_len, LANES), 1) < GLA_DK
    st_row = lax.broadcasted_iota(jnp.int32, (2 * GLA_DV, LANES), 0)
    st_col = lax.broadcasted_iota(jnp.int32, (2 * GLA_DV, LANES), 1)
    blockdiag = (st_row < GLA_DV) == (st_col < GLA_DK)

    sf_ref[...] = jnp.zeros_like(sf_ref)
    sb_ref[...] = jnp.zeros_like(sb_ref)

    def direction(c, s_ref, w2_ref, b2_ref, cum, mask):
        rows = pl.ds(pl.multiple_of(c * c_len, c_len), c_len)
        q = q_ref[rows, :].astype(F32) * (GLA_DK ** -0.5)
        k = k_ref[rows, :].astype(F32)
        vb = v_ref[rows, :]
        z = jnp.dot(lr_ref[rows, :], w2_ref[...], preferred_element_type=F32, precision=HIGHEST) + b2_ref[...]
        la = (jnp.minimum(z, 0.0) - jnp.log(1.0 + jnp.exp(-jnp.abs(z)))) * (1.0 / GLA_TAU)
        b = jnp.dot(cum, la, preferred_element_type=F32, precision=HIGHEST)
        btot = jnp.sum(la, axis=0, keepdims=True)
        q_t = q * jnp.exp(b)
        k_t = (k * jnp.exp(-b)).astype(BF16)
        k_end = (k * jnp.exp(btot - b)).astype(BF16)
        zero = jnp.zeros_like(q_t)
        q_stack = jnp.concatenate([jnp.where(head_a, q_t, zero), jnp.where(head_a, zero, q_t)], axis=0).astype(BF16)
        a = jnp.where(mask, _dot_nt(q_stack, k_t), 0.0)
        oi = _dot(a.astype(BF16), vb)
        o_intra = jnp.concatenate([oi[0:c_len, 0:GLA_DV], oi[c_len:, GLA_DV:]], axis=1)
        st = s_ref[...]
        o_inter = _dot_nt(q_t.astype(BF16), st.astype(BF16))
        upd = _dot_tn(vb, k_end)
        s_ref[...] = st * jnp.exp(btot) + jnp.where(blockdiag, upd, 0.0)
        return rows, o_intra + o_inter

    def finalize(rows, o):
        halves = []
        for hh in range(2):
            oh = o[:, hh * GLA_DV:(hh + 1) * GLA_DV]
            ms = jnp.mean(oh * oh, axis=-1, keepdims=True)
            halves.append(oh * lax.rsqrt(ms + EPS) * ng_ref[...])
        y = jnp.concatenate(halves, axis=1)
        g = g_ref[rows, :].astype(F32)
        o_ref[rows, :] = (y * (g * jax.nn.sigmoid(g))).astype(o_ref.dtype)

    def first_half(c, carry):
        rows, o = direction(c, sf_ref, w2f_ref, b2f_ref, cum_f, mask_f)
        acc_ref[rows, :] = o
        rows, o = direction(n - 1 - c, sb_ref, w2b_ref, b2b_ref, cum_b, mask_b)
        acc_ref[rows, :] = o
        return carry

    def second_half(c, carry):
        rows, o = direction(c, sf_ref, w2f_ref, b2f_ref, cum_f, mask_f)
        finalize(rows, acc_ref[rows, :] + o)
        rows, o = direction(n - 1 - c, sb_ref, w2b_ref, b2b_ref, cum_b, mask_b)
        finalize(rows, acc_ref[rows, :] + o)
        return carry

    lax.fori_loop(0, n // 2, first_half, 0, unroll=2)
    lax.fori_loop(n // 2, n, second_half, 0, unroll=2)


def _gla(pg, plr, w2f, b2f, w2b, b2b, norm_g, *, seq, batches, block0):
    pair_w = 2 * GLA_DK
    pair_v = 2 * GLA_DV
    k0 = GLA_QK // pair_w
    v0 = 2 * GLA_QK // pair_v
    g0 = (2 * GLA_QK + GLA_V) // pair_v
    in_specs = [pl.BlockSpec((seq, pair_w), lambda b, p: (block0 + b, p)),
                pl.BlockSpec((seq, pair_w), lambda b, p: (block0 + b, k0 + p)),
                pl.BlockSpec((seq, pair_v), lambda b, p: (block0 + b, v0 + p)),
                pl.BlockSpec((seq, pair_v), lambda b, p: (block0 + b, g0 + p)),
                pl.BlockSpec((seq, LANES), lambda b, p: (block0 + b, 0)),
                pl.BlockSpec((LANES, pair_w), lambda b, p: (0, p)),
                pl.BlockSpec((1, pair_w), lambda b, p: (0, p)),
                pl.BlockSpec((LANES, pair_w), lambda b, p: (0, p)),
                pl.BlockSpec((1, pair_w), lambda b, p: (0, p)),
                pl.BlockSpec((1, GLA_DV), lambda b, p: (0, 0))]
    return pl.pallas_call(
        functools.partial(_gla_kernel, seq=seq),
        grid=(batches, GLA_HEADS // 2),
        in_specs=in_specs,
        out_specs=pl.BlockSpec((seq, pair_v), lambda b, p: (b, p)),
        out_shape=jax.ShapeDtypeStruct((batches * seq, GLA_V), BF16),
        scratch_shapes=[pltpu.VMEM((seq, pair_v), F32),
                        pltpu.VMEM((pair_v, pair_w), F32),
                        pltpu.VMEM((pair_v, pair_w), F32)],
        compiler_params=_cparams(("parallel", "parallel")),
    )(pg, pg, pg, pg, plr, w2f, b2f, w2b, b2b, norm_g)


def _nat_kernel(q_ref, k_ref, v_ref, tab_ref, o_ref, *, rows):
    w = GRID_W
    head = lax.broadcasted_iota(jnp.int32, (w, LANES), 1) // NAT_HD
    win = NAT_KH * w

    def body(i, carry):
        r0 = jnp.clip(i - NAT_KH // 2, 0, rows - NAT_KH)
        base = r0 - i + NAT_KH - 1
        qrows = pl.ds(pl.multiple_of(i * w, w), w)
        krows = pl.ds(pl.multiple_of(r0 * w, w), win)
        q = q_ref[qrows, :]
        zero = jnp.zeros_like(q)
        q_stack = jnp.concatenate([jnp.where(head == h, q, zero) for h in range(NAT_GROUP)], axis=0)
        s = _dot_nt(q_stack, k_ref[krows, :]) * (NAT_HD ** -0.5) + tab_ref[0, base]
        m = jnp.max(s, axis=-1, keepdims=True)
        e = jnp.exp(s - m)
        l = jnp.sum(e, axis=-1, keepdims=True)
        o = _dot(e.astype(BF16), v_ref[krows, :]) / l
        out = jnp.zeros((w, LANES), F32)
        for h in range(NAT_GROUP):
            out = out + jnp.where(head == h, o[h * w:(h + 1) * w, :], 0.0)
        o_ref[qrows, :] = out.astype(o_ref.dtype)
        return carry

    lax.fori_loop(0, rows, body, 0, unroll=2)


def _nat(pn, tab, *, seq, batches, block0):
    rows = seq // GRID_W
    groups = NAT_HEADS // NAT_GROUP
    in_specs = [pl.BlockSpec((seq, LANES), lambda g, b: (block0 + b, g)),
                pl.BlockSpec((seq, LANES), lambda g, b: (block0 + b, groups + g)),
                pl.BlockSpec((seq, LANES), lambda g, b: (block0 + b, 2 * groups + g)),
                pl.BlockSpec((1,) + tab.shape[1:], lambda g, b: (g, 0, 0, 0))]
    return pl.pallas_call(
        functools.partial(_nat_kernel, rows=rows),
        grid=(groups, batches),
        in_specs=in_specs,
        out_specs=pl.BlockSpec((seq, LANES), lambda g, b: (b, g)),
        out_shape=jax.ShapeDtypeStruct((batches * seq, NAT_W), BF16),
        compiler_params=_cparams(("parallel", "parallel")),
    )(pn, pn, pn, tab)


def _nat_bias_table(rpb):
    w = GRID_W
    jq = np.arange(w)[:, None]
    jk = np.arange(w)[None, :]
    c0 = np.clip(jq - NAT_KW // 2, 0, w - NAT_KW)
    valid = (jk >= c0) & (jk < c0 + NAT_KW)
    dcol = np.clip(jk - jq + NAT_KW - 1, 0, 2 * NAT_KW - 2)
    drow = np.arange(NAT_KH)[:, None] + np.arange(NAT_KH)[None, :]
    t = rpb.astype(F32)[:, drow][:, :, :, dcol]
    t = jnp.where(valid[None, None, None], t, -jnp.inf)
    t = t.transpose(0, 1, 3, 2, 4).reshape(NAT_HEADS, NAT_KH, w, NAT_KH * w)
    t = t.reshape(NAT_HEADS // NAT_GROUP, NAT_GROUP, NAT_KH, w, NAT_KH * w).transpose(0, 2, 1, 3, 4)
    return t.reshape(NAT_HEADS // NAT_GROUP, NAT_KH, NAT_GROUP * w, NAT_KH * w)


def _out_kernel(ogp_ref, ogs_ref, onp_ref, ons_ref, xp_ref, xs_ref, mod_ref, wg_ref, wn_ref, n2_ref, rw_ref, rb_ref,
                x1_ref, h2_ref, ti_ref, gt_ref, rk_ref, cnt_ref, carry_ref, *, n_prompt_tiles):
    tm = x1_ref.shape[0]

    @pl.when(pl.program_id(0) == 0)
    def _():
        carry_ref[...] = jnp.zeros_like(carry_ref)

    def residual(stream):
        og_ref, on_ref, x_ref = ((ogp_ref, onp_ref, xp_ref), (ogs_ref, ons_ref, xs_ref))[stream]
        mix = _dot(og_ref[...], wg_ref[...]) + _dot(on_ref[...], wn_ref[...])
        x1_ref[...] = x_ref[...] + mod_ref[0, 2:3, :] * mix

    _per_stream(n_prompt_tiles, residual)
    x1 = x1_ref[...]
    ms = jnp.mean(x1 * x1, axis=-1, keepdims=True)
    h2 = x1 * lax.rsqrt(ms + EPS) * n2_ref[...]
    h2 = h2 * (1.0 + mod_ref[0, 4:5, :]) + mod_ref[0, 3:4, :]
    for s in range(SUBLANES):
        h2_ref[pl.ds(s, tm, stride=SUBLANES), :] = h2[:, s * LANES:(s + 1) * LANES]

    logits = _dot_nt(rw_ref[...], h2, precision=HIGHEST) + rb_ref[...]
    eidx = lax.broadcasted_iota(jnp.int32, logits.shape, 0)
    vals = logits
    top_v, top_i = [], []
    for _ in range(TOP_K):
        m = jnp.max(vals, axis=0, keepdims=True)
        idx = jnp.min(jnp.where(vals == m, eidx, N_EXPERTS), axis=0, keepdims=True)
        top_v.append(m)
        top_i.append(idx)
        vals = jnp.where(eidx == idx, -jnp.inf, vals)
    ev = [jnp.exp(v - top_v[0]) for v in top_v]
    den = ev[0] + ev[1] + ev[2] + ev[3]
    ti_ref[...] = jnp.concatenate(top_i, axis=0)
    gt_ref[...] = jnp.concatenate([e / den for e in ev], axis=0)

    onehot = jnp.zeros(logits.shape, F32)
    for idx in top_i:
        onehot = onehot + (eidx == idx).astype(F32)
    r = lax.broadcasted_iota(jnp.int32, (tm, tm), 0)
    c = lax.broadcasted_iota(jnp.int32, (tm, tm), 1)
    earlier = (r < c).astype(BF16)
    before = carry_ref[...] + _dot(onehot.astype(BF16), earlier)
    rk_ref[...] = jnp.concatenate(
        [jnp.sum(jnp.where(eidx == idx, before, 0.0), axis=0, keepdims=True) for idx in top_i], axis=0).astype(jnp.int32)
    carry_ref[...] = carry_ref[...] + jnp.sum(onehot, axis=1, keepdims=True)
    cnt_ref[...] = jnp.broadcast_to(carry_ref[...], cnt_ref.shape)


def _out_proj(og, on, x, mod3, w_g, w_n, norm_g, rw_t, rb, batch_of_tile):
    d = x[0].shape[1]
    n = x[0].shape[0] + x[1].shape[0]
    tm = TOKEN_TILE
    n_prompt_tiles = x[0].shape[0] // tm
    tok = lambda i: (i, 0)
    const = lambda i: (0, 0)
    return pl.pallas_call(
        functools.partial(_out_kernel, n_prompt_tiles=n_prompt_tiles),
        grid=(n // tm,),
        in_specs=[*_stream_specs((tm, GLA_V), n_prompt_tiles),
                  *_stream_specs((tm, NAT_W), n_prompt_tiles),
                  *_stream_specs((tm, d), n_prompt_tiles),
                  pl.BlockSpec((1, 6, d), lambda i: (batch_of_tile(i), 0, 0)),
                  pl.BlockSpec(w_g.shape, const),
                  pl.BlockSpec(w_n.shape, const),
                  pl.BlockSpec((1, d), const),
                  pl.BlockSpec(rw_t.shape, const),
                  pl.BlockSpec((N_EXPERTS, 1), const)],
        out_specs=[pl.BlockSpec((tm, d), tok),
                   pl.BlockSpec((tm * SUBLANES, LANES), tok),
                   pl.BlockSpec((TOP_K, tm), lambda i: (0, i)),
                   pl.BlockSpec((TOP_K, tm), lambda i: (0, i)),
                   pl.BlockSpec((TOP_K, tm), lambda i: (0, i)),
                   pl.BlockSpec((N_EXPERTS, LANES), const)],
        out_shape=[jax.ShapeDtypeStruct((n, d), F32),
                   jax.ShapeDtypeStruct((n * SUBLANES, LANES), F32),
                   jax.ShapeDtypeStruct((TOP_K, n), jnp.int32),
                   jax.ShapeDtypeStruct((TOP_K, n), F32),
                   jax.ShapeDtypeStruct((TOP_K, n), jnp.int32),
                   jax.ShapeDtypeStruct((N_EXPERTS, LANES), F32)],
        scratch_shapes=[pltpu.VMEM((N_EXPERTS, 1), F32)],
        compiler_params=_cparams(("arbitrary",)),
    )(*og, *on, *x, mod3, w_g, w_n, norm_g.reshape(1, d), rw_t, rb.reshape(N_EXPERTS, 1))


def _row_copy(src, dst, sem):
    return pltpu.make_async_copy(src, dst, sem)


def _dispatch_kernel(dest_ref, h_ref, xs_hbm, sem):
    tt = dest_ref.shape[1]

    def src(t):
        return h_ref.at[pl.ds(pl.multiple_of(t * SUBLANES, SUBLANES), SUBLANES), :]

    def issue(t, carry):
        for k in range(TOP_K):
            _row_copy(src(t), xs_hbm.at[dest_ref[k, t]], sem).start()
        return carry

    def drain(t, carry):
        for k in range(TOP_K):
            _row_copy(src(t), xs_hbm.at[dest_ref[k, t]], sem).wait()
        return carry

    lax.fori_loop(0, tt, issue, 0)
    lax.fori_loop(0, tt, drain, 0)


def _dispatch(dest_t, h2):
    n = h2.shape[0] // SUBLANES
    tt = GATHER_TILE
    return pl.pallas_call(
        _dispatch_kernel,
        grid=(n // tt,),
        in_specs=[pl.BlockSpec((TOP_K, tt), lambda i: (0, i), memory_space=pltpu.SMEM),
                  pl.BlockSpec((tt * SUBLANES, LANES), lambda i: (i, 0))],
        out_specs=pl.BlockSpec(memory_space=pl.ANY),
        out_shape=jax.ShapeDtypeStruct((n * TOP_K, SUBLANES, LANES), F32),
        scratch_shapes=[pltpu.SemaphoreType.DMA(())],
        compiler_params=_cparams(("arbitrary",)),
    )(dest_t, h2)


def _deinterleave_kernel(w_ref, o_ref):
    f2 = w_ref.shape[2]
    slab = 2 * LANES
    r = lax.broadcasted_iota(jnp.int32, (slab, slab), 0)
    c = lax.broadcasted_iota(jnp.int32, (slab, slab), 1)
    perm = (r == jnp.where(c < LANES, 2 * c, 2 * (c - LANES) + 1)).astype(BF16)
    for j in range(f2 // slab):
        y = _dot(w_ref[0, :, j * slab:(j + 1) * slab].astype(BF16), perm)
        o_ref[0, :, j * LANES:(j + 1) * LANES] = y[:, :LANES].astype(BF16)
        o_ref[0, :, f2 // 2 + j * LANES:f2 // 2 + (j + 1) * LANES] = y[:, LANES:].astype(BF16)


def _deinterleave(w):
    e, d, f2 = w.shape
    return pl.pallas_call(
        _deinterleave_kernel,
        grid=(e,),
        in_specs=[pl.BlockSpec((1, d, f2), lambda i: (i, 0, 0))],
        out_specs=pl.BlockSpec((1, d, f2), lambda i: (i, 0, 0)),
        out_shape=jax.ShapeDtypeStruct((e, d, f2), BF16),
        compiler_params=_cparams(("parallel",)),
    )(w)


def _expert_kernel(blk_ref, exp_ref, lo_ref, hi_ref, xs_ref, wgu_ref, bgu_ref, wd_ref, bd_ref, ys_ref):
    del blk_ref, exp_ref
    rows = xs_ref.shape[0] // SUBLANES
    f = wd_ref.shape[1]
    i = pl.program_id(0)
    lo = lo_ref[i]
    hi = hi_ref[i]

    @pl.when(hi > lo)
    def _():
        x = jnp.concatenate([xs_ref[pl.ds(s, rows, stride=SUBLANES), :] for s in range(SUBLANES)], axis=1).astype(BF16)
        gu = _dot(x, wgu_ref[0]) + bgu_ref[0]
        gate = jnp.minimum(gu[:, :f], SWIGLU_LIMIT)
        up = jnp.clip(gu[:, f:], -SWIGLU_LIMIT, SWIGLU_LIMIT)
        act = (up + 1.0) * gate * jax.nn.sigmoid(SWIGLU_ALPHA * gate)
        y = _dot(act.astype(BF16), wd_ref[0]) + bd_ref[0]
        r = lax.broadcasted_iota(jnp.int32, (rows, LANES), 0)
        mine = (r >= lo) & (r < hi)

        @pl.when(lo == 0)
        def _():
            for s in range(SUBLANES):
                ys_ref[pl.ds(s, rows, stride=SUBLANES), :] = jnp.where(mine, y[:, s * LANES:(s + 1) * LANES], 0.0)

        @pl.when(lo > 0)
        def _():
            for s in range(SUBLANES):
                sl = pl.ds(s, rows, stride=SUBLANES)
                ys_ref[sl, :] = jnp.where(mine, y[:, s * LANES:(s + 1) * LANES], ys_ref[sl, :])


def _experts(items, xs2, w_gu, b_gu, w_down, b_down):
    item_blk, item_exp, item_lo, item_hi = items
    rows8 = MOE_ROWS * SUBLANES
    d, f2 = w_gu.shape[1], w_gu.shape[2]
    wmap = lambda i, blk, exp, lo, hi: (exp[i], 0, 0)
    xmap = lambda i, blk, exp, lo, hi: (blk[i], 0)
    grid_spec = pltpu.PrefetchScalarGridSpec(
        num_scalar_prefetch=4,
        grid=(item_blk.shape[0],),
        in_specs=[pl.BlockSpec((rows8, LANES), xmap),
                  pl.BlockSpec((1, d, f2), wmap),
                  pl.BlockSpec((1, 1, f2), wmap),
                  pl.BlockSpec((1, f2 // 2, d), wmap),
                  pl.BlockSpec((1, 1, d), wmap)],
        out_specs=pl.BlockSpec((rows8, LANES), xmap),
    )
    return pl.pallas_call(
        _expert_kernel,
        grid_spec=grid_spec,
        out_shape=jax.ShapeDtypeStruct(xs2.shape, F32),
        compiler_params=_cparams(("arbitrary",)),
    )(item_blk, item_exp, item_lo, item_hi, xs2, w_gu, b_gu, w_down, b_down)


def _expert_work_items(counts, n_rows):
    ends = jnp.cumsum(counts)
    starts = ends - counts
    n_blk = n_rows // MOE_ROWS
    cuts = jnp.sort(jnp.concatenate([jnp.arange(n_blk, dtype=jnp.int32) * MOE_ROWS, starts[1:]]))
    nxt = jnp.concatenate([cuts[1:], jnp.full((1,), n_rows, jnp.int32)])
    blk = jnp.minimum(cuts // MOE_ROWS, n_blk - 1)
    exp = jnp.minimum(jnp.sum(ends[None, :] <= cuts[:, None], axis=1), N_EXPERTS - 1).astype(jnp.int32)
    lo = cuts - blk * MOE_ROWS
    hi = nxt - blk * MOE_ROWS
    return (blk.astype(jnp.int32), exp, lo.astype(jnp.int32), hi.astype(jnp.int32)), starts


def _combine_kernel(dest_ref, gates_ref, x1_ref, mod_ref, fg_ref, ys_hbm, op_ref, os_ref, rows_ref, sem, *,
                    n_prompt_tiles):
    tt = x1_ref.shape[0]

    def slot(k, t):
        return rows_ref.at[pl.ds(pl.multiple_of((k * tt + t) * SUBLANES, SUBLANES), SUBLANES), :]

    def issue(t, carry):
        for k in range(TOP_K):
            _row_copy(ys_hbm.at[dest_ref[k, t]], slot(k, t), sem).start()
        return carry

    def drain(t, carry):
        for k in range(TOP_K):
            _row_copy(ys_hbm.at[dest_ref[k, t]], slot(k, t), sem).wait()
        return carry

    lax.fori_loop(0, tt, issue, 0)
    lax.fori_loop(0, tt, drain, 0)

    gates = gates_ref[...]
    y = jnp.zeros(x1_ref.shape, F32)
    for k in range(TOP_K):
        yk = jnp.concatenate(
            [rows_ref[pl.ds(k * tt * SUBLANES + s, tt, stride=SUBLANES), :] for s in range(SUBLANES)], axis=1)
        y = y + gates[:, k:k + 1] * yk
    x2 = x1_ref[...] + mod_ref[0, 5:6, :] * y
    ms = jnp.mean(x2 * x2, axis=-1, keepdims=True)
    out = x2 * lax.rsqrt(ms + EPS) * fg_ref[...]

    def write(stream):
        (op_ref, os_ref)[stream][...] = out

    _per_stream(n_prompt_tiles, write)


def _combine(dest_t, gates, x1, mod3, final_g, ys3, batch_of_tile, n_prompt):
    n, d = x1.shape
    tt = GATHER_TILE
    n_prompt_tiles = n_prompt // tt
    return pl.pallas_call(
        functools.partial(_combine_kernel, n_prompt_tiles=n_prompt_tiles),
        grid=(n // tt,),
        in_specs=[pl.BlockSpec((TOP_K, tt), lambda i: (0, i), memory_space=pltpu.SMEM),
                  pl.BlockSpec((tt, TOP_K), lambda i: (i, 0)),
                  pl.BlockSpec((tt, d), lambda i: (i, 0)),
                  pl.BlockSpec((1, 6, d), lambda i: (batch_of_tile(i), 0, 0)),
                  pl.BlockSpec((1, d), lambda i: (0, 0)),
                  pl.BlockSpec(memory_space=pl.ANY)],
        out_specs=list(_stream_specs((tt, d), n_prompt_tiles)),
        out_shape=[jax.ShapeDtypeStruct((n_prompt, d), F32),
                   jax.ShapeDtypeStruct((n - n_prompt, d), F32)],
        scratch_shapes=[pltpu.VMEM((TOP_K * tt * SUBLANES, LANES), F32),
                        pltpu.SemaphoreType.DMA(())],
        compiler_params=_cparams(("arbitrary",)),
    )(dest_t, gates, x1, mod3, final_g.reshape(1, d), ys3)


def _batch_of_tile_fn(tile, bp, tp, ts):
    n_prompt = bp * tp

    def batch_of_tile(i):
        t = i * tile
        return jnp.where(t < n_prompt, t // tp, bp + (t - n_prompt) // ts)

    return batch_of_tile


def kernel(x_prompt, x_sample, c_prompt, c_sample, norm1_g, w_ada, b_ada, w_in, gla_w2_fwd, gla_b2_fwd, gla_w2_bwd, gla_b2_bwd, gla_norm_g, nat_rpb, w_out, norm2_g, router_w, router_b, w_gate_up, b_gate_up, w_down, b_down, final_norm_g):
    assert w_ada.shape[0] == 1, "single-layer encoder"
    bp, tp, d = x_prompt.shape
    bs, ts, _ = x_sample.shape
    n_prompt, n_sample = bp * tp, bs * ts
    n = n_prompt + n_sample
    assert tp % TOKEN_TILE == 0 and ts % TOKEN_TILE == 0 and n_prompt % ts == 0
    assert d == SUBLANES * LANES

    x = (x_prompt.reshape(n_prompt, d), x_sample.reshape(n_sample, d))
    c = jnp.concatenate([c_prompt, c_sample], axis=0)

    sizes = (GLA_QK, GLA_QK, GLA_V, GLA_V, GLA_RANK, GLA_RANK, NAT_W, NAT_W, NAT_W)
    offs = np.concatenate([[0], np.cumsum(sizes)])
    w_in0 = w_in[0]
    seg = lambda j: w_in0[:, offs[j]:offs[j + 1]]
    w_all = jnp.concatenate(
        [seg(0), seg(1), seg(2), seg(3), seg(6), seg(7), seg(8), seg(4), seg(5),
         jnp.zeros((d, LANES - 2 * GLA_RANK), F32)], axis=1).astype(BF16)
    w2f = jnp.zeros((LANES, GLA_QK), F32).at[0:GLA_RANK].set(gla_w2_fwd[0])
    w2b = jnp.zeros((LANES, GLA_QK), F32).at[GLA_RANK:2 * GLA_RANK].set(gla_w2_bwd[0])
    tab = _nat_bias_table(nat_rpb[0])
    w_og = w_out[0, :GLA_V].astype(BF16)
    w_on = w_out[0, GLA_V:].astype(BF16)
    rw_t = router_w[0].T
    w_gu = _deinterleave(w_gate_up[0])
    b_gu = jnp.concatenate([b_gate_up[0, :, None, 0::2], b_gate_up[0, :, None, 1::2]], axis=-1)
    w_dn = w_down[0].astype(BF16)
    b_dn = b_down[0, :, None, :]

    mod3 = _ada(c, w_ada[0], b_ada[0]).reshape(bp + bs, 6, d)
    bot_tok = _batch_of_tile_fn(TOKEN_TILE, bp, tp, ts)
    bot_gat = _batch_of_tile_fn(GATHER_TILE, bp, tp, ts)

    pg, pn, plr = _in_proj(*x, mod3, norm1_g[0], w_all, bot_tok)

    gla_args = (pg, plr, w2f, gla_b2_fwd[0].reshape(1, -1), w2b, gla_b2_bwd[0].reshape(1, -1),
                gla_norm_g[0].reshape(1, -1))
    og = (_gla(*gla_args, seq=tp, batches=bp, block0=0),
          _gla(*gla_args, seq=ts, batches=bs, block0=n_prompt // ts))
    on = (_nat(pn, tab, seq=tp, batches=bp, block0=0),
          _nat(pn, tab, seq=ts, batches=bs, block0=n_prompt // ts))

    x1, h2, top_i, gates_t, rank_t, cnt = _out_proj(og, on, x, mod3, w_og, w_on, norm2_g[0], rw_t, router_b[0], bot_tok)

    n_rows = n * TOP_K
    assert n_rows % MOE_ROWS == 0
    counts = cnt[:, 0].astype(jnp.int32)
    items, starts = _expert_work_items(counts, n_rows)
    chosen = top_i[None] == jnp.arange(N_EXPERTS, dtype=jnp.int32)[:, None, None]
    dest_t = rank_t + jnp.sum(jnp.where(chosen, starts[:, None, None], 0), axis=0)

    xs = _dispatch(dest_t, h2)
    ys = _experts(items, xs.reshape(n_rows * SUBLANES, LANES), w_gu, b_gu, w_dn, b_dn)
    y_prompt, y_sample = _combine(dest_t, gates_t.T, x1, mod3, final_norm_g,
                                  ys.reshape(n_rows, SUBLANES, LANES), bot_gat, n_prompt)

    return (y_prompt.reshape(bp, tp, d), y_sample.reshape(bs, ts, d))
```

```python
import functools

import numpy as np
import jax
import jax.numpy as jnp
from jax import lax
from jax.experimental import pallas as pl
from jax.experimental.pallas import tpu as pltpu

F32 = jnp.float32
BF16 = jnp.bfloat16
HIGHEST = lax.Precision.HIGHEST

EPS = 1e-5
GRID_W = 64
GLA_HEADS = 4
GLA_DK = 64
GLA_DV = 128
GLA_RANK = 16
GLA_TAU = 16.0
GLA_CHUNK = 64
GLA_QK = GLA_HEADS * GLA_DK
GLA_V = GLA_HEADS * GLA_DV
NAT_HEADS = 16
NAT_HD = 32
NAT_W = NAT_HEADS * NAT_HD
NAT_KH = 8
NAT_KW = 16
NAT_GROUP = 4
N_EXPERTS = 32
TOP_K = 4
SWIGLU_LIMIT = 7.0
SWIGLU_ALPHA = 1.702

LANES = 128
SUBLANES = 8
TOKEN_TILE = 512
MOE_ROWS = 512
GATHER_TILE = 256
VMEM_LIMIT = 56 * 1024 * 1024


def _cparams(sem, vmem=VMEM_LIMIT):
    return pltpu.CompilerParams(dimension_semantics=sem, vmem_limit_bytes=vmem)


def _dot(a, b):
    return jnp.dot(a, b, preferred_element_type=F32)


def _dot_nt(a, b, precision=None):
    return lax.dot_general(a, b, (((1,), (1,)), ((), ())), preferred_element_type=F32, precision=precision)


def _dot_tn(a, b):
    return lax.dot_general(a, b, (((0,), (0,)), ((), ())), preferred_element_type=F32)


def _ada_kernel(c_ref, w_ref, b_ref, o_ref):
    c = c_ref[...]
    s = c * jax.nn.sigmoid(c)
    o_ref[...] = jnp.dot(s, w_ref[...], preferred_element_type=F32, precision=HIGHEST) + b_ref[...]


def _ada(c, w, b):
    nb, d = c.shape
    cols = w.shape[1]
    blk = 1024
    return pl.pallas_call(
        _ada_kernel,
        grid=(cols // blk,),
        in_specs=[pl.BlockSpec((nb, d), lambda j: (0, 0)),
                  pl.BlockSpec((d, blk), lambda j: (0, j)),
                  pl.BlockSpec((1, blk), lambda j: (0, j))],
        out_specs=pl.BlockSpec((nb, blk), lambda j: (0, j)),
        out_shape=jax.ShapeDtypeStruct((nb, cols), F32),
        compiler_params=_cparams(("arbitrary",)),
    )(c, w, b.reshape(1, cols))


def _stream_specs(block, n_prompt_tiles):
    return (pl.BlockSpec(block, lambda i: (jnp.minimum(i, n_prompt_tiles - 1), 0)),
            pl.BlockSpec(block, lambda i: (jnp.maximum(i - n_prompt_tiles, 0), 0)))


def _per_stream(n_prompt_tiles, body):
    i = pl.program_id(0)
    pl.when(i < n_prompt_tiles)(functools.partial(body, 0))
    pl.when(i >= n_prompt_tiles)(functools.partial(body, 1))


def _in_kernel(xp_ref, xs_ref, mod_ref, g_ref, w_ref, pg_ref, pn_ref, plr_ref, *, n_prompt_tiles):
    def body(stream):
        x = (xp_ref, xs_ref)[stream][...]
        ms = jnp.mean(x * x, axis=-1, keepdims=True)
        y = x * lax.rsqrt(ms + EPS) * g_ref[...]
        h = y * (1.0 + mod_ref[0, 1:2, :]) + mod_ref[0, 0:1, :]
        hb = h.astype(BF16)
        wg = GLA_QK * 2 + GLA_V * 2
        wn = 3 * NAT_W
        pg_ref[...] = _dot(hb, w_ref[:, 0:wg]).astype(BF16)
        pn_ref[...] = _dot(hb, w_ref[:, wg:wg + wn]).astype(BF16)
        plr_ref[...] = _dot(hb, w_ref[:, wg + wn:wg + wn + LANES])

    _per_stream(n_prompt_tiles, body)


def _in_proj(xp, xs, mod3, norm_g, w_all, batch_of_tile):
    d = xp.shape[1]
    n = xp.shape[0] + xs.shape[0]
    wg = GLA_QK * 2 + GLA_V * 2
    wn = 3 * NAT_W
    tm = TOKEN_TILE
    n_prompt_tiles = xp.shape[0] // tm
    return pl.pallas_call(
        functools.partial(_in_kernel, n_prompt_tiles=n_prompt_tiles),
        grid=(n // tm,),
        in_specs=[*_stream_specs((tm, d), n_prompt_tiles),
                  pl.BlockSpec((1, 6, d), lambda i: (batch_of_tile(i), 0, 0)),
                  pl.BlockSpec((1, d), lambda i: (0, 0)),
                  pl.BlockSpec(w_all.shape, lambda i: (0, 0))],
        out_specs=[pl.BlockSpec((tm, wg), lambda i: (i, 0)),
                   pl.BlockSpec((tm, wn), lambda i: (i, 0)),
                   pl.BlockSpec((tm, LANES), lambda i: (i, 0))],
        out_shape=[jax.ShapeDtypeStruct((n, wg), BF16),
                   jax.ShapeDtypeStruct((n, wn), BF16),
                   jax.ShapeDtypeStruct((n, LANES), F32)],
        compiler_params=_cparams(("arbitrary",)),
    )(xp, xs, mod3, norm_g.reshape(1, d), w_all)


def _gla_kernel(q_ref, k_ref, v_ref, g_ref, lr_ref, wz_ref, bz_ref, ng_ref,
                o_ref, la_ref, acc_ref, qt_ref, ke_ref, dec_ref, sf_ref, sb_ref, *, seq):
    c_len = GLA_CHUNK
    n = seq // c_len
    row = lax.broadcasted_iota(jnp.int32, (c_len, c_len), 0)
    col = lax.broadcasted_iota(jnp.int32, (c_len, c_len), 1)
    tril = col <= row
    triu = col >= row
    cum_row = lax.broadcasted_iota(jnp.int32, (c_len, 2 * c_len), 0)
    cum_col = lax.broadcasted_iota(jnp.int32, (c_len, 2 * c_len), 1) % c_len
    cum = (cum_col <= cum_row).astype(F32).astype(BF16)
    mask_f = jnp.concatenate([tril, tril], axis=0)
    mask_b = jnp.concatenate([triu, triu], axis=0)
    head_a = lax.broadcasted_iota(jnp.int32, (c_len, LANES), 1) < GLA_DK
    st_row = lax.broadcasted_iota(jnp.int32, (2 * GLA_DV, LANES), 0)
    st_col = lax.broadcasted_iota(jnp.int32, (2 * GLA_DV, LANES), 1)
    blockdiag = (st_row < GLA_DV) == (st_col < GLA_DK)

    z = _dot(lr_ref[...].astype(BF16), wz_ref[...]) + bz_ref[...]
    la_ref[...] = (jnp.minimum(z, 0.0) - jnp.log(1.0 + jnp.exp(-jnp.abs(z)))) * (1.0 / GLA_TAU)

    def chunk_rows(c):
        return pl.ds(pl.multiple_of(c * c_len, c_len), c_len)

    def stack_heads(x):
        zero = jnp.zeros_like(x)
        return jnp.concatenate([jnp.where(head_a, x, zero), jnp.where(head_a, zero, x)], axis=0).astype(BF16)

    def local(c, carry):
        rows = chunk_rows(c)
        la = la_ref[rows, :]
        hi = la.astype(BF16)
        lo = (la - hi.astype(F32)).astype(BF16)
        binc = _dot(cum, jnp.concatenate([hi, lo], axis=0))
        btot = jnp.sum(la, axis=0, keepdims=True)
        b_f = binc[:, :LANES]
        b_b = btot[:, LANES:] - binc[:, LANES:] + la[:, LANES:]
        e_tot = jnp.exp(btot)
        q = q_ref[rows, :].astype(F32) * (GLA_DK ** -0.5)
        k = k_ref[rows, :].astype(F32)
        qt_f = q * jnp.exp(b_f)
        qt_b = q * jnp.exp(b_b)
        kt_f = k * jnp.exp(-b_f)
        kt_b = k * jnp.exp(-b_b)
        a = (jnp.where(mask_f, _dot_nt(stack_heads(qt_f), kt_f.astype(BF16)), 0.0)
             + jnp.where(mask_b, _dot_nt(stack_heads(qt_b), kt_b.astype(BF16)), 0.0))
        oi = _dot(a.astype(BF16), v_ref[rows, :])
        acc_ref[rows, :] = jnp.concatenate([oi[0:c_len, 0:GLA_DV], oi[c_len:, GLA_DV:]], axis=1)
        qt_ref[rows, :] = jnp.concatenate([qt_f, qt_b], axis=1).astype(BF16)
        ke_ref[rows, :] = jnp.concatenate([kt_f * e_tot[:, :LANES], kt_b * e_tot[:, LANES:]], axis=1).astype(BF16)
        dec_ref[pl.ds(pl.multiple_of(c * SUBLANES, SUBLANES), SUBLANES), :] = jnp.broadcast_to(e_tot, (SUBLANES, 2 * LANES))
        return carry

    def carried(c, s_ref, half):
        rows = chunk_rows(c)
        lanes = slice(half * LANES, (half + 1) * LANES)
        st = s_ref[...]
        acc_ref[rows, :] += _dot_nt(qt_ref[rows, lanes], st.astype(BF16))
        upd = _dot_tn(v_ref[rows, :], ke_ref[rows, lanes])
        dec = dec_ref[pl.ds(pl.multiple_of(c * SUBLANES, SUBLANES), 1), lanes]
        s_ref[...] = st * dec + jnp.where(blockdiag, upd, 0.0)

    def scan(c, carry):
        carried(c, sf_ref, 0)
        carried(n - 1 - c, sb_ref, 1)
        return carry

    def finalize(c, carry):
        rows = chunk_rows(c)
        o = acc_ref[rows, :]
        halves = []
        for hh in range(2):
            oh = o[:, hh * GLA_DV:(hh + 1) * GLA_DV]
            ms = jnp.mean(oh * oh, axis=-1, keepdims=True)
            halves.append(oh * lax.rsqrt(ms + EPS) * ng_ref[...])
        y = jnp.concatenate(halves, axis=1)
        g = g_ref[rows, :].astype(F32)
        o_ref[rows, :] = (y * (g * jax.nn.sigmoid(g))).astype(o_ref.dtype)
        return carry

    sf_ref[...] = jnp.zeros_like(sf_ref)
    sb_ref[...] = jnp.zeros_like(sb_ref)
    lax.fori_loop(0, n, local, 0, unroll=2)
    lax.fori_loop(0, n, scan, 0, unroll=2)
    lax.fori_loop(0, n, finalize, 0, unroll=2)


def _gla(pg, plr, wz, bz, norm_g, *, seq, batches, block0):
    pair_w = 2 * GLA_DK
    pair_v = 2 * GLA_DV
    k0 = GLA_QK // pair_w
    v0 = 2 * GLA_QK // pair_v
    g0 = (2 * GLA_QK + GLA_V) // pair_v
    n_chunks = seq // GLA_CHUNK
    in_specs = [pl.BlockSpec((seq, pair_w), lambda b, p: (block0 + b, p)),
                pl.BlockSpec((seq, pair_w), lambda b, p: (block0 + b, k0 + p)),
                pl.BlockSpec((seq, pair_v), lambda b, p: (block0 + b, v0 + p)),
                pl.BlockSpec((seq, pair_v), lambda b, p: (block0 + b, g0 + p)),
                pl.BlockSpec((seq, LANES), lambda b, p: (block0 + b, 0)),
                pl.BlockSpec((LANES, 2 * pair_w), lambda b, p: (0, p)),
                pl.BlockSpec((1, 2 * pair_w), lambda b, p: (0, p)),
                pl.BlockSpec((1, GLA_DV), lambda b, p: (0, 0))]
    return pl.pallas_call(
        functools.partial(_gla_kernel, seq=seq),
        grid=(batches, GLA_HEADS // 2),
        in_specs=in_specs,
        out_specs=pl.BlockSpec((seq, pair_v), lambda b, p: (b, p)),
        out_shape=jax.ShapeDtypeStruct((batches * seq, GLA_V), BF16),
        scratch_shapes=[pltpu.VMEM((seq, 2 * pair_w), F32),
                        pltpu.VMEM((seq, pair_v), F32),
                        pltpu.VMEM((seq, 2 * pair_w), BF16),
                        pltpu.VMEM((seq, 2 * pair_w), BF16),
                        pltpu.VMEM((n_chunks * SUBLANES, 2 * pair_w), F32),
                        pltpu.VMEM((pair_v, pair_w), F32),
                        pltpu.VMEM((pair_v, pair_w), F32)],
        compiler_params=_cparams(("parallel", "parallel")),
    )(pg, pg, pg, pg, plr, wz, bz, norm_g)


def _nat_kernel(q_ref, k_ref, v_ref, tab_ref, o_ref, *, rows):
    w = GRID_W
    head = lax.broadcasted_iota(jnp.int32, (w, LANES), 1) // NAT_HD
    win = NAT_KH * w

    def body(i, carry):
        r0 = jnp.clip(i - NAT_KH // 2, 0, rows - NAT_KH)
        base = r0 - i + NAT_KH - 1
        qrows = pl.ds(pl.multiple_of(i * w, w), w)
        krows = pl.ds(pl.multiple_of(r0 * w, w), win)
        q = q_ref[qrows, :]
        zero = jnp.zeros_like(q)
        q_stack = jnp.concatenate([jnp.where(head == h, q, zero) for h in range(NAT_GROUP)], axis=0)
        s = _dot_nt(q_stack, k_ref[krows, :]) * (NAT_HD ** -0.5) + tab_ref[0, base]
        m = jnp.max(s, axis=-1, keepdims=True)
        e = jnp.exp(s - m)
        l = jnp.sum(e, axis=-1, keepdims=True)
        o = _dot(e.astype(BF16), v_ref[krows, :]) / l
        out = jnp.zeros((w, LANES), F32)
        for h in range(NAT_GROUP):
            out = out + jnp.where(head == h, o[h * w:(h + 1) * w, :], 0.0)
        o_ref[qrows, :] = out.astype(o_ref.dtype)
        return carry

    lax.fori_loop(0, rows, body, 0, unroll=2)


def _nat(pn, tab, *, seq, batches, block0):
    rows = seq // GRID_W
    groups = NAT_HEADS // NAT_GROUP
    in_specs = [pl.BlockSpec((seq, LANES), lambda g, b: (block0 + b, g)),
                pl.BlockSpec((seq, LANES), lambda g, b: (block0 + b, groups + g)),
                pl.BlockSpec((seq, LANES), lambda g, b: (block0 + b, 2 * groups + g)),
                pl.BlockSpec((1,) + tab.shape[1:], lambda g, b: (g, 0, 0, 0))]
    return pl.pallas_call(
        functools.partial(_nat_kernel, rows=rows),
        grid=(groups, batches),
        in_specs=in_specs,
        out_specs=pl.BlockSpec((seq, LANES), lambda g, b: (b, g)),
        out_shape=jax.ShapeDtypeStruct((batches * seq, NAT_W), BF16),
        compiler_params=_cparams(("parallel", "parallel")),
    )(pn, pn, pn, tab)


def _nat_bias_table(rpb):
    w = GRID_W
    jq = np.arange(w)[:, None]
    jk = np.arange(w)[None, :]
    c0 = np.clip(jq - NAT_KW // 2, 0, w - NAT_KW)
    valid = (jk >= c0) & (jk < c0 + NAT_KW)
    dcol = np.clip(jk - jq + NAT_KW - 1, 0, 2 * NAT_KW - 2)
    drow = np.arange(NAT_KH)[:, None] + np.arange(NAT_KH)[None, :]
    t = rpb.astype(F32)[:, drow][:, :, :, dcol]
    t = jnp.where(valid[None, None, None], t, -jnp.inf)
    t = t.transpose(0, 1, 3, 2, 4).reshape(NAT_HEADS, NAT_KH, w, NAT_KH * w)
    t = t.reshape(NAT_HEADS // NAT_GROUP, NAT_GROUP, NAT_KH, w, NAT_KH * w).transpose(0, 2, 1, 3, 4)
    return t.reshape(NAT_HEADS // NAT_GROUP, NAT_KH, NAT_GROUP * w, NAT_KH * w)


def _out_kernel(ogp_ref, ogs_ref, onp_ref, ons_ref, xp_ref, xs_ref, mod_ref, wg_ref, wn_ref, n2_ref, rw_ref, rb_ref,
                x1_ref, h2_ref, ti_ref, gt_ref, rk_ref, cnt_ref, carry_ref, *, n_prompt_tiles):
    tm = x1_ref.shape[0]

    @pl.when(pl.program_id(0) == 0)
    def _():
        carry_ref[...] = jnp.zeros_like(carry_ref)

    def residual(stream):
        og_ref, on_ref, x_ref = ((ogp_ref, onp_ref, xp_ref), (ogs_ref, ons_ref, xs_ref))[stream]
        mix = _dot(og_ref[...], wg_ref[...]) + _dot(on_ref[...], wn_ref[...])
        x1_ref[...] = x_ref[...] + mod_ref[0, 2:3, :] * mix

    _per_stream(n_prompt_tiles, residual)
    x1 = x1_ref[...]
    ms = jnp.mean(x1 * x1, axis=-1, keepdims=True)
    h2 = x1 * lax.rsqrt(ms + EPS) * n2_ref[...]
    h2 = h2 * (1.0 + mod_ref[0, 4:5, :]) + mod_ref[0, 3:4, :]
    for s in range(SUBLANES):
        h2_ref[pl.ds(s, tm, stride=SUBLANES), :] = h2[:, s * LANES:(s + 1) * LANES]

    logits = _dot_nt(rw_ref[...], h2, precision=HIGHEST) + rb_ref[...]
    eidx = lax.broadcasted_iota(jnp.int32, logits.shape, 0)
    vals = logits
    top_v, top_i = [], []
    for _ in range(TOP_K):
        m = jnp.max(vals, axis=0, keepdims=True)
        idx = jnp.min(jnp.where(vals == m, eidx, N_EXPERTS), axis=0, keepdims=True)
        top_v.append(m)
        top_i.append(idx)
        vals = jnp.where(eidx == idx, -jnp.inf, vals)
    ev = [jnp.exp(v - top_v[0]) for v in top_v]
    den = ev[0] + ev[1] + ev[2] + ev[3]
    ti_ref[...] = jnp.concatenate(top_i, axis=0)
    gt_ref[...] = jnp.concatenate([e / den for e in ev], axis=0)

    onehot = jnp.zeros(logits.shape, F32)
    for idx in top_i:
        onehot = onehot + (eidx == idx).astype(F32)
    r = lax.broadcasted_iota(jnp.int32, (tm, tm), 0)
    c = lax.broadcasted_iota(jnp.int32, (tm, tm), 1)
    earlier = (r < c).astype(BF16)
    before = carry_ref[...] + _dot(onehot.astype(BF16), earlier)
    rk_ref[...] = jnp.concatenate(
        [jnp.sum(jnp.where(eidx == idx, before, 0.0), axis=0, keepdims=True) for idx in top_i], axis=0).astype(jnp.int32)
    carry_ref[...] = carry_ref[...] + jnp.sum(onehot, axis=1, keepdims=True)
    cnt_ref[...] = jnp.broadcast_to(carry_ref[...], cnt_ref.shape)


def _out_proj(og, on, x, mod3, w_g, w_n, norm_g, rw_t, rb, batch_of_tile):
    d = x[0].shape[1]
    n = x[0].shape[0] + x[1].shape[0]
    tm = TOKEN_TILE
    n_prompt_tiles = x[0].shape[0] // tm
    tok = lambda i: (i, 0)
    const = lambda i: (0, 0)
    return pl.pallas_call(
        functools.partial(_out_kernel, n_prompt_tiles=n_prompt_tiles),
        grid=(n // tm,),
        in_specs=[*_stream_specs((tm, GLA_V), n_prompt_tiles),
                  *_stream_specs((tm, NAT_W), n_prompt_tiles),
                  *_stream_specs((tm, d), n_prompt_tiles),
                  pl.BlockSpec((1, 6, d), lambda i: (batch_of_tile(i), 0, 0)),
                  pl.BlockSpec(w_g.shape, const),
                  pl.BlockSpec(w_n.shape, const),
                  pl.BlockSpec((1, d), const),
                  pl.BlockSpec(rw_t.shape, const),
                  pl.BlockSpec((N_EXPERTS, 1), const)],
        out_specs=[pl.BlockSpec((tm, d), tok),
                   pl.BlockSpec((tm * SUBLANES, LANES), tok),
                   pl.BlockSpec((TOP_K, tm), lambda i: (0, i)),
                   pl.BlockSpec((TOP_K, tm), lambda i: (0, i)),
                   pl.BlockSpec((TOP_K, tm), lambda i: (0, i)),
                   pl.BlockSpec((N_EXPERTS, LANES), const)],
        out_shape=[jax.ShapeDtypeStruct((n, d), F32),
                   jax.ShapeDtypeStruct((n * SUBLANES, LANES), F32),
                   jax.ShapeDtypeStruct((TOP_K, n), jnp.int32),
                   jax.ShapeDtypeStruct((TOP_K, n), F32),
                   jax.ShapeDtypeStruct((TOP_K, n), jnp.int32),
                   jax.ShapeDtypeStruct((N_EXPERTS, LANES), F32)],
        scratch_shapes=[pltpu.VMEM((N_EXPERTS, 1), F32)],
        compiler_params=_cparams(("arbitrary",)),
    )(*og, *on, *x, mod3, w_g, w_n, norm_g.reshape(1, d), rw_t, rb.reshape(N_EXPERTS, 1))


def _row_copy(src, dst, sem):
    return pltpu.make_async_copy(src, dst, sem)


def _dispatch_kernel(dest_ref, h_ref, xs_hbm, sem):
    tt = dest_ref.shape[1]

    def src(t):
        return h_ref.at[pl.ds(pl.multiple_of(t * SUBLANES, SUBLANES), SUBLANES), :]

    def issue(t, carry):
        for k in range(TOP_K):
            _row_copy(src(t), xs_hbm.at[dest_ref[k, t]], sem).start()
        return carry

    def drain(t, carry):
        for k in range(TOP_K):
            _row_copy(src(t), xs_hbm.at[dest_ref[k, t]], sem).wait()
        return carry

    lax.fori_loop(0, tt, issue, 0)
    lax.fori_loop(0, tt, drain, 0)


def _dispatch(dest_t, h2):
    n = h2.shape[0] // SUBLANES
    tt = GATHER_TILE
    return pl.pallas_call(
        _dispatch_kernel,
        grid=(n // tt,),
        in_specs=[pl.BlockSpec((TOP_K, tt), lambda i: (0, i), memory_space=pltpu.SMEM),
                  pl.BlockSpec((tt * SUBLANES, LANES), lambda i: (i, 0))],
        out_specs=pl.BlockSpec(memory_space=pl.ANY),
        out_shape=jax.ShapeDtypeStruct((n * TOP_K, SUBLANES, LANES), F32),
        scratch_shapes=[pltpu.SemaphoreType.DMA(())],
        compiler_params=_cparams(("arbitrary",)),
    )(dest_t, h2)


def _deinterleave_kernel(w_ref, o_ref):
    f2 = w_ref.shape[2]
    slab = 2 * LANES
    r = lax.broadcasted_iota(jnp.int32, (slab, slab), 0)
    c = lax.broadcasted_iota(jnp.int32, (slab, slab), 1)
    perm = (r == jnp.where(c < LANES, 2 * c, 2 * (c - LANES) + 1)).astype(BF16)
    for j in range(f2 // slab):
        y = _dot(w_ref[0, :, j * slab:(j + 1) * slab].astype(BF16), perm)
        o_ref[0, :, j * LANES:(j + 1) * LANES] = y[:, :LANES].astype(BF16)
        o_ref[0, :, f2 // 2 + j * LANES:f2 // 2 + (j + 1) * LANES] = y[:, LANES:].astype(BF16)


def _deinterleave(w):
    e, d, f2 = w.shape
    return pl.pallas_call(
        _deinterleave_kernel,
        grid=(e,),
        in_specs=[pl.BlockSpec((1, d, f2), lambda i: (i, 0, 0))],
        out_specs=pl.BlockSpec((1, d, f2), lambda i: (i, 0, 0)),
        out_shape=jax.ShapeDtypeStruct((e, d, f2), BF16),
        compiler_params=_cparams(("parallel",)),
    )(w)


def _expert_kernel(blk_ref, exp_ref, lo_ref, hi_ref, xs_ref, wgu_ref, bgu_ref, wd_ref, bd_ref, ys_ref):
    del blk_ref, exp_ref
    rows = xs_ref.shape[0] // SUBLANES
    f = wd_ref.shape[1]
    i = pl.program_id(0)
    lo = lo_ref[i]
    hi = hi_ref[i]

    @pl.when(hi > lo)
    def _():
        x = jnp.concatenate([xs_ref[pl.ds(s, rows, stride=SUBLANES), :] for s in range(SUBLANES)], axis=1).astype(BF16)
        gu = _dot(x, wgu_ref[0]) + bgu_ref[0]
        gate = jnp.minimum(gu[:, :f], SWIGLU_LIMIT)
        up = jnp.clip(gu[:, f:], -SWIGLU_LIMIT, SWIGLU_LIMIT)
        act = (up + 1.0) * gate * jax.nn.sigmoid(SWIGLU_ALPHA * gate)
        y = _dot(act.astype(BF16), wd_ref[0]) + bd_ref[0]
        r = lax.broadcasted_iota(jnp.int32, (rows, LANES), 0)
        mine = (r >= lo) & (r < hi)

        @pl.when(lo == 0)
        def _():
            for s in range(SUBLANES):
                ys_ref[pl.ds(s, rows, stride=SUBLANES), :] = jnp.where(mine, y[:, s * LANES:(s + 1) * LANES], 0.0)

        @pl.when(lo > 0)
        def _():
            for s in range(SUBLANES):
                sl = pl.ds(s, rows, stride=SUBLANES)
                ys_ref[sl, :] = jnp.where(mine, y[:, s * LANES:(s + 1) * LANES], ys_ref[sl, :])


def _experts(items, xs2, w_gu, b_gu, w_down, b_down):
    item_blk, item_exp, item_lo, item_hi = items
    rows8 = MOE_ROWS * SUBLANES
    d, f2 = w_gu.shape[1], w_gu.shape[2]
    wmap = lambda i, blk, exp, lo, hi: (exp[i], 0, 0)
    xmap = lambda i, blk, exp, lo, hi: (blk[i], 0)
    grid_spec = pltpu.PrefetchScalarGridSpec(
        num_scalar_prefetch=4,
        grid=(item_blk.shape[0],),
        in_specs=[pl.BlockSpec((rows8, LANES), xmap),
                  pl.BlockSpec((1, d, f2), wmap),
                  pl.BlockSpec((1, 1, f2), wmap),
                  pl.BlockSpec((1, f2 // 2, d), wmap),
                  pl.BlockSpec((1, 1, d), wmap)],
        out_specs=pl.BlockSpec((rows8, LANES), xmap),
    )
    return pl.pallas_call(
        _expert_kernel,
        grid_spec=grid_spec,
        out_shape=jax.ShapeDtypeStruct(xs2.shape, F32),
        compiler_params=_cparams(("arbitrary",)),
    )(item_blk, item_exp, item_lo, item_hi, xs2, w_gu, b_gu, w_down, b_down)


def _expert_work_items(counts, n_rows):
    ends = jnp.cumsum(counts)
    starts = ends - counts
    n_blk = n_rows // MOE_ROWS
    cuts = jnp.sort(jnp.concatenate([jnp.arange(n_blk, dtype=jnp.int32) * MOE_ROWS, starts[1:]]))
    nxt = jnp.concatenate([cuts[1:], jnp.full((1,), n_rows, jnp.int32)])
    blk = jnp.minimum(cuts // MOE_ROWS, n_blk - 1)
    exp = jnp.minimum(jnp.sum(ends[None, :] <= cuts[:, None], axis=1), N_EXPERTS - 1).astype(jnp.int32)
    lo = cuts - blk * MOE_ROWS
    hi = nxt - blk * MOE_ROWS
    return (blk.astype(jnp.int32), exp, lo.astype(jnp.int32), hi.astype(jnp.int32)), starts


def _combine_kernel(dest_ref, gates_ref, x1_ref, mod_ref, fg_ref, ys_hbm, op_ref, os_ref, rows_ref, sem, *,
                    n_prompt_tiles):
    tt = x1_ref.shape[0]

    def slot(k, t):
        return rows_ref.at[pl.ds(pl.multiple_of((k * tt + t) * SUBLANES, SUBLANES), SUBLANES), :]

    def issue(t, carry):
        for k in range(TOP_K):
            _row_copy(ys_hbm.at[dest_ref[k, t]], slot(k, t), sem).start()
        return carry

    def drain(t, carry):
        for k in range(TOP_K):
            _row_copy(ys_hbm.at[dest_ref[k, t]], slot(k, t), sem).wait()
        return carry

    lax.fori_loop(0, tt, issue, 0)
    lax.fori_loop(0, tt, drain, 0)

    gates = gates_ref[...]
    y = jnp.zeros(x1_ref.shape, F32)
    for k in range(TOP_K):
        yk = jnp.concatenate(
            [rows_ref[pl.ds(k * tt * SUBLANES + s, tt, stride=SUBLANES), :] for s in range(SUBLANES)], axis=1)
        y = y + gates[:, k:k + 1] * yk
    x2 = x1_ref[...] + mod_ref[0, 5:6, :] * y
    ms = jnp.mean(x2 * x2, axis=-1, keepdims=True)
    out = x2 * lax.rsqrt(ms + EPS) * fg_ref[...]

    def write(stream):
        (op_ref, os_ref)[stream][...] = out

    _per_stream(n_prompt_tiles, write)


def _combine(dest_t, gates, x1, mod3, final_g, ys3, batch_of_tile, n_prompt):
    n, d = x1.shape
    tt = GATHER_TILE
    n_prompt_tiles = n_prompt // tt
    return pl.pallas_call(
        functools.partial(_combine_kernel, n_prompt_tiles=n_prompt_tiles),
        grid=(n // tt,),
        in_specs=[pl.BlockSpec((TOP_K, tt), lambda i: (0, i), memory_space=pltpu.SMEM),
                  pl.BlockSpec((tt, TOP_K), lambda i: (i, 0)),
                  pl.BlockSpec((tt, d), lambda i: (i, 0)),
                  pl.BlockSpec((1, 6, d), lambda i: (batch_of_tile(i), 0, 0)),
                  pl.BlockSpec((1, d), lambda i: (0, 0)),
                  pl.BlockSpec(memory_space=pl.ANY)],
        out_specs=list(_stream_specs((tt, d), n_prompt_tiles)),
        out_shape=[jax.ShapeDtypeStruct((n_prompt, d), F32),
                   jax.ShapeDtypeStruct((n - n_prompt, d), F32)],
        scratch_shapes=[pltpu.VMEM((TOP_K * tt * SUBLANES, LANES), F32),
                        pltpu.SemaphoreType.DMA(())],
        compiler_params=_cparams(("arbitrary",)),
    )(dest_t, gates, x1, mod3, final_g.reshape(1, d), ys3)


def _batch_of_tile_fn(tile, bp, tp, ts):
    n_prompt = bp * tp

    def batch_of_tile(i):
        t = i * tile
        return jnp.where(t < n_prompt, t // tp, bp + (t - n_prompt) // ts)

    return batch_of_tile


def kernel(x_prompt, x_sample, c_prompt, c_sample, norm1_g, w_ada, b_ada, w_in, gla_w2_fwd, gla_b2_fwd, gla_w2_bwd, gla_b2_bwd, gla_norm_g, nat_rpb, w_out, norm2_g, router_w, router_b, w_gate_up, b_gate_up, w_down, b_down, final_norm_g):
    assert w_ada.shape[0] == 1, "single-layer encoder"
    bp, tp, d = x_prompt.shape
    bs, ts, _ = x_sample.shape
    n_prompt, n_sample = bp * tp, bs * ts
    n = n_prompt + n_sample
    assert tp % TOKEN_TILE == 0 and ts % TOKEN_TILE == 0 and n_prompt % ts == 0
    assert d == SUBLANES * LANES

    x = (x_prompt.reshape(n_prompt, d), x_sample.reshape(n_sample, d))
    c = jnp.concatenate([c_prompt, c_sample], axis=0)

    sizes = (GLA_QK, GLA_QK, GLA_V, GLA_V, GLA_RANK, GLA_RANK, NAT_W, NAT_W, NAT_W)
    offs = np.concatenate([[0], np.cumsum(sizes)])
    w_in0 = w_in[0]
    seg = lambda j: w_in0[:, offs[j]:offs[j + 1]]
    w_all = jnp.concatenate(
        [seg(0), seg(1), seg(2), seg(3), seg(6), seg(7), seg(8), seg(4), seg(5),
         jnp.zeros((d, LANES - 2 * GLA_RANK), F32)], axis=1).astype(BF16)
    pairs = GLA_HEADS // 2
    w2 = jnp.zeros((LANES, pairs, 2, 2 * GLA_DK), F32)
    w2 = w2.at[0:GLA_RANK, :, 0].set(gla_w2_fwd[0].reshape(GLA_RANK, pairs, 2 * GLA_DK))
    w2 = w2.at[GLA_RANK:2 * GLA_RANK, :, 1].set(gla_w2_bwd[0].reshape(GLA_RANK, pairs, 2 * GLA_DK))
    wz = w2.reshape(LANES, pairs * 4 * GLA_DK).astype(BF16)
    bz = jnp.stack([gla_b2_fwd[0].reshape(pairs, 2 * GLA_DK), gla_b2_bwd[0].reshape(pairs, 2 * GLA_DK)],
                   axis=1).reshape(1, pairs * 4 * GLA_DK)
    tab = _nat_bias_table(nat_rpb[0])
    w_og = w_out[0, :GLA_V].astype(BF16)
    w_on = w_out[0, GLA_V:].astype(BF16)
    rw_t = router_w[0].T
    w_gu = _deinterleave(w_gate_up[0])
    b_gu = jnp.concatenate([b_gate_up[0, :, None, 0::2], b_gate_up[0, :, None, 1::2]], axis=-1)
    w_dn = w_down[0].astype(BF16)
    b_dn = b_down[0, :, None, :]

    mod3 = _ada(c, w_ada[0], b_ada[0]).reshape(bp + bs, 6, d)
    bot_tok = _batch_of_tile_fn(TOKEN_TILE, bp, tp, ts)
    bot_gat = _batch_of_tile_fn(GATHER_TILE, bp, tp, ts)

    pg, pn, plr = _in_proj(*x, mod3, norm1_g[0], w_all, bot_tok)

    gla_args = (pg, plr, wz, bz, gla_norm_g[0].reshape(1, -1))
    og = (_gla(*gla_args, seq=tp, batches=bp, block0=0),
          _gla(*gla_args, seq=ts, batches=bs, block0=n_prompt // ts))
    on = (_nat(pn, tab, seq=tp, batches=bp, block0=0),
          _nat(pn, tab, seq=ts, batches=bs, block0=n_prompt // ts))

    x1, h2, top_i, gates_t, rank_t, cnt = _out_proj(og, on, x, mod3, w_og, w_on, norm2_g[0], rw_t, router_b[0], bot_tok)

    n_rows = n * TOP_K
    assert n_rows % MOE_ROWS == 0
    counts = cnt[:, 0].astype(jnp.int32)
    items, starts = _expert_work_items(counts, n_rows)
    chosen = top_i[None] == jnp.arange(N_EXPERTS, dtype=jnp.int32)[:, None, None]
    dest_t = rank_t + jnp.sum(jnp.where(chosen, starts[:, None, None], 0), axis=0)

    xs = _dispatch(dest_t, h2)
    ys = _experts(items, xs.reshape(n_rows * SUBLANES, LANES), w_gu, b_gu, w_dn, b_dn)
    y_prompt, y_sample = _combine(dest_t, gates_t.T, x1, mod3, final_norm_g,
                                  ys.reshape(n_rows, SUBLANES, LANES), bot_gat, n_prompt)

    return (y_prompt.reshape(bp, tp, d), y_sample.reshape(bs, ts, d))
```

```python
import functools

import numpy as np
import jax
import jax.numpy as jnp
from jax import lax
from jax.experimental import pallas as pl
from jax.experimental.pallas import tpu as pltpu

F32 = jnp.float32
BF16 = jnp.bfloat16
HIGHEST = lax.Precision.HIGHEST

EPS = 1e-5
GRID_W = 64
GLA_HEADS = 4
GLA_DK = 64
GLA_DV = 128
GLA_RANK = 16
GLA_TAU = 16.0
GLA_CHUNK = 64
GLA_QK = GLA_HEADS * GLA_DK
GLA_V = GLA_HEADS * GLA_DV
NAT_HEADS = 16
NAT_HD = 32
NAT_W = NAT_HEADS * NAT_HD
NAT_KH = 8
NAT_KW = 16
NAT_GROUP = 4
N_EXPERTS = 32
TOP_K = 4
SWIGLU_LIMIT = 7.0
SWIGLU_ALPHA = 1.702

LANES = 128
SUBLANES = 8
TOKEN_TILE = 512
MOE_ROWS = 512
EXPERT_ROW_SPLIT = 1
SWIGLU_GROUP = 256
GATHER_TILE = 256
VMEM_LIMIT = 56 * 1024 * 1024


def _cparams(sem, vmem=VMEM_LIMIT):
    return pltpu.CompilerParams(dimension_semantics=sem, vmem_limit_bytes=vmem)


def _dot(a, b):
    return jnp.dot(a, b, preferred_element_type=F32)


def _dot_nt(a, b, precision=None):
    return lax.dot_general(a, b, (((1,), (1,)), ((), ())), preferred_element_type=F32, precision=precision)


LANE_TILES = 8


def _tiled_shape(rows):
    return (rows // SUBLANES, LANE_TILES, SUBLANES, LANES)


def _store_tiled(ref, x, tile0=0):
    tiles = x.shape[0] // SUBLANES
    for lt in range(LANE_TILES):
        ref[tile0:tile0 + tiles, lt] = x[:, lt * LANES:(lt + 1) * LANES].reshape(tiles, SUBLANES, LANES)


def _load_tiled(ref, tile0=0, tiles=None):
    tiles = ref.shape[0] if tiles is None else tiles
    return jnp.concatenate([ref[tile0:tile0 + tiles, lt].reshape(tiles * SUBLANES, LANES) for lt in range(LANE_TILES)],
                           axis=1)


def _tiled_row(ref, r):
    return ref.at[r // SUBLANES, :, r % SUBLANES, :]


def _dot_tn(a, b):
    return lax.dot_general(a, b, (((0,), (0,)), ((), ())), preferred_element_type=F32)


def _ada_kernel(c_ref, w_ref, b_ref, o_ref):
    c = c_ref[...]
    s = c * jax.nn.sigmoid(c)
    o_ref[...] = jnp.dot(s, w_ref[...], preferred_element_type=F32, precision=HIGHEST) + b_ref[...]


def _ada(c, w, b):
    nb, d = c.shape
    cols = w.shape[1]
    blk = 1024
    return pl.pallas_call(
        _ada_kernel,
        grid=(cols // blk,),
        in_specs=[pl.BlockSpec((nb, d), lambda j: (0, 0)),
                  pl.BlockSpec((d, blk), lambda j: (0, j)),
                  pl.BlockSpec((1, blk), lambda j: (0, j))],
        out_specs=pl.BlockSpec((nb, blk), lambda j: (0, j)),
        out_shape=jax.ShapeDtypeStruct((nb, cols), F32),
        compiler_params=_cparams(("arbitrary",)),
    )(c, w, b.reshape(1, cols))


def _stream_specs(block, n_prompt_tiles):
    return (pl.BlockSpec(block, lambda i: (jnp.minimum(i, n_prompt_tiles - 1), 0)),
            pl.BlockSpec(block, lambda i: (jnp.maximum(i - n_prompt_tiles, 0), 0)))


def _per_stream(n_prompt_tiles, body):
    i = pl.program_id(0)
    pl.when(i < n_prompt_tiles)(functools.partial(body, 0))
    pl.when(i >= n_prompt_tiles)(functools.partial(body, 1))


def _in_kernel(xp_ref, xs_ref, mod_ref, g_ref, w_ref, pg_ref, pn_ref, plr_ref, *, n_prompt_tiles):
    def body(stream):
        x = (xp_ref, xs_ref)[stream][...]
        ms = jnp.mean(x * x, axis=-1, keepdims=True)
        y = x * lax.rsqrt(ms + EPS) * g_ref[...]
        h = y * (1.0 + mod_ref[0, 1:2, :]) + mod_ref[0, 0:1, :]
        hb = h.astype(BF16)
        wg = GLA_QK * 2 + GLA_V * 2
        wn = 3 * NAT_W
        pg_ref[...] = _dot(hb, w_ref[:, 0:wg]).astype(BF16)
        pn_ref[...] = _dot(hb, w_ref[:, wg:wg + wn]).astype(BF16)
        plr_ref[...] = _dot(hb, w_ref[:, wg + wn:wg + wn + LANES])

    _per_stream(n_prompt_tiles, body)


def _in_proj(xp, xs, mod3, norm_g, w_all, batch_of_tile):
    d = xp.shape[1]
    n = xp.shape[0] + xs.shape[0]
    wg = GLA_QK * 2 + GLA_V * 2
    wn = 3 * NAT_W
    tm = TOKEN_TILE
    n_prompt_tiles = xp.shape[0] // tm
    return pl.pallas_call(
        functools.partial(_in_kernel, n_prompt_tiles=n_prompt_tiles),
        grid=(n // tm,),
        in_specs=[*_stream_specs((tm, d), n_prompt_tiles),
                  pl.BlockSpec((1, 6, d), lambda i: (batch_of_tile(i), 0, 0)),
                  pl.BlockSpec((1, d), lambda i: (0, 0)),
                  pl.BlockSpec(w_all.shape, lambda i: (0, 0))],
        out_specs=[pl.BlockSpec((tm, wg), lambda i: (i, 0)),
                   pl.BlockSpec((tm, wn), lambda i: (i, 0)),
                   pl.BlockSpec((tm, LANES), lambda i: (i, 0))],
        out_shape=[jax.ShapeDtypeStruct((n, wg), BF16),
                   jax.ShapeDtypeStruct((n, wn), BF16),
                   jax.ShapeDtypeStruct((n, LANES), F32)],
        compiler_params=_cparams(("arbitrary",)),
    )(xp, xs, mod3, norm_g.reshape(1, d), w_all)


def _gla_kernel(q_ref, k_ref, v_ref, g_ref, lr_ref, wz_ref, bz_ref, ng_ref,
                o_ref, la_ref, acc_ref, qt_ref, ke_ref, dec_ref, sf_ref, sb_ref, *, seq):
    c_len = GLA_CHUNK
    n = seq // c_len
    row = lax.broadcasted_iota(jnp.int32, (c_len, c_len), 0)
    col = lax.broadcasted_iota(jnp.int32, (c_len, c_len), 1)
    tril = col <= row
    triu = col >= row
    cum_row = lax.broadcasted_iota(jnp.int32, (c_len, 2 * c_len), 0)
    cum_col = lax.broadcasted_iota(jnp.int32, (c_len, 2 * c_len), 1) % c_len
    cum = (cum_col <= cum_row).astype(F32).astype(BF16)
    mask_f = jnp.concatenate([tril, tril], axis=0)
    mask_b = jnp.concatenate([triu, triu], axis=0)
    head_a = lax.broadcasted_iota(jnp.int32, (c_len, LANES), 1) < GLA_DK
    st_row = lax.broadcasted_iota(jnp.int32, (2 * GLA_DV, LANES), 0)
    st_col = lax.broadcasted_iota(jnp.int32, (2 * GLA_DV, LANES), 1)
    blockdiag = (st_row < GLA_DV) == (st_col < GLA_DK)

    z = _dot(lr_ref[...].astype(BF16), wz_ref[...]) + bz_ref[...]
    la_ref[...] = (jnp.minimum(z, 0.0) - jnp.log(1.0 + jnp.exp(-jnp.abs(z)))) * (1.0 / GLA_TAU)

    def chunk_rows(c):
        return pl.ds(pl.multiple_of(c * c_len, c_len), c_len)

    def stack_heads(x):
        zero = jnp.zeros_like(x)
        return jnp.concatenate([jnp.where(head_a, x, zero), jnp.where(head_a, zero, x)], axis=0).astype(BF16)

    def local(c, carry):
        rows = chunk_rows(c)
        la = la_ref[rows, :]
        hi = la.astype(BF16)
        lo = (la - hi.astype(F32)).astype(BF16)
        binc = _dot(cum, jnp.concatenate([hi, lo], axis=0))
        btot = jnp.sum(la, axis=0, keepdims=True)
        b_f = binc[:, :LANES]
        b_b = btot[:, LANES:] - binc[:, LANES:] + la[:, LANES:]
        e_tot = jnp.exp(btot)
        q = q_ref[rows, :].astype(F32) * (GLA_DK ** -0.5)
        k = k_ref[rows, :].astype(F32)
        qt_f = q * jnp.exp(b_f)
        qt_b = q * jnp.exp(b_b)
        kt_f = k * jnp.exp(-b_f)
        kt_b = k * jnp.exp(-b_b)
        a = (jnp.where(mask_f, _dot_nt(stack_heads(qt_f), kt_f.astype(BF16)), 0.0)
             + jnp.where(mask_b, _dot_nt(stack_heads(qt_b), kt_b.astype(BF16)), 0.0))
        oi = _dot(a.astype(BF16), v_ref[rows, :])
        acc_ref[rows, :] = jnp.concatenate([oi[0:c_len, 0:GLA_DV], oi[c_len:, GLA_DV:]], axis=1)
        qt_ref[rows, :] = jnp.concatenate([qt_f, qt_b], axis=1).astype(BF16)
        ke_ref[rows, :] = jnp.concatenate([kt_f * e_tot[:, :LANES], kt_b * e_tot[:, LANES:]], axis=1).astype(BF16)
        dec_ref[pl.ds(pl.multiple_of(c * SUBLANES, SUBLANES), SUBLANES), :] = jnp.broadcast_to(e_tot, (SUBLANES, 2 * LANES))
        return carry

    def carried(c, s_ref, half):
        rows = chunk_rows(c)
        lanes = slice(half * LANES, (half + 1) * LANES)
        st = s_ref[...]
        acc_ref[rows, :] += _dot_nt(qt_ref[rows, lanes], st.astype(BF16))
        upd = _dot_tn(v_ref[rows, :], ke_ref[rows, lanes])
        dec = dec_ref[pl.ds(pl.multiple_of(c * SUBLANES, SUBLANES), 1), lanes]
        s_ref[...] = st * dec + jnp.where(blockdiag, upd, 0.0)

    def scan(c, carry):
        carried(c, sf_ref, 0)
        carried(n - 1 - c, sb_ref, 1)
        return carry

    def finalize(c, carry):
        rows = chunk_rows(c)
        o = acc_ref[rows, :]
        halves = []
        for hh in range(2):
            oh = o[:, hh * GLA_DV:(hh + 1) * GLA_DV]
            ms = jnp.mean(oh * oh, axis=-1, keepdims=True)
            halves.append(oh * lax.rsqrt(ms + EPS) * ng_ref[...])
        y = jnp.concatenate(halves, axis=1)
        g = g_ref[rows, :].astype(F32)
        o_ref[rows, :] = (y * (g * jax.nn.sigmoid(g))).astype(o_ref.dtype)
        return carry

    sf_ref[...] = jnp.zeros_like(sf_ref)
    sb_ref[...] = jnp.zeros_like(sb_ref)
    lax.fori_loop(0, n, local, 0, unroll=2)
    lax.fori_loop(0, n, scan, 0, unroll=2)
    lax.fori_loop(0, n, finalize, 0, unroll=2)


def _gla(pg, plr, wz, bz, norm_g, *, seq, batches, block0):
    pair_w = 2 * GLA_DK
    pair_v = 2 * GLA_DV
    k0 = GLA_QK // pair_w
    v0 = 2 * GLA_QK // pair_v
    g0 = (2 * GLA_QK + GLA_V) // pair_v
    n_chunks = seq // GLA_CHUNK
    in_specs = [pl.BlockSpec((seq, pair_w), lambda b, p: (block0 + b, p)),
                pl.BlockSpec((seq, pair_w), lambda b, p: (block0 + b, k0 + p)),
                pl.BlockSpec((seq, pair_v), lambda b, p: (block0 + b, v0 + p)),
                pl.BlockSpec((seq, pair_v), lambda b, p: (block0 + b, g0 + p)),
                pl.BlockSpec((seq, LANES), lambda b, p: (block0 + b, 0)),
                pl.BlockSpec((LANES, 2 * pair_w), lambda b, p: (0, p)),
                pl.BlockSpec((1, 2 * pair_w), lambda b, p: (0, p)),
                pl.BlockSpec((1, GLA_DV), lambda b, p: (0, 0))]
    return pl.pallas_call(
        functools.partial(_gla_kernel, seq=seq),
        grid=(batches, GLA_HEADS // 2),
        in_specs=in_specs,
        out_specs=pl.BlockSpec((seq, pair_v), lambda b, p: (b, p)),
        out_shape=jax.ShapeDtypeStruct((batches * seq, GLA_V), BF16),
        scratch_shapes=[pltpu.VMEM((seq, 2 * pair_w), F32),
                        pltpu.VMEM((seq, pair_v), F32),
                        pltpu.VMEM((seq, 2 * pair_w), BF16),
                        pltpu.VMEM((seq, 2 * pair_w), BF16),
                        pltpu.VMEM((n_chunks * SUBLANES, 2 * pair_w), F32),
                        pltpu.VMEM((pair_v, pair_w), F32),
                        pltpu.VMEM((pair_v, pair_w), F32)],
        compiler_params=_cparams(("parallel", "parallel")),
    )(pg, pg, pg, pg, plr, wz, bz, norm_g)


def _nat_kernel(q_ref, k_ref, v_ref, tab_ref, o_ref, *, rows):
    w = GRID_W
    head = lax.broadcasted_iota(jnp.int32, (w, LANES), 1) // NAT_HD
    win = NAT_KH * w

    def body(i, carry):
        r0 = jnp.clip(i - NAT_KH // 2, 0, rows - NAT_KH)
        base = r0 - i + NAT_KH - 1
        qrows = pl.ds(pl.multiple_of(i * w, w), w)
        krows = pl.ds(pl.multiple_of(r0 * w, w), win)
        q = q_ref[qrows, :]
        zero = jnp.zeros_like(q)
        q_stack = jnp.concatenate([jnp.where(head == h, q, zero) for h in range(NAT_GROUP)], axis=0)
        s = _dot_nt(q_stack, k_ref[krows, :]) * (NAT_HD ** -0.5) + tab_ref[0, base]
        m = jnp.max(s, axis=-1, keepdims=True)
        e = jnp.exp(s - m)
        l = jnp.sum(e, axis=-1, keepdims=True)
        o = _dot(e.astype(BF16), v_ref[krows, :]) / l
        out = jnp.zeros((w, LANES), F32)
        for h in range(NAT_GROUP):
            out = out + jnp.where(head == h, o[h * w:(h + 1) * w, :], 0.0)
        o_ref[qrows, :] = out.astype(o_ref.dtype)
        return carry

    lax.fori_loop(0, rows, body, 0, unroll=2)


def _nat(pn, tab, *, seq, batches, block0):
    rows = seq // GRID_W
    groups = NAT_HEADS // NAT_GROUP
    in_specs = [pl.BlockSpec((seq, LANES), lambda g, b: (block0 + b, g)),
                pl.BlockSpec((seq, LANES), lambda g, b: (block0 + b, groups + g)),
                pl.BlockSpec((seq, LANES), lambda g, b: (block0 + b, 2 * groups + g)),
                pl.BlockSpec((1,) + tab.shape[1:], lambda g, b: (g, 0, 0, 0))]
    return pl.pallas_call(
        functools.partial(_nat_kernel, rows=rows),
        grid=(groups, batches),
        in_specs=in_specs,
        out_specs=pl.BlockSpec((seq, LANES), lambda g, b: (b, g)),
        out_shape=jax.ShapeDtypeStruct((batches * seq, NAT_W), BF16),
        compiler_params=_cparams(("parallel", "parallel")),
    )(pn, pn, pn, tab)


def _nat_bias_table(rpb):
    w = GRID_W
    jq = np.arange(w)[:, None]
    jk = np.arange(w)[None, :]
    c0 = np.clip(jq - NAT_KW // 2, 0, w - NAT_KW)
    valid = (jk >= c0) & (jk < c0 + NAT_KW)
    dcol = np.clip(jk - jq + NAT_KW - 1, 0, 2 * NAT_KW - 2)
    drow = np.arange(NAT_KH)[:, None] + np.arange(NAT_KH)[None, :]
    t = rpb.astype(F32)[:, drow][:, :, :, dcol]
    t = jnp.where(valid[None, None, None], t, -jnp.inf)
    t = t.transpose(0, 1, 3, 2, 4).reshape(NAT_HEADS, NAT_KH, w, NAT_KH * w)
    t = t.reshape(NAT_HEADS // NAT_GROUP, NAT_GROUP, NAT_KH, w, NAT_KH * w).transpose(0, 2, 1, 3, 4)
    return t.reshape(NAT_HEADS // NAT_GROUP, NAT_KH, NAT_GROUP * w, NAT_KH * w)


def _out_kernel(ogp_ref, ogs_ref, onp_ref, ons_ref, xp_ref, xs_ref, mod_ref, wg_ref, wn_ref, n2_ref, rw_ref, rb_ref,
                x1_ref, h2_ref, ti_ref, gt_ref, rk_ref, cnt_ref, carry_ref, *, n_prompt_tiles):
    tm = x1_ref.shape[0]

    @pl.when(pl.program_id(0) == 0)
    def _():
        carry_ref[...] = jnp.zeros_like(carry_ref)

    def residual(stream):
        og_ref, on_ref, x_ref = ((ogp_ref, onp_ref, xp_ref), (ogs_ref, ons_ref, xs_ref))[stream]
        mix = _dot(og_ref[...], wg_ref[...]) + _dot(on_ref[...], wn_ref[...])
        x1_ref[...] = x_ref[...] + mod_ref[0, 2:3, :] * mix

    _per_stream(n_prompt_tiles, residual)
    x1 = x1_ref[...]
    ms = jnp.mean(x1 * x1, axis=-1, keepdims=True)
    h2 = x1 * lax.rsqrt(ms + EPS) * n2_ref[...]
    h2 = h2 * (1.0 + mod_ref[0, 4:5, :]) + mod_ref[0, 3:4, :]
    _store_tiled(h2_ref, h2)

    logits = _dot_nt(rw_ref[...], h2, precision=HIGHEST) + rb_ref[...]
    eidx = lax.broadcasted_iota(jnp.int32, logits.shape, 0)
    vals = logits
    top_v, top_i = [], []
    for _ in range(TOP_K):
        m = jnp.max(vals, axis=0, keepdims=True)
        idx = jnp.min(jnp.where(vals == m, eidx, N_EXPERTS), axis=0, keepdims=True)
        top_v.append(m)
        top_i.append(idx)
        vals = jnp.where(eidx == idx, -jnp.inf, vals)
    ev = [jnp.exp(v - top_v[0]) for v in top_v]
    den = ev[0] + ev[1] + ev[2] + ev[3]
    ti_ref[...] = jnp.concatenate(top_i, axis=0)
    gt_ref[...] = jnp.concatenate([e / den for e in ev], axis=0)

    onehot = jnp.zeros(logits.shape, F32)
    for idx in top_i:
        onehot = onehot + (eidx == idx).astype(F32)
    r = lax.broadcasted_iota(jnp.int32, (tm, tm), 0)
    c = lax.broadcasted_iota(jnp.int32, (tm, tm), 1)
    earlier = (r < c).astype(BF16)
    before = carry_ref[...] + _dot(onehot.astype(BF16), earlier)
    rk_ref[...] = jnp.concatenate(
        [jnp.sum(jnp.where(eidx == idx, before, 0.0), axis=0, keepdims=True) for idx in top_i], axis=0).astype(jnp.int32)
    carry_ref[...] = carry_ref[...] + jnp.sum(onehot, axis=1, keepdims=True)
    cnt_ref[...] = jnp.broadcast_to(carry_ref[...], cnt_ref.shape)


def _out_proj(og, on, x, mod3, w_g, w_n, norm_g, rw_t, rb, batch_of_tile):
    d = x[0].shape[1]
    n = x[0].shape[0] + x[1].shape[0]
    tm = TOKEN_TILE
    n_prompt_tiles = x[0].shape[0] // tm
    tok = lambda i: (i, 0)
    const = lambda i: (0, 0)
    return pl.pallas_call(
        functools.partial(_out_kernel, n_prompt_tiles=n_prompt_tiles),
        grid=(n // tm,),
        in_specs=[*_stream_specs((tm, GLA_V), n_prompt_tiles),
                  *_stream_specs((tm, NAT_W), n_prompt_tiles),
                  *_stream_specs((tm, d), n_prompt_tiles),
                  pl.BlockSpec((1, 6, d), lambda i: (batch_of_tile(i), 0, 0)),
                  pl.BlockSpec(w_g.shape, const),
                  pl.BlockSpec(w_n.shape, const),
                  pl.BlockSpec((1, d), const),
                  pl.BlockSpec(rw_t.shape, const),
                  pl.BlockSpec((N_EXPERTS, 1), const)],
        out_specs=[pl.BlockSpec((tm, d), tok),
                   pl.BlockSpec(_tiled_shape(tm), lambda i: (i, 0, 0, 0)),
                   pl.BlockSpec((TOP_K, tm), lambda i: (0, i)),
                   pl.BlockSpec((TOP_K, tm), lambda i: (0, i)),
                   pl.BlockSpec((TOP_K, tm), lambda i: (0, i)),
                   pl.BlockSpec((N_EXPERTS, LANES), const)],
        out_shape=[jax.ShapeDtypeStruct((n, d), F32),
                   jax.ShapeDtypeStruct(_tiled_shape(n), F32),
                   jax.ShapeDtypeStruct((TOP_K, n), jnp.int32),
                   jax.ShapeDtypeStruct((TOP_K, n), F32),
                   jax.ShapeDtypeStruct((TOP_K, n), jnp.int32),
                   jax.ShapeDtypeStruct((N_EXPERTS, LANES), F32)],
        scratch_shapes=[pltpu.VMEM((N_EXPERTS, 1), F32)],
        compiler_params=_cparams(("arbitrary",)),
    )(*og, *on, *x, mod3, w_g, w_n, norm_g.reshape(1, d), rw_t, rb.reshape(N_EXPERTS, 1))


def _row_copy(src, dst, sem):
    return pltpu.make_async_copy(src, dst, sem)


def _dispatch_kernel(dest_ref, h_ref, xs_hbm, sem):
    tt = dest_ref.shape[1]

    def issue(t, carry):
        for k in range(TOP_K):
            _row_copy(_tiled_row(h_ref, t), _tiled_row(xs_hbm, dest_ref[k, t]), sem).start()
        return carry

    def drain(t, carry):
        for k in range(TOP_K):
            _row_copy(_tiled_row(h_ref, t), _tiled_row(xs_hbm, dest_ref[k, t]), sem).wait()
        return carry

    lax.fori_loop(0, tt, issue, 0)
    lax.fori_loop(0, tt, drain, 0)


def _dispatch(dest_t, h2):
    n = h2.shape[0] * SUBLANES
    tt = GATHER_TILE
    return pl.pallas_call(
        _dispatch_kernel,
        grid=(n // tt,),
        in_specs=[pl.BlockSpec((TOP_K, tt), lambda i: (0, i), memory_space=pltpu.SMEM),
                  pl.BlockSpec(_tiled_shape(tt), lambda i: (i, 0, 0, 0))],
        out_specs=pl.BlockSpec(memory_space=pl.ANY),
        out_shape=jax.ShapeDtypeStruct(_tiled_shape(n * TOP_K), F32),
        scratch_shapes=[pltpu.SemaphoreType.DMA(())],
        compiler_params=_cparams(("arbitrary",)),
    )(dest_t, h2)


def _swiglu_col(feature0, up):
    group, within = divmod(feature0, SWIGLU_GROUP)
    return (2 * group + up) * SWIGLU_GROUP + within


def _deinterleave_kernel(w_ref, o_ref):
    f2 = w_ref.shape[2]
    slab = 2 * LANES
    r = lax.broadcasted_iota(jnp.int32, (slab, slab), 0)
    c = lax.broadcasted_iota(jnp.int32, (slab, slab), 1)
    perm = (r == jnp.where(c < LANES, 2 * c, 2 * (c - LANES) + 1)).astype(BF16)
    for j in range(f2 // slab):
        y = _dot(w_ref[0, :, j * slab:(j + 1) * slab].astype(BF16), perm)
        for up in range(2):
            c0 = _swiglu_col(j * LANES, up)
            o_ref[0, :, c0:c0 + LANES] = y[:, up * LANES:(up + 1) * LANES].astype(BF16)


def _deinterleave(w):
    e, d, f2 = w.shape
    return pl.pallas_call(
        _deinterleave_kernel,
        grid=(e,),
        in_specs=[pl.BlockSpec((1, d, f2), lambda i: (i, 0, 0))],
        out_specs=pl.BlockSpec((1, d, f2), lambda i: (i, 0, 0)),
        out_shape=jax.ShapeDtypeStruct((e, d, f2), BF16),
        compiler_params=_cparams(("parallel",)),
    )(w)


def _expert_kernel(blk_ref, exp_ref, lo_ref, hi_ref, xs_ref, wgu_ref, bgu_ref, wd_ref, bd_ref, ys_ref):
    del blk_ref, exp_ref
    rows = xs_ref.shape[0] * SUBLANES
    f = wd_ref.shape[1]
    i = pl.program_id(0)
    lo = lo_ref[i]
    hi = hi_ref[i]

    @pl.when(jnp.logical_and(hi > lo, lo == 0))
    def _():
        ys_ref[...] = jnp.zeros_like(ys_ref)

    @pl.when(hi > lo)
    def _():
        g = SWIGLU_GROUP
        part = rows // EXPERT_ROW_SPLIT
        for p in range(EXPERT_ROW_SPLIT):
            tile0, tiles = p * part // SUBLANES, part // SUBLANES
            x = _load_tiled(xs_ref, tile0, tiles).astype(BF16)
            acts = []
            for j in range(f // g):
                gu = _dot(x, wgu_ref[0, :, 2 * j * g:2 * (j + 1) * g]) + bgu_ref[0, :, 2 * j * g:2 * (j + 1) * g]
                gate = jnp.minimum(gu[:, :g], SWIGLU_LIMIT)
                up = jnp.clip(gu[:, g:], -SWIGLU_LIMIT, SWIGLU_LIMIT)
                acts.append(((up + 1.0) * gate * jax.nn.sigmoid(SWIGLU_ALPHA * gate)).astype(BF16))
            y = _dot(jnp.concatenate(acts, axis=1), wd_ref[0]) + bd_ref[0]
            r = lax.broadcasted_iota(jnp.int32, y.shape, 0) + p * part
            mine = (r >= lo) & (r < hi)
            _store_tiled(ys_ref, jnp.where(mine, y, _load_tiled(ys_ref, tile0, tiles)), tile0)


def _experts(items, xs2, w_gu, b_gu, w_down, b_down):
    item_blk, item_exp, item_lo, item_hi = items
    d, f2 = w_gu.shape[1], w_gu.shape[2]
    wmap = lambda i, blk, exp, lo, hi: (exp[i], 0, 0)
    xmap = lambda i, blk, exp, lo, hi: (blk[i], 0, 0, 0)
    grid_spec = pltpu.PrefetchScalarGridSpec(
        num_scalar_prefetch=4,
        grid=(item_blk.shape[0],),
        in_specs=[pl.BlockSpec(_tiled_shape(MOE_ROWS), xmap),
                  pl.BlockSpec((1, d, f2), wmap),
                  pl.BlockSpec((1, 1, f2), wmap),
                  pl.BlockSpec((1, f2 // 2, d), wmap),
                  pl.BlockSpec((1, 1, d), wmap)],
        out_specs=pl.BlockSpec(_tiled_shape(MOE_ROWS), xmap),
    )
    return pl.pallas_call(
        _expert_kernel,
        grid_spec=grid_spec,
        out_shape=jax.ShapeDtypeStruct(xs2.shape, F32),
        compiler_params=_cparams(("arbitrary",)),
    )(item_blk, item_exp, item_lo, item_hi, xs2, w_gu, b_gu, w_down, b_down)


def _expert_work_items(counts, n_rows):
    ends = jnp.cumsum(counts)
    starts = ends - counts
    n_blk = n_rows // MOE_ROWS
    cuts = jnp.sort(jnp.concatenate([jnp.arange(n_blk, dtype=jnp.int32) * MOE_ROWS, starts[1:]]))
    nxt = jnp.concatenate([cuts[1:], jnp.full((1,), n_rows, jnp.int32)])
    blk = jnp.minimum(cuts // MOE_ROWS, n_blk - 1)
    exp = jnp.minimum(jnp.sum(ends[None, :] <= cuts[:, None], axis=1), N_EXPERTS - 1).astype(jnp.int32)
    lo = cuts - blk * MOE_ROWS
    hi = nxt - blk * MOE_ROWS
    return (blk.astype(jnp.int32), exp, lo.astype(jnp.int32), hi.astype(jnp.int32)), starts


def _combine_kernel(dest_ref, gates_ref, x1_ref, mod_ref, fg_ref, ys_hbm, op_ref, os_ref, rows_ref, sem, *,
                    n_prompt_tiles):
    tt = x1_ref.shape[0]

    def slot(k, t):
        return _tiled_row(rows_ref, k * tt + t)

    def issue(t, carry):
        for k in range(TOP_K):
            _row_copy(_tiled_row(ys_hbm, dest_ref[k, t]), slot(k, t), sem).start()
        return carry

    def drain(t, carry):
        for k in range(TOP_K):
            _row_copy(_tiled_row(ys_hbm, dest_ref[k, t]), slot(k, t), sem).wait()
        return carry

    lax.fori_loop(0, tt, issue, 0)
    lax.fori_loop(0, tt, drain, 0)

    gates = gates_ref[...]
    y = jnp.zeros(x1_ref.shape, F32)
    for k in range(TOP_K):
        yk = _load_tiled(rows_ref, k * tt // SUBLANES, tt // SUBLANES)
        y = y + gates[:, k:k + 1] * yk
    x2 = x1_ref[...] + mod_ref[0, 5:6, :] * y
    ms = jnp.mean(x2 * x2, axis=-1, keepdims=True)
    out = x2 * lax.rsqrt(ms + EPS) * fg_ref[...]

    def write(stream):
        (op_ref, os_ref)[stream][...] = out

    _per_stream(n_prompt_tiles, write)


def _combine(dest_t, gates, x1, mod3, final_g, ys3, batch_of_tile, n_prompt):
    n, d = x1.shape
    tt = GATHER_TILE
    n_prompt_tiles = n_prompt // tt
    return pl.pallas_call(
        functools.partial(_combine_kernel, n_prompt_tiles=n_prompt_tiles),
        grid=(n // tt,),
        in_specs=[pl.BlockSpec((TOP_K, tt), lambda i: (0, i), memory_space=pltpu.SMEM),
                  pl.BlockSpec((tt, TOP_K), lambda i: (i, 0)),
                  pl.BlockSpec((tt, d), lambda i: (i, 0)),
                  pl.BlockSpec((1, 6, d), lambda i: (batch_of_tile(i), 0, 0)),
                  pl.BlockSpec((1, d), lambda i: (0, 0)),
                  pl.BlockSpec(memory_space=pl.ANY)],
        out_specs=list(_stream_specs((tt, d), n_prompt_tiles)),
        out_shape=[jax.ShapeDtypeStruct((n_prompt, d), F32),
                   jax.ShapeDtypeStruct((n - n_prompt, d), F32)],
        scratch_shapes=[pltpu.VMEM(_tiled_shape(TOP_K * tt), F32),
                        pltpu.SemaphoreType.DMA(())],
        compiler_params=_cparams(("arbitrary",)),
    )(dest_t, gates, x1, mod3, final_g.reshape(1, d), ys3)


def _batch_of_tile_fn(tile, bp, tp, ts):
    n_prompt = bp * tp

    def batch_of_tile(i):
        t = i * tile
        return jnp.where(t < n_prompt, t // tp, bp + (t - n_prompt) // ts)

    return batch_of_tile


def kernel(x_prompt, x_sample, c_prompt, c_sample, norm1_g, w_ada, b_ada, w_in, gla_w2_fwd, gla_b2_fwd, gla_w2_bwd, gla_b2_bwd, gla_norm_g, nat_rpb, w_out, norm2_g, router_w, router_b, w_gate_up, b_gate_up, w_down, b_down, final_norm_g):
    assert w_ada.shape[0] == 1, "single-layer encoder"
    bp, tp, d = x_prompt.shape
    bs, ts, _ = x_sample.shape
    n_prompt, n_sample = bp * tp, bs * ts
    n = n_prompt + n_sample
    assert tp % TOKEN_TILE == 0 and ts % TOKEN_TILE == 0 and n_prompt % ts == 0
    assert d == LANE_TILES * LANES

    x = (x_prompt.reshape(n_prompt, d), x_sample.reshape(n_sample, d))
    c = jnp.concatenate([c_prompt, c_sample], axis=0)

    sizes = (GLA_QK, GLA_QK, GLA_V, GLA_V, GLA_RANK, GLA_RANK, NAT_W, NAT_W, NAT_W)
    offs = np.concatenate([[0], np.cumsum(sizes)])
    w_in0 = w_in[0]
    seg = lambda j: w_in0[:, offs[j]:offs[j + 1]]
    w_all = jnp.concatenate(
        [seg(0), seg(1), seg(2), seg(3), seg(6), seg(7), seg(8), seg(4), seg(5),
         jnp.zeros((d, LANES - 2 * GLA_RANK), F32)], axis=1).astype(BF16)
    pairs = GLA_HEADS // 2
    w2 = jnp.zeros((LANES, pairs, 2, 2 * GLA_DK), F32)
    w2 = w2.at[0:GLA_RANK, :, 0].set(gla_w2_fwd[0].reshape(GLA_RANK, pairs, 2 * GLA_DK))
    w2 = w2.at[GLA_RANK:2 * GLA_RANK, :, 1].set(gla_w2_bwd[0].reshape(GLA_RANK, pairs, 2 * GLA_DK))
    wz = w2.reshape(LANES, pairs * 4 * GLA_DK).astype(BF16)
    bz = jnp.stack([gla_b2_fwd[0].reshape(pairs, 2 * GLA_DK), gla_b2_bwd[0].reshape(pairs, 2 * GLA_DK)],
                   axis=1).reshape(1, pairs * 4 * GLA_DK)
    tab = _nat_bias_table(nat_rpb[0])
    w_og = w_out[0, :GLA_V].astype(BF16)
    w_on = w_out[0, GLA_V:].astype(BF16)
    rw_t = router_w[0].T
    w_gu = _deinterleave(w_gate_up[0])
    n_e, f2 = b_gate_up.shape[1:]
    b_gu = (b_gate_up[0].reshape(n_e, f2 // (2 * SWIGLU_GROUP), SWIGLU_GROUP, 2)
            .transpose(0, 1, 3, 2).reshape(n_e, 1, f2))
    w_dn = w_down[0].astype(BF16)
    b_dn = b_down[0, :, None, :]

    mod3 = _ada(c, w_ada[0], b_ada[0]).reshape(bp + bs, 6, d)
    bot_tok = _batch_of_tile_fn(TOKEN_TILE, bp, tp, ts)
    bot_gat = _batch_of_tile_fn(GATHER_TILE, bp, tp, ts)

    pg, pn, plr = _in_proj(*x, mod3, norm1_g[0], w_all, bot_tok)

    gla_args = (pg, plr, wz, bz, gla_norm_g[0].reshape(1, -1))
    og = (_gla(*gla_args, seq=tp, batches=bp, block0=0),
          _gla(*gla_args, seq=ts, batches=bs, block0=n_prompt // ts))
    on = (_nat(pn, tab, seq=tp, batches=bp, block0=0),
          _nat(pn, tab, seq=ts, batches=bs, block0=n_prompt // ts))

    x1, h2, top_i, gates_t, rank_t, cnt = _out_proj(og, on, x, mod3, w_og, w_on, norm2_g[0], rw_t, router_b[0], bot_tok)

    n_rows = n * TOP_K
    assert n_rows % MOE_ROWS == 0
    counts = cnt[:, 0].astype(jnp.int32)
    items, starts = _expert_work_items(counts, n_rows)
    chosen = top_i[None] == jnp.arange(N_EXPERTS, dtype=jnp.int32)[:, None, None]
    dest_t = rank_t + jnp.sum(jnp.where(chosen, starts[:, None, None], 0), axis=0)

    xs = _dispatch(dest_t, h2)
    ys = _experts(items, xs, w_gu, b_gu, w_dn, b_dn)
    y_prompt, y_sample = _combine(dest_t, gates_t.T, x1, mod3, final_norm_g, ys, bot_gat, n_prompt)

    return (y_prompt.reshape(bp, tp, d), y_sample.reshape(bs, ts, d))
```

```python
import functools

import numpy as np
import jax
import jax.numpy as jnp
from jax import lax
from jax.experimental import pallas as pl
from jax.experimental.pallas import tpu as pltpu

F32 = jnp.float32
BF16 = jnp.bfloat16
HIGHEST = lax.Precision.HIGHEST

EPS = 1e-5
GRID_W = 64
GLA_HEADS = 4
GLA_DK = 64
GLA_DV = 128
GLA_RANK = 16
GLA_TAU = 16.0
GLA_CHUNK = 64
GLA_QK = GLA_HEADS * GLA_DK
GLA_V = GLA_HEADS * GLA_DV
NAT_HEADS = 16
NAT_HD = 32
NAT_W = NAT_HEADS * NAT_HD
NAT_KH = 8
NAT_KW = 16
NAT_GROUP = 4
N_EXPERTS = 32
TOP_K = 4
SWIGLU_LIMIT = 7.0
SWIGLU_ALPHA = 1.702

LANES = 128
SUBLANES = 8
TOKEN_TILE = 512
MOE_ROWS = 512
SWIGLU_GROUP = 256
GATHER_TILE = 256
VMEM_LIMIT = 56 * 1024 * 1024


def _cparams(sem, vmem=VMEM_LIMIT):
    return pltpu.CompilerParams(dimension_semantics=sem, vmem_limit_bytes=vmem)


def _dot(a, b):
    return jnp.dot(a, b, preferred_element_type=F32)


def _dot_nt(a, b, precision=None):
    return lax.dot_general(a, b, (((1,), (1,)), ((), ())), preferred_element_type=F32, precision=precision)


LANE_TILES = 8


def _row_table_shape(rows):
    return (rows * LANE_TILES, LANES)


def _store_rows(ref, x, row0=0):
    rows = x.shape[0]
    for s in range(LANE_TILES):
        ref[pl.ds(row0 * LANE_TILES + s, rows, stride=LANE_TILES), :] = x[:, s * LANES:(s + 1) * LANES]


def _load_rows(ref, row0, rows):
    return jnp.concatenate(
        [ref[pl.ds(row0 * LANE_TILES + s, rows, stride=LANE_TILES), :] for s in range(LANE_TILES)], axis=1)


def _vmem_row(ref, r):
    return ref.at[pl.ds(pl.multiple_of(r * LANE_TILES, LANE_TILES), LANE_TILES), :]


def _dot_tn(a, b):
    return lax.dot_general(a, b, (((0,), (0,)), ((), ())), preferred_element_type=F32)


def _ada_kernel(c_ref, w_ref, b_ref, o_ref):
    c = c_ref[...]
    s = c * jax.nn.sigmoid(c)
    o_ref[...] = jnp.dot(s, w_ref[...], preferred_element_type=F32, precision=HIGHEST) + b_ref[...]


def _ada(c, w, b):
    nb, d = c.shape
    cols = w.shape[1]
    blk = 1024
    return pl.pallas_call(
        _ada_kernel,
        grid=(cols // blk,),
        in_specs=[pl.BlockSpec((nb, d), lambda j: (0, 0)),
                  pl.BlockSpec((d, blk), lambda j: (0, j)),
                  pl.BlockSpec((1, blk), lambda j: (0, j))],
        out_specs=pl.BlockSpec((nb, blk), lambda j: (0, j)),
        out_shape=jax.ShapeDtypeStruct((nb, cols), F32),
        compiler_params=_cparams(("arbitrary",)),
    )(c, w, b.reshape(1, cols))


def _stream_specs(block, n_prompt_tiles):
    return (pl.BlockSpec(block, lambda i: (jnp.minimum(i, n_prompt_tiles - 1), 0)),
            pl.BlockSpec(block, lambda i: (jnp.maximum(i - n_prompt_tiles, 0), 0)))


def _per_stream(n_prompt_tiles, body):
    i = pl.program_id(0)
    pl.when(i < n_prompt_tiles)(functools.partial(body, 0))
    pl.when(i >= n_prompt_tiles)(functools.partial(body, 1))


def _in_kernel(xp_ref, xs_ref, mod_ref, g_ref, w_ref, pg_ref, pn_ref, plr_ref, *, n_prompt_tiles):
    def body(stream):
        x = (xp_ref, xs_ref)[stream][...]
        ms = jnp.mean(x * x, axis=-1, keepdims=True)
        y = x * lax.rsqrt(ms + EPS) * g_ref[...]
        h = y * (1.0 + mod_ref[0, 1:2, :]) + mod_ref[0, 0:1, :]
        hb = h.astype(BF16)
        wg = GLA_QK * 2 + GLA_V * 2
        wn = 3 * NAT_W
        pg_ref[...] = _dot(hb, w_ref[:, 0:wg]).astype(BF16)
        pn_ref[...] = _dot(hb, w_ref[:, wg:wg + wn]).astype(BF16)
        plr_ref[...] = _dot(hb, w_ref[:, wg + wn:wg + wn + LANES])

    _per_stream(n_prompt_tiles, body)


def _in_proj(xp, xs, mod3, norm_g, w_all, batch_of_tile):
    d = xp.shape[1]
    n = xp.shape[0] + xs.shape[0]
    wg = GLA_QK * 2 + GLA_V * 2
    wn = 3 * NAT_W
    tm = TOKEN_TILE
    n_prompt_tiles = xp.shape[0] // tm
    return pl.pallas_call(
        functools.partial(_in_kernel, n_prompt_tiles=n_prompt_tiles),
        grid=(n // tm,),
        in_specs=[*_stream_specs((tm, d), n_prompt_tiles),
                  pl.BlockSpec((1, 6, d), lambda i: (batch_of_tile(i), 0, 0)),
                  pl.BlockSpec((1, d), lambda i: (0, 0)),
                  pl.BlockSpec(w_all.shape, lambda i: (0, 0))],
        out_specs=[pl.BlockSpec((tm, wg), lambda i: (i, 0)),
                   pl.BlockSpec((tm, wn), lambda i: (i, 0)),
                   pl.BlockSpec((tm, LANES), lambda i: (i, 0))],
        out_shape=[jax.ShapeDtypeStruct((n, wg), BF16),
                   jax.ShapeDtypeStruct((n, wn), BF16),
                   jax.ShapeDtypeStruct((n, LANES), F32)],
        compiler_params=_cparams(("arbitrary",)),
    )(xp, xs, mod3, norm_g.reshape(1, d), w_all)


def _gla_kernel(q_ref, k_ref, v_ref, g_ref, lr_ref, wz_ref, bz_ref, ng_ref,
                o_ref, la_ref, acc_ref, qt_ref, ke_ref, dec_ref, sf_ref, sb_ref, *, seq):
    c_len = GLA_CHUNK
    n = seq // c_len
    row = lax.broadcasted_iota(jnp.int32, (c_len, c_len), 0)
    col = lax.broadcasted_iota(jnp.int32, (c_len, c_len), 1)
    tril = col <= row
    triu = col >= row
    cum_row = lax.broadcasted_iota(jnp.int32, (c_len, 2 * c_len), 0)
    cum_col = lax.broadcasted_iota(jnp.int32, (c_len, 2 * c_len), 1) % c_len
    cum = (cum_col <= cum_row).astype(F32).astype(BF16)
    mask_f = jnp.concatenate([tril, tril], axis=0)
    mask_b = jnp.concatenate([triu, triu], axis=0)
    head_a = lax.broadcasted_iota(jnp.int32, (c_len, LANES), 1) < GLA_DK
    st_row = lax.broadcasted_iota(jnp.int32, (2 * GLA_DV, LANES), 0)
    st_col = lax.broadcasted_iota(jnp.int32, (2 * GLA_DV, LANES), 1)
    blockdiag = (st_row < GLA_DV) == (st_col < GLA_DK)

    z = _dot(lr_ref[...].astype(BF16), wz_ref[...]) + bz_ref[...]
    la_ref[...] = (jnp.minimum(z, 0.0) - jnp.log(1.0 + jnp.exp(-jnp.abs(z)))) * (1.0 / GLA_TAU)

    def chunk_rows(c):
        return pl.ds(pl.multiple_of(c * c_len, c_len), c_len)

    def stack_heads(x):
        zero = jnp.zeros_like(x)
        return jnp.concatenate([jnp.where(head_a, x, zero), jnp.where(head_a, zero, x)], axis=0).astype(BF16)

    def local(c, carry):
        rows = chunk_rows(c)
        la = la_ref[rows, :]
        hi = la.astype(BF16)
        lo = (la - hi.astype(F32)).astype(BF16)
        binc = _dot(cum, jnp.concatenate([hi, lo], axis=0))
        btot = jnp.sum(la, axis=0, keepdims=True)
        b_f = binc[:, :LANES]
        b_b = btot[:, LANES:] - binc[:, LANES:] + la[:, LANES:]
        e_tot = jnp.exp(btot)
        q = q_ref[rows, :].astype(F32) * (GLA_DK ** -0.5)
        k = k_ref[rows, :].astype(F32)
        qt_f = q * jnp.exp(b_f)
        qt_b = q * jnp.exp(b_b)
        kt_f = k * jnp.exp(-b_f)
        kt_b = k * jnp.exp(-b_b)
        a = (jnp.where(mask_f, _dot_nt(stack_heads(qt_f), kt_f.astype(BF16)), 0.0)
             + jnp.where(mask_b, _dot_nt(stack_heads(qt_b), kt_b.astype(BF16)), 0.0))
        oi = _dot(a.astype(BF16), v_ref[rows, :])
        acc_ref[rows, :] = jnp.concatenate([oi[0:c_len, 0:GLA_DV], oi[c_len:, GLA_DV:]], axis=1)
        qt_ref[rows, :] = jnp.concatenate([qt_f, qt_b], axis=1).astype(BF16)
        ke_ref[rows, :] = jnp.concatenate([kt_f * e_tot[:, :LANES], kt_b * e_tot[:, LANES:]], axis=1).astype(BF16)
        dec_ref[pl.ds(pl.multiple_of(c * SUBLANES, SUBLANES), SUBLANES), :] = jnp.broadcast_to(e_tot, (SUBLANES, 2 * LANES))
        return carry

    def carried(c, s_ref, half):
        rows = chunk_rows(c)
        lanes = slice(half * LANES, (half + 1) * LANES)
        st = s_ref[...]
        acc_ref[rows, :] += _dot_nt(qt_ref[rows, lanes], st.astype(BF16))
        upd = _dot_tn(v_ref[rows, :], ke_ref[rows, lanes])
        dec = dec_ref[pl.ds(pl.multiple_of(c * SUBLANES, SUBLANES), 1), lanes]
        s_ref[...] = st * dec + jnp.where(blockdiag, upd, 0.0)

    def scan(c, carry):
        carried(c, sf_ref, 0)
        carried(n - 1 - c, sb_ref, 1)
        return carry

    def finalize(c, carry):
        rows = chunk_rows(c)
        o = acc_ref[rows, :]
        halves = []
        for hh in range(2):
            oh = o[:, hh * GLA_DV:(hh + 1) * GLA_DV]
            ms = jnp.mean(oh * oh, axis=-1, keepdims=True)
            halves.append(oh * lax.rsqrt(ms + EPS) * ng_ref[...])
        y = jnp.concatenate(halves, axis=1)
        g = g_ref[rows, :].astype(F32)
        o_ref[rows, :] = (y * (g * jax.nn.sigmoid(g))).astype(o_ref.dtype)
        return carry

    sf_ref[...] = jnp.zeros_like(sf_ref)
    sb_ref[...] = jnp.zeros_like(sb_ref)
    lax.fori_loop(0, n, local, 0, unroll=2)
    lax.fori_loop(0, n, scan, 0, unroll=2)
    lax.fori_loop(0, n, finalize, 0, unroll=2)


def _gla(pg, plr, wz, bz, norm_g, *, seq, batches, block0):
    pair_w = 2 * GLA_DK
    pair_v = 2 * GLA_DV
    k0 = GLA_QK // pair_w
    v0 = 2 * GLA_QK // pair_v
    g0 = (2 * GLA_QK + GLA_V) // pair_v
    n_chunks = seq // GLA_CHUNK
    in_specs = [pl.BlockSpec((seq, pair_w), lambda b, p: (block0 + b, p)),
                pl.BlockSpec((seq, pair_w), lambda b, p: (block0 + b, k0 + p)),
                pl.BlockSpec((seq, pair_v), lambda b, p: (block0 + b, v0 + p)),
                pl.BlockSpec((seq, pair_v), lambda b, p: (block0 + b, g0 + p)),
                pl.BlockSpec((seq, LANES), lambda b, p: (block0 + b, 0)),
                pl.BlockSpec((LANES, 2 * pair_w), lambda b, p: (0, p)),
                pl.BlockSpec((1, 2 * pair_w), lambda b, p: (0, p)),
                pl.BlockSpec((1, GLA_DV), lambda b, p: (0, 0))]
    return pl.pallas_call(
        functools.partial(_gla_kernel, seq=seq),
        grid=(batches, GLA_HEADS // 2),
        in_specs=in_specs,
        out_specs=pl.BlockSpec((seq, pair_v), lambda b, p: (b, p)),
        out_shape=jax.ShapeDtypeStruct((batches * seq, GLA_V), BF16),
        scratch_shapes=[pltpu.VMEM((seq, 2 * pair_w), F32),
                        pltpu.VMEM((seq, pair_v), F32),
                        pltpu.VMEM((seq, 2 * pair_w), BF16),
                        pltpu.VMEM((seq, 2 * pair_w), BF16),
                        pltpu.VMEM((n_chunks * SUBLANES, 2 * pair_w), F32),
                        pltpu.VMEM((pair_v, pair_w), F32),
                        pltpu.VMEM((pair_v, pair_w), F32)],
        compiler_params=_cparams(("parallel", "parallel")),
    )(pg, pg, pg, pg, plr, wz, bz, norm_g)


def _nat_kernel(q_ref, k_ref, v_ref, tab_ref, o_ref, *, rows):
    w = GRID_W
    head = lax.broadcasted_iota(jnp.int32, (w, LANES), 1) // NAT_HD
    win = NAT_KH * w

    def body(i, carry):
        r0 = jnp.clip(i - NAT_KH // 2, 0, rows - NAT_KH)
        base = r0 - i + NAT_KH - 1
        qrows = pl.ds(pl.multiple_of(i * w, w), w)
        krows = pl.ds(pl.multiple_of(r0 * w, w), win)
        q = q_ref[qrows, :]
        zero = jnp.zeros_like(q)
        q_stack = jnp.concatenate([jnp.where(head == h, q, zero) for h in range(NAT_GROUP)], axis=0)
        s = _dot_nt(q_stack, k_ref[krows, :]) + tab_ref[0, base]
        m = jnp.max(s, axis=-1, keepdims=True)
        e = jnp.exp(s - m)
        l = jnp.sum(e, axis=-1, keepdims=True)
        o = _dot(e.astype(BF16), v_ref[krows, :]) / l
        out = jnp.zeros((w, LANES), F32)
        for h in range(NAT_GROUP):
            out = out + jnp.where(head == h, o[h * w:(h + 1) * w, :], 0.0)
        o_ref[qrows, :] = out.astype(o_ref.dtype)
        return carry

    lax.fori_loop(0, rows, body, 0, unroll=2)


def _nat(pn, tab, *, seq, batches, block0):
    rows = seq // GRID_W
    groups = NAT_HEADS // NAT_GROUP
    in_specs = [pl.BlockSpec((seq, LANES), lambda g, b: (block0 + b, g)),
                pl.BlockSpec((seq, LANES), lambda g, b: (block0 + b, groups + g)),
                pl.BlockSpec((seq, LANES), lambda g, b: (block0 + b, 2 * groups + g)),
                pl.BlockSpec((1,) + tab.shape[1:], lambda g, b: (g, 0, 0, 0))]
    return pl.pallas_call(
        functools.partial(_nat_kernel, rows=rows),
        grid=(groups, batches),
        in_specs=in_specs,
        out_specs=pl.BlockSpec((seq, LANES), lambda g, b: (b, g)),
        out_shape=jax.ShapeDtypeStruct((batches * seq, NAT_W), BF16),
        compiler_params=_cparams(("parallel", "parallel")),
    )(pn, pn, pn, tab)


def _nat_bias_table(rpb):
    w = GRID_W
    jq = np.arange(w)[:, None]
    jk = np.arange(w)[None, :]
    c0 = np.clip(jq - NAT_KW // 2, 0, w - NAT_KW)
    valid = (jk >= c0) & (jk < c0 + NAT_KW)
    dcol = np.clip(jk - jq + NAT_KW - 1, 0, 2 * NAT_KW - 2)
    drow = np.arange(NAT_KH)[:, None] + np.arange(NAT_KH)[None, :]
    t = rpb.astype(F32)[:, drow][:, :, :, dcol]
    t = jnp.where(valid[None, None, None], t, -jnp.inf)
    t = t.transpose(0, 1, 3, 2, 4).reshape(NAT_HEADS, NAT_KH, w, NAT_KH * w)
    t = t.reshape(NAT_HEADS // NAT_GROUP, NAT_GROUP, NAT_KH, w, NAT_KH * w).transpose(0, 2, 1, 3, 4)
    return t.reshape(NAT_HEADS // NAT_GROUP, NAT_KH, NAT_GROUP * w, NAT_KH * w)


def _out_kernel(ogp_ref, ogs_ref, onp_ref, ons_ref, xp_ref, xs_ref, mod_ref, wg_ref, wn_ref, n2_ref, rw_ref, rb_ref,
                x1_ref, h2_ref, ti_ref, gt_ref, rk_ref, cnt_ref, carry_ref, *, n_prompt_tiles):
    tm = x1_ref.shape[0]

    @pl.when(pl.program_id(0) == 0)
    def _():
        carry_ref[...] = jnp.zeros_like(carry_ref)

    def residual(stream):
        og_ref, on_ref, x_ref = ((ogp_ref, onp_ref, xp_ref), (ogs_ref, ons_ref, xs_ref))[stream]
        mix = _dot(og_ref[...], wg_ref[...]) + _dot(on_ref[...], wn_ref[...])
        x1_ref[...] = x_ref[...] + mod_ref[0, 2:3, :] * mix

    _per_stream(n_prompt_tiles, residual)
    x1 = x1_ref[...]
    ms = jnp.mean(x1 * x1, axis=-1, keepdims=True)
    h2 = x1 * lax.rsqrt(ms + EPS) * n2_ref[...]
    h2 = h2 * (1.0 + mod_ref[0, 4:5, :]) + mod_ref[0, 3:4, :]
    _store_rows(h2_ref, h2)

    logits = _dot_nt(rw_ref[...], h2, precision=HIGHEST) + rb_ref[...]
    eidx = lax.broadcasted_iota(jnp.int32, logits.shape, 0)
    vals = logits
    top_v, top_i = [], []
    for _ in range(TOP_K):
        m = jnp.max(vals, axis=0, keepdims=True)
        idx = jnp.min(jnp.where(vals == m, eidx, N_EXPERTS), axis=0, keepdims=True)
        top_v.append(m)
        top_i.append(idx)
        vals = jnp.where(eidx == idx, -jnp.inf, vals)
    ev = [jnp.exp(v - top_v[0]) for v in top_v]
    den = ev[0] + ev[1] + ev[2] + ev[3]
    ti_ref[...] = jnp.concatenate(top_i, axis=0)
    gt_ref[...] = jnp.concatenate([e / den for e in ev], axis=0)

    onehot = jnp.zeros(logits.shape, F32)
    for idx in top_i:
        onehot = onehot + (eidx == idx).astype(F32)
    r = lax.broadcasted_iota(jnp.int32, (tm, tm), 0)
    c = lax.broadcasted_iota(jnp.int32, (tm, tm), 1)
    earlier = (r < c).astype(BF16)
    before = carry_ref[...] + _dot(onehot.astype(BF16), earlier)
    rk_ref[...] = jnp.concatenate(
        [jnp.sum(jnp.where(eidx == idx, before, 0.0), axis=0, keepdims=True) for idx in top_i], axis=0).astype(jnp.int32)
    carry_ref[...] = carry_ref[...] + jnp.sum(onehot, axis=1, keepdims=True)
    cnt_ref[...] = jnp.broadcast_to(carry_ref[...], cnt_ref.shape)


def _out_proj(og, on, x, mod3, w_g, w_n, norm_g, rw_t, rb, batch_of_tile):
    d = x[0].shape[1]
    n = x[0].shape[0] + x[1].shape[0]
    tm = TOKEN_TILE
    n_prompt_tiles = x[0].shape[0] // tm
    tok = lambda i: (i, 0)
    const = lambda i: (0, 0)
    return pl.pallas_call(
        functools.partial(_out_kernel, n_prompt_tiles=n_prompt_tiles),
        grid=(n // tm,),
        in_specs=[*_stream_specs((tm, GLA_V), n_prompt_tiles),
                  *_stream_specs((tm, NAT_W), n_prompt_tiles),
                  *_stream_specs((tm, d), n_prompt_tiles),
                  pl.BlockSpec((1, 6, d), lambda i: (batch_of_tile(i), 0, 0)),
                  pl.BlockSpec(w_g.shape, const),
                  pl.BlockSpec(w_n.shape, const),
                  pl.BlockSpec((1, d), const),
                  pl.BlockSpec(rw_t.shape, const),
                  pl.BlockSpec((N_EXPERTS, 1), const)],
        out_specs=[pl.BlockSpec((tm, d), tok),
                   pl.BlockSpec(_row_table_shape(tm), tok),
                   pl.BlockSpec((TOP_K, tm), lambda i: (0, i)),
                   pl.BlockSpec((TOP_K, tm), lambda i: (0, i)),
                   pl.BlockSpec((TOP_K, tm), lambda i: (0, i)),
                   pl.BlockSpec((N_EXPERTS, LANES), const)],
        out_shape=[jax.ShapeDtypeStruct((n, d), F32),
                   jax.ShapeDtypeStruct(_row_table_shape(n), F32),
                   jax.ShapeDtypeStruct((TOP_K, n), jnp.int32),
                   jax.ShapeDtypeStruct((TOP_K, n), F32),
                   jax.ShapeDtypeStruct((TOP_K, n), jnp.int32),
                   jax.ShapeDtypeStruct((N_EXPERTS, LANES), F32)],
        scratch_shapes=[pltpu.VMEM((N_EXPERTS, 1), F32)],
        compiler_params=_cparams(("arbitrary",)),
    )(*og, *on, *x, mod3, w_g, w_n, norm_g.reshape(1, d), rw_t, rb.reshape(N_EXPERTS, 1))


def _row_copy(src, dst, sem):
    return pltpu.make_async_copy(src, dst, sem)


def _dispatch_kernel(dest_ref, h_ref, xs_hbm, sem):
    tt = dest_ref.shape[1]

    def issue(t, carry):
        for k in range(TOP_K):
            _row_copy(_vmem_row(h_ref, t), xs_hbm.at[dest_ref[k, t]], sem).start()
        return carry

    def drain(t, carry):
        for k in range(TOP_K):
            _row_copy(_vmem_row(h_ref, t), xs_hbm.at[dest_ref[k, t]], sem).wait()
        return carry

    lax.fori_loop(0, tt, issue, 0)
    lax.fori_loop(0, tt, drain, 0)


def _dispatch(dest_t, h2):
    n = h2.shape[0] // LANE_TILES
    tt = GATHER_TILE
    return pl.pallas_call(
        _dispatch_kernel,
        grid=(n // tt,),
        in_specs=[pl.BlockSpec((TOP_K, tt), lambda i: (0, i), memory_space=pltpu.SMEM),
                  pl.BlockSpec(_row_table_shape(tt), lambda i: (i, 0))],
        out_specs=pl.BlockSpec(memory_space=pl.ANY),
        out_shape=jax.ShapeDtypeStruct((n * TOP_K, LANE_TILES, LANES), F32),
        scratch_shapes=[pltpu.SemaphoreType.DMA(())],
        compiler_params=_cparams(("arbitrary",)),
    )(dest_t, h2)


def _swiglu_col(feature0, up):
    group, within = divmod(feature0, SWIGLU_GROUP)
    return (2 * group + up) * SWIGLU_GROUP + within


def _deinterleave_kernel(w_ref, o_ref):
    f2 = w_ref.shape[2]
    slab = 2 * LANES
    r = lax.broadcasted_iota(jnp.int32, (slab, slab), 0)
    c = lax.broadcasted_iota(jnp.int32, (slab, slab), 1)
    perm = (r == jnp.where(c < LANES, 2 * c, 2 * (c - LANES) + 1)).astype(BF16)
    for j in range(f2 // slab):
        y = _dot(w_ref[0, :, j * slab:(j + 1) * slab].astype(BF16), perm)
        for up in range(2):
            c0 = _swiglu_col(j * LANES, up)
            o_ref[0, :, c0:c0 + LANES] = y[:, up * LANES:(up + 1) * LANES].astype(BF16)


def _deinterleave(w):
    e, d, f2 = w.shape
    return pl.pallas_call(
        _deinterleave_kernel,
        grid=(e,),
        in_specs=[pl.BlockSpec((1, d, f2), lambda i: (i, 0, 0))],
        out_specs=pl.BlockSpec((1, d, f2), lambda i: (i, 0, 0)),
        out_shape=jax.ShapeDtypeStruct((e, d, f2), BF16),
        compiler_params=_cparams(("parallel",)),
    )(w)


def _expert_kernel(blk_ref, exp_ref, lo_ref, hi_ref, xs_ref, wgu_ref, bgu_ref, wd_ref, bd_ref, ys_ref):
    del blk_ref, exp_ref
    rows = xs_ref.shape[0] // LANE_TILES
    f = wd_ref.shape[1]
    i = pl.program_id(0)
    lo = lo_ref[i]
    hi = hi_ref[i]

    @pl.when(hi > lo)
    def _():
        g = SWIGLU_GROUP
        x = _load_rows(xs_ref, 0, rows).astype(BF16)
        acts = []
        for j in range(f // g):
            gu = _dot(x, wgu_ref[0, :, 2 * j * g:2 * (j + 1) * g]) + bgu_ref[0, :, 2 * j * g:2 * (j + 1) * g]
            gate = jnp.minimum(gu[:, :g], SWIGLU_LIMIT)
            up = jnp.clip(gu[:, g:], -SWIGLU_LIMIT, SWIGLU_LIMIT)
            acts.append(((up + 1.0) * gate * jax.nn.sigmoid(SWIGLU_ALPHA * gate)).astype(BF16))
        y = _dot(jnp.concatenate(acts, axis=1), wd_ref[0]) + bd_ref[0]
        r = lax.broadcasted_iota(jnp.int32, y.shape, 0)
        mine = (r >= lo) & (r < hi)

        @pl.when(lo == 0)
        def _():
            _store_rows(ys_ref, jnp.where(mine, y, 0.0))

        @pl.when(lo > 0)
        def _():
            _store_rows(ys_ref, jnp.where(mine, y, _load_rows(ys_ref, 0, rows)))


def _experts(items, xs2, w_gu, b_gu, w_down, b_down):
    item_blk, item_exp, item_lo, item_hi = items
    d, f2 = w_gu.shape[1], w_gu.shape[2]
    wmap = lambda i, blk, exp, lo, hi: (exp[i], 0, 0)
    xmap = lambda i, blk, exp, lo, hi: (blk[i], 0)
    grid_spec = pltpu.PrefetchScalarGridSpec(
        num_scalar_prefetch=4,
        grid=(item_blk.shape[0],),
        in_specs=[pl.BlockSpec(_row_table_shape(MOE_ROWS), xmap),
                  pl.BlockSpec((1, d, f2), wmap),
                  pl.BlockSpec((1, 1, f2), wmap),
                  pl.BlockSpec((1, f2 // 2, d), wmap),
                  pl.BlockSpec((1, 1, d), wmap)],
        out_specs=pl.BlockSpec(_row_table_shape(MOE_ROWS), xmap),
    )
    return pl.pallas_call(
        _expert_kernel,
        grid_spec=grid_spec,
        out_shape=jax.ShapeDtypeStruct(xs2.shape, F32),
        compiler_params=_cparams(("arbitrary",)),
    )(item_blk, item_exp, item_lo, item_hi, xs2, w_gu, b_gu, w_down, b_down)


def _expert_work_items(counts, n_rows):
    ends = jnp.cumsum(counts)
    starts = ends - counts
    n_blk = n_rows // MOE_ROWS
    cuts = jnp.sort(jnp.concatenate([jnp.arange(n_blk, dtype=jnp.int32) * MOE_ROWS, starts[1:]]))
    nxt = jnp.concatenate([cuts[1:], jnp.full((1,), n_rows, jnp.int32)])
    blk = jnp.minimum(cuts // MOE_ROWS, n_blk - 1)
    exp = jnp.minimum(jnp.sum(ends[None, :] <= cuts[:, None], axis=1), N_EXPERTS - 1).astype(jnp.int32)
    lo = cuts - blk * MOE_ROWS
    hi = nxt - blk * MOE_ROWS
    return (blk.astype(jnp.int32), exp, lo.astype(jnp.int32), hi.astype(jnp.int32)), starts


def _combine_kernel(dest_ref, next_ref, gates_ref, x1_ref, mod_ref, fg_ref, ys_hbm, op_ref, os_ref,
                    rows_ref, y_ref, sem, *, n_prompt_tiles):
    tt = x1_ref.shape[0]
    i = pl.program_id(0)
    buf = i % 2

    def gather(d_ref, b, start):
        def body(t, carry):
            for k in range(TOP_K):
                copy = _row_copy(ys_hbm.at[d_ref[k, t]], _vmem_row(rows_ref, (b * TOP_K + k) * tt + t), sem.at[b])
                copy.start() if start else copy.wait()
            return carry

        lax.fori_loop(0, tt, body, 0)

    @pl.when(i == 0)
    def _():
        gather(dest_ref, 0, True)

    @pl.when(i + 1 < pl.num_programs(0))
    def _():
        gather(next_ref, 1 - buf, True)

    gather(dest_ref, buf, False)

    gates = gates_ref[...]
    for b in range(2):
        @pl.when(buf == b)
        def _():
            y = jnp.zeros(x1_ref.shape, F32)
            for k in range(TOP_K):
                y = y + gates[:, k:k + 1] * _load_rows(rows_ref, (b * TOP_K + k) * tt, tt)
            y_ref[...] = y

    x2 = x1_ref[...] + mod_ref[0, 5:6, :] * y_ref[...]
    ms = jnp.mean(x2 * x2, axis=-1, keepdims=True)
    out = x2 * lax.rsqrt(ms + EPS) * fg_ref[...]

    def write(stream):
        (op_ref, os_ref)[stream][...] = out

    _per_stream(n_prompt_tiles, write)


def _combine(dest_t, gates, x1, mod3, final_g, ys3, batch_of_tile, n_prompt):
    n, d = x1.shape
    tt = GATHER_TILE
    n_prompt_tiles = n_prompt // tt
    n_tiles = n // tt
    return pl.pallas_call(
        functools.partial(_combine_kernel, n_prompt_tiles=n_prompt_tiles),
        grid=(n_tiles,),
        in_specs=[pl.BlockSpec((TOP_K, tt), lambda i: (0, i), memory_space=pltpu.SMEM),
                  pl.BlockSpec((TOP_K, tt), lambda i: (0, jnp.minimum(i + 1, n_tiles - 1)), memory_space=pltpu.SMEM),
                  pl.BlockSpec((tt, TOP_K), lambda i: (i, 0)),
                  pl.BlockSpec((tt, d), lambda i: (i, 0)),
                  pl.BlockSpec((1, 6, d), lambda i: (batch_of_tile(i), 0, 0)),
                  pl.BlockSpec((1, d), lambda i: (0, 0)),
                  pl.BlockSpec(memory_space=pl.ANY)],
        out_specs=list(_stream_specs((tt, d), n_prompt_tiles)),
        out_shape=[jax.ShapeDtypeStruct((n_prompt, d), F32),
                   jax.ShapeDtypeStruct((n - n_prompt, d), F32)],
        scratch_shapes=[pltpu.VMEM(_row_table_shape(2 * TOP_K * tt), F32),
                        pltpu.VMEM((tt, d), F32),
                        pltpu.SemaphoreType.DMA((2,))],
        compiler_params=_cparams(("arbitrary",)),
    )(dest_t, dest_t, gates, x1, mod3, final_g.reshape(1, d), ys3)


def _batch_of_tile_fn(tile, bp, tp, ts):
    n_prompt = bp * tp

    def batch_of_tile(i):
        t = i * tile
        return jnp.where(t < n_prompt, t // tp, bp + (t - n_prompt) // ts)

    return batch_of_tile


def kernel(x_prompt, x_sample, c_prompt, c_sample, norm1_g, w_ada, b_ada, w_in, gla_w2_fwd, gla_b2_fwd, gla_w2_bwd, gla_b2_bwd, gla_norm_g, nat_rpb, w_out, norm2_g, router_w, router_b, w_gate_up, b_gate_up, w_down, b_down, final_norm_g):
    assert w_ada.shape[0] == 1, "single-layer encoder"
    bp, tp, d = x_prompt.shape
    bs, ts, _ = x_sample.shape
    n_prompt, n_sample = bp * tp, bs * ts
    n = n_prompt + n_sample
    assert tp % TOKEN_TILE == 0 and ts % TOKEN_TILE == 0 and n_prompt % ts == 0
    assert d == LANE_TILES * LANES

    x = (x_prompt.reshape(n_prompt, d), x_sample.reshape(n_sample, d))
    c = jnp.concatenate([c_prompt, c_sample], axis=0)

    sizes = (GLA_QK, GLA_QK, GLA_V, GLA_V, GLA_RANK, GLA_RANK, NAT_W, NAT_W, NAT_W)
    offs = np.concatenate([[0], np.cumsum(sizes)])
    w_in0 = w_in[0]
    seg = lambda j: w_in0[:, offs[j]:offs[j + 1]]
    w_all = jnp.concatenate(
        [seg(0), seg(1), seg(2), seg(3), seg(6) * (NAT_HD ** -0.5), seg(7), seg(8), seg(4), seg(5),
         jnp.zeros((d, LANES - 2 * GLA_RANK), F32)], axis=1).astype(BF16)
    pairs = GLA_HEADS // 2
    w2 = jnp.zeros((LANES, pairs, 2, 2 * GLA_DK), F32)
    w2 = w2.at[0:GLA_RANK, :, 0].set(gla_w2_fwd[0].reshape(GLA_RANK, pairs, 2 * GLA_DK))
    w2 = w2.at[GLA_RANK:2 * GLA_RANK, :, 1].set(gla_w2_bwd[0].reshape(GLA_RANK, pairs, 2 * GLA_DK))
    wz = w2.reshape(LANES, pairs * 4 * GLA_DK).astype(BF16)
    bz = jnp.stack([gla_b2_fwd[0].reshape(pairs, 2 * GLA_DK), gla_b2_bwd[0].reshape(pairs, 2 * GLA_DK)],
                   axis=1).reshape(1, pairs * 4 * GLA_DK)
    tab = _nat_bias_table(nat_rpb[0])
    w_og = w_out[0, :GLA_V].astype(BF16)
    w_on = w_out[0, GLA_V:].astype(BF16)
    rw_t = router_w[0].T
    w_gu = _deinterleave(w_gate_up[0])
    n_e, f2 = b_gate_up.shape[1:]
    b_gu = (b_gate_up[0].reshape(n_e, f2 // (2 * SWIGLU_GROUP), SWIGLU_GROUP, 2)
            .transpose(0, 1, 3, 2).reshape(n_e, 1, f2))
    w_dn = w_down[0].astype(BF16)
    b_dn = b_down[0, :, None, :]

    mod3 = _ada(c, w_ada[0], b_ada[0]).reshape(bp + bs, 6, d)
    bot_tok = _batch_of_tile_fn(TOKEN_TILE, bp, tp, ts)
    bot_gat = _batch_of_tile_fn(GATHER_TILE, bp, tp, ts)

    pg, pn, plr = _in_proj(*x, mod3, norm1_g[0], w_all, bot_tok)

    gla_args = (pg, plr, wz, bz, gla_norm_g[0].reshape(1, -1))
    og = (_gla(*gla_args, seq=tp, batches=bp, block0=0),
          _gla(*gla_args, seq=ts, batches=bs, block0=n_prompt // ts))
    on = (_nat(pn, tab, seq=tp, batches=bp, block0=0),
          _nat(pn, tab, seq=ts, batches=bs, block0=n_prompt // ts))

    x1, h2, top_i, gates_t, rank_t, cnt = _out_proj(og, on, x, mod3, w_og, w_on, norm2_g[0], rw_t, router_b[0], bot_tok)

    n_rows = n * TOP_K
    assert n_rows % MOE_ROWS == 0
    counts = cnt[:, 0].astype(jnp.int32)
    items, starts = _expert_work_items(counts, n_rows)
    chosen = top_i[None] == jnp.arange(N_EXPERTS, dtype=jnp.int32)[:, None, None]
    dest_t = rank_t + jnp.sum(jnp.where(chosen, starts[:, None, None], 0), axis=0)

    xs = _dispatch(dest_t, h2)
    ys = _experts(items, xs.reshape(_row_table_shape(n_rows)), w_gu, b_gu, w_dn, b_dn)
    y_prompt, y_sample = _combine(dest_t, gates_t.T, x1, mod3, final_norm_g,
                                  ys.reshape(n_rows, LANE_TILES, LANES), bot_gat, n_prompt)

    return (y_prompt.reshape(bp, tp, d), y_sample.reshape(bs, ts, d))
```

```python
import functools

import numpy as np
import jax
import jax.numpy as jnp
from jax import lax
from jax.experimental import pallas as pl
from jax.experimental.pallas import tpu as pltpu

F32 = jnp.float32
BF16 = jnp.bfloat16
HIGHEST = lax.Precision.HIGHEST

EPS = 1e-5
GRID_W = 64
GLA_HEADS = 4
GLA_DK = 64
GLA_DV = 128
GLA_RANK = 16
GLA_TAU = 16.0
GLA_CHUNK = 64
GLA_QK = GLA_HEADS * GLA_DK
GLA_V = GLA_HEADS * GLA_DV
NAT_HEADS = 16
NAT_HD = 32
NAT_W = NAT_HEADS * NAT_HD
NAT_KH = 8
NAT_KW = 16
NAT_GROUP = 4
N_EXPERTS = 32
TOP_K = 4
SWIGLU_LIMIT = 7.0
SWIGLU_ALPHA = 1.702

LANES = 128
SUBLANES = 8
TOKEN_TILE = 512
MOE_ROWS = 512
SWIGLU_GROUP = 1024
GLA_UNROLL = 8
NAT_UNROLL = 4
DMA_ISSUE_UNROLL = 4
LOG2E = 1.4426950408889634
GATHER_TILE = 256
VMEM_LIMIT = 56 * 1024 * 1024


def _cparams(sem, vmem=VMEM_LIMIT):
    return pltpu.CompilerParams(dimension_semantics=sem, vmem_limit_bytes=vmem)


def _dot(a, b):
    return jnp.dot(a, b, preferred_element_type=F32)


def _dot_nt(a, b, precision=None):
    return lax.dot_general(a, b, (((1,), (1,)), ((), ())), preferred_element_type=F32, precision=precision)


LANE_TILES = 8


def _row_table_shape(rows):
    return (rows * LANE_TILES, LANES)


def _store_rows(ref, x, row0=0):
    rows = x.shape[0]
    for s in range(LANE_TILES):
        ref[pl.ds(row0 * LANE_TILES + s, rows, stride=LANE_TILES), :] = x[:, s * LANES:(s + 1) * LANES]


def _load_rows(ref, row0, rows):
    return jnp.concatenate(
        [ref[pl.ds(row0 * LANE_TILES + s, rows, stride=LANE_TILES), :] for s in range(LANE_TILES)], axis=1)


def _row_of(ref, r):
    return ref.at[pl.ds(pl.multiple_of(r * LANE_TILES, LANE_TILES), LANE_TILES), :]


def _dot_tn(a, b):
    return lax.dot_general(a, b, (((0,), (0,)), ((), ())), preferred_element_type=F32)


def _ada_kernel(c_ref, w_ref, b_ref, o_ref):
    c = c_ref[...]
    s = c * jax.nn.sigmoid(c)
    o_ref[...] = jnp.dot(s, w_ref[...], preferred_element_type=F32, precision=HIGHEST) + b_ref[...]


def _ada(c, w, b):
    nb, d = c.shape
    cols = w.shape[1]
    blk = 1024
    return pl.pallas_call(
        _ada_kernel,
        grid=(cols // blk,),
        in_specs=[pl.BlockSpec((nb, d), lambda j: (0, 0)),
                  pl.BlockSpec((d, blk), lambda j: (0, j)),
                  pl.BlockSpec((1, blk), lambda j: (0, j))],
        out_specs=pl.BlockSpec((nb, blk), lambda j: (0, j)),
        out_shape=jax.ShapeDtypeStruct((nb, cols), F32),
        compiler_params=_cparams(("arbitrary",)),
    )(c, w, b.reshape(1, cols))


def _stream_specs(block, n_prompt_tiles):
    return (pl.BlockSpec(block, lambda i: (jnp.minimum(i, n_prompt_tiles - 1), 0)),
            pl.BlockSpec(block, lambda i: (jnp.maximum(i - n_prompt_tiles, 0), 0)))


def _per_stream(n_prompt_tiles, body):
    i = pl.program_id(0)
    pl.when(i < n_prompt_tiles)(functools.partial(body, 0))
    pl.when(i >= n_prompt_tiles)(functools.partial(body, 1))


def _in_kernel(xp_ref, xs_ref, mod_ref, g_ref, w_ref, pg_ref, pn_ref, plr_ref, *, n_prompt_tiles):
    def body(stream):
        x = (xp_ref, xs_ref)[stream][...]
        ms = jnp.mean(x * x, axis=-1, keepdims=True)
        y = x * lax.rsqrt(ms + EPS) * g_ref[...]
        h = y * (1.0 + mod_ref[0, 1:2, :]) + mod_ref[0, 0:1, :]
        hb = h.astype(BF16)
        wg = GLA_QK * 2 + GLA_V * 2
        wn = 3 * NAT_W
        pg_ref[...] = _dot(hb, w_ref[:, 0:wg]).astype(BF16)
        pn_ref[...] = _dot(hb, w_ref[:, wg:wg + wn]).astype(BF16)
        plr_ref[...] = _dot(hb, w_ref[:, wg + wn:wg + wn + LANES])

    _per_stream(n_prompt_tiles, body)


def _in_proj(xp, xs, mod3, norm_g, w_all, batch_of_tile):
    d = xp.shape[1]
    n = xp.shape[0] + xs.shape[0]
    wg = GLA_QK * 2 + GLA_V * 2
    wn = 3 * NAT_W
    tm = TOKEN_TILE
    n_prompt_tiles = xp.shape[0] // tm
    return pl.pallas_call(
        functools.partial(_in_kernel, n_prompt_tiles=n_prompt_tiles),
        grid=(n // tm,),
        in_specs=[*_stream_specs((tm, d), n_prompt_tiles),
                  pl.BlockSpec((1, 6, d), lambda i: (batch_of_tile(i), 0, 0)),
                  pl.BlockSpec((1, d), lambda i: (0, 0)),
                  pl.BlockSpec(w_all.shape, lambda i: (0, 0))],
        out_specs=[pl.BlockSpec((tm, wg), lambda i: (i, 0)),
                   pl.BlockSpec((tm, wn), lambda i: (i, 0)),
                   pl.BlockSpec((tm, LANES), lambda i: (i, 0))],
        out_shape=[jax.ShapeDtypeStruct((n, wg), BF16),
                   jax.ShapeDtypeStruct((n, wn), BF16),
                   jax.ShapeDtypeStruct((n, LANES), F32)],
        compiler_params=_cparams(("arbitrary",)),
    )(xp, xs, mod3, norm_g.reshape(1, d), w_all)


def _gla_kernel(q_ref, k_ref, v_ref, g_ref, lr_ref, wz_ref, bz_ref, ng_ref,
                o_ref, la_ref, acc_ref, qt_ref, ke_ref, dec_ref, sf_ref, sb_ref, *, seq):
    c_len = GLA_CHUNK
    n = seq // c_len
    row = lax.broadcasted_iota(jnp.int32, (c_len, c_len), 0)
    col = lax.broadcasted_iota(jnp.int32, (c_len, c_len), 1)
    tril = col <= row
    triu = col >= row
    cum_row = lax.broadcasted_iota(jnp.int32, (c_len, 2 * c_len), 0)
    cum_col = lax.broadcasted_iota(jnp.int32, (c_len, 2 * c_len), 1) % c_len
    cum = (cum_col <= cum_row).astype(F32).astype(BF16)
    mask_f = jnp.concatenate([tril, tril], axis=0)
    mask_b = jnp.concatenate([triu, triu], axis=0)
    head_a = lax.broadcasted_iota(jnp.int32, (c_len, LANES), 1) < GLA_DK
    st_row = lax.broadcasted_iota(jnp.int32, (2 * GLA_DV, LANES), 0)
    st_col = lax.broadcasted_iota(jnp.int32, (2 * GLA_DV, LANES), 1)
    blockdiag = (st_row < GLA_DV) == (st_col < GLA_DK)

    z = _dot(lr_ref[...].astype(BF16), wz_ref[...]) + bz_ref[...]
    la_ref[...] = (jnp.minimum(z, 0.0) - jnp.log(1.0 + jnp.exp(-jnp.abs(z)))) * (1.0 / GLA_TAU)

    def chunk_rows(c):
        return pl.ds(pl.multiple_of(c * c_len, c_len), c_len)

    def stack_heads(x):
        zero = jnp.zeros_like(x)
        return jnp.concatenate([jnp.where(head_a, x, zero), jnp.where(head_a, zero, x)], axis=0).astype(BF16)

    def local(c, carry):
        rows = chunk_rows(c)
        la = la_ref[rows, :]
        hi = la.astype(BF16)
        lo = (la - hi.astype(F32)).astype(BF16)
        binc = _dot(cum, jnp.concatenate([hi, lo], axis=0))
        btot = jnp.sum(la, axis=0, keepdims=True)
        b_f = binc[:, :LANES]
        b_b = btot[:, LANES:] - binc[:, LANES:] + la[:, LANES:]
        e_tot = jnp.exp(btot)
        q = q_ref[rows, :].astype(F32) * (GLA_DK ** -0.5)
        k = k_ref[rows, :].astype(F32)
        qt_f = q * jnp.exp(b_f)
        qt_b = q * jnp.exp(b_b)
        kt_f = k * jnp.exp(-b_f)
        kt_b = k * jnp.exp(-b_b)
        a = (jnp.where(mask_f, _dot_nt(stack_heads(qt_f), kt_f.astype(BF16)), 0.0)
             + jnp.where(mask_b, _dot_nt(stack_heads(qt_b), kt_b.astype(BF16)), 0.0))
        oi = _dot(a.astype(BF16), v_ref[rows, :])
        acc_ref[rows, :] = jnp.concatenate([oi[0:c_len, 0:GLA_DV], oi[c_len:, GLA_DV:]], axis=1)
        qt_ref[rows, :] = jnp.concatenate([qt_f, qt_b], axis=1).astype(BF16)
        ke_ref[rows, :] = jnp.concatenate([kt_f * e_tot[:, :LANES], kt_b * e_tot[:, LANES:]], axis=1).astype(BF16)
        dec_ref[pl.ds(pl.multiple_of(c * SUBLANES, SUBLANES), SUBLANES), :] = jnp.broadcast_to(e_tot, (SUBLANES, 2 * LANES))
        return carry

    def carried(c, s_ref, half):
        rows = chunk_rows(c)
        lanes = slice(half * LANES, (half + 1) * LANES)
        st = s_ref[...]
        acc_ref[rows, :] += _dot_nt(qt_ref[rows, lanes], st.astype(BF16))
        upd = _dot_tn(v_ref[rows, :], ke_ref[rows, lanes])
        dec = dec_ref[pl.ds(pl.multiple_of(c * SUBLANES, SUBLANES), 1), lanes]
        s_ref[...] = st * dec + jnp.where(blockdiag, upd, 0.0)

    def scan(c, carry):
        carried(c, sf_ref, 0)
        carried(n - 1 - c, sb_ref, 1)
        return carry

    def finalize(c, carry):
        rows = chunk_rows(c)
        o = acc_ref[rows, :]
        halves = []
        for hh in range(2):
            oh = o[:, hh * GLA_DV:(hh + 1) * GLA_DV]
            ms = jnp.mean(oh * oh, axis=-1, keepdims=True)
            halves.append(oh * lax.rsqrt(ms + EPS) * ng_ref[...])
        y = jnp.concatenate(halves, axis=1)
        g = g_ref[rows, :].astype(F32)
        o_ref[rows, :] = (y * (g * jax.nn.sigmoid(g))).astype(o_ref.dtype)
        return carry

    sf_ref[...] = jnp.zeros_like(sf_ref)
    sb_ref[...] = jnp.zeros_like(sb_ref)
    lax.fori_loop(0, n, local, 0, unroll=GLA_UNROLL)
    lax.fori_loop(0, n, scan, 0, unroll=GLA_UNROLL)
    lax.fori_loop(0, n, finalize, 0, unroll=GLA_UNROLL)


def _gla(pg, plr, wz, bz, norm_g, *, seq, batches, block0):
    pair_w = 2 * GLA_DK
    pair_v = 2 * GLA_DV
    k0 = GLA_QK // pair_w
    v0 = 2 * GLA_QK // pair_v
    g0 = (2 * GLA_QK + GLA_V) // pair_v
    n_chunks = seq // GLA_CHUNK
    in_specs = [pl.BlockSpec((seq, pair_w), lambda b, p: (block0 + b, p)),
                pl.BlockSpec((seq, pair_w), lambda b, p: (block0 + b, k0 + p)),
                pl.BlockSpec((seq, pair_v), lambda b, p: (block0 + b, v0 + p)),
                pl.BlockSpec((seq, pair_v), lambda b, p: (block0 + b, g0 + p)),
                pl.BlockSpec((seq, LANES), lambda b, p: (block0 + b, 0)),
                pl.BlockSpec((LANES, 2 * pair_w), lambda b, p: (0, p)),
                pl.BlockSpec((1, 2 * pair_w), lambda b, p: (0, p)),
                pl.BlockSpec((1, GLA_DV), lambda b, p: (0, 0))]
    return pl.pallas_call(
        functools.partial(_gla_kernel, seq=seq),
        grid=(batches, GLA_HEADS // 2),
        in_specs=in_specs,
        out_specs=pl.BlockSpec((seq, pair_v), lambda b, p: (b, p)),
        out_shape=jax.ShapeDtypeStruct((batches * seq, GLA_V), BF16),
        scratch_shapes=[pltpu.VMEM((seq, 2 * pair_w), F32),
                        pltpu.VMEM((seq, pair_v), F32),
                        pltpu.VMEM((seq, 2 * pair_w), BF16),
                        pltpu.VMEM((seq, 2 * pair_w), BF16),
                        pltpu.VMEM((n_chunks * SUBLANES, 2 * pair_w), F32),
                        pltpu.VMEM((pair_v, pair_w), F32),
                        pltpu.VMEM((pair_v, pair_w), F32)],
        compiler_params=_cparams(("parallel", "parallel")),
    )(pg, pg, pg, pg, plr, wz, bz, norm_g)


def _nat_kernel(q_ref, k_ref, v_ref, tab_ref, o_ref, *, rows):
    w = GRID_W
    head = lax.broadcasted_iota(jnp.int32, (w, LANES), 1) // NAT_HD
    win = NAT_KH * w

    def body(i, carry):
        r0 = jnp.clip(i - NAT_KH // 2, 0, rows - NAT_KH)
        base = r0 - i + NAT_KH - 1
        qrows = pl.ds(pl.multiple_of(i * w, w), w)
        krows = pl.ds(pl.multiple_of(r0 * w, w), win)
        q = q_ref[qrows, :]
        zero = jnp.zeros_like(q)
        q_stack = jnp.concatenate([jnp.where(head == h, q, zero) for h in range(NAT_GROUP)], axis=0)
        s = _dot_nt(q_stack, k_ref[krows, :]) + tab_ref[0, base]
        m = jnp.max(s, axis=-1, keepdims=True)
        e = jnp.exp2(s - m)
        l = jnp.sum(e, axis=-1, keepdims=True)
        o = _dot(e.astype(BF16), v_ref[krows, :]) / l
        out = jnp.zeros((w, LANES), F32)
        for h in range(NAT_GROUP):
            out = out + jnp.where(head == h, o[h * w:(h + 1) * w, :], 0.0)
        o_ref[qrows, :] = out.astype(o_ref.dtype)
        return carry

    lax.fori_loop(0, rows, body, 0, unroll=NAT_UNROLL)


def _nat(pn, tab, *, seq, batches, block0):
    rows = seq // GRID_W
    groups = NAT_HEADS // NAT_GROUP
    in_specs = [pl.BlockSpec((seq, LANES), lambda g, b: (block0 + b, g)),
                pl.BlockSpec((seq, LANES), lambda g, b: (block0 + b, groups + g)),
                pl.BlockSpec((seq, LANES), lambda g, b: (block0 + b, 2 * groups + g)),
                pl.BlockSpec((1,) + tab.shape[1:], lambda g, b: (g, 0, 0, 0))]
    return pl.pallas_call(
        functools.partial(_nat_kernel, rows=rows),
        grid=(groups, batches),
        in_specs=in_specs,
        out_specs=pl.BlockSpec((seq, LANES), lambda g, b: (b, g)),
        out_shape=jax.ShapeDtypeStruct((batches * seq, NAT_W), BF16),
        compiler_params=_cparams(("parallel", "parallel")),
    )(pn, pn, pn, tab)


def _nat_bias_table(rpb):
    w = GRID_W
    jq = np.arange(w)[:, None]
    jk = np.arange(w)[None, :]
    c0 = np.clip(jq - NAT_KW // 2, 0, w - NAT_KW)
    valid = (jk >= c0) & (jk < c0 + NAT_KW)
    dcol = np.clip(jk - jq + NAT_KW - 1, 0, 2 * NAT_KW - 2)
    drow = np.arange(NAT_KH)[:, None] + np.arange(NAT_KH)[None, :]
    t = rpb.astype(F32)[:, drow][:, :, :, dcol]
    t = jnp.where(valid[None, None, None], t * LOG2E, -jnp.inf)
    t = t.transpose(0, 1, 3, 2, 4).reshape(NAT_HEADS, NAT_KH, w, NAT_KH * w)
    t = t.reshape(NAT_HEADS // NAT_GROUP, NAT_GROUP, NAT_KH, w, NAT_KH * w).transpose(0, 2, 1, 3, 4)
    return t.reshape(NAT_HEADS // NAT_GROUP, NAT_KH, NAT_GROUP * w, NAT_KH * w)


def _out_kernel(ogp_ref, ogs_ref, onp_ref, ons_ref, xp_ref, xs_ref, mod_ref, wg_ref, wn_ref, n2_ref, rw_ref, rb_ref,
                x1_ref, h2_ref, ti_ref, gt_ref, rk_ref, cnt_ref, carry_ref, *, n_prompt_tiles):
    tm = x1_ref.shape[0]

    @pl.when(pl.program_id(0) == 0)
    def _():
        carry_ref[...] = jnp.zeros_like(carry_ref)

    def residual(stream):
        og_ref, on_ref, x_ref = ((ogp_ref, onp_ref, xp_ref), (ogs_ref, ons_ref, xs_ref))[stream]
        mix = _dot(og_ref[...], wg_ref[...]) + _dot(on_ref[...], wn_ref[...])
        x1_ref[...] = x_ref[...] + mod_ref[0, 2:3, :] * mix

    _per_stream(n_prompt_tiles, residual)
    x1 = x1_ref[...]
    ms = jnp.mean(x1 * x1, axis=-1, keepdims=True)
    h2 = x1 * lax.rsqrt(ms + EPS) * n2_ref[...]
    h2 = h2 * (1.0 + mod_ref[0, 4:5, :]) + mod_ref[0, 3:4, :]
    _store_rows(h2_ref, h2)

    logits = _dot_nt(rw_ref[...], h2, precision=HIGHEST) + rb_ref[...]
    eidx = lax.broadcasted_iota(jnp.int32, logits.shape, 0)
    vals = logits
    top_v, top_i = [], []
    for _ in range(TOP_K):
        m = jnp.max(vals, axis=0, keepdims=True)
        idx = jnp.min(jnp.where(vals == m, eidx, N_EXPERTS), axis=0, keepdims=True)
        top_v.append(m)
        top_i.append(idx)
        vals = jnp.where(eidx == idx, -jnp.inf, vals)
    ev = [jnp.exp(v - top_v[0]) for v in top_v]
    den = ev[0] + ev[1] + ev[2] + ev[3]
    ti_ref[...] = jnp.concatenate(top_i, axis=0)
    gt_ref[...] = jnp.concatenate([e / den for e in ev], axis=0)

    onehot = jnp.zeros(logits.shape, F32)
    for idx in top_i:
        onehot = onehot + (eidx == idx).astype(F32)
    r = lax.broadcasted_iota(jnp.int32, (tm, tm), 0)
    c = lax.broadcasted_iota(jnp.int32, (tm, tm), 1)
    earlier = (r < c).astype(BF16)
    before = carry_ref[...] + _dot(onehot.astype(BF16), earlier)
    rk_ref[...] = jnp.concatenate(
        [jnp.sum(jnp.where(eidx == idx, before, 0.0), axis=0, keepdims=True) for idx in top_i], axis=0).astype(jnp.int32)
    carry_ref[...] = carry_ref[...] + jnp.sum(onehot, axis=1, keepdims=True)
    cnt_ref[...] = jnp.broadcast_to(carry_ref[...], cnt_ref.shape)


def _out_proj(og, on, x, mod3, w_g, w_n, norm_g, rw_t, rb, batch_of_tile):
    d = x[0].shape[1]
    n = x[0].shape[0] + x[1].shape[0]
    tm = TOKEN_TILE
    n_prompt_tiles = x[0].shape[0] // tm
    tok = lambda i: (i, 0)
    const = lambda i: (0, 0)
    return pl.pallas_call(
        functools.partial(_out_kernel, n_prompt_tiles=n_prompt_tiles),
        grid=(n // tm,),
        in_specs=[*_stream_specs((tm, GLA_V), n_prompt_tiles),
                  *_stream_specs((tm, NAT_W), n_prompt_tiles),
                  *_stream_specs((tm, d), n_prompt_tiles),
                  pl.BlockSpec((1, 6, d), lambda i: (batch_of_tile(i), 0, 0)),
                  pl.BlockSpec(w_g.shape, const),
                  pl.BlockSpec(w_n.shape, const),
                  pl.BlockSpec((1, d), const),
                  pl.BlockSpec(rw_t.shape, const),
                  pl.BlockSpec((N_EXPERTS, 1), const)],
        out_specs=[pl.BlockSpec((tm, d), tok),
                   pl.BlockSpec(_row_table_shape(tm), tok),
                   pl.BlockSpec((TOP_K, tm), lambda i: (0, i)),
                   pl.BlockSpec((TOP_K, tm), lambda i: (0, i)),
                   pl.BlockSpec((TOP_K, tm), lambda i: (0, i)),
                   pl.BlockSpec((N_EXPERTS, LANES), const)],
        out_shape=[jax.ShapeDtypeStruct((n, d), F32),
                   jax.ShapeDtypeStruct(_row_table_shape(n), F32),
                   jax.ShapeDtypeStruct((TOP_K, n), jnp.int32),
                   jax.ShapeDtypeStruct((TOP_K, n), F32),
                   jax.ShapeDtypeStruct((TOP_K, n), jnp.int32),
                   jax.ShapeDtypeStruct((N_EXPERTS, LANES), F32)],
        scratch_shapes=[pltpu.VMEM((N_EXPERTS, 1), F32)],
        compiler_params=_cparams(("arbitrary",)),
    )(*og, *on, *x, mod3, w_g, w_n, norm_g.reshape(1, d), rw_t, rb.reshape(N_EXPERTS, 1))


def _row_copy(src, dst, sem):
    return pltpu.make_async_copy(src, dst, sem)


def _dispatch_kernel(dest_ref, h_ref, xs_hbm, sem):
    tt = dest_ref.shape[1]

    def issue(t, carry):
        for k in range(TOP_K):
            _row_copy(_row_of(h_ref, t), _row_of(xs_hbm, dest_ref[k, t]), sem).start()
        return carry

    lax.fori_loop(0, tt, issue, 0, unroll=DMA_ISSUE_UNROLL)
    for k in range(TOP_K):
        _row_copy(h_ref, xs_hbm.at[pl.ds(0, h_ref.shape[0]), :], sem).wait()


def _dispatch(dest_t, h2):
    n = h2.shape[0] // LANE_TILES
    tt = GATHER_TILE
    return pl.pallas_call(
        _dispatch_kernel,
        grid=(n // tt,),
        in_specs=[pl.BlockSpec((TOP_K, tt), lambda i: (0, i), memory_space=pltpu.SMEM),
                  pl.BlockSpec(_row_table_shape(tt), lambda i: (i, 0))],
        out_specs=pl.BlockSpec(memory_space=pl.ANY),
        out_shape=jax.ShapeDtypeStruct(_row_table_shape(n * TOP_K), F32),
        scratch_shapes=[pltpu.SemaphoreType.DMA(())],
        compiler_params=_cparams(("arbitrary",)),
    )(dest_t, h2)


def _swiglu_col(feature0, up):
    group, within = divmod(feature0, SWIGLU_GROUP)
    return (2 * group + up) * SWIGLU_GROUP + within


def _deinterleave_kernel(w_ref, o_ref):
    f2 = w_ref.shape[2]
    slab = 2 * LANES
    r = lax.broadcasted_iota(jnp.int32, (slab, slab), 0)
    c = lax.broadcasted_iota(jnp.int32, (slab, slab), 1)
    perm = (r == jnp.where(c < LANES, 2 * c, 2 * (c - LANES) + 1)).astype(BF16)
    for j in range(f2 // slab):
        y = _dot(w_ref[0, :, j * slab:(j + 1) * slab].astype(BF16), perm)
        for up in range(2):
            c0 = _swiglu_col(j * LANES, up)
            o_ref[0, :, c0:c0 + LANES] = y[:, up * LANES:(up + 1) * LANES].astype(BF16)


def _deinterleave(w):
    e, d, f2 = w.shape
    return pl.pallas_call(
        _deinterleave_kernel,
        grid=(e,),
        in_specs=[pl.BlockSpec((1, d, f2), lambda i: (i, 0, 0))],
        out_specs=pl.BlockSpec((1, d, f2), lambda i: (i, 0, 0)),
        out_shape=jax.ShapeDtypeStruct((e, d, f2), BF16),
        compiler_params=_cparams(("parallel",)),
    )(w)


def _expert_kernel(blk_ref, exp_ref, lo_ref, hi_ref, xs_ref, wgu_ref, bgu_ref, wd_ref, bd_ref, ys_ref):
    del blk_ref, exp_ref
    rows = xs_ref.shape[0] // LANE_TILES
    f = wd_ref.shape[1]
    i = pl.program_id(0)
    lo = lo_ref[i]
    hi = hi_ref[i]

    @pl.when(hi > lo)
    def _():
        g = SWIGLU_GROUP
        x = _load_rows(xs_ref, 0, rows).astype(BF16)
        acts = []
        for j in range(f // g):
            gu = _dot(x, wgu_ref[0, :, 2 * j * g:2 * (j + 1) * g]) + bgu_ref[0, :, 2 * j * g:2 * (j + 1) * g]
            gate = jnp.minimum(gu[:, :g], SWIGLU_LIMIT)
            up = jnp.clip(gu[:, g:], -SWIGLU_LIMIT, SWIGLU_LIMIT)
            acts.append(((up + 1.0) * gate * jax.nn.sigmoid(SWIGLU_ALPHA * gate)).astype(BF16))
        y = _dot(jnp.concatenate(acts, axis=1), wd_ref[0]) + bd_ref[0]
        r = lax.broadcasted_iota(jnp.int32, y.shape, 0)
        mine = (r >= lo) & (r < hi)

        @pl.when(lo == 0)
        def _():
            _store_rows(ys_ref, jnp.where(mine, y, 0.0))

        @pl.when(lo > 0)
        def _():
            _store_rows(ys_ref, jnp.where(mine, y, _load_rows(ys_ref, 0, rows)))


def _experts(items, xs2, w_gu, b_gu, w_down, b_down):
    item_blk, item_exp, item_lo, item_hi = items
    d, f2 = w_gu.shape[1], w_gu.shape[2]
    wmap = lambda i, blk, exp, lo, hi: (exp[i], 0, 0)
    xmap = lambda i, blk, exp, lo, hi: (blk[i], 0)
    grid_spec = pltpu.PrefetchScalarGridSpec(
        num_scalar_prefetch=4,
        grid=(item_blk.shape[0],),
        in_specs=[pl.BlockSpec(_row_table_shape(MOE_ROWS), xmap),
                  pl.BlockSpec((1, d, f2), wmap),
                  pl.BlockSpec((1, 1, f2), wmap),
                  pl.BlockSpec((1, f2 // 2, d), wmap),
                  pl.BlockSpec((1, 1, d), wmap)],
        out_specs=pl.BlockSpec(_row_table_shape(MOE_ROWS), xmap),
    )
    return pl.pallas_call(
        _expert_kernel,
        grid_spec=grid_spec,
        out_shape=jax.ShapeDtypeStruct(xs2.shape, F32),
        compiler_params=_cparams(("arbitrary",)),
    )(item_blk, item_exp, item_lo, item_hi, xs2, w_gu, b_gu, w_down, b_down)


def _expert_work_items(counts, n_rows):
    ends = jnp.cumsum(counts)
    starts = ends - counts
    n_blk = n_rows // MOE_ROWS
    cuts = jnp.sort(jnp.concatenate([jnp.arange(n_blk, dtype=jnp.int32) * MOE_ROWS, starts[1:]]))
    nxt = jnp.concatenate([cuts[1:], jnp.full((1,), n_rows, jnp.int32)])
    blk = jnp.minimum(cuts // MOE_ROWS, n_blk - 1)
    exp = jnp.minimum(jnp.sum(ends[None, :] <= cuts[:, None], axis=1), N_EXPERTS - 1).astype(jnp.int32)
    lo = cuts - blk * MOE_ROWS
    hi = nxt - blk * MOE_ROWS
    return (blk.astype(jnp.int32), exp, lo.astype(jnp.int32), hi.astype(jnp.int32)), starts


def _combine_kernel(dest_ref, gates_ref, x1_ref, mod_ref, fg_ref, ys_hbm, op_ref, os_ref, rows_ref, sem, *,
                    n_prompt_tiles):
    tt = x1_ref.shape[0]

    def issue(t, carry):
        for k in range(TOP_K):
            _row_copy(_row_of(ys_hbm, dest_ref[k, t]), _row_of(rows_ref, k * tt + t), sem).start()
        return carry

    lax.fori_loop(0, tt, issue, 0, unroll=DMA_ISSUE_UNROLL)
    _row_copy(ys_hbm.at[pl.ds(0, rows_ref.shape[0]), :], rows_ref, sem).wait()

    gates = gates_ref[...]
    y = jnp.zeros(x1_ref.shape, F32)
    for k in range(TOP_K):
        y = y + gates[:, k:k + 1] * _load_rows(rows_ref, k * tt, tt)
    x2 = x1_ref[...] + mod_ref[0, 5:6, :] * y
    ms = jnp.mean(x2 * x2, axis=-1, keepdims=True)
    out = x2 * lax.rsqrt(ms + EPS) * fg_ref[...]

    def write(stream):
        (op_ref, os_ref)[stream][...] = out

    _per_stream(n_prompt_tiles, write)


def _combine(dest_t, gates, x1, mod3, final_g, ys, batch_of_tile, n_prompt):
    n, d = x1.shape
    tt = GATHER_TILE
    n_prompt_tiles = n_prompt // tt
    return pl.pallas_call(
        functools.partial(_combine_kernel, n_prompt_tiles=n_prompt_tiles),
        grid=(n // tt,),
        in_specs=[pl.BlockSpec((TOP_K, tt), lambda i: (0, i), memory_space=pltpu.SMEM),
                  pl.BlockSpec((tt, TOP_K), lambda i: (i, 0)),
                  pl.BlockSpec((tt, d), lambda i: (i, 0)),
                  pl.BlockSpec((1, 6, d), lambda i: (batch_of_tile(i), 0, 0)),
                  pl.BlockSpec((1, d), lambda i: (0, 0)),
                  pl.BlockSpec(memory_space=pl.ANY)],
        out_specs=list(_stream_specs((tt, d), n_prompt_tiles)),
        out_shape=[jax.ShapeDtypeStruct((n_prompt, d), F32),
                   jax.ShapeDtypeStruct((n - n_prompt, d), F32)],
        scratch_shapes=[pltpu.VMEM(_row_table_shape(TOP_K * tt), F32),
                        pltpu.SemaphoreType.DMA(())],
        compiler_params=_cparams(("arbitrary",)),
    )(dest_t, gates, x1, mod3, final_g.reshape(1, d), ys)


def _batch_of_tile_fn(tile, bp, tp, ts):
    n_prompt = bp * tp

    def batch_of_tile(i):
        t = i * tile
        return jnp.where(t < n_prompt, t // tp, bp + (t - n_prompt) // ts)

    return batch_of_tile


def kernel(x_prompt, x_sample, c_prompt, c_sample, norm1_g, w_ada, b_ada, w_in, gla_w2_fwd, gla_b2_fwd, gla_w2_bwd, gla_b2_bwd, gla_norm_g, nat_rpb, w_out, norm2_g, router_w, router_b, w_gate_up, b_gate_up, w_down, b_down, final_norm_g):
    assert w_ada.shape[0] == 1, "single-layer encoder"
    bp, tp, d = x_prompt.shape
    bs, ts, _ = x_sample.shape
    n_prompt, n_sample = bp * tp, bs * ts
    n = n_prompt + n_sample
    assert tp % TOKEN_TILE == 0 and ts % TOKEN_TILE == 0 and n_prompt % ts == 0
    assert d == LANE_TILES * LANES

    x = (x_prompt.reshape(n_prompt, d), x_sample.reshape(n_sample, d))
    c = jnp.concatenate([c_prompt, c_sample], axis=0)

    sizes = (GLA_QK, GLA_QK, GLA_V, GLA_V, GLA_RANK, GLA_RANK, NAT_W, NAT_W, NAT_W)
    offs = np.concatenate([[0], np.cumsum(sizes)])
    w_in0 = w_in[0]
    seg = lambda j: w_in0[:, offs[j]:offs[j + 1]]
    w_all = jnp.concatenate(
        [seg(0), seg(1), seg(2), seg(3), seg(6) * (NAT_HD ** -0.5 * LOG2E), seg(7), seg(8), seg(4), seg(5),
         jnp.zeros((d, LANES - 2 * GLA_RANK), F32)], axis=1).astype(BF16)
    pairs = GLA_HEADS // 2
    w2 = jnp.zeros((LANES, pairs, 2, 2 * GLA_DK), F32)
    w2 = w2.at[0:GLA_RANK, :, 0].set(gla_w2_fwd[0].reshape(GLA_RANK, pairs, 2 * GLA_DK))
    w2 = w2.at[GLA_RANK:2 * GLA_RANK, :, 1].set(gla_w2_bwd[0].reshape(GLA_RANK, pairs, 2 * GLA_DK))
    wz = w2.reshape(LANES, pairs * 4 * GLA_DK).astype(BF16)
    bz = jnp.stack([gla_b2_fwd[0].reshape(pairs, 2 * GLA_DK), gla_b2_bwd[0].reshape(pairs, 2 * GLA_DK)],
                   axis=1).reshape(1, pairs * 4 * GLA_DK)
    tab = _nat_bias_table(nat_rpb[0])
    w_og = w_out[0, :GLA_V].astype(BF16)
    w_on = w_out[0, GLA_V:].astype(BF16)
    rw_t = router_w[0].T
    w_gu = _deinterleave(w_gate_up[0])
    n_e, f2 = b_gate_up.shape[1:]
    b_gu = (b_gate_up[0].reshape(n_e, f2 // (2 * SWIGLU_GROUP), SWIGLU_GROUP, 2)
            .transpose(0, 1, 3, 2).reshape(n_e, 1, f2))
    w_dn = w_down[0].astype(BF16)
    b_dn = b_down[0, :, None, :]

    mod3 = _ada(c, w_ada[0], b_ada[0]).reshape(bp + bs, 6, d)
    bot_tok = _batch_of_tile_fn(TOKEN_TILE, bp, tp, ts)
    bot_gat = _batch_of_tile_fn(GATHER_TILE, bp, tp, ts)

    pg, pn, plr = _in_proj(*x, mod3, norm1_g[0], w_all, bot_tok)

    gla_args = (pg, plr, wz, bz, gla_norm_g[0].reshape(1, -1))
    og = (_gla(*gla_args, seq=tp, batches=bp, block0=0),
          _gla(*gla_args, seq=ts, batches=bs, block0=n_prompt // ts))
    on = (_nat(pn, tab, seq=tp, batches=bp, block0=0),
          _nat(pn, tab, seq=ts, batches=bs, block0=n_prompt // ts))

    x1, h2, top_i, gates_t, rank_t, cnt = _out_proj(og, on, x, mod3, w_og, w_on, norm2_g[0], rw_t, router_b[0], bot_tok)

    n_rows = n * TOP_K
    assert n_rows % MOE_ROWS == 0
    counts = cnt[:, 0].astype(jnp.int32)
    items, starts = _expert_work_items(counts, n_rows)
    chosen = top_i[None] == jnp.arange(N_EXPERTS, dtype=jnp.int32)[:, None, None]
    dest_t = rank_t + jnp.sum(jnp.where(chosen, starts[:, None, None], 0), axis=0)

    xs = _dispatch(dest_t, h2)
    ys = _experts(items, xs, w_gu, b_gu, w_dn, b_dn)
    y_prompt, y_sample = _combine(dest_t, gates_t.T, x1, mod3, final_norm_g, ys, bot_gat, n_prompt)

    return (y_prompt.reshape(bp, tp, d), y_sample.reshape(bs, ts, d))
```

```python
import functools

import numpy as np
import jax
import jax.numpy as jnp
from jax import lax
from jax.experimental import pallas as pl
from jax.experimental.pallas import tpu as pltpu

F32 = jnp.float32
BF16 = jnp.bfloat16
HIGHEST = lax.Precision.HIGHEST

EPS = 1e-5
GRID_W = 64
GLA_HEADS = 4
GLA_DK = 64
GLA_DV = 128
GLA_RANK = 16
GLA_TAU = 16.0
GLA_CHUNK = 64
GLA_QK = GLA_HEADS * GLA_DK
GLA_V = GLA_HEADS * GLA_DV
NAT_HEADS = 16
NAT_HD = 32
NAT_W = NAT_HEADS * NAT_HD
NAT_KH = 8
NAT_KW = 16
NAT_GROUP = 4
N_EXPERTS = 32
TOP_K = 4
SWIGLU_LIMIT = 7.0
SWIGLU_ALPHA = 1.702

LANES = 128
SUBLANES = 8
TOKEN_TILE = 512
MOE_ROWS = 512
SWIGLU_GROUP = 1024
GLA_UNROLL = 8
NAT_UNROLL = 4
DMA_ISSUE_UNROLL = 4
SEGMENT_CHUNK = 16
LOG2E = 1.4426950408889634
VMEM_LIMIT = 56 * 1024 * 1024


def _cparams(sem, vmem=VMEM_LIMIT):
    return pltpu.CompilerParams(dimension_semantics=sem, vmem_limit_bytes=vmem)


def _dot(a, b):
    return jnp.dot(a, b, preferred_element_type=F32)


def _dot_nt(a, b, precision=None):
    return lax.dot_general(a, b, (((1,), (1,)), ((), ())), preferred_element_type=F32, precision=precision)


LANE_TILES = 8


def _row_table_shape(rows):
    return (rows * LANE_TILES, LANES)


def _store_rows(ref, x, row0=0):
    rows = x.shape[0]
    for s in range(LANE_TILES):
        ref[pl.ds(row0 * LANE_TILES + s, rows, stride=LANE_TILES), :] = x[:, s * LANES:(s + 1) * LANES]


def _load_rows(ref, row0, rows):
    return jnp.concatenate(
        [ref[pl.ds(row0 * LANE_TILES + s, rows, stride=LANE_TILES), :] for s in range(LANE_TILES)], axis=1)


def _row_slice(row0, rows=1):
    return pl.ds(pl.multiple_of(row0 * LANE_TILES, LANE_TILES), rows * LANE_TILES)


def _rows_of(ref, row0, rows):
    return ref.at[_row_slice(row0, rows), :]


def _dot_tn(a, b):
    return lax.dot_general(a, b, (((0,), (0,)), ((), ())), preferred_element_type=F32)


def _ada_kernel(c_ref, w_ref, b_ref, o_ref):
    c = c_ref[...]
    s = c * jax.nn.sigmoid(c)
    o_ref[...] = jnp.dot(s, w_ref[...], preferred_element_type=F32, precision=HIGHEST) + b_ref[...]


def _ada(c, w, b):
    nb, d = c.shape
    cols = w.shape[1]
    blk = 1024
    return pl.pallas_call(
        _ada_kernel,
        grid=(cols // blk,),
        in_specs=[pl.BlockSpec((nb, d), lambda j: (0, 0)),
                  pl.BlockSpec((d, blk), lambda j: (0, j)),
                  pl.BlockSpec((1, blk), lambda j: (0, j))],
        out_specs=pl.BlockSpec((nb, blk), lambda j: (0, j)),
        out_shape=jax.ShapeDtypeStruct((nb, cols), F32),
        compiler_params=_cparams(("arbitrary",)),
    )(c, w, b.reshape(1, cols))


def _stream_specs(block, n_prompt_tiles):
    return (pl.BlockSpec(block, lambda i, *_: (jnp.minimum(i, n_prompt_tiles - 1), 0)),
            pl.BlockSpec(block, lambda i, *_: (jnp.maximum(i - n_prompt_tiles, 0), 0)))


def _per_stream(n_prompt_tiles, body):
    i = pl.program_id(0)
    pl.when(i < n_prompt_tiles)(functools.partial(body, 0))
    pl.when(i >= n_prompt_tiles)(functools.partial(body, 1))


def _in_kernel(xp_ref, xs_ref, mod_ref, g_ref, w_ref, pg_ref, pn_ref, plr_ref, *, n_prompt_tiles):
    def body(stream):
        x = (xp_ref, xs_ref)[stream][...]
        ms = jnp.mean(x * x, axis=-1, keepdims=True)
        y = x * lax.rsqrt(ms + EPS) * g_ref[...]
        h = y * (1.0 + mod_ref[0, 1:2, :]) + mod_ref[0, 0:1, :]
        hb = h.astype(BF16)
        wg = GLA_QK * 2 + GLA_V * 2
        wn = 3 * NAT_W
        pg_ref[...] = _dot(hb, w_ref[:, 0:wg]).astype(BF16)
        pn_ref[...] = _dot(hb, w_ref[:, wg:wg + wn]).astype(BF16)
        plr_ref[...] = _dot(hb, w_ref[:, wg + wn:wg + wn + LANES])

    _per_stream(n_prompt_tiles, body)


def _in_proj(xp, xs, mod3, norm_g, w_all, batch_of_tile):
    d = xp.shape[1]
    n = xp.shape[0] + xs.shape[0]
    wg = GLA_QK * 2 + GLA_V * 2
    wn = 3 * NAT_W
    tm = TOKEN_TILE
    n_prompt_tiles = xp.shape[0] // tm
    return pl.pallas_call(
        functools.partial(_in_kernel, n_prompt_tiles=n_prompt_tiles),
        grid=(n // tm,),
        in_specs=[*_stream_specs((tm, d), n_prompt_tiles),
                  pl.BlockSpec((1, 6, d), lambda i: (batch_of_tile(i), 0, 0)),
                  pl.BlockSpec((1, d), lambda i: (0, 0)),
                  pl.BlockSpec(w_all.shape, lambda i: (0, 0))],
        out_specs=[pl.BlockSpec((tm, wg), lambda i: (i, 0)),
                   pl.BlockSpec((tm, wn), lambda i: (i, 0)),
                   pl.BlockSpec((tm, LANES), lambda i: (i, 0))],
        out_shape=[jax.ShapeDtypeStruct((n, wg), BF16),
                   jax.ShapeDtypeStruct((n, wn), BF16),
                   jax.ShapeDtypeStruct((n, LANES), F32)],
        compiler_params=_cparams(("arbitrary",)),
    )(xp, xs, mod3, norm_g.reshape(1, d), w_all)


def _gla_kernel(q_ref, k_ref, v_ref, g_ref, lr_ref, wz_ref, bz_ref, ng_ref,
                o_ref, la_ref, acc_ref, qt_ref, ke_ref, dec_ref, sf_ref, sb_ref, *, seq):
    c_len = GLA_CHUNK
    n = seq // c_len
    row = lax.broadcasted_iota(jnp.int32, (c_len, c_len), 0)
    col = lax.broadcasted_iota(jnp.int32, (c_len, c_len), 1)
    tril = col <= row
    triu = col >= row
    cum_row = lax.broadcasted_iota(jnp.int32, (c_len, 2 * c_len), 0)
    cum_col = lax.broadcasted_iota(jnp.int32, (c_len, 2 * c_len), 1) % c_len
    cum = (cum_col <= cum_row).astype(F32).astype(BF16)
    mask_f = jnp.concatenate([tril, tril], axis=0)
    mask_b = jnp.concatenate([triu, triu], axis=0)
    head_a = lax.broadcasted_iota(jnp.int32, (c_len, LANES), 1) < GLA_DK
    st_row = lax.broadcasted_iota(jnp.int32, (2 * GLA_DV, LANES), 0)
    st_col = lax.broadcasted_iota(jnp.int32, (2 * GLA_DV, LANES), 1)
    blockdiag = (st_row < GLA_DV) == (st_col < GLA_DK)

    z = _dot(lr_ref[...].astype(BF16), wz_ref[...]) + bz_ref[...]
    la_ref[...] = (jnp.minimum(z, 0.0) - jnp.log(1.0 + jnp.exp(-jnp.abs(z)))) * (1.0 / GLA_TAU)

    def chunk_rows(c):
        return pl.ds(pl.multiple_of(c * c_len, c_len), c_len)

    def stack_heads(x):
        zero = jnp.zeros_like(x)
        return jnp.concatenate([jnp.where(head_a, x, zero), jnp.where(head_a, zero, x)], axis=0).astype(BF16)

    def local(c, carry):
        rows = chunk_rows(c)
        la = la_ref[rows, :]
        hi = la.astype(BF16)
        lo = (la - hi.astype(F32)).astype(BF16)
        binc = _dot(cum, jnp.concatenate([hi, lo], axis=0))
        btot = jnp.sum(la, axis=0, keepdims=True)
        b_f = binc[:, :LANES]
        b_b = btot[:, LANES:] - binc[:, LANES:] + la[:, LANES:]
        e_tot = jnp.exp(btot)
        q = q_ref[rows, :].astype(F32) * (GLA_DK ** -0.5)
        k = k_ref[rows, :].astype(F32)
        qt_f = q * jnp.exp(b_f)
        qt_b = q * jnp.exp(b_b)
        kt_f = k * jnp.exp(-b_f)
        kt_b = k * jnp.exp(-b_b)
        a = (jnp.where(mask_f, _dot_nt(stack_heads(qt_f), kt_f.astype(BF16)), 0.0)
             + jnp.where(mask_b, _dot_nt(stack_heads(qt_b), kt_b.astype(BF16)), 0.0))
        oi = _dot(a.astype(BF16), v_ref[rows, :])
        acc_ref[rows, :] = jnp.concatenate([oi[0:c_len, 0:GLA_DV], oi[c_len:, GLA_DV:]], axis=1)
        qt_ref[rows, :] = jnp.concatenate([qt_f, qt_b], axis=1).astype(BF16)
        ke_ref[rows, :] = jnp.concatenate([kt_f * e_tot[:, :LANES], kt_b * e_tot[:, LANES:]], axis=1).astype(BF16)
        dec_ref[pl.ds(pl.multiple_of(c * SUBLANES, SUBLANES), SUBLANES), :] = jnp.broadcast_to(e_tot, (SUBLANES, 2 * LANES))
        return carry

    def carried(c, s_ref, half):
        rows = chunk_rows(c)
        lanes = slice(half * LANES, (half + 1) * LANES)
        st = s_ref[...]
        acc_ref[rows, :] += _dot_nt(qt_ref[rows, lanes], st.astype(BF16))
        upd = _dot_tn(v_ref[rows, :], ke_ref[rows, lanes])
        dec = dec_ref[pl.ds(pl.multiple_of(c * SUBLANES, SUBLANES), 1), lanes]
        s_ref[...] = st * dec + jnp.where(blockdiag, upd, 0.0)

    def scan(c, carry):
        carried(c, sf_ref, 0)
        carried(n - 1 - c, sb_ref, 1)
        return carry

    def finalize(c, carry):
        rows = chunk_rows(c)
        o = acc_ref[rows, :]
        halves = []
        for hh in range(2):
            oh = o[:, hh * GLA_DV:(hh + 1) * GLA_DV]
            ms = jnp.mean(oh * oh, axis=-1, keepdims=True)
            halves.append(oh * lax.rsqrt(ms + EPS) * ng_ref[...])
        y = jnp.concatenate(halves, axis=1)
        g = g_ref[rows, :].astype(F32)
        o_ref[rows, :] = (y * (g * jax.nn.sigmoid(g))).astype(o_ref.dtype)
        return carry

    sf_ref[...] = jnp.zeros_like(sf_ref)
    sb_ref[...] = jnp.zeros_like(sb_ref)
    lax.fori_loop(0, n, local, 0, unroll=GLA_UNROLL)
    lax.fori_loop(0, n, scan, 0, unroll=GLA_UNROLL)
    lax.fori_loop(0, n, finalize, 0, unroll=GLA_UNROLL)


def _gla(pg, plr, wz, bz, norm_g, *, seq, batches, block0):
    pair_w = 2 * GLA_DK
    pair_v = 2 * GLA_DV
    k0 = GLA_QK // pair_w
    v0 = 2 * GLA_QK // pair_v
    g0 = (2 * GLA_QK + GLA_V) // pair_v
    n_chunks = seq // GLA_CHUNK
    in_specs = [pl.BlockSpec((seq, pair_w), lambda b, p: (block0 + b, p)),
                pl.BlockSpec((seq, pair_w), lambda b, p: (block0 + b, k0 + p)),
                pl.BlockSpec((seq, pair_v), lambda b, p: (block0 + b, v0 + p)),
                pl.BlockSpec((seq, pair_v), lambda b, p: (block0 + b, g0 + p)),
                pl.BlockSpec((seq, LANES), lambda b, p: (block0 + b, 0)),
                pl.BlockSpec((LANES, 2 * pair_w), lambda b, p: (0, p)),
                pl.BlockSpec((1, 2 * pair_w), lambda b, p: (0, p)),
                pl.BlockSpec((1, GLA_DV), lambda b, p: (0, 0))]
    return pl.pallas_call(
        functools.partial(_gla_kernel, seq=seq),
        grid=(batches, GLA_HEADS // 2),
        in_specs=in_specs,
        out_specs=pl.BlockSpec((seq, pair_v), lambda b, p: (b, p)),
        out_shape=jax.ShapeDtypeStruct((batches * seq, GLA_V), BF16),
        scratch_shapes=[pltpu.VMEM((seq, 2 * pair_w), F32),
                        pltpu.VMEM((seq, pair_v), F32),
                        pltpu.VMEM((seq, 2 * pair_w), BF16),
                        pltpu.VMEM((seq, 2 * pair_w), BF16),
                        pltpu.VMEM((n_chunks * SUBLANES, 2 * pair_w), F32),
                        pltpu.VMEM((pair_v, pair_w), F32),
                        pltpu.VMEM((pair_v, pair_w), F32)],
        compiler_params=_cparams(("parallel", "parallel")),
    )(pg, pg, pg, pg, plr, wz, bz, norm_g)


def _nat_kernel(q_ref, k_ref, v_ref, tab_ref, o_ref, *, rows):
    w = GRID_W
    head = lax.broadcasted_iota(jnp.int32, (w, LANES), 1) // NAT_HD
    win = NAT_KH * w

    def body(i, carry):
        r0 = jnp.clip(i - NAT_KH // 2, 0, rows - NAT_KH)
        base = r0 - i + NAT_KH - 1
        qrows = pl.ds(pl.multiple_of(i * w, w), w)
        krows = pl.ds(pl.multiple_of(r0 * w, w), win)
        q = q_ref[qrows, :]
        zero = jnp.zeros_like(q)
        q_stack = jnp.concatenate([jnp.where(head == h, q, zero) for h in range(NAT_GROUP)], axis=0)
        s = _dot_nt(q_stack, k_ref[krows, :]) + tab_ref[0, base]
        m = jnp.max(s, axis=-1, keepdims=True)
        e = jnp.exp2(s - m)
        l = jnp.sum(e, axis=-1, keepdims=True)
        o = _dot(e.astype(BF16), v_ref[krows, :]) / l
        out = jnp.zeros((w, LANES), F32)
        for h in range(NAT_GROUP):
            out = out + jnp.where(head == h, o[h * w:(h + 1) * w, :], 0.0)
        o_ref[qrows, :] = out.astype(o_ref.dtype)
        return carry

    lax.fori_loop(0, rows, body, 0, unroll=NAT_UNROLL)


def _nat(pn, tab, *, seq, batches, block0):
    rows = seq // GRID_W
    groups = NAT_HEADS // NAT_GROUP
    in_specs = [pl.BlockSpec((seq, LANES), lambda g, b: (block0 + b, g)),
                pl.BlockSpec((seq, LANES), lambda g, b: (block0 + b, groups + g)),
                pl.BlockSpec((seq, LANES), lambda g, b: (block0 + b, 2 * groups + g)),
                pl.BlockSpec((1,) + tab.shape[1:], lambda g, b: (g, 0, 0, 0))]
    return pl.pallas_call(
        functools.partial(_nat_kernel, rows=rows),
        grid=(groups, batches),
        in_specs=in_specs,
        out_specs=pl.BlockSpec((seq, LANES), lambda g, b: (b, g)),
        out_shape=jax.ShapeDtypeStruct((batches * seq, NAT_W), BF16),
        compiler_params=_cparams(("parallel", "parallel")),
    )(pn, pn, pn, tab)


def _nat_bias_table(rpb):
    w = GRID_W
    jq = np.arange(w)[:, None]
    jk = np.arange(w)[None, :]
    c0 = np.clip(jq - NAT_KW // 2, 0, w - NAT_KW)
    valid = (jk >= c0) & (jk < c0 + NAT_KW)
    dcol = np.clip(jk - jq + NAT_KW - 1, 0, 2 * NAT_KW - 2)
    drow = np.arange(NAT_KH)[:, None] + np.arange(NAT_KH)[None, :]
    t = rpb.astype(F32)[:, drow][:, :, :, dcol]
    t = jnp.where(valid[None, None, None], t * LOG2E, -jnp.inf)
    t = t.transpose(0, 1, 3, 2, 4).reshape(NAT_HEADS, NAT_KH, w, NAT_KH * w)
    t = t.reshape(NAT_HEADS // NAT_GROUP, NAT_GROUP, NAT_KH, w, NAT_KH * w).transpose(0, 2, 1, 3, 4)
    return t.reshape(NAT_HEADS // NAT_GROUP, NAT_KH, NAT_GROUP * w, NAT_KH * w)


def _out_kernel(ogp_ref, ogs_ref, onp_ref, ons_ref, xp_ref, xs_ref, mod_ref, wg_ref, wn_ref, n2_ref, rw_ref, rb_ref,
                x1_ref, h2_ref, ti_ref, gt_ref, rk_ref, cnt_ref, tcnt_ref, tcar_ref, carry_ref, *, n_prompt_tiles):
    tm = x1_ref.shape[0]

    @pl.when(pl.program_id(0) == 0)
    def _():
        carry_ref[...] = jnp.zeros_like(carry_ref)

    def residual(stream):
        og_ref, on_ref, x_ref = ((ogp_ref, onp_ref, xp_ref), (ogs_ref, ons_ref, xs_ref))[stream]
        mix = _dot(og_ref[...], wg_ref[...]) + _dot(on_ref[...], wn_ref[...])
        x1_ref[...] = x_ref[...] + mod_ref[0, 2:3, :] * mix

    _per_stream(n_prompt_tiles, residual)
    x1 = x1_ref[...]
    ms = jnp.mean(x1 * x1, axis=-1, keepdims=True)
    h2 = x1 * lax.rsqrt(ms + EPS) * n2_ref[...]
    h2 = h2 * (1.0 + mod_ref[0, 4:5, :]) + mod_ref[0, 3:4, :]
    _store_rows(h2_ref, h2)

    logits = _dot_nt(rw_ref[...], h2, precision=HIGHEST) + rb_ref[...]
    eidx = lax.broadcasted_iota(jnp.int32, logits.shape, 0)
    vals = logits
    top_v, top_i = [], []
    for _ in range(TOP_K):
        m = jnp.max(vals, axis=0, keepdims=True)
        idx = jnp.min(jnp.where(vals == m, eidx, N_EXPERTS), axis=0, keepdims=True)
        top_v.append(m)
        top_i.append(idx)
        vals = jnp.where(eidx == idx, -jnp.inf, vals)
    ev = [jnp.exp(v - top_v[0]) for v in top_v]
    den = ev[0] + ev[1] + ev[2] + ev[3]
    ti_ref[...] = jnp.concatenate(top_i, axis=0)
    gt_ref[...] = jnp.concatenate([e / den for e in ev], axis=0)

    onehot = jnp.zeros(logits.shape, F32)
    for idx in top_i:
        onehot = onehot + (eidx == idx).astype(F32)
    r = lax.broadcasted_iota(jnp.int32, (tm, tm), 0)
    c = lax.broadcasted_iota(jnp.int32, (tm, tm), 1)
    earlier = (r < c).astype(BF16)
    before = _dot(onehot.astype(BF16), earlier)
    rk_ref[...] = jnp.concatenate(
        [jnp.sum(jnp.where(eidx == idx, before, 0.0), axis=0, keepdims=True) for idx in top_i], axis=0).astype(jnp.int32)
    tile_counts = jnp.sum(onehot, axis=1, keepdims=True)
    tcar_ref[0] = jnp.broadcast_to(carry_ref[...], tcar_ref.shape[1:])
    tcnt_ref[0] = jnp.broadcast_to(tile_counts, tcnt_ref.shape[1:])
    carry_ref[...] = carry_ref[...] + tile_counts
    cnt_ref[...] = jnp.broadcast_to(carry_ref[...], cnt_ref.shape)


def _out_proj(og, on, x, mod3, w_g, w_n, norm_g, rw_t, rb, batch_of_tile):
    d = x[0].shape[1]
    n = x[0].shape[0] + x[1].shape[0]
    tm = TOKEN_TILE
    n_prompt_tiles = x[0].shape[0] // tm
    tok = lambda i: (i, 0)
    const = lambda i: (0, 0)
    return pl.pallas_call(
        functools.partial(_out_kernel, n_prompt_tiles=n_prompt_tiles),
        grid=(n // tm,),
        in_specs=[*_stream_specs((tm, GLA_V), n_prompt_tiles),
                  *_stream_specs((tm, NAT_W), n_prompt_tiles),
                  *_stream_specs((tm, d), n_prompt_tiles),
                  pl.BlockSpec((1, 6, d), lambda i: (batch_of_tile(i), 0, 0)),
                  pl.BlockSpec(w_g.shape, const),
                  pl.BlockSpec(w_n.shape, const),
                  pl.BlockSpec((1, d), const),
                  pl.BlockSpec(rw_t.shape, const),
                  pl.BlockSpec((N_EXPERTS, 1), const)],
        out_specs=[pl.BlockSpec((tm, d), tok),
                   pl.BlockSpec(_row_table_shape(tm), tok),
                   pl.BlockSpec((TOP_K, tm), lambda i: (0, i)),
                   pl.BlockSpec((TOP_K, tm), lambda i: (0, i)),
                   pl.BlockSpec((TOP_K, tm), lambda i: (0, i)),
                   pl.BlockSpec((N_EXPERTS, LANES), const),
                   pl.BlockSpec((1, N_EXPERTS, LANES), lambda i: (i, 0, 0)),
                   pl.BlockSpec((1, N_EXPERTS, LANES), lambda i: (i, 0, 0))],
        out_shape=[jax.ShapeDtypeStruct((n, d), F32),
                   jax.ShapeDtypeStruct(_row_table_shape(n), F32),
                   jax.ShapeDtypeStruct((TOP_K, n), jnp.int32),
                   jax.ShapeDtypeStruct((TOP_K, n), F32),
                   jax.ShapeDtypeStruct((TOP_K, n), jnp.int32),
                   jax.ShapeDtypeStruct((N_EXPERTS, LANES), F32),
                   jax.ShapeDtypeStruct((n // tm, N_EXPERTS, LANES), F32),
                   jax.ShapeDtypeStruct((n // tm, N_EXPERTS, LANES), F32)],
        scratch_shapes=[pltpu.VMEM((N_EXPERTS, 1), F32)],
        compiler_params=_cparams(("arbitrary",)),
    )(*og, *on, *x, mod3, w_g, w_n, norm_g.reshape(1, d), rw_t, rb.reshape(N_EXPERTS, 1))


def _row_copy(src, dst, sem):
    return pltpu.make_async_copy(src, dst, sem)


def _segment_copies(cnt_ref, far_ref, near_ref, tile, copy):
    def per_expert(e, carry):
        idx = tile * N_EXPERTS + e
        c, far, near = cnt_ref[idx], far_ref[idx], near_ref[idx]

        def whole(i, carry2):
            copy(near + i * SEGMENT_CHUNK, far + i * SEGMENT_CHUNK, SEGMENT_CHUNK).start()
            return carry2

        lax.fori_loop(0, c // SEGMENT_CHUNK, whole, 0)
        for bit in reversed(range(SEGMENT_CHUNK.bit_length() - 1)):
            size = 1 << bit
            done = c & -(2 * size)

            @pl.when((c & size) != 0)
            def _(done=done, size=size):
                copy(near + done, far + done, size).start()

        return carry

    lax.fori_loop(0, N_EXPERTS, per_expert, 0)


def _dispatch_kernel(cnt_ref, far_ref, near_ref, pos_ref, h_ref, xs_hbm, buf_ref, sem):
    tt = pos_ref.shape[1]
    pairs = TOP_K * tt
    i = pl.program_id(0)
    buf = i % 2
    base = buf * pairs

    def place(t, carry):
        row = h_ref[_row_slice(t), :]
        for k in range(TOP_K):
            buf_ref[_row_slice(base + pos_ref[k, t]), :] = row
        return carry

    lax.fori_loop(0, tt, place, 0, unroll=DMA_ISSUE_UNROLL)

    def copy(near, far, rows):
        return _row_copy(_rows_of(buf_ref, base + near, rows), _rows_of(xs_hbm, far, rows), sem.at[buf])

    _segment_copies(cnt_ref, far_ref, near_ref, i, copy)

    def drain(b):
        _row_copy(_rows_of(buf_ref, b * pairs, pairs), _rows_of(xs_hbm, 0, pairs), sem.at[b]).wait()

    pl.when(i > 0)(lambda: drain(1 - buf))
    pl.when(i == pl.num_programs(0) - 1)(lambda: drain(buf))


def _dispatch(seg, pos_t, h2):
    n = h2.shape[0] // LANE_TILES
    tt = TOKEN_TILE
    grid_spec = pltpu.PrefetchScalarGridSpec(
        num_scalar_prefetch=3,
        grid=(n // tt,),
        in_specs=[pl.BlockSpec((TOP_K, tt), lambda i, *_: (0, i), memory_space=pltpu.SMEM),
                  pl.BlockSpec(_row_table_shape(tt), lambda i, *_: (i, 0))],
        out_specs=pl.BlockSpec(memory_space=pl.ANY),
        scratch_shapes=[pltpu.VMEM(_row_table_shape(2 * TOP_K * tt), F32),
                        pltpu.SemaphoreType.DMA((2,))],
    )
    return pl.pallas_call(
        _dispatch_kernel,
        grid_spec=grid_spec,
        out_shape=jax.ShapeDtypeStruct(_row_table_shape(n * TOP_K), F32),
        compiler_params=_cparams(("arbitrary",)),
    )(*seg, pos_t, h2)


def _swiglu_col(feature0, up):
    group, within = divmod(feature0, SWIGLU_GROUP)
    return (2 * group + up) * SWIGLU_GROUP + within


def _deinterleave_kernel(w_ref, o_ref):
    f2 = w_ref.shape[2]
    slab = 2 * LANES
    r = lax.broadcasted_iota(jnp.int32, (slab, slab), 0)
    c = lax.broadcasted_iota(jnp.int32, (slab, slab), 1)
    perm = (r == jnp.where(c < LANES, 2 * c, 2 * (c - LANES) + 1)).astype(BF16)
    for j in range(f2 // slab):
        y = _dot(w_ref[0, :, j * slab:(j + 1) * slab].astype(BF16), perm)
        for up in range(2):
            c0 = _swiglu_col(j * LANES, up)
            o_ref[0, :, c0:c0 + LANES] = y[:, up * LANES:(up + 1) * LANES].astype(BF16)


def _deinterleave(w):
    e, d, f2 = w.shape
    return pl.pallas_call(
        _deinterleave_kernel,
        grid=(e,),
        in_specs=[pl.BlockSpec((1, d, f2), lambda i: (i, 0, 0))],
        out_specs=pl.BlockSpec((1, d, f2), lambda i: (i, 0, 0)),
        out_shape=jax.ShapeDtypeStruct((e, d, f2), BF16),
        compiler_params=_cparams(("parallel",)),
    )(w)


def _expert_kernel(blk_ref, exp_ref, lo_ref, hi_ref, xs_ref, wgu_ref, bgu_ref, wd_ref, bd_ref, ys_ref):
    del blk_ref, exp_ref
    rows = xs_ref.shape[0] // LANE_TILES
    f = wd_ref.shape[1]
    i = pl.program_id(0)
    lo = lo_ref[i]
    hi = hi_ref[i]

    @pl.when(hi > lo)
    def _():
        g = SWIGLU_GROUP
        x = _load_rows(xs_ref, 0, rows).astype(BF16)
        acts = []
        for j in range(f // g):
            gu = _dot(x, wgu_ref[0, :, 2 * j * g:2 * (j + 1) * g]) + bgu_ref[0, :, 2 * j * g:2 * (j + 1) * g]
            gate = jnp.minimum(gu[:, :g], SWIGLU_LIMIT)
            up = jnp.clip(gu[:, g:], -SWIGLU_LIMIT, SWIGLU_LIMIT)
            acts.append(((up + 1.0) * gate * jax.nn.sigmoid(SWIGLU_ALPHA * gate)).astype(BF16))
        y = _dot(jnp.concatenate(acts, axis=1), wd_ref[0]) + bd_ref[0]
        r = lax.broadcasted_iota(jnp.int32, y.shape, 0)
        mine = (r >= lo) & (r < hi)

        @pl.when(lo == 0)
        def _():
            _store_rows(ys_ref, jnp.where(mine, y, 0.0))

        @pl.when(lo > 0)
        def _():
            _store_rows(ys_ref, jnp.where(mine, y, _load_rows(ys_ref, 0, rows)))


def _experts(items, xs2, w_gu, b_gu, w_down, b_down):
    item_blk, item_exp, item_lo, item_hi = items
    d, f2 = w_gu.shape[1], w_gu.shape[2]
    wmap = lambda i, blk, exp, lo, hi: (exp[i], 0, 0)
    xmap = lambda i, blk, exp, lo, hi: (blk[i], 0)
    grid_spec = pltpu.PrefetchScalarGridSpec(
        num_scalar_prefetch=4,
        grid=(item_blk.shape[0],),
        in_specs=[pl.BlockSpec(_row_table_shape(MOE_ROWS), xmap),
                  pl.BlockSpec((1, d, f2), wmap),
                  pl.BlockSpec((1, 1, f2), wmap),
                  pl.BlockSpec((1, f2 // 2, d), wmap),
                  pl.BlockSpec((1, 1, d), wmap)],
        out_specs=pl.BlockSpec(_row_table_shape(MOE_ROWS), xmap),
    )
    return pl.pallas_call(
        _expert_kernel,
        grid_spec=grid_spec,
        out_shape=jax.ShapeDtypeStruct(xs2.shape, F32),
        compiler_params=_cparams(("arbitrary",)),
    )(item_blk, item_exp, item_lo, item_hi, xs2, w_gu, b_gu, w_down, b_down)


def _expert_work_items(counts, n_rows):
    ends = jnp.cumsum(counts)
    starts = ends - counts
    n_blk = n_rows // MOE_ROWS
    cuts = jnp.sort(jnp.concatenate([jnp.arange(n_blk, dtype=jnp.int32) * MOE_ROWS, starts[1:]]))
    nxt = jnp.concatenate([cuts[1:], jnp.full((1,), n_rows, jnp.int32)])
    blk = jnp.minimum(cuts // MOE_ROWS, n_blk - 1)
    exp = jnp.minimum(jnp.sum(ends[None, :] <= cuts[:, None], axis=1), N_EXPERTS - 1).astype(jnp.int32)
    lo = cuts - blk * MOE_ROWS
    hi = nxt - blk * MOE_ROWS
    return (blk.astype(jnp.int32), exp, lo.astype(jnp.int32), hi.astype(jnp.int32)), starts


def _combine_kernel(cnt_ref, far_ref, near_ref, pos_ref, gates_ref, x1_ref, mod_ref, fg_ref, ys_hbm,
                    op_ref, os_ref, buf_ref, y_ref, sem, *, n_prompt_tiles):
    tt = x1_ref.shape[0]
    pairs = TOP_K * tt
    i = pl.program_id(0)
    buf = i % 2
    base = buf * pairs

    def fetch(tile, b):
        def copy(near, far, rows):
            return _row_copy(_rows_of(ys_hbm, far, rows), _rows_of(buf_ref, b * pairs + near, rows), sem.at[b])

        _segment_copies(cnt_ref, far_ref, near_ref, tile, copy)

    pl.when(i == 0)(lambda: fetch(0, 0))
    pl.when(i + 1 < pl.num_programs(0))(lambda: fetch(i + 1, 1 - buf))
    _row_copy(_rows_of(ys_hbm, 0, pairs), _rows_of(buf_ref, base, pairs), sem.at[buf]).wait()

    def weighted_sum(t, carry):
        acc = None
        for k in range(TOP_K):
            term = gates_ref[k, t] * buf_ref[_row_slice(base + pos_ref[k, t]), :]
            acc = term if acc is None else acc + term
        y_ref[_row_slice(t), :] = acc
        return carry

    lax.fori_loop(0, tt, weighted_sum, 0, unroll=DMA_ISSUE_UNROLL)

    x2 = x1_ref[...] + mod_ref[0, 5:6, :] * _load_rows(y_ref, 0, tt)
    ms = jnp.mean(x2 * x2, axis=-1, keepdims=True)
    out = x2 * lax.rsqrt(ms + EPS) * fg_ref[...]

    def write(stream):
        (op_ref, os_ref)[stream][...] = out

    _per_stream(n_prompt_tiles, write)


def _combine(seg, pos_t, gates_t, x1, mod3, final_g, ys, batch_of_tile, n_prompt):
    n, d = x1.shape
    tt = TOKEN_TILE
    n_prompt_tiles = n_prompt // tt
    grid_spec = pltpu.PrefetchScalarGridSpec(
        num_scalar_prefetch=3,
        grid=(n // tt,),
        in_specs=[pl.BlockSpec((TOP_K, tt), lambda i, *_: (0, i), memory_space=pltpu.SMEM),
                  pl.BlockSpec((TOP_K, tt), lambda i, *_: (0, i), memory_space=pltpu.SMEM),
                  pl.BlockSpec((tt, d), lambda i, *_: (i, 0)),
                  pl.BlockSpec((1, 6, d), lambda i, *_: (batch_of_tile(i), 0, 0)),
                  pl.BlockSpec((1, d), lambda i, *_: (0, 0)),
                  pl.BlockSpec(memory_space=pl.ANY)],
        out_specs=list(_stream_specs((tt, d), n_prompt_tiles)),
        scratch_shapes=[pltpu.VMEM(_row_table_shape(2 * TOP_K * tt), F32),
                        pltpu.VMEM(_row_table_shape(tt), F32),
                        pltpu.SemaphoreType.DMA((2,))],
    )
    return pl.pallas_call(
        functools.partial(_combine_kernel, n_prompt_tiles=n_prompt_tiles),
        grid_spec=grid_spec,
        out_shape=[jax.ShapeDtypeStruct((n_prompt, d), F32),
                   jax.ShapeDtypeStruct((n - n_prompt, d), F32)],
        compiler_params=_cparams(("arbitrary",)),
    )(*seg, pos_t, gates_t, x1, mod3, final_g.reshape(1, d), ys)


def _batch_of_tile_fn(tile, bp, tp, ts):
    n_prompt = bp * tp

    def batch_of_tile(i):
        t = i * tile
        return jnp.where(t < n_prompt, t // tp, bp + (t - n_prompt) // ts)

    return batch_of_tile


def kernel(x_prompt, x_sample, c_prompt, c_sample, norm1_g, w_ada, b_ada, w_in, gla_w2_fwd, gla_b2_fwd, gla_w2_bwd, gla_b2_bwd, gla_norm_g, nat_rpb, w_out, norm2_g, router_w, router_b, w_gate_up, b_gate_up, w_down, b_down, final_norm_g):
    assert w_ada.shape[0] == 1, "single-layer encoder"
    bp, tp, d = x_prompt.shape
    bs, ts, _ = x_sample.shape
    n_prompt, n_sample = bp * tp, bs * ts
    n = n_prompt + n_sample
    assert tp % TOKEN_TILE == 0 and ts % TOKEN_TILE == 0 and n_prompt % ts == 0
    assert d == LANE_TILES * LANES

    x = (x_prompt.reshape(n_prompt, d), x_sample.reshape(n_sample, d))
    c = jnp.concatenate([c_prompt, c_sample], axis=0)

    sizes = (GLA_QK, GLA_QK, GLA_V, GLA_V, GLA_RANK, GLA_RANK, NAT_W, NAT_W, NAT_W)
    offs = np.concatenate([[0], np.cumsum(sizes)])
    w_in0 = w_in[0]
    seg = lambda j: w_in0[:, offs[j]:offs[j + 1]]
    w_all = jnp.concatenate(
        [seg(0), seg(1), seg(2), seg(3), seg(6) * (NAT_HD ** -0.5 * LOG2E), seg(7), seg(8), seg(4), seg(5),
         jnp.zeros((d, LANES - 2 * GLA_RANK), F32)], axis=1).astype(BF16)
    pairs = GLA_HEADS // 2
    w2 = jnp.zeros((LANES, pairs, 2, 2 * GLA_DK), F32)
    w2 = w2.at[0:GLA_RANK, :, 0].set(gla_w2_fwd[0].reshape(GLA_RANK, pairs, 2 * GLA_DK))
    w2 = w2.at[GLA_RANK:2 * GLA_RANK, :, 1].set(gla_w2_bwd[0].reshape(GLA_RANK, pairs, 2 * GLA_DK))
    wz = w2.reshape(LANES, pairs * 4 * GLA_DK).astype(BF16)
    bz = jnp.stack([gla_b2_fwd[0].reshape(pairs, 2 * GLA_DK), gla_b2_bwd[0].reshape(pairs, 2 * GLA_DK)],
                   axis=1).reshape(1, pairs * 4 * GLA_DK)
    tab = _nat_bias_table(nat_rpb[0])
    w_og = w_out[0, :GLA_V].astype(BF16)
    w_on = w_out[0, GLA_V:].astype(BF16)
    rw_t = router_w[0].T
    w_gu = _deinterleave(w_gate_up[0])
    n_e, f2 = b_gate_up.shape[1:]
    b_gu = (b_gate_up[0].reshape(n_e, f2 // (2 * SWIGLU_GROUP), SWIGLU_GROUP, 2)
            .transpose(0, 1, 3, 2).reshape(n_e, 1, f2))
    w_dn = w_down[0].astype(BF16)
    b_dn = b_down[0, :, None, :]

    mod3 = _ada(c, w_ada[0], b_ada[0]).reshape(bp + bs, 6, d)
    bot_tok = _batch_of_tile_fn(TOKEN_TILE, bp, tp, ts)

    pg, pn, plr = _in_proj(*x, mod3, norm1_g[0], w_all, bot_tok)

    gla_args = (pg, plr, wz, bz, gla_norm_g[0].reshape(1, -1))
    og = (_gla(*gla_args, seq=tp, batches=bp, block0=0),
          _gla(*gla_args, seq=ts, batches=bs, block0=n_prompt // ts))
    on = (_nat(pn, tab, seq=tp, batches=bp, block0=0),
          _nat(pn, tab, seq=ts, batches=bs, block0=n_prompt // ts))

    x1, h2, top_i, gates_t, rank_t, cnt, tile_cnt, tile_carry = _out_proj(
        og, on, x, mod3, w_og, w_on, norm2_g[0], rw_t, router_b[0], bot_tok)

    n_rows = n * TOP_K
    n_tiles = n // TOKEN_TILE
    assert n_rows % MOE_ROWS == 0
    counts = cnt[:, 0].astype(jnp.int32)
    items, starts = _expert_work_items(counts, n_rows)
    seg_cnt = tile_cnt[:, :, 0].astype(jnp.int32)
    seg_far = starts[None, :] + tile_carry[:, :, 0].astype(jnp.int32)
    seg_near = jnp.cumsum(seg_cnt, axis=1) - seg_cnt
    seg = tuple(a.reshape(n_tiles * N_EXPERTS) for a in (seg_cnt, seg_far, seg_near))
    chosen = (top_i.reshape(1, TOP_K, n_tiles, TOKEN_TILE)
              == jnp.arange(N_EXPERTS, dtype=jnp.int32)[:, None, None, None])
    pos_t = rank_t + jnp.sum(jnp.where(chosen, seg_near.T[:, None, :, None], 0), axis=0).reshape(TOP_K, n)

    xs = _dispatch(seg, pos_t, h2)
    ys = _experts(items, xs, w_gu, b_gu, w_dn, b_dn)
    y_prompt, y_sample = _combine(seg, pos_t, gates_t, x1, mod3, final_norm_g, ys, bot_tok, n_prompt)

    return (y_prompt.reshape(bp, tp, d), y_sample.reshape(bs, ts, d))
```

```python
import functools

import numpy as np
import jax
import jax.numpy as jnp
from jax import lax
from jax.experimental import pallas as pl
from jax.experimental.pallas import tpu as pltpu

F32 = jnp.float32
BF16 = jnp.bfloat16
HIGHEST = lax.Precision.HIGHEST

EPS = 1e-5
GRID_W = 64
GLA_HEADS = 4
GLA_DK = 64
GLA_DV = 128
GLA_RANK = 16
GLA_TAU = 16.0
GLA_CHUNK = 64
GLA_QK = GLA_HEADS * GLA_DK
GLA_V = GLA_HEADS * GLA_DV
NAT_HEADS = 16
NAT_HD = 32
NAT_W = NAT_HEADS * NAT_HD
NAT_KH = 8
NAT_KW = 16
NAT_GROUP = 4
N_EXPERTS = 32
TOP_K = 4
SWIGLU_LIMIT = 7.0
SWIGLU_ALPHA = 1.702

LANES = 128
SUBLANES = 8
TOKEN_TILE = 512
MOE_ROWS = 512
SWIGLU_GROUP = 1024
GLA_UNROLL = 8
NAT_UNROLL = 4
ROW_LOOP_UNROLL = 8
SEGMENT_CHUNK = 16
LOG2E = 1.4426950408889634
VMEM_LIMIT = 56 * 1024 * 1024


def _cparams(sem, vmem=VMEM_LIMIT):
    return pltpu.CompilerParams(dimension_semantics=sem, vmem_limit_bytes=vmem)


def _dot(a, b):
    return jnp.dot(a, b, preferred_element_type=F32)


def _dot_nt(a, b, precision=None):
    return lax.dot_general(a, b, (((1,), (1,)), ((), ())), preferred_element_type=F32, precision=precision)


LANE_TILES = 8


def _row_table_shape(rows):
    return (rows * LANE_TILES, LANES)


def _store_rows(ref, x, row0=0):
    rows = x.shape[0]
    for s in range(LANE_TILES):
        ref[pl.ds(row0 * LANE_TILES + s, rows, stride=LANE_TILES), :] = x[:, s * LANES:(s + 1) * LANES]


def _load_rows(ref, row0, rows):
    return jnp.concatenate(
        [ref[pl.ds(row0 * LANE_TILES + s, rows, stride=LANE_TILES), :] for s in range(LANE_TILES)], axis=1)


def _row_slice(row0, rows=1):
    return pl.ds(pl.multiple_of(row0 * LANE_TILES, LANE_TILES), rows * LANE_TILES)


def _rows_of(ref, row0, rows):
    return ref.at[_row_slice(row0, rows), :]


def _dot_tn(a, b):
    return lax.dot_general(a, b, (((0,), (0,)), ((), ())), preferred_element_type=F32)


def _ada_kernel(c_ref, w_ref, b_ref, o_ref):
    c = c_ref[...]
    s = c * jax.nn.sigmoid(c)
    o_ref[...] = jnp.dot(s, w_ref[...], preferred_element_type=F32, precision=HIGHEST) + b_ref[...]


def _ada(c, w, b):
    nb, d = c.shape
    cols = w.shape[1]
    blk = 1024
    return pl.pallas_call(
        _ada_kernel,
        grid=(cols // blk,),
        in_specs=[pl.BlockSpec((nb, d), lambda j: (0, 0)),
                  pl.BlockSpec((d, blk), lambda j: (0, j)),
                  pl.BlockSpec((1, blk), lambda j: (0, j))],
        out_specs=pl.BlockSpec((nb, blk), lambda j: (0, j)),
        out_shape=jax.ShapeDtypeStruct((nb, cols), F32),
        compiler_params=_cparams(("arbitrary",)),
    )(c, w, b.reshape(1, cols))


def _stream_specs(block, n_prompt_tiles):
    return (pl.BlockSpec(block, lambda i, *_: (jnp.minimum(i, n_prompt_tiles - 1), 0)),
            pl.BlockSpec(block, lambda i, *_: (jnp.maximum(i - n_prompt_tiles, 0), 0)))


def _per_stream(n_prompt_tiles, body):
    i = pl.program_id(0)
    pl.when(i < n_prompt_tiles)(functools.partial(body, 0))
    pl.when(i >= n_prompt_tiles)(functools.partial(body, 1))


def _in_kernel(xp_ref, xs_ref, mod_ref, g_ref, w_ref, pg_ref, pn_ref, plr_ref, *, n_prompt_tiles):
    def body(stream):
        x = (xp_ref, xs_ref)[stream][...]
        ms = jnp.mean(x * x, axis=-1, keepdims=True)
        y = x * lax.rsqrt(ms + EPS) * g_ref[...]
        h = y * (1.0 + mod_ref[0, 1:2, :]) + mod_ref[0, 0:1, :]
        hb = h.astype(BF16)
        wg = GLA_QK * 2 + GLA_V * 2
        wn = 3 * NAT_W
        pg_ref[...] = _dot(hb, w_ref[:, 0:wg]).astype(BF16)
        pn_ref[...] = _dot(hb, w_ref[:, wg:wg + wn]).astype(BF16)
        plr_ref[...] = _dot(hb, w_ref[:, wg + wn:wg + wn + LANES])

    _per_stream(n_prompt_tiles, body)


def _in_proj(xp, xs, mod3, norm_g, w_all, batch_of_tile):
    d = xp.shape[1]
    n = xp.shape[0] + xs.shape[0]
    wg = GLA_QK * 2 + GLA_V * 2
    wn = 3 * NAT_W
    tm = TOKEN_TILE
    n_prompt_tiles = xp.shape[0] // tm
    return pl.pallas_call(
        functools.partial(_in_kernel, n_prompt_tiles=n_prompt_tiles),
        grid=(n // tm,),
        in_specs=[*_stream_specs((tm, d), n_prompt_tiles),
                  pl.BlockSpec((1, 6, d), lambda i: (batch_of_tile(i), 0, 0)),
                  pl.BlockSpec((1, d), lambda i: (0, 0)),
                  pl.BlockSpec(w_all.shape, lambda i: (0, 0))],
        out_specs=[pl.BlockSpec((tm, wg), lambda i: (i, 0)),
                   pl.BlockSpec((tm, wn), lambda i: (i, 0)),
                   pl.BlockSpec((tm, LANES), lambda i: (i, 0))],
        out_shape=[jax.ShapeDtypeStruct((n, wg), BF16),
                   jax.ShapeDtypeStruct((n, wn), BF16),
                   jax.ShapeDtypeStruct((n, LANES), F32)],
        compiler_params=_cparams(("arbitrary",)),
    )(xp, xs, mod3, norm_g.reshape(1, d), w_all)


def _gla_kernel(q_ref, k_ref, v_ref, g_ref, lr_ref, wz_ref, bz_ref, ng_ref,
                o_ref, la_ref, acc_ref, qt_ref, ke_ref, dec_ref, sf_ref, sb_ref, *, seq):
    c_len = GLA_CHUNK
    n = seq // c_len
    row = lax.broadcasted_iota(jnp.int32, (c_len, c_len), 0)
    col = lax.broadcasted_iota(jnp.int32, (c_len, c_len), 1)
    tril = col <= row
    triu = col >= row
    cum_row = lax.broadcasted_iota(jnp.int32, (c_len, 2 * c_len), 0)
    cum_col = lax.broadcasted_iota(jnp.int32, (c_len, 2 * c_len), 1) % c_len
    cum = (cum_col <= cum_row).astype(F32).astype(BF16)
    mask_f = jnp.concatenate([tril, tril], axis=0)
    mask_b = jnp.concatenate([triu, triu], axis=0)
    head_a = lax.broadcasted_iota(jnp.int32, (c_len, LANES), 1) < GLA_DK
    st_row = lax.broadcasted_iota(jnp.int32, (2 * GLA_DV, LANES), 0)
    st_col = lax.broadcasted_iota(jnp.int32, (2 * GLA_DV, LANES), 1)
    blockdiag = (st_row < GLA_DV) == (st_col < GLA_DK)

    z = _dot(lr_ref[...].astype(BF16), wz_ref[...]) + bz_ref[...]
    la_ref[...] = (jnp.minimum(z, 0.0) - jnp.log(1.0 + jnp.exp(-jnp.abs(z)))) * (1.0 / GLA_TAU)

    def chunk_rows(c):
        return pl.ds(pl.multiple_of(c * c_len, c_len), c_len)

    def stack_heads(x):
        zero = jnp.zeros_like(x)
        return jnp.concatenate([jnp.where(head_a, x, zero), jnp.where(head_a, zero, x)], axis=0).astype(BF16)

    def local(c, carry):
        rows = chunk_rows(c)
        la = la_ref[rows, :]
        hi = la.astype(BF16)
        lo = (la - hi.astype(F32)).astype(BF16)
        binc = _dot(cum, jnp.concatenate([hi, lo], axis=0))
        btot = jnp.sum(la, axis=0, keepdims=True)
        b_f = binc[:, :LANES]
        b_b = btot[:, LANES:] - binc[:, LANES:] + la[:, LANES:]
        e_tot = jnp.exp(btot)
        q = q_ref[rows, :].astype(F32) * (GLA_DK ** -0.5)
        k = k_ref[rows, :].astype(F32)
        qt_f = q * jnp.exp(b_f)
        qt_b = q * jnp.exp(b_b)
        kt_f = k * jnp.exp(-b_f)
        kt_b = k * jnp.exp(-b_b)
        a = (jnp.where(mask_f, _dot_nt(stack_heads(qt_f), kt_f.astype(BF16)), 0.0)
             + jnp.where(mask_b, _dot_nt(stack_heads(qt_b), kt_b.astype(BF16)), 0.0))
        oi = _dot(a.astype(BF16), v_ref[rows, :])
        acc_ref[rows, :] = jnp.concatenate([oi[0:c_len, 0:GLA_DV], oi[c_len:, GLA_DV:]], axis=1)
        qt_ref[rows, :] = jnp.concatenate([qt_f, qt_b], axis=1).astype(BF16)
        ke_ref[rows, :] = jnp.concatenate([kt_f * e_tot[:, :LANES], kt_b * e_tot[:, LANES:]], axis=1).astype(BF16)
        dec_ref[pl.ds(pl.multiple_of(c * SUBLANES, SUBLANES), SUBLANES), :] = jnp.broadcast_to(e_tot, (SUBLANES, 2 * LANES))
        return carry

    def carried(c, s_ref, half):
        rows = chunk_rows(c)
        lanes = slice(half * LANES, (half + 1) * LANES)
        st = s_ref[...]
        acc_ref[rows, :] += _dot_nt(qt_ref[rows, lanes], st.astype(BF16))
        upd = _dot_tn(v_ref[rows, :], ke_ref[rows, lanes])
        dec = dec_ref[pl.ds(pl.multiple_of(c * SUBLANES, SUBLANES), 1), lanes]
        s_ref[...] = st * dec + jnp.where(blockdiag, upd, 0.0)

    def scan(c, carry):
        carried(c, sf_ref, 0)
        carried(n - 1 - c, sb_ref, 1)
        return carry

    def finalize(c, carry):
        rows = chunk_rows(c)
        o = acc_ref[rows, :]
        halves = []
        for hh in range(2):
            oh = o[:, hh * GLA_DV:(hh + 1) * GLA_DV]
            ms = jnp.mean(oh * oh, axis=-1, keepdims=True)
            halves.append(oh * lax.rsqrt(ms + EPS) * ng_ref[...])
        y = jnp.concatenate(halves, axis=1)
        g = g_ref[rows, :].astype(F32)
        o_ref[rows, :] = (y * (g * jax.nn.sigmoid(g))).astype(o_ref.dtype)
        return carry

    sf_ref[...] = jnp.zeros_like(sf_ref)
    sb_ref[...] = jnp.zeros_like(sb_ref)
    lax.fori_loop(0, n, local, 0, unroll=GLA_UNROLL)
    lax.fori_loop(0, n, scan, 0, unroll=GLA_UNROLL)
    lax.fori_loop(0, n, finalize, 0, unroll=GLA_UNROLL)


def _gla(pg, plr, wz, bz, norm_g, *, seq, batches, block0):
    pair_w = 2 * GLA_DK
    pair_v = 2 * GLA_DV
    k0 = GLA_QK // pair_w
    v0 = 2 * GLA_QK // pair_v
    g0 = (2 * GLA_QK + GLA_V) // pair_v
    n_chunks = seq // GLA_CHUNK
    in_specs = [pl.BlockSpec((seq, pair_w), lambda b, p: (block0 + b, p)),
                pl.BlockSpec((seq, pair_w), lambda b, p: (block0 + b, k0 + p)),
                pl.BlockSpec((seq, pair_v), lambda b, p: (block0 + b, v0 + p)),
                pl.BlockSpec((seq, pair_v), lambda b, p: (block0 + b, g0 + p)),
                pl.BlockSpec((seq, LANES), lambda b, p: (block0 + b, 0)),
                pl.BlockSpec((LANES, 2 * pair_w), lambda b, p: (0, p)),
                pl.BlockSpec((1, 2 * pair_w), lambda b, p: (0, p)),
                pl.BlockSpec((1, GLA_DV), lambda b, p: (0, 0))]
    return pl.pallas_call(
        functools.partial(_gla_kernel, seq=seq),
        grid=(batches, GLA_HEADS // 2),
        in_specs=in_specs,
        out_specs=pl.BlockSpec((seq, pair_v), lambda b, p: (b, p)),
        out_shape=jax.ShapeDtypeStruct((batches * seq, GLA_V), BF16),
        scratch_shapes=[pltpu.VMEM((seq, 2 * pair_w), F32),
                        pltpu.VMEM((seq, pair_v), F32),
                        pltpu.VMEM((seq, 2 * pair_w), BF16),
                        pltpu.VMEM((seq, 2 * pair_w), BF16),
                        pltpu.VMEM((n_chunks * SUBLANES, 2 * pair_w), F32),
                        pltpu.VMEM((pair_v, pair_w), F32),
                        pltpu.VMEM((pair_v, pair_w), F32)],
        compiler_params=_cparams(("parallel", "parallel")),
    )(pg, pg, pg, pg, plr, wz, bz, norm_g)


def _nat_kernel(q_ref, k_ref, v_ref, tab_ref, o_ref, *, rows):
    w = GRID_W
    head = lax.broadcasted_iota(jnp.int32, (w, LANES), 1) // NAT_HD
    win = NAT_KH * w

    def body(i, carry):
        r0 = jnp.clip(i - NAT_KH // 2, 0, rows - NAT_KH)
        base = r0 - i + NAT_KH - 1
        qrows = pl.ds(pl.multiple_of(i * w, w), w)
        krows = pl.ds(pl.multiple_of(r0 * w, w), win)
        q = q_ref[qrows, :]
        zero = jnp.zeros_like(q)
        q_stack = jnp.concatenate([jnp.where(head == h, q, zero) for h in range(NAT_GROUP)], axis=0)
        s = _dot_nt(q_stack, k_ref[krows, :]) + tab_ref[0, base]
        m = jnp.max(s, axis=-1, keepdims=True)
        e = jnp.exp2(s - m)
        l = jnp.sum(e, axis=-1, keepdims=True)
        o = _dot(e.astype(BF16), v_ref[krows, :]) / l
        out = jnp.zeros((w, LANES), F32)
        for h in range(NAT_GROUP):
            out = out + jnp.where(head == h, o[h * w:(h + 1) * w, :], 0.0)
        o_ref[qrows, :] = out.astype(o_ref.dtype)
        return carry

    lax.fori_loop(0, rows, body, 0, unroll=NAT_UNROLL)


def _nat(pn, tab, *, seq, batches, block0):
    rows = seq // GRID_W
    groups = NAT_HEADS // NAT_GROUP
    in_specs = [pl.BlockSpec((seq, LANES), lambda g, b: (block0 + b, g)),
                pl.BlockSpec((seq, LANES), lambda g, b: (block0 + b, groups + g)),
                pl.BlockSpec((seq, LANES), lambda g, b: (block0 + b, 2 * groups + g)),
                pl.BlockSpec((1,) + tab.shape[1:], lambda g, b: (g, 0, 0, 0))]
    return pl.pallas_call(
        functools.partial(_nat_kernel, rows=rows),
        grid=(groups, batches),
        in_specs=in_specs,
        out_specs=pl.BlockSpec((seq, LANES), lambda g, b: (b, g)),
        out_shape=jax.ShapeDtypeStruct((batches * seq, NAT_W), BF16),
        compiler_params=_cparams(("parallel", "parallel")),
    )(pn, pn, pn, tab)


def _nat_bias_table(rpb):
    w = GRID_W
    jq = np.arange(w)[:, None]
    jk = np.arange(w)[None, :]
    c0 = np.clip(jq - NAT_KW // 2, 0, w - NAT_KW)
    valid = (jk >= c0) & (jk < c0 + NAT_KW)
    dcol = np.clip(jk - jq + NAT_KW - 1, 0, 2 * NAT_KW - 2)
    drow = np.arange(NAT_KH)[:, None] + np.arange(NAT_KH)[None, :]
    t = rpb.astype(F32)[:, drow][:, :, :, dcol]
    t = jnp.where(valid[None, None, None], t * LOG2E, -jnp.inf)
    t = t.transpose(0, 1, 3, 2, 4).reshape(NAT_HEADS, NAT_KH, w, NAT_KH * w)
    t = t.reshape(NAT_HEADS // NAT_GROUP, NAT_GROUP, NAT_KH, w, NAT_KH * w).transpose(0, 2, 1, 3, 4)
    return t.reshape(NAT_HEADS // NAT_GROUP, NAT_KH, NAT_GROUP * w, NAT_KH * w)


def _out_kernel(ogp_ref, ogs_ref, onp_ref, ons_ref, xp_ref, xs_ref, mod_ref, wg_ref, wn_ref, n2_ref, rw_ref, rb_ref,
                x1_ref, h2_ref, ti_ref, gt_ref, rk_ref, cnt_ref, tcnt_ref, tcar_ref, carry_ref, *, n_prompt_tiles):
    tm = x1_ref.shape[0]

    @pl.when(pl.program_id(0) == 0)
    def _():
        carry_ref[...] = jnp.zeros_like(carry_ref)

    def residual(stream):
        og_ref, on_ref, x_ref = ((ogp_ref, onp_ref, xp_ref), (ogs_ref, ons_ref, xs_ref))[stream]
        mix = _dot(og_ref[...], wg_ref[...]) + _dot(on_ref[...], wn_ref[...])
        x1_ref[...] = x_ref[...] + mod_ref[0, 2:3, :] * mix

    _per_stream(n_prompt_tiles, residual)
    x1 = x1_ref[...]
    ms = jnp.mean(x1 * x1, axis=-1, keepdims=True)
    h2 = x1 * lax.rsqrt(ms + EPS) * n2_ref[...]
    h2 = h2 * (1.0 + mod_ref[0, 4:5, :]) + mod_ref[0, 3:4, :]
    _store_rows(h2_ref, h2)

    logits = _dot_nt(rw_ref[...], h2, precision=HIGHEST) + rb_ref[...]
    eidx = lax.broadcasted_iota(jnp.int32, logits.shape, 0)
    vals = logits
    top_v, top_i = [], []
    for _ in range(TOP_K):
        m = jnp.max(vals, axis=0, keepdims=True)
        idx = jnp.min(jnp.where(vals == m, eidx, N_EXPERTS), axis=0, keepdims=True)
        top_v.append(m)
        top_i.append(idx)
        vals = jnp.where(eidx == idx, -jnp.inf, vals)
    ev = [jnp.exp(v - top_v[0]) for v in top_v]
    den = ev[0] + ev[1] + ev[2] + ev[3]
    ti_ref[...] = jnp.concatenate(top_i, axis=0)
    gt_ref[...] = jnp.concatenate([e / den for e in ev], axis=0)

    onehot = jnp.zeros(logits.shape, F32)
    for idx in top_i:
        onehot = onehot + (eidx == idx).astype(F32)
    r = lax.broadcasted_iota(jnp.int32, (tm, tm), 0)
    c = lax.broadcasted_iota(jnp.int32, (tm, tm), 1)
    earlier = (r < c).astype(BF16)
    before = _dot(onehot.astype(BF16), earlier)
    rk_ref[...] = jnp.concatenate(
        [jnp.sum(jnp.where(eidx == idx, before, 0.0), axis=0, keepdims=True) for idx in top_i], axis=0).astype(jnp.int32)
    tile_counts = jnp.sum(onehot, axis=1, keepdims=True)
    tcar_ref[0] = jnp.broadcast_to(carry_ref[...], tcar_ref.shape[1:])
    tcnt_ref[0] = jnp.broadcast_to(tile_counts, tcnt_ref.shape[1:])
    carry_ref[...] = carry_ref[...] + tile_counts
    cnt_ref[...] = jnp.broadcast_to(carry_ref[...], cnt_ref.shape)


def _out_proj(og, on, x, mod3, w_g, w_n, norm_g, rw_t, rb, batch_of_tile):
    d = x[0].shape[1]
    n = x[0].shape[0] + x[1].shape[0]
    tm = TOKEN_TILE
    n_prompt_tiles = x[0].shape[0] // tm
    tok = lambda i: (i, 0)
    const = lambda i: (0, 0)
    return pl.pallas_call(
        functools.partial(_out_kernel, n_prompt_tiles=n_prompt_tiles),
        grid=(n // tm,),
        in_specs=[*_stream_specs((tm, GLA_V), n_prompt_tiles),
                  *_stream_specs((tm, NAT_W), n_prompt_tiles),
                  *_stream_specs((tm, d), n_prompt_tiles),
                  pl.BlockSpec((1, 6, d), lambda i: (batch_of_tile(i), 0, 0)),
                  pl.BlockSpec(w_g.shape, const),
                  pl.BlockSpec(w_n.shape, const),
                  pl.BlockSpec((1, d), const),
                  pl.BlockSpec(rw_t.shape, const),
                  pl.BlockSpec((N_EXPERTS, 1), const)],
        out_specs=[pl.BlockSpec((tm, d), tok),
                   pl.BlockSpec(_row_table_shape(tm), tok),
                   pl.BlockSpec((TOP_K, tm), lambda i: (0, i)),
                   pl.BlockSpec((TOP_K, tm), lambda i: (0, i)),
                   pl.BlockSpec((TOP_K, tm), lambda i: (0, i)),
                   pl.BlockSpec((N_EXPERTS, LANES), const),
                   pl.BlockSpec((1, N_EXPERTS, LANES), lambda i: (i, 0, 0)),
                   pl.BlockSpec((1, N_EXPERTS, LANES), lambda i: (i, 0, 0))],
        out_shape=[jax.ShapeDtypeStruct((n, d), F32),
                   jax.ShapeDtypeStruct(_row_table_shape(n), F32),
                   jax.ShapeDtypeStruct((TOP_K, n), jnp.int32),
                   jax.ShapeDtypeStruct((TOP_K, n), F32),
                   jax.ShapeDtypeStruct((TOP_K, n), jnp.int32),
                   jax.ShapeDtypeStruct((N_EXPERTS, LANES), F32),
                   jax.ShapeDtypeStruct((n // tm, N_EXPERTS, LANES), F32),
                   jax.ShapeDtypeStruct((n // tm, N_EXPERTS, LANES), F32)],
        scratch_shapes=[pltpu.VMEM((N_EXPERTS, 1), F32)],
        compiler_params=_cparams(("arbitrary",)),
    )(*og, *on, *x, mod3, w_g, w_n, norm_g.reshape(1, d), rw_t, rb.reshape(N_EXPERTS, 1))


def _row_copy(src, dst, sem):
    return pltpu.make_async_copy(src, dst, sem)


def _segment_copies(cnt_ref, far_ref, near_ref, tile, copy):
    def per_expert(e, carry):
        idx = tile * N_EXPERTS + e
        c, far, near = cnt_ref[idx], far_ref[idx], near_ref[idx]

        def whole(i, carry2):
            copy(near + i * SEGMENT_CHUNK, far + i * SEGMENT_CHUNK, SEGMENT_CHUNK).start()
            return carry2

        lax.fori_loop(0, c // SEGMENT_CHUNK, whole, 0)
        for bit in reversed(range(SEGMENT_CHUNK.bit_length() - 1)):
            size = 1 << bit
            done = c & -(2 * size)

            @pl.when((c & size) != 0)
            def _(done=done, size=size):
                copy(near + done, far + done, size).start()

        return carry

    lax.fori_loop(0, N_EXPERTS, per_expert, 0)


def _dispatch_kernel(cnt_ref, far_ref, near_ref, pos_ref, h_ref, xs_hbm, buf_ref, sem):
    pairs = pos_ref.shape[2]
    tt = pairs // TOP_K
    i = pl.program_id(0)
    buf = i % 2
    base = buf * pairs

    def place(t, carry):
        row = h_ref[_row_slice(t), :]
        for k in range(TOP_K):
            buf_ref[_row_slice(base + pos_ref[0, 0, TOP_K * t + k]), :] = row
        return carry

    lax.fori_loop(0, tt, place, 0, unroll=ROW_LOOP_UNROLL)

    def copy(near, far, rows):
        return _row_copy(_rows_of(buf_ref, base + near, rows), _rows_of(xs_hbm, far, rows), sem.at[buf])

    _segment_copies(cnt_ref, far_ref, near_ref, i, copy)

    def drain(b):
        _row_copy(_rows_of(buf_ref, b * pairs, pairs), _rows_of(xs_hbm, 0, pairs), sem.at[b]).wait()

    pl.when(i > 0)(lambda: drain(1 - buf))
    pl.when(i == pl.num_programs(0) - 1)(lambda: drain(buf))


def _dispatch(seg, pos, h2):
    n = h2.shape[0] // LANE_TILES
    tt = TOKEN_TILE
    grid_spec = pltpu.PrefetchScalarGridSpec(
        num_scalar_prefetch=3,
        grid=(n // tt,),
        in_specs=[pl.BlockSpec((1, 1, TOP_K * tt), lambda i, *_: (i, 0, 0), memory_space=pltpu.SMEM),
                  pl.BlockSpec(_row_table_shape(tt), lambda i, *_: (i, 0))],
        out_specs=pl.BlockSpec(memory_space=pl.ANY),
        scratch_shapes=[pltpu.VMEM(_row_table_shape(2 * TOP_K * tt), F32),
                        pltpu.SemaphoreType.DMA((2,))],
    )
    return pl.pallas_call(
        _dispatch_kernel,
        grid_spec=grid_spec,
        out_shape=jax.ShapeDtypeStruct(_row_table_shape(n * TOP_K), F32),
        compiler_params=_cparams(("arbitrary",)),
    )(*seg, pos, h2)


def _swiglu_col(feature0, up):
    group, within = divmod(feature0, SWIGLU_GROUP)
    return (2 * group + up) * SWIGLU_GROUP + within


def _deinterleave_kernel(w_ref, o_ref):
    f2 = w_ref.shape[2]
    slab = 2 * LANES
    r = lax.broadcasted_iota(jnp.int32, (slab, slab), 0)
    c = lax.broadcasted_iota(jnp.int32, (slab, slab), 1)
    perm = (r == jnp.where(c < LANES, 2 * c, 2 * (c - LANES) + 1)).astype(BF16)
    for j in range(f2 // slab):
        y = _dot(w_ref[0, :, j * slab:(j + 1) * slab].astype(BF16), perm)
        for up in range(2):
            c0 = _swiglu_col(j * LANES, up)
            o_ref[0, :, c0:c0 + LANES] = y[:, up * LANES:(up + 1) * LANES].astype(BF16)


def _deinterleave(w):
    e, d, f2 = w.shape
    return pl.pallas_call(
        _deinterleave_kernel,
        grid=(e,),
        in_specs=[pl.BlockSpec((1, d, f2), lambda i: (i, 0, 0))],
        out_specs=pl.BlockSpec((1, d, f2), lambda i: (i, 0, 0)),
        out_shape=jax.ShapeDtypeStruct((e, d, f2), BF16),
        compiler_params=_cparams(("parallel",)),
    )(w)


def _expert_kernel(blk_ref, exp_ref, lo_ref, hi_ref, xs_ref, wgu_ref, bgu_ref, wd_ref, bd_ref, ys_ref):
    del blk_ref, exp_ref
    rows = xs_ref.shape[0] // LANE_TILES
    f = wd_ref.shape[1]
    i = pl.program_id(0)
    lo = lo_ref[i]
    hi = hi_ref[i]

    @pl.when(hi > lo)
    def _():
        g = SWIGLU_GROUP
        x = _load_rows(xs_ref, 0, rows).astype(BF16)
        acts = []
        for j in range(f // g):
            gu = _dot(x, wgu_ref[0, :, 2 * j * g:2 * (j + 1) * g]) + bgu_ref[0, :, 2 * j * g:2 * (j + 1) * g]
            gate = jnp.minimum(gu[:, :g], SWIGLU_LIMIT)
            up = jnp.clip(gu[:, g:], -SWIGLU_LIMIT, SWIGLU_LIMIT)
            acts.append(((up + 1.0) * gate * jax.nn.sigmoid(SWIGLU_ALPHA * gate)).astype(BF16))
        y = _dot(jnp.concatenate(acts, axis=1), wd_ref[0]) + bd_ref[0]
        r = lax.broadcasted_iota(jnp.int32, y.shape, 0)
        mine = (r >= lo) & (r < hi)

        @pl.when(lo == 0)
        def _():
            _store_rows(ys_ref, jnp.where(mine, y, 0.0))

        @pl.when(lo > 0)
        def _():
            _store_rows(ys_ref, jnp.where(mine, y, _load_rows(ys_ref, 0, rows)))


def _experts(items, xs2, w_gu, b_gu, w_down, b_down):
    item_blk, item_exp, item_lo, item_hi = items
    d, f2 = w_gu.shape[1], w_gu.shape[2]
    wmap = lambda i, blk, exp, lo, hi: (exp[i], 0, 0)
    xmap = lambda i, blk, exp, lo, hi: (blk[i], 0)
    grid_spec = pltpu.PrefetchScalarGridSpec(
        num_scalar_prefetch=4,
        grid=(item_blk.shape[0],),
        in_specs=[pl.BlockSpec(_row_table_shape(MOE_ROWS), xmap),
                  pl.BlockSpec((1, d, f2), wmap),
                  pl.BlockSpec((1, 1, f2), wmap),
                  pl.BlockSpec((1, f2 // 2, d), wmap),
                  pl.BlockSpec((1, 1, d), wmap)],
        out_specs=pl.BlockSpec(_row_table_shape(MOE_ROWS), xmap),
    )
    return pl.pallas_call(
        _expert_kernel,
        grid_spec=grid_spec,
        out_shape=jax.ShapeDtypeStruct(xs2.shape, F32),
        compiler_params=_cparams(("arbitrary",)),
    )(item_blk, item_exp, item_lo, item_hi, xs2, w_gu, b_gu, w_down, b_down)


def _expert_work_items(counts, n_rows):
    ends = jnp.cumsum(counts)
    starts = ends - counts
    n_blk = n_rows // MOE_ROWS
    cuts = jnp.sort(jnp.concatenate([jnp.arange(n_blk, dtype=jnp.int32) * MOE_ROWS, starts[1:]]))
    nxt = jnp.concatenate([cuts[1:], jnp.full((1,), n_rows, jnp.int32)])
    blk = jnp.minimum(cuts // MOE_ROWS, n_blk - 1)
    exp = jnp.minimum(jnp.sum(ends[None, :] <= cuts[:, None], axis=1), N_EXPERTS - 1).astype(jnp.int32)
    lo = cuts - blk * MOE_ROWS
    hi = nxt - blk * MOE_ROWS
    return (blk.astype(jnp.int32), exp, lo.astype(jnp.int32), hi.astype(jnp.int32)), starts


def _combine_kernel(cnt_ref, far_ref, near_ref, pos_ref, gates_ref, x1_ref, mod_ref, fg_ref, ys_hbm,
                    op_ref, os_ref, buf_ref, y_ref, sem, *, n_prompt_tiles):
    tt = x1_ref.shape[0]
    pairs = TOP_K * tt
    i = pl.program_id(0)
    buf = i % 2
    base = buf * pairs

    def fetch(tile, b):
        def copy(near, far, rows):
            return _row_copy(_rows_of(ys_hbm, far, rows), _rows_of(buf_ref, b * pairs + near, rows), sem.at[b])

        _segment_copies(cnt_ref, far_ref, near_ref, tile, copy)

    pl.when(i == 0)(lambda: fetch(0, 0))
    pl.when(i + 1 < pl.num_programs(0))(lambda: fetch(i + 1, 1 - buf))
    _row_copy(_rows_of(ys_hbm, 0, pairs), _rows_of(buf_ref, base, pairs), sem.at[buf]).wait()

    def weighted_sum(t, carry):
        acc = None
        for k in range(TOP_K):
            term = gates_ref[0, 0, TOP_K * t + k] * buf_ref[_row_slice(base + pos_ref[0, 0, TOP_K * t + k]), :]
            acc = term if acc is None else acc + term
        y_ref[_row_slice(t), :] = acc
        return carry

    lax.fori_loop(0, tt, weighted_sum, 0, unroll=ROW_LOOP_UNROLL)

    x2 = x1_ref[...] + mod_ref[0, 5:6, :] * _load_rows(y_ref, 0, tt)
    ms = jnp.mean(x2 * x2, axis=-1, keepdims=True)
    out = x2 * lax.rsqrt(ms + EPS) * fg_ref[...]

    def write(stream):
        (op_ref, os_ref)[stream][...] = out

    _per_stream(n_prompt_tiles, write)


def _combine(seg, pos, gates, x1, mod3, final_g, ys, batch_of_tile, n_prompt):
    n, d = x1.shape
    tt = TOKEN_TILE
    n_prompt_tiles = n_prompt // tt
    grid_spec = pltpu.PrefetchScalarGridSpec(
        num_scalar_prefetch=3,
        grid=(n // tt,),
        in_specs=[pl.BlockSpec((1, 1, TOP_K * tt), lambda i, *_: (i, 0, 0), memory_space=pltpu.SMEM),
                  pl.BlockSpec((1, 1, TOP_K * tt), lambda i, *_: (i, 0, 0), memory_space=pltpu.SMEM),
                  pl.BlockSpec((tt, d), lambda i, *_: (i, 0)),
                  pl.BlockSpec((1, 6, d), lambda i, *_: (batch_of_tile(i), 0, 0)),
                  pl.BlockSpec((1, d), lambda i, *_: (0, 0)),
                  pl.BlockSpec(memory_space=pl.ANY)],
        out_specs=list(_stream_specs((tt, d), n_prompt_tiles)),
        scratch_shapes=[pltpu.VMEM(_row_table_shape(2 * TOP_K * tt), F32),
                        pltpu.VMEM(_row_table_shape(tt), F32),
                        pltpu.SemaphoreType.DMA((2,))],
    )
    return pl.pallas_call(
        functools.partial(_combine_kernel, n_prompt_tiles=n_prompt_tiles),
        grid_spec=grid_spec,
        out_shape=[jax.ShapeDtypeStruct((n_prompt, d), F32),
                   jax.ShapeDtypeStruct((n - n_prompt, d), F32)],
        compiler_params=_cparams(("arbitrary",)),
    )(*seg, pos, gates, x1, mod3, final_g.reshape(1, d), ys)


def _batch_of_tile_fn(tile, bp, tp, ts):
    n_prompt = bp * tp

    def batch_of_tile(i):
        t = i * tile
        return jnp.where(t < n_prompt, t // tp, bp + (t - n_prompt) // ts)

    return batch_of_tile


def kernel(x_prompt, x_sample, c_prompt, c_sample, norm1_g, w_ada, b_ada, w_in, gla_w2_fwd, gla_b2_fwd, gla_w2_bwd, gla_b2_bwd, gla_norm_g, nat_rpb, w_out, norm2_g, router_w, router_b, w_gate_up, b_gate_up, w_down, b_down, final_norm_g):
    assert w_ada.shape[0] == 1, "single-layer encoder"
    bp, tp, d = x_prompt.shape
    bs, ts, _ = x_sample.shape
    n_prompt, n_sample = bp * tp, bs * ts
    n = n_prompt + n_sample
    assert tp % TOKEN_TILE == 0 and ts % TOKEN_TILE == 0 and n_prompt % ts == 0
    assert d == LANE_TILES * LANES

    x = (x_prompt.reshape(n_prompt, d), x_sample.reshape(n_sample, d))
    c = jnp.concatenate([c_prompt, c_sample], axis=0)

    sizes = (GLA_QK, GLA_QK, GLA_V, GLA_V, GLA_RANK, GLA_RANK, NAT_W, NAT_W, NAT_W)
    offs = np.concatenate([[0], np.cumsum(sizes)])
    w_in0 = w_in[0]
    seg = lambda j: w_in0[:, offs[j]:offs[j + 1]]
    w_all = jnp.concatenate(
        [seg(0), seg(1), seg(2), seg(3), seg(6) * (NAT_HD ** -0.5 * LOG2E), seg(7), seg(8), seg(4), seg(5),
         jnp.zeros((d, LANES - 2 * GLA_RANK), F32)], axis=1).astype(BF16)
    pairs = GLA_HEADS // 2
    w2 = jnp.zeros((LANES, pairs, 2, 2 * GLA_DK), F32)
    w2 = w2.at[0:GLA_RANK, :, 0].set(gla_w2_fwd[0].reshape(GLA_RANK, pairs, 2 * GLA_DK))
    w2 = w2.at[GLA_RANK:2 * GLA_RANK, :, 1].set(gla_w2_bwd[0].reshape(GLA_RANK, pairs, 2 * GLA_DK))
    wz = w2.reshape(LANES, pairs * 4 * GLA_DK).astype(BF16)
    bz = jnp.stack([gla_b2_fwd[0].reshape(pairs, 2 * GLA_DK), gla_b2_bwd[0].reshape(pairs, 2 * GLA_DK)],
                   axis=1).reshape(1, pairs * 4 * GLA_DK)
    tab = _nat_bias_table(nat_rpb[0])
    w_og = w_out[0, :GLA_V].astype(BF16)
    w_on = w_out[0, GLA_V:].astype(BF16)
    rw_t = router_w[0].T
    w_gu = _deinterleave(w_gate_up[0])
    n_e, f2 = b_gate_up.shape[1:]
    b_gu = (b_gate_up[0].reshape(n_e, f2 // (2 * SWIGLU_GROUP), SWIGLU_GROUP, 2)
            .transpose(0, 1, 3, 2).reshape(n_e, 1, f2))
    w_dn = w_down[0].astype(BF16)
    b_dn = b_down[0, :, None, :]

    mod3 = _ada(c, w_ada[0], b_ada[0]).reshape(bp + bs, 6, d)
    bot_tok = _batch_of_tile_fn(TOKEN_TILE, bp, tp, ts)

    pg, pn, plr = _in_proj(*x, mod3, norm1_g[0], w_all, bot_tok)

    gla_args = (pg, plr, wz, bz, gla_norm_g[0].reshape(1, -1))
    og = (_gla(*gla_args, seq=tp, batches=bp, block0=0),
          _gla(*gla_args, seq=ts, batches=bs, block0=n_prompt // ts))
    on = (_nat(pn, tab, seq=tp, batches=bp, block0=0),
          _nat(pn, tab, seq=ts, batches=bs, block0=n_prompt // ts))

    x1, h2, top_i, gates_t, rank_t, cnt, tile_cnt, tile_carry = _out_proj(
        og, on, x, mod3, w_og, w_on, norm2_g[0], rw_t, router_b[0], bot_tok)

    n_rows = n * TOP_K
    n_tiles = n // TOKEN_TILE
    assert n_rows % MOE_ROWS == 0
    counts = cnt[:, 0].astype(jnp.int32)
    items, starts = _expert_work_items(counts, n_rows)
    seg_cnt = tile_cnt[:, :, 0].astype(jnp.int32)
    seg_far = starts[None, :] + tile_carry[:, :, 0].astype(jnp.int32)
    seg_near = jnp.cumsum(seg_cnt, axis=1) - seg_cnt
    seg = tuple(a.reshape(n_tiles * N_EXPERTS) for a in (seg_cnt, seg_far, seg_near))
    chosen = (top_i.reshape(1, TOP_K, n_tiles, TOKEN_TILE)
              == jnp.arange(N_EXPERTS, dtype=jnp.int32)[:, None, None, None])
    pos_t = rank_t + jnp.sum(jnp.where(chosen, seg_near.T[:, None, :, None], 0), axis=0).reshape(TOP_K, n)
    per_tile = lambda a: a.T.reshape(n_tiles, 1, TOKEN_TILE * TOP_K)
    pos, gates = per_tile(pos_t), per_tile(gates_t)

    xs = _dispatch(seg, pos, h2)
    ys = _experts(items, xs, w_gu, b_gu, w_dn, b_dn)
    y_prompt, y_sample = _combine(seg, pos, gates, x1, mod3, final_norm_g, ys, bot_tok, n_prompt)

    return (y_prompt.reshape(bp, tp, d), y_sample.reshape(bs, ts, d))
```

```python
import functools

import numpy as np
import jax
import jax.numpy as jnp
from jax import lax
from jax.experimental import pallas as pl
from jax.experimental.pallas import tpu as pltpu

F32 = jnp.float32
BF16 = jnp.bfloat16
HIGHEST = lax.Precision.HIGHEST

EPS = 1e-5
GRID_W = 64
GLA_HEADS = 4
GLA_DK = 64
GLA_DV = 128
GLA_RANK = 16
GLA_TAU = 16.0
GLA_CHUNK = 64
GLA_QK = GLA_HEADS * GLA_DK
GLA_V = GLA_HEADS * GLA_DV
NAT_HEADS = 16
NAT_HD = 32
NAT_W = NAT_HEADS * NAT_HD
NAT_KH = 8
NAT_KW = 16
NAT_GROUP = 4
N_EXPERTS = 32
TOP_K = 4
SWIGLU_LIMIT = 7.0
SWIGLU_ALPHA = 1.702

LANES = 128
SUBLANES = 8
TOKEN_TILE = 512
MOE_ROWS = 512
SWIGLU_GROUP = 1024
GLA_UNROLL = 8
NAT_UNROLL = 8
ROW_LOOP_UNROLL = 8
SEGMENT_CHUNK = 16
LOG2E = 1.4426950408889634
VMEM_LIMIT = 56 * 1024 * 1024


def _cparams(sem, vmem=VMEM_LIMIT):
    return pltpu.CompilerParams(dimension_semantics=sem, vmem_limit_bytes=vmem)


def _dot(a, b):
    return jnp.dot(a, b, preferred_element_type=F32)


def _dot_nt(a, b, precision=None):
    return lax.dot_general(a, b, (((1,), (1,)), ((), ())), preferred_element_type=F32, precision=precision)


LANE_TILES = 8


def _row_table_shape(rows):
    return (rows * LANE_TILES, LANES)


def _store_rows(ref, x, row0=0):
    rows = x.shape[0]
    for s in range(LANE_TILES):
        ref[pl.ds(row0 * LANE_TILES + s, rows, stride=LANE_TILES), :] = x[:, s * LANES:(s + 1) * LANES]


def _load_rows(ref, row0, rows):
    return jnp.concatenate(
        [ref[pl.ds(row0 * LANE_TILES + s, rows, stride=LANE_TILES), :] for s in range(LANE_TILES)], axis=1)


def _row_slice(row0, rows=1):
    return pl.ds(pl.multiple_of(row0 * LANE_TILES, LANE_TILES), rows * LANE_TILES)


def _rows_of(ref, row0, rows):
    return ref.at[_row_slice(row0, rows), :]


def _dot_tn(a, b):
    return lax.dot_general(a, b, (((0,), (0,)), ((), ())), preferred_element_type=F32)


def _ada_kernel(c_ref, w_ref, b_ref, o_ref):
    c = c_ref[...]
    s = c * jax.nn.sigmoid(c)
    o_ref[...] = jnp.dot(s, w_ref[...], preferred_element_type=F32, precision=HIGHEST) + b_ref[...]


def _ada(c, w, b):
    nb, d = c.shape
    cols = w.shape[1]
    blk = 1024
    return pl.pallas_call(
        _ada_kernel,
        grid=(cols // blk,),
        in_specs=[pl.BlockSpec((nb, d), lambda j: (0, 0)),
                  pl.BlockSpec((d, blk), lambda j: (0, j)),
                  pl.BlockSpec((1, blk), lambda j: (0, j))],
        out_specs=pl.BlockSpec((nb, blk), lambda j: (0, j)),
        out_shape=jax.ShapeDtypeStruct((nb, cols), F32),
        compiler_params=_cparams(("arbitrary",)),
    )(c, w, b.reshape(1, cols))


def _stream_specs(block, n_prompt_tiles):
    return (pl.BlockSpec(block, lambda i, *_: (jnp.minimum(i, n_prompt_tiles - 1), 0)),
            pl.BlockSpec(block, lambda i, *_: (jnp.maximum(i - n_prompt_tiles, 0), 0)))


def _per_stream(n_prompt_tiles, body):
    i = pl.program_id(0)
    pl.when(i < n_prompt_tiles)(functools.partial(body, 0))
    pl.when(i >= n_prompt_tiles)(functools.partial(body, 1))


def _in_kernel(xp_ref, xs_ref, mod_ref, g_ref, w_ref, pg_ref, pn_ref, plr_ref, *, n_prompt_tiles):
    def body(stream):
        x = (xp_ref, xs_ref)[stream][...]
        ms = jnp.mean(x * x, axis=-1, keepdims=True)
        y = x * lax.rsqrt(ms + EPS) * g_ref[...]
        h = y * (1.0 + mod_ref[0, 1:2, :]) + mod_ref[0, 0:1, :]
        hb = h.astype(BF16)
        wg = GLA_QK * 2 + GLA_V * 2
        wn = 3 * NAT_W
        pg_ref[...] = _dot(hb, w_ref[:, 0:wg]).astype(BF16)
        pn_ref[...] = _dot(hb, w_ref[:, wg:wg + wn]).astype(BF16)
        plr_ref[...] = _dot(hb, w_ref[:, wg + wn:wg + wn + LANES])

    _per_stream(n_prompt_tiles, body)


def _in_proj(xp, xs, mod3, norm_g, w_all, batch_of_tile):
    d = xp.shape[1]
    n = xp.shape[0] + xs.shape[0]
    wg = GLA_QK * 2 + GLA_V * 2
    wn = 3 * NAT_W
    tm = TOKEN_TILE
    n_prompt_tiles = xp.shape[0] // tm
    return pl.pallas_call(
        functools.partial(_in_kernel, n_prompt_tiles=n_prompt_tiles),
        grid=(n // tm,),
        in_specs=[*_stream_specs((tm, d), n_prompt_tiles),
                  pl.BlockSpec((1, 6, d), lambda i: (batch_of_tile(i), 0, 0)),
                  pl.BlockSpec((1, d), lambda i: (0, 0)),
                  pl.BlockSpec(w_all.shape, lambda i: (0, 0))],
        out_specs=[pl.BlockSpec((tm, wg), lambda i: (i, 0)),
                   pl.BlockSpec((tm, wn), lambda i: (i, 0)),
                   pl.BlockSpec((tm, LANES), lambda i: (i, 0))],
        out_shape=[jax.ShapeDtypeStruct((n, wg), BF16),
                   jax.ShapeDtypeStruct((n, wn), BF16),
                   jax.ShapeDtypeStruct((n, LANES), F32)],
        compiler_params=_cparams(("arbitrary",)),
    )(xp, xs, mod3, norm_g.reshape(1, d), w_all)


def _gla_kernel(q_ref, k_ref, v_ref, g_ref, lr_ref, wz_ref, bz_ref, ng_ref,
                o_ref, la_ref, acc_ref, qt_ref, ke_ref, dec_ref, sf_ref, sb_ref, *, seq):
    c_len = GLA_CHUNK
    n = seq // c_len
    row = lax.broadcasted_iota(jnp.int32, (c_len, c_len), 0)
    col = lax.broadcasted_iota(jnp.int32, (c_len, c_len), 1)
    tril = col <= row
    triu = col >= row
    cum_row = lax.broadcasted_iota(jnp.int32, (c_len, 2 * c_len), 0)
    cum_col = lax.broadcasted_iota(jnp.int32, (c_len, 2 * c_len), 1) % c_len
    cum = (cum_col <= cum_row).astype(F32).astype(BF16)
    mask_f = jnp.concatenate([tril, tril], axis=0)
    mask_b = jnp.concatenate([triu, triu], axis=0)
    head_a = lax.broadcasted_iota(jnp.int32, (c_len, LANES), 1) < GLA_DK
    st_row = lax.broadcasted_iota(jnp.int32, (2 * GLA_DV, LANES), 0)
    st_col = lax.broadcasted_iota(jnp.int32, (2 * GLA_DV, LANES), 1)
    blockdiag = (st_row < GLA_DV) == (st_col < GLA_DK)

    z = _dot(lr_ref[...].astype(BF16), wz_ref[...]) + bz_ref[...]
    la_ref[...] = (jnp.minimum(z, 0.0) - jnp.log(1.0 + jnp.exp(-jnp.abs(z)))) * (1.0 / GLA_TAU)

    def chunk_rows(c):
        return pl.ds(pl.multiple_of(c * c_len, c_len), c_len)

    def stack_heads(x):
        zero = jnp.zeros_like(x)
        return jnp.concatenate([jnp.where(head_a, x, zero), jnp.where(head_a, zero, x)], axis=0).astype(BF16)

    def local(c, carry):
        rows = chunk_rows(c)
        la = la_ref[rows, :]
        hi = la.astype(BF16)
        lo = (la - hi.astype(F32)).astype(BF16)
        binc = _dot(cum, jnp.concatenate([hi, lo], axis=0))
        btot = jnp.sum(la, axis=0, keepdims=True)
        b_f = binc[:, :LANES]
        b_b = btot[:, LANES:] - binc[:, LANES:] + la[:, LANES:]
        e_tot = jnp.exp(btot)
        q = q_ref[rows, :].astype(F32) * (GLA_DK ** -0.5)
        k = k_ref[rows, :].astype(F32)
        qt_f = q * jnp.exp(b_f)
        qt_b = q * jnp.exp(b_b)
        kt_f = k * jnp.exp(-b_f)
        kt_b = k * jnp.exp(-b_b)
        a = (jnp.where(mask_f, _dot_nt(stack_heads(qt_f), kt_f.astype(BF16)), 0.0)
             + jnp.where(mask_b, _dot_nt(stack_heads(qt_b), kt_b.astype(BF16)), 0.0))
        oi = _dot(a.astype(BF16), v_ref[rows, :])
        acc_ref[rows, :] = jnp.concatenate([oi[0:c_len, 0:GLA_DV], oi[c_len:, GLA_DV:]], axis=1)
        qt_ref[rows, :] = jnp.concatenate([qt_f, qt_b], axis=1).astype(BF16)
        ke_ref[rows, :] = jnp.concatenate([kt_f * e_tot[:, :LANES], kt_b * e_tot[:, LANES:]], axis=1).astype(BF16)
        dec_ref[pl.ds(pl.multiple_of(c * SUBLANES, SUBLANES), SUBLANES), :] = jnp.broadcast_to(e_tot, (SUBLANES, 2 * LANES))
        return carry

    def carried(c, s_ref, half):
        rows = chunk_rows(c)
        lanes = slice(half * LANES, (half + 1) * LANES)
        st = s_ref[...]
        acc_ref[rows, :] += _dot_nt(qt_ref[rows, lanes], st.astype(BF16))
        upd = _dot_tn(v_ref[rows, :], ke_ref[rows, lanes])
        dec = dec_ref[pl.ds(pl.multiple_of(c * SUBLANES, SUBLANES), 1), lanes]
        s_ref[...] = st * dec + jnp.where(blockdiag, upd, 0.0)

    def scan(c, carry):
        carried(c, sf_ref, 0)
        carried(n - 1 - c, sb_ref, 1)
        return carry

    def finalize(c, carry):
        rows = chunk_rows(c)
        o = acc_ref[rows, :]
        halves = []
        for hh in range(2):
            oh = o[:, hh * GLA_DV:(hh + 1) * GLA_DV]
            ms = jnp.mean(oh * oh, axis=-1, keepdims=True)
            halves.append(oh * lax.rsqrt(ms + EPS) * ng_ref[...])
        y = jnp.concatenate(halves, axis=1)
        g = g_ref[rows, :].astype(F32)
        o_ref[rows, :] = (y * (g * jax.nn.sigmoid(g))).astype(o_ref.dtype)
        return carry

    sf_ref[...] = jnp.zeros_like(sf_ref)
    sb_ref[...] = jnp.zeros_like(sb_ref)
    lax.fori_loop(0, n, local, 0, unroll=GLA_UNROLL)
    lax.fori_loop(0, n, scan, 0, unroll=GLA_UNROLL)
    lax.fori_loop(0, n, finalize, 0, unroll=GLA_UNROLL)


def _gla(pg, plr, wz, bz, norm_g, *, seq, batches, block0):
    pair_w = 2 * GLA_DK
    pair_v = 2 * GLA_DV
    k0 = GLA_QK // pair_w
    v0 = 2 * GLA_QK // pair_v
    g0 = (2 * GLA_QK + GLA_V) // pair_v
    n_chunks = seq // GLA_CHUNK
    in_specs = [pl.BlockSpec((seq, pair_w), lambda b, p: (block0 + b, p)),
                pl.BlockSpec((seq, pair_w), lambda b, p: (block0 + b, k0 + p)),
                pl.BlockSpec((seq, pair_v), lambda b, p: (block0 + b, v0 + p)),
                pl.BlockSpec((seq, pair_v), lambda b, p: (block0 + b, g0 + p)),
                pl.BlockSpec((seq, LANES), lambda b, p: (block0 + b, 0)),
                pl.BlockSpec((LANES, 2 * pair_w), lambda b, p: (0, p)),
                pl.BlockSpec((1, 2 * pair_w), lambda b, p: (0, p)),
                pl.BlockSpec((1, GLA_DV), lambda b, p: (0, 0))]
    return pl.pallas_call(
        functools.partial(_gla_kernel, seq=seq),
        grid=(batches, GLA_HEADS // 2),
        in_specs=in_specs,
        out_specs=pl.BlockSpec((seq, pair_v), lambda b, p: (b, p)),
        out_shape=jax.ShapeDtypeStruct((batches * seq, GLA_V), BF16),
        scratch_shapes=[pltpu.VMEM((seq, 2 * pair_w), F32),
                        pltpu.VMEM((seq, pair_v), F32),
                        pltpu.VMEM((seq, 2 * pair_w), BF16),
                        pltpu.VMEM((seq, 2 * pair_w), BF16),
                        pltpu.VMEM((n_chunks * SUBLANES, 2 * pair_w), F32),
                        pltpu.VMEM((pair_v, pair_w), F32),
                        pltpu.VMEM((pair_v, pair_w), F32)],
        compiler_params=_cparams(("parallel", "parallel")),
    )(pg, pg, pg, pg, plr, wz, bz, norm_g)


def _nat_kernel(q_ref, k_ref, v_ref, tab_ref, o_ref, s_ref, *, rows):
    w = GRID_W
    head = lax.broadcasted_iota(jnp.int32, (w, LANES), 1) // NAT_HD
    win = NAT_KH * w

    def window(i):
        r0 = jnp.clip(i - NAT_KH // 2, 0, rows - NAT_KH)
        return r0 - i + NAT_KH - 1, pl.ds(pl.multiple_of(r0 * w, w), win)

    def scores(i, slot):
        base, krows = window(i)
        q = q_ref[pl.ds(pl.multiple_of(i * w, w), w), :]
        zero = jnp.zeros_like(q)
        q_stack = jnp.concatenate([jnp.where(head == h, q, zero) for h in range(NAT_GROUP)], axis=0)
        s_ref[slot] = _dot_nt(q_stack, k_ref[krows, :]) + tab_ref[0, base]

    def attend(i, slot):
        _, krows = window(i)
        s = s_ref[slot]
        m = jnp.max(s, axis=-1, keepdims=True)
        e = jnp.exp2(s - m)
        l = jnp.sum(e, axis=-1, keepdims=True)
        o = _dot(e.astype(BF16), v_ref[krows, :]) / l
        out = jnp.zeros((w, LANES), F32)
        for h in range(NAT_GROUP):
            out = out + jnp.where(head == h, o[h * w:(h + 1) * w, :], 0.0)
        o_ref[pl.ds(pl.multiple_of(i * w, w), w), :] = out.astype(o_ref.dtype)

    scores(0, 0)

    def body(j, carry):
        for u in range(2):
            i = 2 * j + u
            scores(jnp.minimum(i + 1, rows - 1), 1 - u)
            attend(i, u)
        return carry

    lax.fori_loop(0, rows // 2, body, 0, unroll=NAT_UNROLL // 2)


def _nat(pn, tab, *, seq, batches, block0):
    rows = seq // GRID_W
    groups = NAT_HEADS // NAT_GROUP
    in_specs = [pl.BlockSpec((seq, LANES), lambda g, b: (block0 + b, g)),
                pl.BlockSpec((seq, LANES), lambda g, b: (block0 + b, groups + g)),
                pl.BlockSpec((seq, LANES), lambda g, b: (block0 + b, 2 * groups + g)),
                pl.BlockSpec((1,) + tab.shape[1:], lambda g, b: (g, 0, 0, 0))]
    return pl.pallas_call(
        functools.partial(_nat_kernel, rows=rows),
        grid=(groups, batches),
        in_specs=in_specs,
        out_specs=pl.BlockSpec((seq, LANES), lambda g, b: (b, g)),
        out_shape=jax.ShapeDtypeStruct((batches * seq, NAT_W), BF16),
        scratch_shapes=[pltpu.VMEM((2, NAT_GROUP * GRID_W, NAT_KH * GRID_W), F32)],
        compiler_params=_cparams(("parallel", "parallel")),
    )(pn, pn, pn, tab)


def _nat_bias_table(rpb):
    w = GRID_W
    jq = np.arange(w)[:, None]
    jk = np.arange(w)[None, :]
    c0 = np.clip(jq - NAT_KW // 2, 0, w - NAT_KW)
    valid = (jk >= c0) & (jk < c0 + NAT_KW)
    dcol = np.clip(jk - jq + NAT_KW - 1, 0, 2 * NAT_KW - 2)
    drow = np.arange(NAT_KH)[:, None] + np.arange(NAT_KH)[None, :]
    t = rpb.astype(F32)[:, drow][:, :, :, dcol]
    t = jnp.where(valid[None, None, None], t * LOG2E, -jnp.inf)
    t = t.transpose(0, 1, 3, 2, 4).reshape(NAT_HEADS, NAT_KH, w, NAT_KH * w)
    t = t.reshape(NAT_HEADS // NAT_GROUP, NAT_GROUP, NAT_KH, w, NAT_KH * w).transpose(0, 2, 1, 3, 4)
    return t.reshape(NAT_HEADS // NAT_GROUP, NAT_KH, NAT_GROUP * w, NAT_KH * w)


def _out_kernel(ogp_ref, ogs_ref, onp_ref, ons_ref, xp_ref, xs_ref, mod_ref, wg_ref, wn_ref, n2_ref, rw_ref, rb_ref,
                x1_ref, h2_ref, ti_ref, gt_ref, rk_ref, cnt_ref, tcnt_ref, tcar_ref, carry_ref, *, n_prompt_tiles):
    tm = x1_ref.shape[0]

    @pl.when(pl.program_id(0) == 0)
    def _():
        carry_ref[...] = jnp.zeros_like(carry_ref)

    def residual(stream):
        og_ref, on_ref, x_ref = ((ogp_ref, onp_ref, xp_ref), (ogs_ref, ons_ref, xs_ref))[stream]
        mix = _dot(og_ref[...], wg_ref[...]) + _dot(on_ref[...], wn_ref[...])
        x1_ref[...] = x_ref[...] + mod_ref[0, 2:3, :] * mix

    _per_stream(n_prompt_tiles, residual)
    x1 = x1_ref[...]
    ms = jnp.mean(x1 * x1, axis=-1, keepdims=True)
    h2 = x1 * lax.rsqrt(ms + EPS) * n2_ref[...]
    h2 = h2 * (1.0 + mod_ref[0, 4:5, :]) + mod_ref[0, 3:4, :]
    _store_rows(h2_ref, h2)

    logits = _dot_nt(rw_ref[...], h2, precision=HIGHEST) + rb_ref[...]
    eidx = lax.broadcasted_iota(jnp.int32, logits.shape, 0)
    vals = logits
    top_v, top_i = [], []
    for _ in range(TOP_K):
        m = jnp.max(vals, axis=0, keepdims=True)
        idx = jnp.min(jnp.where(vals == m, eidx, N_EXPERTS), axis=0, keepdims=True)
        top_v.append(m)
        top_i.append(idx)
        vals = jnp.where(eidx == idx, -jnp.inf, vals)
    ev = [jnp.exp(v - top_v[0]) for v in top_v]
    den = ev[0] + ev[1] + ev[2] + ev[3]
    ti_ref[...] = jnp.concatenate(top_i, axis=0)
    gt_ref[...] = jnp.concatenate([e / den for e in ev], axis=0)

    onehot = jnp.zeros(logits.shape, F32)
    for idx in top_i:
        onehot = onehot + (eidx == idx).astype(F32)
    r = lax.broadcasted_iota(jnp.int32, (tm, tm), 0)
    c = lax.broadcasted_iota(jnp.int32, (tm, tm), 1)
    earlier = (r < c).astype(BF16)
    before = _dot(onehot.astype(BF16), earlier)
    rk_ref[...] = jnp.concatenate(
        [jnp.sum(jnp.where(eidx == idx, before, 0.0), axis=0, keepdims=True) for idx in top_i], axis=0).astype(jnp.int32)
    tile_counts = jnp.sum(onehot, axis=1, keepdims=True)
    tcar_ref[0] = jnp.broadcast_to(carry_ref[...], tcar_ref.shape[1:])
    tcnt_ref[0] = jnp.broadcast_to(tile_counts, tcnt_ref.shape[1:])
    carry_ref[...] = carry_ref[...] + tile_counts
    cnt_ref[...] = jnp.broadcast_to(carry_ref[...], cnt_ref.shape)


def _out_proj(og, on, x, mod3, w_g, w_n, norm_g, rw_t, rb, batch_of_tile):
    d = x[0].shape[1]
    n = x[0].shape[0] + x[1].shape[0]
    tm = TOKEN_TILE
    n_prompt_tiles = x[0].shape[0] // tm
    tok = lambda i: (i, 0)
    const = lambda i: (0, 0)
    return pl.pallas_call(
        functools.partial(_out_kernel, n_prompt_tiles=n_prompt_tiles),
        grid=(n // tm,),
        in_specs=[*_stream_specs((tm, GLA_V), n_prompt_tiles),
                  *_stream_specs((tm, NAT_W), n_prompt_tiles),
                  *_stream_specs((tm, d), n_prompt_tiles),
                  pl.BlockSpec((1, 6, d), lambda i: (batch_of_tile(i), 0, 0)),
                  pl.BlockSpec(w_g.shape, const),
                  pl.BlockSpec(w_n.shape, const),
                  pl.BlockSpec((1, d), const),
                  pl.BlockSpec(rw_t.shape, const),
                  pl.BlockSpec((N_EXPERTS, 1), const)],
        out_specs=[pl.BlockSpec((tm, d), tok),
                   pl.BlockSpec(_row_table_shape(tm), tok),
                   pl.BlockSpec((TOP_K, tm), lambda i: (0, i)),
                   pl.BlockSpec((TOP_K, tm), lambda i: (0, i)),
                   pl.BlockSpec((TOP_K, tm), lambda i: (0, i)),
                   pl.BlockSpec((N_EXPERTS, LANES), const),
                   pl.BlockSpec((1, N_EXPERTS, LANES), lambda i: (i, 0, 0)),
                   pl.BlockSpec((1, N_EXPERTS, LANES), lambda i: (i, 0, 0))],
        out_shape=[jax.ShapeDtypeStruct((n, d), F32),
                   jax.ShapeDtypeStruct(_row_table_shape(n), F32),
                   jax.ShapeDtypeStruct((TOP_K, n), jnp.int32),
                   jax.ShapeDtypeStruct((TOP_K, n), F32),
                   jax.ShapeDtypeStruct((TOP_K, n), jnp.int32),
                   jax.ShapeDtypeStruct((N_EXPERTS, LANES), F32),
                   jax.ShapeDtypeStruct((n // tm, N_EXPERTS, LANES), F32),
                   jax.ShapeDtypeStruct((n // tm, N_EXPERTS, LANES), F32)],
        scratch_shapes=[pltpu.VMEM((N_EXPERTS, 1), F32)],
        compiler_params=_cparams(("arbitrary",)),
    )(*og, *on, *x, mod3, w_g, w_n, norm_g.reshape(1, d), rw_t, rb.reshape(N_EXPERTS, 1))


def _row_copy(src, dst, sem):
    return pltpu.make_async_copy(src, dst, sem)


def _segment_copies(cnt_ref, far_ref, near_ref, tile, copy):
    def per_expert(e, carry):
        idx = tile * N_EXPERTS + e
        c, far, near = cnt_ref[idx], far_ref[idx], near_ref[idx]

        def whole(i, carry2):
            copy(near + i * SEGMENT_CHUNK, far + i * SEGMENT_CHUNK, SEGMENT_CHUNK).start()
            return carry2

        lax.fori_loop(0, c // SEGMENT_CHUNK, whole, 0)
        for bit in reversed(range(SEGMENT_CHUNK.bit_length() - 1)):
            size = 1 << bit
            done = c & -(2 * size)

            @pl.when((c & size) != 0)
            def _(done=done, size=size):
                copy(near + done, far + done, size).start()

        return carry

    lax.fori_loop(0, N_EXPERTS, per_expert, 0)


def _dispatch_kernel(cnt_ref, far_ref, near_ref, pos_ref, h_ref, xs_hbm, buf_ref, sem):
    pairs = pos_ref.shape[2]
    tt = pairs // TOP_K
    i = pl.program_id(0)
    buf = i % 2
    base = buf * pairs

    def place(t, carry):
        row = h_ref[_row_slice(t), :]
        for k in range(TOP_K):
            buf_ref[_row_slice(base + pos_ref[0, 0, TOP_K * t + k]), :] = row
        return carry

    lax.fori_loop(0, tt, place, 0, unroll=ROW_LOOP_UNROLL)

    def copy(near, far, rows):
        return _row_copy(_rows_of(buf_ref, base + near, rows), _rows_of(xs_hbm, far, rows), sem.at[buf])

    _segment_copies(cnt_ref, far_ref, near_ref, i, copy)

    def drain(b):
        _row_copy(_rows_of(buf_ref, b * pairs, pairs), _rows_of(xs_hbm, 0, pairs), sem.at[b]).wait()

    pl.when(i > 0)(lambda: drain(1 - buf))
    pl.when(i == pl.num_programs(0) - 1)(lambda: drain(buf))


def _dispatch(seg, pos, h2):
    n = h2.shape[0] // LANE_TILES
    tt = TOKEN_TILE
    grid_spec = pltpu.PrefetchScalarGridSpec(
        num_scalar_prefetch=3,
        grid=(n // tt,),
        in_specs=[pl.BlockSpec((1, 1, TOP_K * tt), lambda i, *_: (i, 0, 0), memory_space=pltpu.SMEM),
                  pl.BlockSpec(_row_table_shape(tt), lambda i, *_: (i, 0))],
        out_specs=pl.BlockSpec(memory_space=pl.ANY),
        scratch_shapes=[pltpu.VMEM(_row_table_shape(2 * TOP_K * tt), F32),
                        pltpu.SemaphoreType.DMA((2,))],
    )
    return pl.pallas_call(
        _dispatch_kernel,
        grid_spec=grid_spec,
        out_shape=jax.ShapeDtypeStruct(_row_table_shape(n * TOP_K), F32),
        compiler_params=_cparams(("arbitrary",)),
    )(*seg, pos, h2)


def _swiglu_col(feature0, up):
    group, within = divmod(feature0, SWIGLU_GROUP)
    return (2 * group + up) * SWIGLU_GROUP + within


def _deinterleave_kernel(w_ref, o_ref):
    f2 = w_ref.shape[2]
    slab = 2 * LANES
    r = lax.broadcasted_iota(jnp.int32, (slab, slab), 0)
    c = lax.broadcasted_iota(jnp.int32, (slab, slab), 1)
    perm = (r == jnp.where(c < LANES, 2 * c, 2 * (c - LANES) + 1)).astype(BF16)
    for j in range(f2 // slab):
        y = _dot(w_ref[0, :, j * slab:(j + 1) * slab].astype(BF16), perm)
        for up in range(2):
            c0 = _swiglu_col(j * LANES, up)
            o_ref[0, :, c0:c0 + LANES] = y[:, up * LANES:(up + 1) * LANES].astype(BF16)


def _deinterleave(w):
    e, d, f2 = w.shape
    return pl.pallas_call(
        _deinterleave_kernel,
        grid=(e,),
        in_specs=[pl.BlockSpec((1, d, f2), lambda i: (i, 0, 0))],
        out_specs=pl.BlockSpec((1, d, f2), lambda i: (i, 0, 0)),
        out_shape=jax.ShapeDtypeStruct((e, d, f2), BF16),
        compiler_params=_cparams(("parallel",)),
    )(w)


def _expert_kernel(blk_ref, exp_ref, lo_ref, hi_ref, xs_ref, wgu_ref, bgu_ref, wd_ref, bd_ref, ys_ref):
    del blk_ref, exp_ref
    rows = xs_ref.shape[0] // LANE_TILES
    f = wd_ref.shape[1]
    i = pl.program_id(0)
    lo = lo_ref[i]
    hi = hi_ref[i]

    @pl.when(hi > lo)
    def _():
        g = SWIGLU_GROUP
        x = _load_rows(xs_ref, 0, rows).astype(BF16)
        acts = []
        for j in range(f // g):
            gu = _dot(x, wgu_ref[0, :, 2 * j * g:2 * (j + 1) * g]) + bgu_ref[0, :, 2 * j * g:2 * (j + 1) * g]
            gate = jnp.minimum(gu[:, :g], SWIGLU_LIMIT)
            up = jnp.clip(gu[:, g:], -SWIGLU_LIMIT, SWIGLU_LIMIT)
            acts.append(((up + 1.0) * gate * jax.nn.sigmoid(SWIGLU_ALPHA * gate)).astype(BF16))
        y = _dot(jnp.concatenate(acts, axis=1), wd_ref[0]) + bd_ref[0]
        r = lax.broadcasted_iota(jnp.int32, y.shape, 0)
        mine = (r >= lo) & (r < hi)

        @pl.when(lo == 0)
        def _():
            _store_rows(ys_ref, jnp.where(mine, y, 0.0))

        @pl.when(lo > 0)
        def _():
            _store_rows(ys_ref, jnp.where(mine, y, _load_rows(ys_ref, 0, rows)))


def _experts(items, xs2, w_gu, b_gu, w_down, b_down):
    item_blk, item_exp, item_lo, item_hi = items
    d, f2 = w_gu.shape[1], w_gu.shape[2]
    wmap = lambda i, blk, exp, lo, hi: (exp[i], 0, 0)
    xmap = lambda i, blk, exp, lo, hi: (blk[i], 0)
    grid_spec = pltpu.PrefetchScalarGridSpec(
        num_scalar_prefetch=4,
        grid=(item_blk.shape[0],),
        in_specs=[pl.BlockSpec(_row_table_shape(MOE_ROWS), xmap),
                  pl.BlockSpec((1, d, f2), wmap),
                  pl.BlockSpec((1, 1, f2), wmap),
                  pl.BlockSpec((1, f2 // 2, d), wmap),
                  pl.BlockSpec((1, 1, d), wmap)],
        out_specs=pl.BlockSpec(_row_table_shape(MOE_ROWS), xmap),
    )
    return pl.pallas_call(
        _expert_kernel,
        grid_spec=grid_spec,
        out_shape=jax.ShapeDtypeStruct(xs2.shape, F32),
        compiler_params=_cparams(("arbitrary",)),
    )(item_blk, item_exp, item_lo, item_hi, xs2, w_gu, b_gu, w_down, b_down)


def _expert_work_items(counts, n_rows):
    ends = jnp.cumsum(counts)
    starts = ends - counts
    n_blk = n_rows // MOE_ROWS
    cuts = jnp.sort(jnp.concatenate([jnp.arange(n_blk, dtype=jnp.int32) * MOE_ROWS, starts[1:]]))
    nxt = jnp.concatenate([cuts[1:], jnp.full((1,), n_rows, jnp.int32)])
    blk = jnp.minimum(cuts // MOE_ROWS, n_blk - 1)
    exp = jnp.minimum(jnp.sum(ends[None, :] <= cuts[:, None], axis=1), N_EXPERTS - 1).astype(jnp.int32)
    lo = cuts - blk * MOE_ROWS
    hi = nxt - blk * MOE_ROWS
    return (blk.astype(jnp.int32), exp, lo.astype(jnp.int32), hi.astype(jnp.int32)), starts


def _combine_kernel(cnt_ref, far_ref, near_ref, pos_ref, gates_ref, x1_ref, mod_ref, fg_ref, ys_hbm,
                    op_ref, os_ref, buf_ref, y_ref, sem, *, n_prompt_tiles):
    tt = x1_ref.shape[0]
    pairs = TOP_K * tt
    i = pl.program_id(0)
    buf = i % 2
    base = buf * pairs

    def fetch(tile, b):
        def copy(near, far, rows):
            return _row_copy(_rows_of(ys_hbm, far, rows), _rows_of(buf_ref, b * pairs + near, rows), sem.at[b])

        _segment_copies(cnt_ref, far_ref, near_ref, tile, copy)

    pl.when(i == 0)(lambda: fetch(0, 0))
    pl.when(i + 1 < pl.num_programs(0))(lambda: fetch(i + 1, 1 - buf))
    _row_copy(_rows_of(ys_hbm, 0, pairs), _rows_of(buf_ref, base, pairs), sem.at[buf]).wait()

    def weighted_sum(t, carry):
        acc = None
        for k in range(TOP_K):
            term = gates_ref[0, 0, TOP_K * t + k] * buf_ref[_row_slice(base + pos_ref[0, 0, TOP_K * t + k]), :]
            acc = term if acc is None else acc + term
        y_ref[_row_slice(t), :] = acc
        return carry

    lax.fori_loop(0, tt, weighted_sum, 0, unroll=ROW_LOOP_UNROLL)

    x2 = x1_ref[...] + mod_ref[0, 5:6, :] * _load_rows(y_ref, 0, tt)
    ms = jnp.mean(x2 * x2, axis=-1, keepdims=True)
    out = x2 * lax.rsqrt(ms + EPS) * fg_ref[...]

    def write(stream):
        (op_ref, os_ref)[stream][...] = out

    _per_stream(n_prompt_tiles, write)


def _combine(seg, pos, gates, x1, mod3, final_g, ys, batch_of_tile, n_prompt):
    n, d = x1.shape
    tt = TOKEN_TILE
    n_prompt_tiles = n_prompt // tt
    grid_spec = pltpu.PrefetchScalarGridSpec(
        num_scalar_prefetch=3,
        grid=(n // tt,),
        in_specs=[pl.BlockSpec((1, 1, TOP_K * tt), lambda i, *_: (i, 0, 0), memory_space=pltpu.SMEM),
                  pl.BlockSpec((1, 1, TOP_K * tt), lambda i, *_: (i, 0, 0), memory_space=pltpu.SMEM),
                  pl.BlockSpec((tt, d), lambda i, *_: (i, 0)),
                  pl.BlockSpec((1, 6, d), lambda i, *_: (batch_of_tile(i), 0, 0)),
                  pl.BlockSpec((1, d), lambda i, *_: (0, 0)),
                  pl.BlockSpec(memory_space=pl.ANY)],
        out_specs=list(_stream_specs((tt, d), n_prompt_tiles)),
        scratch_shapes=[pltpu.VMEM(_row_table_shape(2 * TOP_K * tt), F32),
                        pltpu.VMEM(_row_table_shape(tt), F32),
                        pltpu.SemaphoreType.DMA((2,))],
    )
    return pl.pallas_call(
        functools.partial(_combine_kernel, n_prompt_tiles=n_prompt_tiles),
        grid_spec=grid_spec,
        out_shape=[jax.ShapeDtypeStruct((n_prompt, d), F32),
                   jax.ShapeDtypeStruct((n - n_prompt, d), F32)],
        compiler_params=_cparams(("arbitrary",)),
    )(*seg, pos, gates, x1, mod3, final_g.reshape(1, d), ys)


def _batch_of_tile_fn(tile, bp, tp, ts):
    n_prompt = bp * tp

    def batch_of_tile(i):
        t = i * tile
        return jnp.where(t < n_prompt, t // tp, bp + (t - n_prompt) // ts)

    return batch_of_tile


def kernel(x_prompt, x_sample, c_prompt, c_sample, norm1_g, w_ada, b_ada, w_in, gla_w2_fwd, gla_b2_fwd, gla_w2_bwd, gla_b2_bwd, gla_norm_g, nat_rpb, w_out, norm2_g, router_w, router_b, w_gate_up, b_gate_up, w_down, b_down, final_norm_g):
    assert w_ada.shape[0] == 1, "single-layer encoder"
    bp, tp, d = x_prompt.shape
    bs, ts, _ = x_sample.shape
    n_prompt, n_sample = bp * tp, bs * ts
    n = n_prompt + n_sample
    assert tp % TOKEN_TILE == 0 and ts % TOKEN_TILE == 0 and n_prompt % ts == 0
    assert d == LANE_TILES * LANES

    x = (x_prompt.reshape(n_prompt, d), x_sample.reshape(n_sample, d))
    c = jnp.concatenate([c_prompt, c_sample], axis=0)

    sizes = (GLA_QK, GLA_QK, GLA_V, GLA_V, GLA_RANK, GLA_RANK, NAT_W, NAT_W, NAT_W)
    offs = np.concatenate([[0], np.cumsum(sizes)])
    w_in0 = w_in[0]
    seg = lambda j: w_in0[:, offs[j]:offs[j + 1]]
    w_all = jnp.concatenate(
        [seg(0), seg(1), seg(2), seg(3), seg(6) * (NAT_HD ** -0.5 * LOG2E), seg(7), seg(8), seg(4), seg(5),
         jnp.zeros((d, LANES - 2 * GLA_RANK), F32)], axis=1).astype(BF16)
    pairs = GLA_HEADS // 2
    w2 = jnp.zeros((LANES, pairs, 2, 2 * GLA_DK), F32)
    w2 = w2.at[0:GLA_RANK, :, 0].set(gla_w2_fwd[0].reshape(GLA_RANK, pairs, 2 * GLA_DK))
    w2 = w2.at[GLA_RANK:2 * GLA_RANK, :, 1].set(gla_w2_bwd[0].reshape(GLA_RANK, pairs, 2 * GLA_DK))
    wz = w2.reshape(LANES, pairs * 4 * GLA_DK).astype(BF16)
    bz = jnp.stack([gla_b2_fwd[0].reshape(pairs, 2 * GLA_DK), gla_b2_bwd[0].reshape(pairs, 2 * GLA_DK)],
                   axis=1).reshape(1, pairs * 4 * GLA_DK)
    tab = _nat_bias_table(nat_rpb[0])
    w_og = w_out[0, :GLA_V].astype(BF16)
    w_on = w_out[0, GLA_V:].astype(BF16)
    rw_t = router_w[0].T
    w_gu = _deinterleave(w_gate_up[0])
    n_e, f2 = b_gate_up.shape[1:]
    b_gu = (b_gate_up[0].reshape(n_e, f2 // (2 * SWIGLU_GROUP), SWIGLU_GROUP, 2)
            .transpose(0, 1, 3, 2).reshape(n_e, 1, f2))
    w_dn = w_down[0].astype(BF16)
    b_dn = b_down[0, :, None, :]

    mod3 = _ada(c, w_ada[0], b_ada[0]).reshape(bp + bs, 6, d)
    bot_tok = _batch_of_tile_fn(TOKEN_TILE, bp, tp, ts)

    pg, pn, plr = _in_proj(*x, mod3, norm1_g[0], w_all, bot_tok)

    gla_args = (pg, plr, wz, bz, gla_norm_g[0].reshape(1, -1))
    og = (_gla(*gla_args, seq=tp, batches=bp, block0=0),
          _gla(*gla_args, seq=ts, batches=bs, block0=n_prompt // ts))
    on = (_nat(pn, tab, seq=tp, batches=bp, block0=0),
          _nat(pn, tab, seq=ts, batches=bs, block0=n_prompt // ts))

    x1, h2, top_i, gates_t, rank_t, cnt, tile_cnt, tile_carry = _out_proj(
        og, on, x, mod3, w_og, w_on, norm2_g[0], rw_t, router_b[0], bot_tok)

    n_rows = n * TOP_K
    n_tiles = n // TOKEN_TILE
    assert n_rows % MOE_ROWS == 0
    counts = cnt[:, 0].astype(jnp.int32)
    items, starts = _expert_work_items(counts, n_rows)
    seg_cnt = tile_cnt[:, :, 0].astype(jnp.int32)
    seg_far = starts[None, :] + tile_carry[:, :, 0].astype(jnp.int32)
    seg_near = jnp.cumsum(seg_cnt, axis=1) - seg_cnt
    seg = tuple(a.reshape(n_tiles * N_EXPERTS) for a in (seg_cnt, seg_far, seg_near))
    chosen = (top_i.reshape(1, TOP_K, n_tiles, TOKEN_TILE)
              == jnp.arange(N_EXPERTS, dtype=jnp.int32)[:, None, None, None])
    pos_t = rank_t + jnp.sum(jnp.where(chosen, seg_near.T[:, None, :, None], 0), axis=0).reshape(TOP_K, n)
    per_tile = lambda a: a.T.reshape(n_tiles, 1, TOKEN_TILE * TOP_K)
    pos, gates = per_tile(pos_t), per_tile(gates_t)

    xs = _dispatch(seg, pos, h2)
    ys = _experts(items, xs, w_gu, b_gu, w_dn, b_dn)
    y_prompt, y_sample = _combine(seg, pos, gates, x1, mod3, final_norm_g, ys, bot_tok, n_prompt)

    return (y_prompt.reshape(bp, tp, d), y_sample.reshape(bs, ts, d))
```

```python
import functools

import numpy as np
import jax
import jax.numpy as jnp
from jax import lax
from jax.experimental import pallas as pl
from jax.experimental.pallas import tpu as pltpu

F32 = jnp.float32
BF16 = jnp.bfloat16
HIGHEST = lax.Precision.HIGHEST

EPS = 1e-5
GRID_W = 64
GLA_HEADS = 4
GLA_DK = 64
GLA_DV = 128
GLA_RANK = 16
GLA_TAU = 16.0
GLA_CHUNK = 64
GLA_QK = GLA_HEADS * GLA_DK
GLA_V = GLA_HEADS * GLA_DV
NAT_HEADS = 16
NAT_HD = 32
NAT_W = NAT_HEADS * NAT_HD
NAT_KH = 8
NAT_KW = 16
NAT_GROUP = 4
N_EXPERTS = 32
TOP_K = 4
SWIGLU_LIMIT = 7.0
SWIGLU_ALPHA = 1.702

LANES = 128
SUBLANES = 8
TOKEN_TILE = 512
MOE_ROWS = 512
SWIGLU_GROUP = 1024
GLA_UNROLL = 16
NAT_UNROLL = 8
ROW_LOOP_UNROLL = 8
SEGMENT_CHUNK = 16
LOG2E = 1.4426950408889634
VMEM_LIMIT = 56 * 1024 * 1024


def _cparams(sem, vmem=VMEM_LIMIT):
    return pltpu.CompilerParams(dimension_semantics=sem, vmem_limit_bytes=vmem)


def _dot(a, b):
    return jnp.dot(a, b, preferred_element_type=F32)


def _dot_nt(a, b, precision=None):
    return lax.dot_general(a, b, (((1,), (1,)), ((), ())), preferred_element_type=F32, precision=precision)


LANE_TILES = 8


def _row_table_shape(rows):
    return (rows * LANE_TILES, LANES)


def _store_rows(ref, x, row0=0):
    rows = x.shape[0]
    for s in range(LANE_TILES):
        ref[pl.ds(row0 * LANE_TILES + s, rows, stride=LANE_TILES), :] = x[:, s * LANES:(s + 1) * LANES]


def _load_rows(ref, row0, rows):
    return jnp.concatenate(
        [ref[pl.ds(row0 * LANE_TILES + s, rows, stride=LANE_TILES), :] for s in range(LANE_TILES)], axis=1)


def _row_slice(row0, rows=1):
    return pl.ds(pl.multiple_of(row0 * LANE_TILES, LANE_TILES), rows * LANE_TILES)


def _rows_of(ref, row0, rows):
    return ref.at[_row_slice(row0, rows), :]


def _dot_tn(a, b):
    return lax.dot_general(a, b, (((0,), (0,)), ((), ())), preferred_element_type=F32)


def _ada_kernel(c_ref, w_ref, b_ref, o_ref):
    c = c_ref[...]
    s = c * jax.nn.sigmoid(c)
    o_ref[...] = jnp.dot(s, w_ref[...], preferred_element_type=F32, precision=HIGHEST) + b_ref[...]


def _ada(c, w, b):
    nb, d = c.shape
    cols = w.shape[1]
    blk = 1024
    return pl.pallas_call(
        _ada_kernel,
        grid=(cols // blk,),
        in_specs=[pl.BlockSpec((nb, d), lambda j: (0, 0)),
                  pl.BlockSpec((d, blk), lambda j: (0, j)),
                  pl.BlockSpec((1, blk), lambda j: (0, j))],
        out_specs=pl.BlockSpec((nb, blk), lambda j: (0, j)),
        out_shape=jax.ShapeDtypeStruct((nb, cols), F32),
        compiler_params=_cparams(("arbitrary",)),
    )(c, w, b.reshape(1, cols))


def _stream_specs(block, n_prompt_tiles):
    return (pl.BlockSpec(block, lambda i, *_: (jnp.minimum(i, n_prompt_tiles - 1), 0)),
            pl.BlockSpec(block, lambda i, *_: (jnp.maximum(i - n_prompt_tiles, 0), 0)))


def _per_stream(n_prompt_tiles, body):
    i = pl.program_id(0)
    pl.when(i < n_prompt_tiles)(functools.partial(body, 0))
    pl.when(i >= n_prompt_tiles)(functools.partial(body, 1))


def _in_kernel(xp_ref, xs_ref, mod_ref, g_ref, w_ref, pg_ref, pn_ref, plr_ref, *, n_prompt_tiles):
    def body(stream):
        x = (xp_ref, xs_ref)[stream][...]
        ms = jnp.mean(x * x, axis=-1, keepdims=True)
        y = x * lax.rsqrt(ms + EPS) * g_ref[...]
        h = y * (1.0 + mod_ref[0, 1:2, :]) + mod_ref[0, 0:1, :]
        hb = h.astype(BF16)
        wg = GLA_QK * 2 + GLA_V * 2
        wn = 3 * NAT_W
        pg_ref[...] = _dot(hb, w_ref[:, 0:wg]).astype(BF16)
        pn_ref[...] = _dot(hb, w_ref[:, wg:wg + wn]).astype(BF16)
        plr_ref[...] = _dot(hb, w_ref[:, wg + wn:wg + wn + LANES])

    _per_stream(n_prompt_tiles, body)


def _in_proj(xp, xs, mod3, norm_g, w_all, batch_of_tile):
    d = xp.shape[1]
    n = xp.shape[0] + xs.shape[0]
    wg = GLA_QK * 2 + GLA_V * 2
    wn = 3 * NAT_W
    tm = TOKEN_TILE
    n_prompt_tiles = xp.shape[0] // tm
    return pl.pallas_call(
        functools.partial(_in_kernel, n_prompt_tiles=n_prompt_tiles),
        grid=(n // tm,),
        in_specs=[*_stream_specs((tm, d), n_prompt_tiles),
                  pl.BlockSpec((1, 6, d), lambda i: (batch_of_tile(i), 0, 0)),
                  pl.BlockSpec((1, d), lambda i: (0, 0)),
                  pl.BlockSpec(w_all.shape, lambda i: (0, 0))],
        out_specs=[pl.BlockSpec((tm, wg), lambda i: (i, 0)),
                   pl.BlockSpec((tm, wn), lambda i: (i, 0)),
                   pl.BlockSpec((tm, LANES), lambda i: (i, 0))],
        out_shape=[jax.ShapeDtypeStruct((n, wg), BF16),
                   jax.ShapeDtypeStruct((n, wn), BF16),
                   jax.ShapeDtypeStruct((n, LANES), F32)],
        compiler_params=_cparams(("arbitrary",)),
    )(xp, xs, mod3, norm_g.reshape(1, d), w_all)


def _gla_kernel(q_ref, k_ref, v_ref, g_ref, lr_ref, wz_ref, bz_ref, ng_ref,
                o_ref, la_ref, acc_ref, qt_ref, ke_ref, dec_ref, sf_ref, sb_ref, *, seq):
    c_len = GLA_CHUNK
    n = seq // c_len
    row = lax.broadcasted_iota(jnp.int32, (c_len, c_len), 0)
    col = lax.broadcasted_iota(jnp.int32, (c_len, c_len), 1)
    tril = col <= row
    triu = col >= row
    cum_row = lax.broadcasted_iota(jnp.int32, (c_len, 2 * c_len), 0)
    cum_col = lax.broadcasted_iota(jnp.int32, (c_len, 2 * c_len), 1) % c_len
    cum = (cum_col <= cum_row).astype(F32).astype(BF16)
    mask_f = jnp.concatenate([tril, tril], axis=0)
    mask_b = jnp.concatenate([triu, triu], axis=0)
    head_a = lax.broadcasted_iota(jnp.int32, (c_len, LANES), 1) < GLA_DK
    st_row = lax.broadcasted_iota(jnp.int32, (2 * GLA_DV, LANES), 0)
    st_col = lax.broadcasted_iota(jnp.int32, (2 * GLA_DV, LANES), 1)
    blockdiag = (st_row < GLA_DV) == (st_col < GLA_DK)

    z = _dot(lr_ref[...].astype(BF16), wz_ref[...]) + bz_ref[...]
    la_ref[...] = (jnp.minimum(z, 0.0) - jnp.log(1.0 + jnp.exp(-jnp.abs(z)))) * (1.0 / GLA_TAU)

    def chunk_rows(c):
        return pl.ds(pl.multiple_of(c * c_len, c_len), c_len)

    def stack_heads(x):
        zero = jnp.zeros_like(x)
        return jnp.concatenate([jnp.where(head_a, x, zero), jnp.where(head_a, zero, x)], axis=0).astype(BF16)

    def local(c, carry):
        rows = chunk_rows(c)
        la = la_ref[rows, :]
        hi = la.astype(BF16)
        lo = (la - hi.astype(F32)).astype(BF16)
        binc = _dot(cum, jnp.concatenate([hi, lo], axis=0))
        btot = jnp.sum(la, axis=0, keepdims=True)
        b_f = binc[:, :LANES]
        b_b = btot[:, LANES:] - binc[:, LANES:] + la[:, LANES:]
        e_tot = jnp.exp(btot)
        q = q_ref[rows, :].astype(F32) * (GLA_DK ** -0.5)
        k = k_ref[rows, :].astype(F32)
        qt_f = q * jnp.exp(b_f)
        qt_b = q * jnp.exp(b_b)
        kt_f = k * jnp.exp(-b_f)
        kt_b = k * jnp.exp(-b_b)
        a = (jnp.where(mask_f, _dot_nt(stack_heads(qt_f), kt_f.astype(BF16)), 0.0)
             + jnp.where(mask_b, _dot_nt(stack_heads(qt_b), kt_b.astype(BF16)), 0.0))
        oi = _dot(a.astype(BF16), v_ref[rows, :])
        acc_ref[rows, :] = jnp.concatenate([oi[0:c_len, 0:GLA_DV], oi[c_len:, GLA_DV:]], axis=1)
        qt_ref[rows, :] = jnp.concatenate([qt_f, qt_b], axis=1).astype(BF16)
        ke_ref[rows, :] = jnp.concatenate([kt_f * e_tot[:, :LANES], kt_b * e_tot[:, LANES:]], axis=1).astype(BF16)
        dec_ref[pl.ds(pl.multiple_of(c * SUBLANES, SUBLANES), SUBLANES), :] = jnp.broadcast_to(e_tot, (SUBLANES, 2 * LANES))
        return carry

    def carried(c, s_ref, half):
        rows = chunk_rows(c)
        lanes = slice(half * LANES, (half + 1) * LANES)
        st = s_ref[...]
        acc_ref[rows, :] += _dot_nt(qt_ref[rows, lanes], st.astype(BF16))
        upd = _dot_tn(v_ref[rows, :], ke_ref[rows, lanes])
        dec = dec_ref[pl.ds(pl.multiple_of(c * SUBLANES, SUBLANES), 1), lanes]
        s_ref[...] = st * dec + jnp.where(blockdiag, upd, 0.0)

    def scan(c, carry):
        carried(c, sf_ref, 0)
        carried(n - 1 - c, sb_ref, 1)
        return carry

    def finalize(c, carry):
        rows = chunk_rows(c)
        o = acc_ref[rows, :]
        halves = []
        for hh in range(2):
            oh = o[:, hh * GLA_DV:(hh + 1) * GLA_DV]
            ms = jnp.mean(oh * oh, axis=-1, keepdims=True)
            halves.append(oh * lax.rsqrt(ms + EPS) * ng_ref[...])
        y = jnp.concatenate(halves, axis=1)
        g = g_ref[rows, :].astype(F32)
        o_ref[rows, :] = (y * (g * jax.nn.sigmoid(g))).astype(o_ref.dtype)
        return carry

    sf_ref[...] = jnp.zeros_like(sf_ref)
    sb_ref[...] = jnp.zeros_like(sb_ref)
    lax.fori_loop(0, n, local, 0, unroll=GLA_UNROLL)
    lax.fori_loop(0, n, scan, 0, unroll=GLA_UNROLL)
    lax.fori_loop(0, n, finalize, 0, unroll=GLA_UNROLL)


def _gla(pg, plr, wz, bz, norm_g, *, seq, batches, block0):
    pair_w = 2 * GLA_DK
    pair_v = 2 * GLA_DV
    k0 = GLA_QK // pair_w
    v0 = 2 * GLA_QK // pair_v
    g0 = (2 * GLA_QK + GLA_V) // pair_v
    n_chunks = seq // GLA_CHUNK
    in_specs = [pl.BlockSpec((seq, pair_w), lambda b, p: (block0 + b, p)),
                pl.BlockSpec((seq, pair_w), lambda b, p: (block0 + b, k0 + p)),
                pl.BlockSpec((seq, pair_v), lambda b, p: (block0 + b, v0 + p)),
                pl.BlockSpec((seq, pair_v), lambda b, p: (block0 + b, g0 + p)),
                pl.BlockSpec((seq, LANES), lambda b, p: (block0 + b, 0)),
                pl.BlockSpec((LANES, 2 * pair_w), lambda b, p: (0, p)),
                pl.BlockSpec((1, 2 * pair_w), lambda b, p: (0, p)),
                pl.BlockSpec((1, GLA_DV), lambda b, p: (0, 0))]
    return pl.pallas_call(
        functools.partial(_gla_kernel, seq=seq),
        grid=(batches, GLA_HEADS // 2),
        in_specs=in_specs,
        out_specs=pl.BlockSpec((seq, pair_v), lambda b, p: (b, p)),
        out_shape=jax.ShapeDtypeStruct((batches * seq, GLA_V), BF16),
        scratch_shapes=[pltpu.VMEM((seq, 2 * pair_w), F32),
                        pltpu.VMEM((seq, pair_v), F32),
                        pltpu.VMEM((seq, 2 * pair_w), BF16),
                        pltpu.VMEM((seq, 2 * pair_w), BF16),
                        pltpu.VMEM((n_chunks * SUBLANES, 2 * pair_w), F32),
                        pltpu.VMEM((pair_v, pair_w), F32),
                        pltpu.VMEM((pair_v, pair_w), F32)],
        compiler_params=_cparams(("parallel", "parallel")),
    )(pg, pg, pg, pg, plr, wz, bz, norm_g)


def _nat_kernel(q_ref, k_ref, v_ref, tab_ref, o_ref, s_ref, *, rows):
    w = GRID_W
    head = lax.broadcasted_iota(jnp.int32, (w, LANES), 1) // NAT_HD
    win = NAT_KH * w

    def window(i):
        r0 = jnp.clip(i - NAT_KH // 2, 0, rows - NAT_KH)
        return r0 - i + NAT_KH - 1, pl.ds(pl.multiple_of(r0 * w, w), win)

    def scores(i, slot):
        base, krows = window(i)
        q = q_ref[pl.ds(pl.multiple_of(i * w, w), w), :]
        zero = jnp.zeros_like(q)
        q_stack = jnp.concatenate([jnp.where(head == h, q, zero) for h in range(NAT_GROUP)], axis=0)
        s_ref[slot] = _dot_nt(q_stack, k_ref[krows, :]) + tab_ref[0, base]

    def attend(i, slot):
        _, krows = window(i)
        s = s_ref[slot]
        m = jnp.max(s, axis=-1, keepdims=True)
        e = jnp.exp2(s - m)
        l = jnp.sum(e, axis=-1, keepdims=True)
        o = _dot(e.astype(BF16), v_ref[krows, :]) / l
        out = jnp.zeros((w, LANES), F32)
        for h in range(NAT_GROUP):
            out = out + jnp.where(head == h, o[h * w:(h + 1) * w, :], 0.0)
        o_ref[pl.ds(pl.multiple_of(i * w, w), w), :] = out.astype(o_ref.dtype)

    scores(0, 0)

    def body(j, carry):
        for u in range(2):
            i = 2 * j + u
            scores(jnp.minimum(i + 1, rows - 1), 1 - u)
            attend(i, u)
        return carry

    lax.fori_loop(0, rows // 2, body, 0, unroll=NAT_UNROLL // 2)


def _nat(pn, tab, *, seq, batches, block0):
    rows = seq // GRID_W
    groups = NAT_HEADS // NAT_GROUP
    in_specs = [pl.BlockSpec((seq, LANES), lambda g, b: (block0 + b, g)),
                pl.BlockSpec((seq, LANES), lambda g, b: (block0 + b, groups + g)),
                pl.BlockSpec((seq, LANES), lambda g, b: (block0 + b, 2 * groups + g)),
                pl.BlockSpec((1,) + tab.shape[1:], lambda g, b: (g, 0, 0, 0))]
    return pl.pallas_call(
        functools.partial(_nat_kernel, rows=rows),
        grid=(groups, batches),
        in_specs=in_specs,
        out_specs=pl.BlockSpec((seq, LANES), lambda g, b: (b, g)),
        out_shape=jax.ShapeDtypeStruct((batches * seq, NAT_W), BF16),
        scratch_shapes=[pltpu.VMEM((2, NAT_GROUP * GRID_W, NAT_KH * GRID_W), F32)],
        compiler_params=_cparams(("parallel", "parallel")),
    )(pn, pn, pn, tab)


def _nat_bias_table(rpb):
    w = GRID_W
    jq = np.arange(w)[:, None]
    jk = np.arange(w)[None, :]
    c0 = np.clip(jq - NAT_KW // 2, 0, w - NAT_KW)
    valid = (jk >= c0) & (jk < c0 + NAT_KW)
    dcol = np.clip(jk - jq + NAT_KW - 1, 0, 2 * NAT_KW - 2)
    select = (dcol[None] == np.arange(2 * NAT_KW - 1)[:, None, None]).astype(np.float32)
    t = jnp.einsum('hrc,cqk->hrqk', rpb.astype(F32) * LOG2E, select, precision=HIGHEST)
    t = jnp.where(valid[None, None], t, -jnp.inf)
    t = jnp.stack([t[:, b:b + NAT_KH] for b in range(NAT_KH)], axis=1)
    t = t.transpose(0, 1, 3, 2, 4).reshape(NAT_HEADS, NAT_KH, w, NAT_KH * w)
    t = t.reshape(NAT_HEADS // NAT_GROUP, NAT_GROUP, NAT_KH, w, NAT_KH * w).transpose(0, 2, 1, 3, 4)
    return t.reshape(NAT_HEADS // NAT_GROUP, NAT_KH, NAT_GROUP * w, NAT_KH * w)


def _out_kernel(ogp_ref, ogs_ref, onp_ref, ons_ref, xp_ref, xs_ref, mod_ref, wg_ref, wn_ref, n2_ref, rw_ref, rb_ref,
                x1_ref, h2_ref, ti_ref, gt_ref, rk_ref, cnt_ref, tcnt_ref, tcar_ref, carry_ref, *, n_prompt_tiles):
    tm = x1_ref.shape[0]

    @pl.when(pl.program_id(0) == 0)
    def _():
        carry_ref[...] = jnp.zeros_like(carry_ref)

    def residual(stream):
        og_ref, on_ref, x_ref = ((ogp_ref, onp_ref, xp_ref), (ogs_ref, ons_ref, xs_ref))[stream]
        mix = _dot(og_ref[...], wg_ref[...]) + _dot(on_ref[...], wn_ref[...])
        x1_ref[...] = x_ref[...] + mod_ref[0, 2:3, :] * mix

    _per_stream(n_prompt_tiles, residual)
    x1 = x1_ref[...]
    ms = jnp.mean(x1 * x1, axis=-1, keepdims=True)
    h2 = x1 * lax.rsqrt(ms + EPS) * n2_ref[...]
    h2 = h2 * (1.0 + mod_ref[0, 4:5, :]) + mod_ref[0, 3:4, :]
    _store_rows(h2_ref, h2)

    logits = _dot_nt(rw_ref[...], h2, precision=HIGHEST) + rb_ref[...]
    eidx = lax.broadcasted_iota(jnp.int32, logits.shape, 0)
    vals = logits
    top_v, top_i = [], []
    for _ in range(TOP_K):
        m = jnp.max(vals, axis=0, keepdims=True)
        idx = jnp.min(jnp.where(vals == m, eidx, N_EXPERTS), axis=0, keepdims=True)
        top_v.append(m)
        top_i.append(idx)
        vals = jnp.where(eidx == idx, -jnp.inf, vals)
    ev = [jnp.exp(v - top_v[0]) for v in top_v]
    den = ev[0] + ev[1] + ev[2] + ev[3]
    ti_ref[...] = jnp.concatenate(top_i, axis=0)
    gt_ref[...] = jnp.concatenate([e / den for e in ev], axis=0)

    onehot = jnp.zeros(logits.shape, F32)
    for idx in top_i:
        onehot = onehot + (eidx == idx).astype(F32)
    r = lax.broadcasted_iota(jnp.int32, (tm, tm), 0)
    c = lax.broadcasted_iota(jnp.int32, (tm, tm), 1)
    earlier = (r < c).astype(BF16)
    before = _dot(onehot.astype(BF16), earlier)
    rk_ref[...] = jnp.concatenate(
        [jnp.sum(jnp.where(eidx == idx, before, 0.0), axis=0, keepdims=True) for idx in top_i], axis=0).astype(jnp.int32)
    tile_counts = jnp.sum(onehot, axis=1, keepdims=True)
    tcar_ref[0] = jnp.broadcast_to(carry_ref[...], tcar_ref.shape[1:])
    tcnt_ref[0] = jnp.broadcast_to(tile_counts, tcnt_ref.shape[1:])
    carry_ref[...] = carry_ref[...] + tile_counts
    cnt_ref[...] = jnp.broadcast_to(carry_ref[...], cnt_ref.shape)


def _out_proj(og, on, x, mod3, w_g, w_n, norm_g, rw_t, rb, batch_of_tile):
    d = x[0].shape[1]
    n = x[0].shape[0] + x[1].shape[0]
    tm = TOKEN_TILE
    n_prompt_tiles = x[0].shape[0] // tm
    tok = lambda i: (i, 0)
    const = lambda i: (0, 0)
    return pl.pallas_call(
        functools.partial(_out_kernel, n_prompt_tiles=n_prompt_tiles),
        grid=(n // tm,),
        in_specs=[*_stream_specs((tm, GLA_V), n_prompt_tiles),
                  *_stream_specs((tm, NAT_W), n_prompt_tiles),
                  *_stream_specs((tm, d), n_prompt_tiles),
                  pl.BlockSpec((1, 6, d), lambda i: (batch_of_tile(i), 0, 0)),
                  pl.BlockSpec(w_g.shape, const),
                  pl.BlockSpec(w_n.shape, const),
                  pl.BlockSpec((1, d), const),
                  pl.BlockSpec(rw_t.shape, const),
                  pl.BlockSpec((N_EXPERTS, 1), const)],
        out_specs=[pl.BlockSpec((tm, d), tok),
                   pl.BlockSpec(_row_table_shape(tm), tok),
                   pl.BlockSpec((TOP_K, tm), lambda i: (0, i)),
                   pl.BlockSpec((TOP_K, tm), lambda i: (0, i)),
                   pl.BlockSpec((TOP_K, tm), lambda i: (0, i)),
                   pl.BlockSpec((N_EXPERTS, LANES), const),
                   pl.BlockSpec((1, N_EXPERTS, LANES), lambda i: (i, 0, 0)),
                   pl.BlockSpec((1, N_EXPERTS, LANES), lambda i: (i, 0, 0))],
        out_shape=[jax.ShapeDtypeStruct((n, d), F32),
                   jax.ShapeDtypeStruct(_row_table_shape(n), F32),
                   jax.ShapeDtypeStruct((TOP_K, n), jnp.int32),
                   jax.ShapeDtypeStruct((TOP_K, n), F32),
                   jax.ShapeDtypeStruct((TOP_K, n), jnp.int32),
                   jax.ShapeDtypeStruct((N_EXPERTS, LANES), F32),
                   jax.ShapeDtypeStruct((n // tm, N_EXPERTS, LANES), F32),
                   jax.ShapeDtypeStruct((n // tm, N_EXPERTS, LANES), F32)],
        scratch_shapes=[pltpu.VMEM((N_EXPERTS, 1), F32)],
        compiler_params=_cparams(("arbitrary",)),
    )(*og, *on, *x, mod3, w_g, w_n, norm_g.reshape(1, d), rw_t, rb.reshape(N_EXPERTS, 1))


def _row_copy(src, dst, sem):
    return pltpu.make_async_copy(src, dst, sem)


def _segment_copies(cnt_ref, far_ref, near_ref, tile, copy):
    def per_expert(e, carry):
        idx = tile * N_EXPERTS + e
        c, far, near = cnt_ref[idx], far_ref[idx], near_ref[idx]

        def whole(i, carry2):
            copy(near + i * SEGMENT_CHUNK, far + i * SEGMENT_CHUNK, SEGMENT_CHUNK).start()
            return carry2

        lax.fori_loop(0, c // SEGMENT_CHUNK, whole, 0)
        for bit in reversed(range(SEGMENT_CHUNK.bit_length() - 1)):
            size = 1 << bit
            done = c & -(2 * size)

            @pl.when((c & size) != 0)
            def _(done=done, size=size):
                copy(near + done, far + done, size).start()

        return carry

    lax.fori_loop(0, N_EXPERTS, per_expert, 0)


def _dispatch_kernel(cnt_ref, far_ref, near_ref, pos_ref, h_ref, xs_hbm, buf_ref, sem):
    pairs = pos_ref.shape[2]
    tt = pairs // TOP_K
    i = pl.program_id(0)
    buf = i % 2
    base = buf * pairs

    def place(t, carry):
        row = h_ref[_row_slice(t), :]
        for k in range(TOP_K):
            buf_ref[_row_slice(base + pos_ref[0, 0, TOP_K * t + k]), :] = row
        return carry

    lax.fori_loop(0, tt, place, 0, unroll=ROW_LOOP_UNROLL)

    def copy(near, far, rows):
        return _row_copy(_rows_of(buf_ref, base + near, rows), _rows_of(xs_hbm, far, rows), sem.at[buf])

    _segment_copies(cnt_ref, far_ref, near_ref, i, copy)

    def drain(b):
        _row_copy(_rows_of(buf_ref, b * pairs, pairs), _rows_of(xs_hbm, 0, pairs), sem.at[b]).wait()

    pl.when(i > 0)(lambda: drain(1 - buf))
    pl.when(i == pl.num_programs(0) - 1)(lambda: drain(buf))


def _dispatch(seg, pos, h2):
    n = h2.shape[0] // LANE_TILES
    tt = TOKEN_TILE
    grid_spec = pltpu.PrefetchScalarGridSpec(
        num_scalar_prefetch=3,
        grid=(n // tt,),
        in_specs=[pl.BlockSpec((1, 1, TOP_K * tt), lambda i, *_: (i, 0, 0), memory_space=pltpu.SMEM),
                  pl.BlockSpec(_row_table_shape(tt), lambda i, *_: (i, 0))],
        out_specs=pl.BlockSpec(memory_space=pl.ANY),
        scratch_shapes=[pltpu.VMEM(_row_table_shape(2 * TOP_K * tt), F32),
                        pltpu.SemaphoreType.DMA((2,))],
    )
    return pl.pallas_call(
        _dispatch_kernel,
        grid_spec=grid_spec,
        out_shape=jax.ShapeDtypeStruct(_row_table_shape(n * TOP_K), F32),
        compiler_params=_cparams(("arbitrary",)),
    )(*seg, pos, h2)


def _swiglu_col(feature0, up):
    group, within = divmod(feature0, SWIGLU_GROUP)
    return (2 * group + up) * SWIGLU_GROUP + within


def _deinterleave_kernel(w_ref, o_ref):
    f2 = w_ref.shape[2]
    slab = 2 * LANES
    r = lax.broadcasted_iota(jnp.int32, (slab, slab), 0)
    c = lax.broadcasted_iota(jnp.int32, (slab, slab), 1)
    perm = (r == jnp.where(c < LANES, 2 * c, 2 * (c - LANES) + 1)).astype(BF16)
    for j in range(f2 // slab):
        y = _dot(w_ref[0, :, j * slab:(j + 1) * slab].astype(BF16), perm)
        for up in range(2):
            c0 = _swiglu_col(j * LANES, up)
            o_ref[0, :, c0:c0 + LANES] = y[:, up * LANES:(up + 1) * LANES].astype(BF16)


def _deinterleave(w):
    e, d, f2 = w.shape
    return pl.pallas_call(
        _deinterleave_kernel,
        grid=(e,),
        in_specs=[pl.BlockSpec((1, d, f2), lambda i: (i, 0, 0))],
        out_specs=pl.BlockSpec((1, d, f2), lambda i: (i, 0, 0)),
        out_shape=jax.ShapeDtypeStruct((e, d, f2), BF16),
        compiler_params=_cparams(("parallel",)),
    )(w)


def _expert_kernel(blk_ref, exp_ref, lo_ref, hi_ref, xs_ref, wgu_ref, bgu_ref, wd_ref, bd_ref, ys_ref):
    del blk_ref, exp_ref
    rows = xs_ref.shape[0] // LANE_TILES
    f = wd_ref.shape[1]
    i = pl.program_id(0)
    lo = lo_ref[i]
    hi = hi_ref[i]

    @pl.when(hi > lo)
    def _():
        g = SWIGLU_GROUP
        x = _load_rows(xs_ref, 0, rows).astype(BF16)
        acts = []
        for j in range(f // g):
            gu = _dot(x, wgu_ref[0, :, 2 * j * g:2 * (j + 1) * g]) + bgu_ref[0, :, 2 * j * g:2 * (j + 1) * g]
            gate = jnp.minimum(gu[:, :g], SWIGLU_LIMIT)
            up = jnp.clip(gu[:, g:], -SWIGLU_LIMIT, SWIGLU_LIMIT)
            acts.append(((up + 1.0) * gate * jax.nn.sigmoid(SWIGLU_ALPHA * gate)).astype(BF16))
        y = _dot(jnp.concatenate(acts, axis=1), wd_ref[0]) + bd_ref[0]
        r = lax.broadcasted_iota(jnp.int32, y.shape, 0)
        mine = (r >= lo) & (r < hi)

        @pl.when(lo == 0)
        def _():
            _store_rows(ys_ref, jnp.where(mine, y, 0.0))

        @pl.when(lo > 0)
        def _():
            _store_rows(ys_ref, jnp.where(mine, y, _load_rows(ys_ref, 0, rows)))


def _experts(items, xs2, w_gu, b_gu, w_down, b_down):
    item_blk, item_exp, item_lo, item_hi = items
    d, f2 = w_gu.shape[1], w_gu.shape[2]
    wmap = lambda i, blk, exp, lo, hi: (exp[i], 0, 0)
    xmap = lambda i, blk, exp, lo, hi: (blk[i], 0)
    grid_spec = pltpu.PrefetchScalarGridSpec(
        num_scalar_prefetch=4,
        grid=(item_blk.shape[0],),
        in_specs=[pl.BlockSpec(_row_table_shape(MOE_ROWS), xmap),
                  pl.BlockSpec((1, d, f2), wmap),
                  pl.BlockSpec((1, 1, f2), wmap),
                  pl.BlockSpec((1, f2 // 2, d), wmap),
                  pl.BlockSpec((1, 1, d), wmap)],
        out_specs=pl.BlockSpec(_row_table_shape(MOE_ROWS), xmap),
    )
    return pl.pallas_call(
        _expert_kernel,
        grid_spec=grid_spec,
        out_shape=jax.ShapeDtypeStruct(xs2.shape, F32),
        compiler_params=_cparams(("arbitrary",)),
    )(item_blk, item_exp, item_lo, item_hi, xs2, w_gu, b_gu, w_down, b_down)


def _expert_work_items(counts, n_rows):
    ends = jnp.cumsum(counts)
    starts = ends - counts
    n_blk = n_rows // MOE_ROWS
    cuts = jnp.sort(jnp.concatenate([jnp.arange(n_blk, dtype=jnp.int32) * MOE_ROWS, starts[1:]]))
    nxt = jnp.concatenate([cuts[1:], jnp.full((1,), n_rows, jnp.int32)])
    blk = jnp.minimum(cuts // MOE_ROWS, n_blk - 1)
    exp = jnp.minimum(jnp.sum(ends[None, :] <= cuts[:, None], axis=1), N_EXPERTS - 1).astype(jnp.int32)
    lo = cuts - blk * MOE_ROWS
    hi = nxt - blk * MOE_ROWS
    return (blk.astype(jnp.int32), exp, lo.astype(jnp.int32), hi.astype(jnp.int32)), starts


def _combine_kernel(cnt_ref, far_ref, near_ref, pos_ref, gates_ref, x1_ref, mod_ref, fg_ref, ys_hbm,
                    op_ref, os_ref, buf_ref, y_ref, sem, *, n_prompt_tiles):
    tt = x1_ref.shape[0]
    pairs = TOP_K * tt
    i = pl.program_id(0)
    buf = i % 2
    base = buf * pairs

    def fetch(tile, b):
        def copy(near, far, rows):
            return _row_copy(_rows_of(ys_hbm, far, rows), _rows_of(buf_ref, b * pairs + near, rows), sem.at[b])

        _segment_copies(cnt_ref, far_ref, near_ref, tile, copy)

    pl.when(i == 0)(lambda: fetch(0, 0))
    pl.when(i + 1 < pl.num_programs(0))(lambda: fetch(i + 1, 1 - buf))
    _row_copy(_rows_of(ys_hbm, 0, pairs), _rows_of(buf_ref, base, pairs), sem.at[buf]).wait()

    def weighted_sum(t, carry):
        acc = None
        for k in range(TOP_K):
            term = gates_ref[0, 0, TOP_K * t + k] * buf_ref[_row_slice(base + pos_ref[0, 0, TOP_K * t + k]), :]
            acc = term if acc is None else acc + term
        y_ref[_row_slice(t), :] = acc
        return carry

    lax.fori_loop(0, tt, weighted_sum, 0, unroll=ROW_LOOP_UNROLL)

    x2 = x1_ref[...] + mod_ref[0, 5:6, :] * _load_rows(y_ref, 0, tt)
    ms = jnp.mean(x2 * x2, axis=-1, keepdims=True)
    out = x2 * lax.rsqrt(ms + EPS) * fg_ref[...]

    def write(stream):
        (op_ref, os_ref)[stream][...] = out

    _per_stream(n_prompt_tiles, write)


def _combine(seg, pos, gates, x1, mod3, final_g, ys, batch_of_tile, n_prompt):
    n, d = x1.shape
    tt = TOKEN_TILE
    n_prompt_tiles = n_prompt // tt
    grid_spec = pltpu.PrefetchScalarGridSpec(
        num_scalar_prefetch=3,
        grid=(n // tt,),
        in_specs=[pl.BlockSpec((1, 1, TOP_K * tt), lambda i, *_: (i, 0, 0), memory_space=pltpu.SMEM),
                  pl.BlockSpec((1, 1, TOP_K * tt), lambda i, *_: (i, 0, 0), memory_space=pltpu.SMEM),
                  pl.BlockSpec((tt, d), lambda i, *_: (i, 0)),
                  pl.BlockSpec((1, 6, d), lambda i, *_: (batch_of_tile(i), 0, 0)),
                  pl.BlockSpec((1, d), lambda i, *_: (0, 0)),
                  pl.BlockSpec(memory_space=pl.ANY)],
        out_specs=list(_stream_specs((tt, d), n_prompt_tiles)),
        scratch_shapes=[pltpu.VMEM(_row_table_shape(2 * TOP_K * tt), F32),
                        pltpu.VMEM(_row_table_shape(tt), F32),
                        pltpu.SemaphoreType.DMA((2,))],
    )
    return pl.pallas_call(
        functools.partial(_combine_kernel, n_prompt_tiles=n_prompt_tiles),
        grid_spec=grid_spec,
        out_shape=[jax.ShapeDtypeStruct((n_prompt, d), F32),
                   jax.ShapeDtypeStruct((n - n_prompt, d), F32)],
        compiler_params=_cparams(("arbitrary",)),
    )(*seg, pos, gates, x1, mod3, final_g.reshape(1, d), ys)


def _batch_of_tile_fn(tile, bp, tp, ts):
    n_prompt = bp * tp

    def batch_of_tile(i):
        t = i * tile
        return jnp.where(t < n_prompt, t // tp, bp + (t - n_prompt) // ts)

    return batch_of_tile


def kernel(x_prompt, x_sample, c_prompt, c_sample, norm1_g, w_ada, b_ada, w_in, gla_w2_fwd, gla_b2_fwd, gla_w2_bwd, gla_b2_bwd, gla_norm_g, nat_rpb, w_out, norm2_g, router_w, router_b, w_gate_up, b_gate_up, w_down, b_down, final_norm_g):
    assert w_ada.shape[0] == 1, "single-layer encoder"
    bp, tp, d = x_prompt.shape
    bs, ts, _ = x_sample.shape
    n_prompt, n_sample = bp * tp, bs * ts
    n = n_prompt + n_sample
    assert tp % TOKEN_TILE == 0 and ts % TOKEN_TILE == 0 and n_prompt % ts == 0
    assert d == LANE_TILES * LANES

    x = (x_prompt.reshape(n_prompt, d), x_sample.reshape(n_sample, d))
    c = jnp.concatenate([c_prompt, c_sample], axis=0)

    sizes = (GLA_QK, GLA_QK, GLA_V, GLA_V, GLA_RANK, GLA_RANK, NAT_W, NAT_W, NAT_W)
    offs = np.concatenate([[0], np.cumsum(sizes)])
    w_in0 = w_in[0]
    seg = lambda j: w_in0[:, offs[j]:offs[j + 1]]
    w_all = jnp.concatenate(
        [seg(0), seg(1), seg(2), seg(3), seg(6) * (NAT_HD ** -0.5 * LOG2E), seg(7), seg(8), seg(4), seg(5),
         jnp.zeros((d, LANES - 2 * GLA_RANK), F32)], axis=1).astype(BF16)
    pairs = GLA_HEADS // 2
    w2 = jnp.zeros((LANES, pairs, 2, 2 * GLA_DK), F32)
    w2 = w2.at[0:GLA_RANK, :, 0].set(gla_w2_fwd[0].reshape(GLA_RANK, pairs, 2 * GLA_DK))
    w2 = w2.at[GLA_RANK:2 * GLA_RANK, :, 1].set(gla_w2_bwd[0].reshape(GLA_RANK, pairs, 2 * GLA_DK))
    wz = w2.reshape(LANES, pairs * 4 * GLA_DK).astype(BF16)
    bz = jnp.stack([gla_b2_fwd[0].reshape(pairs, 2 * GLA_DK), gla_b2_bwd[0].reshape(pairs, 2 * GLA_DK)],
                   axis=1).reshape(1, pairs * 4 * GLA_DK)
    tab = _nat_bias_table(nat_rpb[0])
    w_og = w_out[0, :GLA_V].astype(BF16)
    w_on = w_out[0, GLA_V:].astype(BF16)
    rw_t = router_w[0].T
    w_gu = _deinterleave(w_gate_up[0])
    n_e, f2 = b_gate_up.shape[1:]
    b_gu = (b_gate_up[0].reshape(n_e, f2 // (2 * SWIGLU_GROUP), SWIGLU_GROUP, 2)
            .transpose(0, 1, 3, 2).reshape(n_e, 1, f2))
    w_dn = w_down[0].astype(BF16)
    b_dn = b_down[0, :, None, :]

    mod3 = _ada(c, w_ada[0], b_ada[0]).reshape(bp + bs, 6, d)
    bot_tok = _batch_of_tile_fn(TOKEN_TILE, bp, tp, ts)

    pg, pn, plr = _in_proj(*x, mod3, norm1_g[0], w_all, bot_tok)

    gla_args = (pg, plr, wz, bz, gla_norm_g[0].reshape(1, -1))
    og = (_gla(*gla_args, seq=tp, batches=bp, block0=0),
          _gla(*gla_args, seq=ts, batches=bs, block0=n_prompt // ts))
    on = (_nat(pn, tab, seq=tp, batches=bp, block0=0),
          _nat(pn, tab, seq=ts, batches=bs, block0=n_prompt // ts))

    x1, h2, top_i, gates_t, rank_t, cnt, tile_cnt, tile_carry = _out_proj(
        og, on, x, mod3, w_og, w_on, norm2_g[0], rw_t, router_b[0], bot_tok)

    n_rows = n * TOP_K
    n_tiles = n // TOKEN_TILE
    assert n_rows % MOE_ROWS == 0
    counts = cnt[:, 0].astype(jnp.int32)
    items, starts = _expert_work_items(counts, n_rows)
    seg_cnt = tile_cnt[:, :, 0].astype(jnp.int32)
    seg_far = starts[None, :] + tile_carry[:, :, 0].astype(jnp.int32)
    seg_near = jnp.cumsum(seg_cnt, axis=1) - seg_cnt
    seg = tuple(a.reshape(n_tiles * N_EXPERTS) for a in (seg_cnt, seg_far, seg_near))
    chosen = (top_i.reshape(1, TOP_K, n_tiles, TOKEN_TILE)
              == jnp.arange(N_EXPERTS, dtype=jnp.int32)[:, None, None, None])
    pos_t = rank_t + jnp.sum(jnp.where(chosen, seg_near.T[:, None, :, None], 0), axis=0).reshape(TOP_K, n)
    per_tile = lambda a: a.T.reshape(n_tiles, 1, TOKEN_TILE * TOP_K)
    pos, gates = per_tile(pos_t), per_tile(gates_t)

    xs = _dispatch(seg, pos, h2)
    ys = _experts(items, xs, w_gu, b_gu, w_dn, b_dn)
    y_prompt, y_sample = _combine(seg, pos, gates, x1, mod3, final_norm_g, ys, bot_tok, n_prompt)

    return (y_prompt.reshape(bp, tp, d), y_sample.reshape(bs, ts, d))
```

```python
import functools

import numpy as np
import jax
import jax.numpy as jnp
from jax import lax
from jax.experimental import pallas as pl
from jax.experimental.pallas import tpu as pltpu

F32 = jnp.float32
BF16 = jnp.bfloat16
HIGHEST = lax.Precision.HIGHEST

EPS = 1e-5
GRID_W = 64
GLA_HEADS = 4
GLA_DK = 64
GLA_DV = 128
GLA_RANK = 16
GLA_TAU = 16.0
GLA_CHUNK = 64
GLA_QK = GLA_HEADS * GLA_DK
GLA_V = GLA_HEADS * GLA_DV
NAT_HEADS = 16
NAT_HD = 32
NAT_W = NAT_HEADS * NAT_HD
NAT_KH = 8
NAT_KW = 16
NAT_GROUP = 4
N_EXPERTS = 32
TOP_K = 4
SWIGLU_LIMIT = 7.0
SWIGLU_ALPHA = 1.702

LANES = 128
SUBLANES = 8
TOKEN_TILE = 512
MOE_ROWS = 512
SWIGLU_GROUP = 1024
GLA_UNROLL = 16
NAT_UNROLL = 8
ROW_LOOP_UNROLL = 8
SEGMENT_CHUNK = 32
LOG2E = 1.4426950408889634
VMEM_LIMIT = 56 * 1024 * 1024


def _cparams(sem, vmem=VMEM_LIMIT):
    return pltpu.CompilerParams(dimension_semantics=sem, vmem_limit_bytes=vmem)


def _dot(a, b):
    return jnp.dot(a, b, preferred_element_type=F32)


def _dot_nt(a, b, precision=None):
    return lax.dot_general(a, b, (((1,), (1,)), ((), ())), preferred_element_type=F32, precision=precision)


LANE_TILES = 8


def _row_table_shape(rows):
    return (rows * LANE_TILES, LANES)


def _store_rows(ref, x, row0=0):
    rows = x.shape[0]
    for s in range(LANE_TILES):
        ref[pl.ds(row0 * LANE_TILES + s, rows, stride=LANE_TILES), :] = x[:, s * LANES:(s + 1) * LANES]


def _load_rows(ref, row0, rows):
    return jnp.concatenate(
        [ref[pl.ds(row0 * LANE_TILES + s, rows, stride=LANE_TILES), :] for s in range(LANE_TILES)], axis=1)


def _row_slice(row0, rows=1):
    return pl.ds(pl.multiple_of(row0 * LANE_TILES, LANE_TILES), rows * LANE_TILES)


def _rows_of(ref, row0, rows):
    return ref.at[_row_slice(row0, rows), :]


def _dot_tn(a, b):
    return lax.dot_general(a, b, (((0,), (0,)), ((), ())), preferred_element_type=F32)


def _ada_kernel(c_ref, w_ref, b_ref, o_ref):
    c = c_ref[...]
    s = c * jax.nn.sigmoid(c)
    o_ref[...] = jnp.dot(s, w_ref[...], preferred_element_type=F32, precision=HIGHEST) + b_ref[...]


def _ada(c, w, b):
    nb, d = c.shape
    cols = w.shape[1]
    blk = 1024
    return pl.pallas_call(
        _ada_kernel,
        grid=(cols // blk,),
        in_specs=[pl.BlockSpec((nb, d), lambda j: (0, 0)),
                  pl.BlockSpec((d, blk), lambda j: (0, j)),
                  pl.BlockSpec((1, blk), lambda j: (0, j))],
        out_specs=pl.BlockSpec((nb, blk), lambda j: (0, j)),
        out_shape=jax.ShapeDtypeStruct((nb, cols), F32),
        compiler_params=_cparams(("arbitrary",)),
    )(c, w, b.reshape(1, cols))


def _stream_specs(block, n_prompt_tiles):
    return (pl.BlockSpec(block, lambda i, *_: (jnp.minimum(i, n_prompt_tiles - 1), 0)),
            pl.BlockSpec(block, lambda i, *_: (jnp.maximum(i - n_prompt_tiles, 0), 0)))


def _per_stream(n_prompt_tiles, body):
    i = pl.program_id(0)
    pl.when(i < n_prompt_tiles)(functools.partial(body, 0))
    pl.when(i >= n_prompt_tiles)(functools.partial(body, 1))


def _in_kernel(xp_ref, xs_ref, mod_ref, g_ref, w_ref, pg_ref, pn_ref, plr_ref, *, n_prompt_tiles):
    def body(stream):
        x = (xp_ref, xs_ref)[stream][...]
        ms = jnp.mean(x * x, axis=-1, keepdims=True)
        y = x * lax.rsqrt(ms + EPS) * g_ref[...]
        h = y * (1.0 + mod_ref[0, 1:2, :]) + mod_ref[0, 0:1, :]
        hb = h.astype(BF16)
        wg = GLA_QK * 2 + GLA_V * 2
        wn = 3 * NAT_W
        pg_ref[...] = _dot(hb, w_ref[:, 0:wg]).astype(BF16)
        pn_ref[...] = _dot(hb, w_ref[:, wg:wg + wn]).astype(BF16)
        plr_ref[...] = _dot(hb, w_ref[:, wg + wn:wg + wn + LANES])

    _per_stream(n_prompt_tiles, body)


def _in_proj(xp, xs, mod3, norm_g, w_all, batch_of_tile):
    d = xp.shape[1]
    n = xp.shape[0] + xs.shape[0]
    wg = GLA_QK * 2 + GLA_V * 2
    wn = 3 * NAT_W
    tm = TOKEN_TILE
    n_prompt_tiles = xp.shape[0] // tm
    return pl.pallas_call(
        functools.partial(_in_kernel, n_prompt_tiles=n_prompt_tiles),
        grid=(n // tm,),
        in_specs=[*_stream_specs((tm, d), n_prompt_tiles),
                  pl.BlockSpec((1, 6, d), lambda i: (batch_of_tile(i), 0, 0)),
                  pl.BlockSpec((1, d), lambda i: (0, 0)),
                  pl.BlockSpec(w_all.shape, lambda i: (0, 0))],
        out_specs=[pl.BlockSpec((tm, wg), lambda i: (i, 0)),
                   pl.BlockSpec((tm, wn), lambda i: (i, 0)),
                   pl.BlockSpec((tm, LANES), lambda i: (i, 0))],
        out_shape=[jax.ShapeDtypeStruct((n, wg), BF16),
                   jax.ShapeDtypeStruct((n, wn), BF16),
                   jax.ShapeDtypeStruct((n, LANES), F32)],
        compiler_params=_cparams(("arbitrary",)),
    )(xp, xs, mod3, norm_g.reshape(1, d), w_all)


def _gla_kernel(q_ref, k_ref, v_ref, g_ref, lr_ref, wz_ref, bz_ref, ng_ref,
                o_ref, la_ref, acc_ref, qt_ref, ke_ref, dec_ref, sf_ref, sb_ref, *, seq):
    c_len = GLA_CHUNK
    n = seq // c_len
    row = lax.broadcasted_iota(jnp.int32, (c_len, c_len), 0)
    col = lax.broadcasted_iota(jnp.int32, (c_len, c_len), 1)
    tril = col <= row
    triu = col >= row
    cum_row = lax.broadcasted_iota(jnp.int32, (c_len, 2 * c_len), 0)
    cum_col = lax.broadcasted_iota(jnp.int32, (c_len, 2 * c_len), 1) % c_len
    cum = (cum_col <= cum_row).astype(F32).astype(BF16)
    mask_f = jnp.concatenate([tril, tril], axis=0)
    mask_b = jnp.concatenate([triu, triu], axis=0)
    head_a = lax.broadcasted_iota(jnp.int32, (c_len, LANES), 1) < GLA_DK
    st_row = lax.broadcasted_iota(jnp.int32, (2 * GLA_DV, LANES), 0)
    st_col = lax.broadcasted_iota(jnp.int32, (2 * GLA_DV, LANES), 1)
    blockdiag = (st_row < GLA_DV) == (st_col < GLA_DK)

    z = _dot(lr_ref[...].astype(BF16), wz_ref[...]) + bz_ref[...]
    la_ref[...] = (jnp.minimum(z, 0.0) - jnp.log(1.0 + jnp.exp(-jnp.abs(z)))) * (1.0 / GLA_TAU)

    def chunk_rows(c):
        return pl.ds(pl.multiple_of(c * c_len, c_len), c_len)

    def stack_heads(x):
        zero = jnp.zeros_like(x)
        return jnp.concatenate([jnp.where(head_a, x, zero), jnp.where(head_a, zero, x)], axis=0).astype(BF16)

    def local(c, carry):
        rows = chunk_rows(c)
        la = la_ref[rows, :]
        hi = la.astype(BF16)
        lo = (la - hi.astype(F32)).astype(BF16)
        binc = _dot(cum, jnp.concatenate([hi, lo], axis=0))
        btot = jnp.sum(la, axis=0, keepdims=True)
        b_f = binc[:, :LANES]
        b_b = btot[:, LANES:] - binc[:, LANES:] + la[:, LANES:]
        e_tot = jnp.exp(btot)
        q = q_ref[rows, :].astype(F32) * (GLA_DK ** -0.5)
        k = k_ref[rows, :].astype(F32)
        qt_f = q * jnp.exp(b_f)
        qt_b = q * jnp.exp(b_b)
        kt_f = k * jnp.exp(-b_f)
        kt_b = k * jnp.exp(-b_b)
        a = (jnp.where(mask_f, _dot_nt(stack_heads(qt_f), kt_f.astype(BF16)), 0.0)
             + jnp.where(mask_b, _dot_nt(stack_heads(qt_b), kt_b.astype(BF16)), 0.0))
        oi = _dot(a.astype(BF16), v_ref[rows, :])
        acc_ref[rows, :] = jnp.concatenate([oi[0:c_len, 0:GLA_DV], oi[c_len:, GLA_DV:]], axis=1)
        qt_ref[rows, :] = jnp.concatenate([qt_f, qt_b], axis=1).astype(BF16)
        ke_ref[rows, :] = jnp.concatenate([kt_f * e_tot[:, :LANES], kt_b * e_tot[:, LANES:]], axis=1).astype(BF16)
        dec_ref[pl.ds(pl.multiple_of(c * SUBLANES, SUBLANES), SUBLANES), :] = jnp.broadcast_to(e_tot, (SUBLANES, 2 * LANES))
        return carry

    def carried(c, s_ref, half):
        rows = chunk_rows(c)
        lanes = slice(half * LANES, (half + 1) * LANES)
        st = s_ref[...]
        acc_ref[rows, :] += _dot_nt(qt_ref[rows, lanes], st.astype(BF16))
        upd = _dot_tn(v_ref[rows, :], ke_ref[rows, lanes])
        dec = dec_ref[pl.ds(pl.multiple_of(c * SUBLANES, SUBLANES), 1), lanes]
        s_ref[...] = st * dec + jnp.where(blockdiag, upd, 0.0)

    def scan(c, carry):
        carried(c, sf_ref, 0)
        carried(n - 1 - c, sb_ref, 1)
        return carry

    def finalize(c, carry):
        rows = chunk_rows(c)
        o = acc_ref[rows, :]
        halves = []
        for hh in range(2):
            oh = o[:, hh * GLA_DV:(hh + 1) * GLA_DV]
            ms = jnp.mean(oh * oh, axis=-1, keepdims=True)
            halves.append(oh * lax.rsqrt(ms + EPS) * ng_ref[...])
        y = jnp.concatenate(halves, axis=1)
        g = g_ref[rows, :].astype(F32)
        o_ref[rows, :] = (y * (g * jax.nn.sigmoid(g))).astype(o_ref.dtype)
        return carry

    sf_ref[...] = jnp.zeros_like(sf_ref)
    sb_ref[...] = jnp.zeros_like(sb_ref)
    lax.fori_loop(0, n, local, 0, unroll=GLA_UNROLL)
    lax.fori_loop(0, n, scan, 0, unroll=GLA_UNROLL)
    lax.fori_loop(0, n, finalize, 0, unroll=GLA_UNROLL)


def _gla(pg, plr, wz, bz, norm_g, *, seq, batches, block0):
    pair_w = 2 * GLA_DK
    pair_v = 2 * GLA_DV
    k0 = GLA_QK // pair_w
    v0 = 2 * GLA_QK // pair_v
    g0 = (2 * GLA_QK + GLA_V) // pair_v
    n_chunks = seq // GLA_CHUNK
    in_specs = [pl.BlockSpec((seq, pair_w), lambda b, p: (block0 + b, p)),
                pl.BlockSpec((seq, pair_w), lambda b, p: (block0 + b, k0 + p)),
                pl.BlockSpec((seq, pair_v), lambda b, p: (block0 + b, v0 + p)),
                pl.BlockSpec((seq, pair_v), lambda b, p: (block0 + b, g0 + p)),
                pl.BlockSpec((seq, LANES), lambda b, p: (block0 + b, 0)),
                pl.BlockSpec((LANES, 2 * pair_w), lambda b, p: (0, p)),
                pl.BlockSpec((1, 2 * pair_w), lambda b, p: (0, p)),
                pl.BlockSpec((1, GLA_DV), lambda b, p: (0, 0))]
    return pl.pallas_call(
        functools.partial(_gla_kernel, seq=seq),
        grid=(batches, GLA_HEADS // 2),
        in_specs=in_specs,
        out_specs=pl.BlockSpec((seq, pair_v), lambda b, p: (b, p)),
        out_shape=jax.ShapeDtypeStruct((batches * seq, GLA_V), BF16),
        scratch_shapes=[pltpu.VMEM((seq, 2 * pair_w), F32),
                        pltpu.VMEM((seq, pair_v), F32),
                        pltpu.VMEM((seq, 2 * pair_w), BF16),
                        pltpu.VMEM((seq, 2 * pair_w), BF16),
                        pltpu.VMEM((n_chunks * SUBLANES, 2 * pair_w), F32),
                        pltpu.VMEM((pair_v, pair_w), F32),
                        pltpu.VMEM((pair_v, pair_w), F32)],
        compiler_params=_cparams(("parallel", "parallel")),
    )(pg, pg, pg, pg, plr, wz, bz, norm_g)


def _nat_kernel(q_ref, k_ref, v_ref, tab_ref, o_ref, s_ref, *, rows):
    w = GRID_W
    head = lax.broadcasted_iota(jnp.int32, (w, LANES), 1) // NAT_HD
    win = NAT_KH * w

    def window(i):
        r0 = jnp.clip(i - NAT_KH // 2, 0, rows - NAT_KH)
        return r0 - i + NAT_KH - 1, pl.ds(pl.multiple_of(r0 * w, w), win)

    def scores(i, slot):
        base, krows = window(i)
        q = q_ref[pl.ds(pl.multiple_of(i * w, w), w), :]
        zero = jnp.zeros_like(q)
        q_stack = jnp.concatenate([jnp.where(head == h, q, zero) for h in range(NAT_GROUP)], axis=0)
        s_ref[slot] = _dot_nt(q_stack, k_ref[krows, :]) + tab_ref[0, base]

    def attend(i, slot):
        _, krows = window(i)
        s = s_ref[slot]
        m = jnp.max(s, axis=-1, keepdims=True)
        e = jnp.exp2(s - m)
        l = jnp.sum(e, axis=-1, keepdims=True)
        o = _dot(e.astype(BF16), v_ref[krows, :]) / l
        out = jnp.zeros((w, LANES), F32)
        for h in range(NAT_GROUP):
            out = out + jnp.where(head == h, o[h * w:(h + 1) * w, :], 0.0)
        o_ref[pl.ds(pl.multiple_of(i * w, w), w), :] = out.astype(o_ref.dtype)

    scores(0, 0)

    def body(j, carry):
        for u in range(2):
            i = 2 * j + u
            scores(jnp.minimum(i + 1, rows - 1), 1 - u)
            attend(i, u)
        return carry

    lax.fori_loop(0, rows // 2, body, 0, unroll=NAT_UNROLL // 2)


def _nat(pn, tab, *, seq, batches, block0):
    rows = seq // GRID_W
    groups = NAT_HEADS // NAT_GROUP
    in_specs = [pl.BlockSpec((seq, LANES), lambda g, b: (block0 + b, g)),
                pl.BlockSpec((seq, LANES), lambda g, b: (block0 + b, groups + g)),
                pl.BlockSpec((seq, LANES), lambda g, b: (block0 + b, 2 * groups + g)),
                pl.BlockSpec((1,) + tab.shape[1:], lambda g, b: (g, 0, 0, 0))]
    return pl.pallas_call(
        functools.partial(_nat_kernel, rows=rows),
        grid=(groups, batches),
        in_specs=in_specs,
        out_specs=pl.BlockSpec((seq, LANES), lambda g, b: (b, g)),
        out_shape=jax.ShapeDtypeStruct((batches * seq, NAT_W), BF16),
        scratch_shapes=[pltpu.VMEM((2, NAT_GROUP * GRID_W, NAT_KH * GRID_W), F32)],
        compiler_params=_cparams(("parallel", "parallel")),
    )(pn, pn, pn, tab)


def _nat_bias_table(rpb):
    w = GRID_W
    jq = np.arange(w)[:, None]
    jk = np.arange(w)[None, :]
    c0 = np.clip(jq - NAT_KW // 2, 0, w - NAT_KW)
    valid = (jk >= c0) & (jk < c0 + NAT_KW)
    dcol = np.clip(jk - jq + NAT_KW - 1, 0, 2 * NAT_KW - 2)
    select = (dcol[None] == np.arange(2 * NAT_KW - 1)[:, None, None]).astype(np.float32)
    t = jnp.einsum('hrc,cqk->hrqk', rpb.astype(F32) * LOG2E, select, precision=HIGHEST)
    t = jnp.where(valid[None, None], t, -jnp.inf)
    t = jnp.stack([t[:, b:b + NAT_KH] for b in range(NAT_KH)], axis=1)
    t = t.transpose(0, 1, 3, 2, 4).reshape(NAT_HEADS, NAT_KH, w, NAT_KH * w)
    t = t.reshape(NAT_HEADS // NAT_GROUP, NAT_GROUP, NAT_KH, w, NAT_KH * w).transpose(0, 2, 1, 3, 4)
    return t.reshape(NAT_HEADS // NAT_GROUP, NAT_KH, NAT_GROUP * w, NAT_KH * w)


def _out_kernel(ogp_ref, ogs_ref, onp_ref, ons_ref, xp_ref, xs_ref, mod_ref, wg_ref, wn_ref, n2_ref, rw_ref, rb_ref,
                x1_ref, h2_ref, ti_ref, gt_ref, rk_ref, cnt_ref, tcnt_ref, tcar_ref, carry_ref, *, n_prompt_tiles):
    tm = x1_ref.shape[0]

    @pl.when(pl.program_id(0) == 0)
    def _():
        carry_ref[...] = jnp.zeros_like(carry_ref)

    def residual(stream):
        og_ref, on_ref, x_ref = ((ogp_ref, onp_ref, xp_ref), (ogs_ref, ons_ref, xs_ref))[stream]
        mix = _dot(og_ref[...], wg_ref[...]) + _dot(on_ref[...], wn_ref[...])
        x1_ref[...] = x_ref[...] + mod_ref[0, 2:3, :] * mix

    _per_stream(n_prompt_tiles, residual)
    x1 = x1_ref[...]
    ms = jnp.mean(x1 * x1, axis=-1, keepdims=True)
    h2 = x1 * lax.rsqrt(ms + EPS) * n2_ref[...]
    h2 = h2 * (1.0 + mod_ref[0, 4:5, :]) + mod_ref[0, 3:4, :]
    _store_rows(h2_ref, h2)

    logits = _dot_nt(rw_ref[...], h2, precision=HIGHEST) + rb_ref[...]
    eidx = lax.broadcasted_iota(jnp.int32, logits.shape, 0)
    vals = logits
    top_v, top_i = [], []
    for _ in range(TOP_K):
        m = jnp.max(vals, axis=0, keepdims=True)
        idx = jnp.min(jnp.where(vals == m, eidx, N_EXPERTS), axis=0, keepdims=True)
        top_v.append(m)
        top_i.append(idx)
        vals = jnp.where(eidx == idx, -jnp.inf, vals)
    ev = [jnp.exp(v - top_v[0]) for v in top_v]
    den = ev[0] + ev[1] + ev[2] + ev[3]
    ti_ref[0] = jnp.concatenate(top_i, axis=0)
    gt_ref[0] = jnp.concatenate([e / den for e in ev], axis=0)

    onehot = jnp.zeros(logits.shape, F32)
    for idx in top_i:
        onehot = onehot + (eidx == idx).astype(F32)
    r = lax.broadcasted_iota(jnp.int32, (tm, tm), 0)
    c = lax.broadcasted_iota(jnp.int32, (tm, tm), 1)
    earlier = (r < c).astype(BF16)
    before = _dot(onehot.astype(BF16), earlier)
    rk_ref[0] = jnp.concatenate(
        [jnp.sum(jnp.where(eidx == idx, before, 0.0), axis=0, keepdims=True) for idx in top_i], axis=0).astype(jnp.int32)
    tile_counts = jnp.sum(onehot, axis=1, keepdims=True)
    tcar_ref[0] = jnp.broadcast_to(carry_ref[...], tcar_ref.shape[1:])
    tcnt_ref[0] = jnp.broadcast_to(tile_counts, tcnt_ref.shape[1:])
    carry_ref[...] = carry_ref[...] + tile_counts
    cnt_ref[...] = jnp.broadcast_to(carry_ref[...], cnt_ref.shape)


def _out_proj(og, on, x, mod3, w_g, w_n, norm_g, rw_t, rb, batch_of_tile):
    d = x[0].shape[1]
    n = x[0].shape[0] + x[1].shape[0]
    tm = TOKEN_TILE
    n_prompt_tiles = x[0].shape[0] // tm
    tok = lambda i: (i, 0)
    const = lambda i: (0, 0)
    return pl.pallas_call(
        functools.partial(_out_kernel, n_prompt_tiles=n_prompt_tiles),
        grid=(n // tm,),
        in_specs=[*_stream_specs((tm, GLA_V), n_prompt_tiles),
                  *_stream_specs((tm, NAT_W), n_prompt_tiles),
                  *_stream_specs((tm, d), n_prompt_tiles),
                  pl.BlockSpec((1, 6, d), lambda i: (batch_of_tile(i), 0, 0)),
                  pl.BlockSpec(w_g.shape, const),
                  pl.BlockSpec(w_n.shape, const),
                  pl.BlockSpec((1, d), const),
                  pl.BlockSpec(rw_t.shape, const),
                  pl.BlockSpec((N_EXPERTS, 1), const)],
        out_specs=[pl.BlockSpec((tm, d), tok),
                   pl.BlockSpec(_row_table_shape(tm), tok),
                   pl.BlockSpec((1, TOP_K, tm), lambda i: (i, 0, 0)),
                   pl.BlockSpec((1, TOP_K, tm), lambda i: (i, 0, 0)),
                   pl.BlockSpec((1, TOP_K, tm), lambda i: (i, 0, 0)),
                   pl.BlockSpec((N_EXPERTS, LANES), const),
                   pl.BlockSpec((1, N_EXPERTS, LANES), lambda i: (i, 0, 0)),
                   pl.BlockSpec((1, N_EXPERTS, LANES), lambda i: (i, 0, 0))],
        out_shape=[jax.ShapeDtypeStruct((n, d), F32),
                   jax.ShapeDtypeStruct(_row_table_shape(n), F32),
                   jax.ShapeDtypeStruct((n // tm, TOP_K, tm), jnp.int32),
                   jax.ShapeDtypeStruct((n // tm, TOP_K, tm), F32),
                   jax.ShapeDtypeStruct((n // tm, TOP_K, tm), jnp.int32),
                   jax.ShapeDtypeStruct((N_EXPERTS, LANES), F32),
                   jax.ShapeDtypeStruct((n // tm, N_EXPERTS, LANES), F32),
                   jax.ShapeDtypeStruct((n // tm, N_EXPERTS, LANES), F32)],
        scratch_shapes=[pltpu.VMEM((N_EXPERTS, 1), F32)],
        compiler_params=_cparams(("arbitrary",)),
    )(*og, *on, *x, mod3, w_g, w_n, norm_g.reshape(1, d), rw_t, rb.reshape(N_EXPERTS, 1))


def _row_copy(src, dst, sem):
    return pltpu.make_async_copy(src, dst, sem)


def _segment_copies(cnt_ref, far_ref, near_ref, tile, copy):
    def per_expert(e, carry):
        idx = tile * N_EXPERTS + e
        c, far, near = cnt_ref[idx], far_ref[idx], near_ref[idx]

        def whole(i, carry2):
            copy(near + i * SEGMENT_CHUNK, far + i * SEGMENT_CHUNK, SEGMENT_CHUNK).start()
            return carry2

        lax.fori_loop(0, c // SEGMENT_CHUNK, whole, 0)
        for bit in reversed(range(SEGMENT_CHUNK.bit_length() - 1)):
            size = 1 << bit
            done = c & -(2 * size)

            @pl.when((c & size) != 0)
            def _(done=done, size=size):
                copy(near + done, far + done, size).start()

        return carry

    lax.fori_loop(0, N_EXPERTS, per_expert, 0)


def _dispatch_kernel(cnt_ref, far_ref, near_ref, pos_ref, h_ref, xs_hbm, buf_ref, sem):
    pairs = pos_ref.shape[2]
    tt = pairs // TOP_K
    i = pl.program_id(0)
    buf = i % 2
    base = buf * pairs

    def place(t, carry):
        row = h_ref[_row_slice(t), :]
        for k in range(TOP_K):
            buf_ref[_row_slice(base + pos_ref[0, 0, k * tt + t]), :] = row
        return carry

    lax.fori_loop(0, tt, place, 0, unroll=ROW_LOOP_UNROLL)

    def copy(near, far, rows):
        return _row_copy(_rows_of(buf_ref, base + near, rows), _rows_of(xs_hbm, far, rows), sem.at[buf])

    _segment_copies(cnt_ref, far_ref, near_ref, i, copy)

    def drain(b):
        _row_copy(_rows_of(buf_ref, b * pairs, pairs), _rows_of(xs_hbm, 0, pairs), sem.at[b]).wait()

    pl.when(i > 0)(lambda: drain(1 - buf))
    pl.when(i == pl.num_programs(0) - 1)(lambda: drain(buf))


def _dispatch(seg, pos, h2):
    n = h2.shape[0] // LANE_TILES
    tt = TOKEN_TILE
    grid_spec = pltpu.PrefetchScalarGridSpec(
        num_scalar_prefetch=3,
        grid=(n // tt,),
        in_specs=[pl.BlockSpec((1, 1, TOP_K * tt), lambda i, *_: (i, 0, 0), memory_space=pltpu.SMEM),
                  pl.BlockSpec(_row_table_shape(tt), lambda i, *_: (i, 0))],
        out_specs=pl.BlockSpec(memory_space=pl.ANY),
        scratch_shapes=[pltpu.VMEM(_row_table_shape(2 * TOP_K * tt), F32),
                        pltpu.SemaphoreType.DMA((2,))],
    )
    return pl.pallas_call(
        _dispatch_kernel,
        grid_spec=grid_spec,
        out_shape=jax.ShapeDtypeStruct(_row_table_shape(n * TOP_K), F32),
        compiler_params=_cparams(("arbitrary",)),
    )(*seg, pos, h2)


def _swiglu_col(feature0, up):
    group, within = divmod(feature0, SWIGLU_GROUP)
    return (2 * group + up) * SWIGLU_GROUP + within


def _deinterleave_kernel(w_ref, o_ref):
    f2 = w_ref.shape[2]
    slab = 2 * LANES
    r = lax.broadcasted_iota(jnp.int32, (slab, slab), 0)
    c = lax.broadcasted_iota(jnp.int32, (slab, slab), 1)
    perm = (r == jnp.where(c < LANES, 2 * c, 2 * (c - LANES) + 1)).astype(BF16)
    for j in range(f2 // slab):
        y = _dot(w_ref[0, :, j * slab:(j + 1) * slab].astype(BF16), perm)
        for up in range(2):
            c0 = _swiglu_col(j * LANES, up)
            o_ref[0, :, c0:c0 + LANES] = y[:, up * LANES:(up + 1) * LANES].astype(BF16)


def _deinterleave(w):
    e, d, f2 = w.shape
    return pl.pallas_call(
        _deinterleave_kernel,
        grid=(e,),
        in_specs=[pl.BlockSpec((1, d, f2), lambda i: (i, 0, 0))],
        out_specs=pl.BlockSpec((1, d, f2), lambda i: (i, 0, 0)),
        out_shape=jax.ShapeDtypeStruct((e, d, f2), BF16),
        compiler_params=_cparams(("parallel",)),
    )(w)


def _expert_kernel(blk_ref, exp_ref, lo_ref, hi_ref, xs_ref, wgu_ref, bgu_ref, wd_ref, bd_ref, ys_ref):
    del blk_ref, exp_ref
    rows = xs_ref.shape[0] // LANE_TILES
    f = wd_ref.shape[1]
    i = pl.program_id(0)
    lo = lo_ref[i]
    hi = hi_ref[i]

    @pl.when(hi > lo)
    def _():
        g = SWIGLU_GROUP
        x = _load_rows(xs_ref, 0, rows).astype(BF16)
        acts = []
        for j in range(f // g):
            gu = _dot(x, wgu_ref[0, :, 2 * j * g:2 * (j + 1) * g]) + bgu_ref[0, :, 2 * j * g:2 * (j + 1) * g]
            gate = jnp.minimum(gu[:, :g], SWIGLU_LIMIT)
            up = jnp.clip(gu[:, g:], -SWIGLU_LIMIT, SWIGLU_LIMIT)
            acts.append(((up + 1.0) * gate * jax.nn.sigmoid(SWIGLU_ALPHA * gate)).astype(BF16))
        y = _dot(jnp.concatenate(acts, axis=1), wd_ref[0]) + bd_ref[0]
        r = lax.broadcasted_iota(jnp.int32, y.shape, 0)
        mine = (r >= lo) & (r < hi)

        @pl.when(lo == 0)
        def _():
            _store_rows(ys_ref, jnp.where(mine, y, 0.0))

        @pl.when(lo > 0)
        def _():
            _store_rows(ys_ref, jnp.where(mine, y, _load_rows(ys_ref, 0, rows)))


def _experts(items, xs2, w_gu, b_gu, w_down, b_down):
    item_blk, item_exp, item_lo, item_hi = items
    d, f2 = w_gu.shape[1], w_gu.shape[2]
    wmap = lambda i, blk, exp, lo, hi: (exp[i], 0, 0)
    xmap = lambda i, blk, exp, lo, hi: (blk[i], 0)
    grid_spec = pltpu.PrefetchScalarGridSpec(
        num_scalar_prefetch=4,
        grid=(item_blk.shape[0],),
        in_specs=[pl.BlockSpec(_row_table_shape(MOE_ROWS), xmap),
                  pl.BlockSpec((1, d, f2), wmap),
                  pl.BlockSpec((1, 1, f2), wmap),
                  pl.BlockSpec((1, f2 // 2, d), wmap),
                  pl.BlockSpec((1, 1, d), wmap)],
        out_specs=pl.BlockSpec(_row_table_shape(MOE_ROWS), xmap),
    )
    return pl.pallas_call(
        _expert_kernel,
        grid_spec=grid_spec,
        out_shape=jax.ShapeDtypeStruct(xs2.shape, F32),
        compiler_params=_cparams(("arbitrary",)),
    )(item_blk, item_exp, item_lo, item_hi, xs2, w_gu, b_gu, w_down, b_down)


def _expert_work_items(counts, n_rows):
    ends = jnp.cumsum(counts)
    starts = ends - counts
    n_blk = n_rows // MOE_ROWS
    cuts = jnp.sort(jnp.concatenate([jnp.arange(n_blk, dtype=jnp.int32) * MOE_ROWS, starts[1:]]))
    nxt = jnp.concatenate([cuts[1:], jnp.full((1,), n_rows, jnp.int32)])
    blk = jnp.minimum(cuts // MOE_ROWS, n_blk - 1)
    exp = jnp.minimum(jnp.sum(ends[None, :] <= cuts[:, None], axis=1), N_EXPERTS - 1).astype(jnp.int32)
    lo = cuts - blk * MOE_ROWS
    hi = nxt - blk * MOE_ROWS
    return (blk.astype(jnp.int32), exp, lo.astype(jnp.int32), hi.astype(jnp.int32)), starts


def _combine_kernel(cnt_ref, far_ref, near_ref, pos_ref, gates_ref, x1_ref, mod_ref, fg_ref, ys_hbm,
                    op_ref, os_ref, buf_ref, y_ref, sem, *, n_prompt_tiles):
    tt = x1_ref.shape[0]
    pairs = TOP_K * tt
    i = pl.program_id(0)
    buf = i % 2
    base = buf * pairs

    def fetch(tile, b):
        def copy(near, far, rows):
            return _row_copy(_rows_of(ys_hbm, far, rows), _rows_of(buf_ref, b * pairs + near, rows), sem.at[b])

        _segment_copies(cnt_ref, far_ref, near_ref, tile, copy)

    pl.when(i == 0)(lambda: fetch(0, 0))
    pl.when(i + 1 < pl.num_programs(0))(lambda: fetch(i + 1, 1 - buf))
    _row_copy(_rows_of(ys_hbm, 0, pairs), _rows_of(buf_ref, base, pairs), sem.at[buf]).wait()

    def weighted_sum(t, carry):
        acc = None
        for k in range(TOP_K):
            term = gates_ref[0, 0, k * tt + t] * buf_ref[_row_slice(base + pos_ref[0, 0, k * tt + t]), :]
            acc = term if acc is None else acc + term
        y_ref[_row_slice(t), :] = acc
        return carry

    lax.fori_loop(0, tt, weighted_sum, 0, unroll=ROW_LOOP_UNROLL)

    x2 = x1_ref[...] + mod_ref[0, 5:6, :] * _load_rows(y_ref, 0, tt)
    ms = jnp.mean(x2 * x2, axis=-1, keepdims=True)
    out = x2 * lax.rsqrt(ms + EPS) * fg_ref[...]

    def write(stream):
        (op_ref, os_ref)[stream][...] = out

    _per_stream(n_prompt_tiles, write)


def _combine(seg, pos, gates, x1, mod3, final_g, ys, batch_of_tile, n_prompt):
    n, d = x1.shape
    tt = TOKEN_TILE
    n_prompt_tiles = n_prompt // tt
    grid_spec = pltpu.PrefetchScalarGridSpec(
        num_scalar_prefetch=3,
        grid=(n // tt,),
        in_specs=[pl.BlockSpec((1, 1, TOP_K * tt), lambda i, *_: (i, 0, 0), memory_space=pltpu.SMEM),
                  pl.BlockSpec((1, 1, TOP_K * tt), lambda i, *_: (i, 0, 0), memory_space=pltpu.SMEM),
                  pl.BlockSpec((tt, d), lambda i, *_: (i, 0)),
                  pl.BlockSpec((1, 6, d), lambda i, *_: (batch_of_tile(i), 0, 0)),
                  pl.BlockSpec((1, d), lambda i, *_: (0, 0)),
                  pl.BlockSpec(memory_space=pl.ANY)],
        out_specs=list(_stream_specs((tt, d), n_prompt_tiles)),
        scratch_shapes=[pltpu.VMEM(_row_table_shape(2 * TOP_K * tt), F32),
                        pltpu.VMEM(_row_table_shape(tt), F32),
                        pltpu.SemaphoreType.DMA((2,))],
    )
    return pl.pallas_call(
        functools.partial(_combine_kernel, n_prompt_tiles=n_prompt_tiles),
        grid_spec=grid_spec,
        out_shape=[jax.ShapeDtypeStruct((n_prompt, d), F32),
                   jax.ShapeDtypeStruct((n - n_prompt, d), F32)],
        compiler_params=_cparams(("arbitrary",)),
    )(*seg, pos, gates, x1, mod3, final_g.reshape(1, d), ys)


def _batch_of_tile_fn(tile, bp, tp, ts):
    n_prompt = bp * tp

    def batch_of_tile(i):
        t = i * tile
        return jnp.where(t < n_prompt, t // tp, bp + (t - n_prompt) // ts)

    return batch_of_tile


def kernel(x_prompt, x_sample, c_prompt, c_sample, norm1_g, w_ada, b_ada, w_in, gla_w2_fwd, gla_b2_fwd, gla_w2_bwd, gla_b2_bwd, gla_norm_g, nat_rpb, w_out, norm2_g, router_w, router_b, w_gate_up, b_gate_up, w_down, b_down, final_norm_g):
    assert w_ada.shape[0] == 1, "single-layer encoder"
    bp, tp, d = x_prompt.shape
    bs, ts, _ = x_sample.shape
    n_prompt, n_sample = bp * tp, bs * ts
    n = n_prompt + n_sample
    assert tp % TOKEN_TILE == 0 and ts % TOKEN_TILE == 0 and n_prompt % ts == 0
    assert d == LANE_TILES * LANES

    x = (x_prompt.reshape(n_prompt, d), x_sample.reshape(n_sample, d))
    c = jnp.concatenate([c_prompt, c_sample], axis=0)

    sizes = (GLA_QK, GLA_QK, GLA_V, GLA_V, GLA_RANK, GLA_RANK, NAT_W, NAT_W, NAT_W)
    offs = np.concatenate([[0], np.cumsum(sizes)])
    w_in0 = w_in[0]
    seg = lambda j: w_in0[:, offs[j]:offs[j + 1]]
    w_all = jnp.concatenate(
        [seg(0), seg(1), seg(2), seg(3), seg(6) * (NAT_HD ** -0.5 * LOG2E), seg(7), seg(8), seg(4), seg(5),
         jnp.zeros((d, LANES - 2 * GLA_RANK), F32)], axis=1).astype(BF16)
    pairs = GLA_HEADS // 2
    w2 = jnp.zeros((LANES, pairs, 2, 2 * GLA_DK), F32)
    w2 = w2.at[0:GLA_RANK, :, 0].set(gla_w2_fwd[0].reshape(GLA_RANK, pairs, 2 * GLA_DK))
    w2 = w2.at[GLA_RANK:2 * GLA_RANK, :, 1].set(gla_w2_bwd[0].reshape(GLA_RANK, pairs, 2 * GLA_DK))
    wz = w2.reshape(LANES, pairs * 4 * GLA_DK).astype(BF16)
    bz = jnp.stack([gla_b2_fwd[0].reshape(pairs, 2 * GLA_DK), gla_b2_bwd[0].reshape(pairs, 2 * GLA_DK)],
                   axis=1).reshape(1, pairs * 4 * GLA_DK)
    tab = _nat_bias_table(nat_rpb[0])
    w_og = w_out[0, :GLA_V].astype(BF16)
    w_on = w_out[0, GLA_V:].astype(BF16)
    rw_t = router_w[0].T
    w_gu = _deinterleave(w_gate_up[0])
    n_e, f2 = b_gate_up.shape[1:]
    b_gu = (b_gate_up[0].reshape(n_e, f2 // (2 * SWIGLU_GROUP), SWIGLU_GROUP, 2)
            .transpose(0, 1, 3, 2).reshape(n_e, 1, f2))
    w_dn = w_down[0].astype(BF16)
    b_dn = b_down[0, :, None, :]

    mod3 = _ada(c, w_ada[0], b_ada[0]).reshape(bp + bs, 6, d)
    bot_tok = _batch_of_tile_fn(TOKEN_TILE, bp, tp, ts)

    pg, pn, plr = _in_proj(*x, mod3, norm1_g[0], w_all, bot_tok)

    gla_args = (pg, plr, wz, bz, gla_norm_g[0].reshape(1, -1))
    og = (_gla(*gla_args, seq=tp, batches=bp, block0=0),
          _gla(*gla_args, seq=ts, batches=bs, block0=n_prompt // ts))
    on = (_nat(pn, tab, seq=tp, batches=bp, block0=0),
          _nat(pn, tab, seq=ts, batches=bs, block0=n_prompt // ts))

    x1, h2, top_i, gates, rank, cnt, tile_cnt, tile_carry = _out_proj(
        og, on, x, mod3, w_og, w_on, norm2_g[0], rw_t, router_b[0], bot_tok)

    n_rows = n * TOP_K
    n_tiles = n // TOKEN_TILE
    assert n_rows % MOE_ROWS == 0
    counts = cnt[:, 0].astype(jnp.int32)
    items, starts = _expert_work_items(counts, n_rows)
    seg_cnt = tile_cnt[:, :, 0].astype(jnp.int32)
    seg_far = starts[None, :] + tile_carry[:, :, 0].astype(jnp.int32)
    seg_near = jnp.cumsum(seg_cnt, axis=1) - seg_cnt
    seg = tuple(a.reshape(n_tiles * N_EXPERTS) for a in (seg_cnt, seg_far, seg_near))
    chosen = top_i[None] == jnp.arange(N_EXPERTS, dtype=jnp.int32)[:, None, None, None]
    pos = rank + jnp.sum(jnp.where(chosen, seg_near.T[:, :, None, None], 0), axis=0)
    per_tile = lambda a: a.reshape(n_tiles, 1, TOP_K * TOKEN_TILE)
    pos, gates = per_tile(pos), per_tile(gates)

    xs = _dispatch(seg, pos, h2)
    ys = _experts(items, xs, w_gu, b_gu, w_dn, b_dn)
    y_prompt, y_sample = _combine(seg, pos, gates, x1, mod3, final_norm_g, ys, bot_tok, n_prompt)

    return (y_prompt.reshape(bp, tp, d), y_sample.reshape(bs, ts, d))
```

```python
import functools

import numpy as np
import jax
import jax.numpy as jnp
from jax import lax
from jax.experimental import pallas as pl
from jax.experimental.pallas import tpu as pltpu

F32 = jnp.float32
BF16 = jnp.bfloat16
HIGHEST = lax.Precision.HIGHEST

EPS = 1e-5
GRID_W = 64
GLA_HEADS = 4
GLA_DK = 64
GLA_DV = 128
GLA_RANK = 16
GLA_TAU = 16.0
GLA_CHUNK = 64
GLA_QK = GLA_HEADS * GLA_DK
GLA_V = GLA_HEADS * GLA_DV
NAT_HEADS = 16
NAT_HD = 32
NAT_W = NAT_HEADS * NAT_HD
NAT_KH = 8
NAT_KW = 16
NAT_GROUP = 4
N_EXPERTS = 32
TOP_K = 4
SWIGLU_LIMIT = 7.0
SWIGLU_ALPHA = 1.702

LANES = 128
SUBLANES = 8
TOKEN_TILE = 512
MOE_ROWS = 512
EXPERT_PARTS = 4
SWIGLU_GROUP = 1024
GLA_UNROLL = 16
NAT_UNROLL = 16
ROW_LOOP_UNROLL = 8
SEGMENT_CHUNK = 32
LOG2E = 1.4426950408889634
VMEM_LIMIT = 56 * 1024 * 1024


def _cparams(sem, vmem=VMEM_LIMIT):
    return pltpu.CompilerParams(dimension_semantics=sem, vmem_limit_bytes=vmem)


def _dot(a, b):
    return jnp.dot(a, b, preferred_element_type=F32)


def _dot_nt(a, b, precision=None):
    return lax.dot_general(a, b, (((1,), (1,)), ((), ())), preferred_element_type=F32, precision=precision)


LANE_TILES = 8


def _row_table_shape(rows):
    return (rows * LANE_TILES, LANES)


def _store_rows(ref, x, row0=0):
    rows = x.shape[0]
    for s in range(LANE_TILES):
        ref[pl.ds(row0 * LANE_TILES + s, rows, stride=LANE_TILES), :] = x[:, s * LANES:(s + 1) * LANES]


def _load_rows(ref, row0, rows):
    return jnp.concatenate(
        [ref[pl.ds(row0 * LANE_TILES + s, rows, stride=LANE_TILES), :] for s in range(LANE_TILES)], axis=1)


def _row_slice(row0, rows=1):
    return pl.ds(pl.multiple_of(row0 * LANE_TILES, LANE_TILES), rows * LANE_TILES)


def _rows_of(ref, row0, rows):
    return ref.at[_row_slice(row0, rows), :]


def _dot_tn(a, b):
    return lax.dot_general(a, b, (((0,), (0,)), ((), ())), preferred_element_type=F32)


def _ada_kernel(c_ref, w_ref, b_ref, o_ref):
    c = c_ref[...]
    s = c * jax.nn.sigmoid(c)
    o_ref[...] = jnp.dot(s, w_ref[...], preferred_element_type=F32, precision=HIGHEST) + b_ref[...]


def _ada(c, w, b):
    nb, d = c.shape
    cols = w.shape[1]
    blk = 1024
    return pl.pallas_call(
        _ada_kernel,
        grid=(cols // blk,),
        in_specs=[pl.BlockSpec((nb, d), lambda j: (0, 0)),
                  pl.BlockSpec((d, blk), lambda j: (0, j)),
                  pl.BlockSpec((1, blk), lambda j: (0, j))],
        out_specs=pl.BlockSpec((nb, blk), lambda j: (0, j)),
        out_shape=jax.ShapeDtypeStruct((nb, cols), F32),
        compiler_params=_cparams(("arbitrary",)),
    )(c, w, b.reshape(1, cols))


def _stream_specs(block, n_prompt_tiles):
    return (pl.BlockSpec(block, lambda i, *_: (jnp.minimum(i, n_prompt_tiles - 1), 0)),
            pl.BlockSpec(block, lambda i, *_: (jnp.maximum(i - n_prompt_tiles, 0), 0)))


def _per_stream(n_prompt_tiles, body):
    i = pl.program_id(0)
    pl.when(i < n_prompt_tiles)(functools.partial(body, 0))
    pl.when(i >= n_prompt_tiles)(functools.partial(body, 1))


def _in_kernel(xp_ref, xs_ref, mod_ref, g_ref, w_ref, pg_ref, pn_ref, plr_ref, *, n_prompt_tiles):
    def body(stream):
        x = (xp_ref, xs_ref)[stream][...]
        ms = jnp.mean(x * x, axis=-1, keepdims=True)
        y = x * lax.rsqrt(ms + EPS) * g_ref[...]
        h = y * (1.0 + mod_ref[0, 1:2, :]) + mod_ref[0, 0:1, :]
        hb = h.astype(BF16)
        wg = GLA_QK * 2 + GLA_V * 2
        wn = 3 * NAT_W
        pg_ref[...] = _dot(hb, w_ref[:, 0:wg]).astype(BF16)
        pn_ref[...] = _dot(hb, w_ref[:, wg:wg + wn]).astype(BF16)
        plr_ref[...] = _dot(hb, w_ref[:, wg + wn:wg + wn + LANES])

    _per_stream(n_prompt_tiles, body)


def _in_proj(xp, xs, mod3, norm_g, w_all, batch_of_tile):
    d = xp.shape[1]
    n = xp.shape[0] + xs.shape[0]
    wg = GLA_QK * 2 + GLA_V * 2
    wn = 3 * NAT_W
    tm = TOKEN_TILE
    n_prompt_tiles = xp.shape[0] // tm
    return pl.pallas_call(
        functools.partial(_in_kernel, n_prompt_tiles=n_prompt_tiles),
        grid=(n // tm,),
        in_specs=[*_stream_specs((tm, d), n_prompt_tiles),
                  pl.BlockSpec((1, 6, d), lambda i: (batch_of_tile(i), 0, 0)),
                  pl.BlockSpec((1, d), lambda i: (0, 0)),
                  pl.BlockSpec(w_all.shape, lambda i: (0, 0))],
        out_specs=[pl.BlockSpec((tm, wg), lambda i: (i, 0)),
                   pl.BlockSpec((tm, wn), lambda i: (i, 0)),
                   pl.BlockSpec((tm, LANES), lambda i: (i, 0))],
        out_shape=[jax.ShapeDtypeStruct((n, wg), BF16),
                   jax.ShapeDtypeStruct((n, wn), BF16),
                   jax.ShapeDtypeStruct((n, LANES), F32)],
        compiler_params=_cparams(("arbitrary",)),
    )(xp, xs, mod3, norm_g.reshape(1, d), w_all)


def _gla_kernel(q_ref, k_ref, v_ref, g_ref, lr_ref, wz_ref, bz_ref, ng_ref,
                o_ref, la_ref, acc_ref, qt_ref, ke_ref, dec_ref, sf_ref, sb_ref, *, seq):
    c_len = GLA_CHUNK
    n = seq // c_len
    row = lax.broadcasted_iota(jnp.int32, (c_len, c_len), 0)
    col = lax.broadcasted_iota(jnp.int32, (c_len, c_len), 1)
    tril = col <= row
    triu = col >= row
    cum_row = lax.broadcasted_iota(jnp.int32, (c_len, 2 * c_len), 0)
    cum_col = lax.broadcasted_iota(jnp.int32, (c_len, 2 * c_len), 1) % c_len
    cum = (cum_col <= cum_row).astype(F32).astype(BF16)
    mask_f = jnp.concatenate([tril, tril], axis=0)
    mask_b = jnp.concatenate([triu, triu], axis=0)
    head_a = lax.broadcasted_iota(jnp.int32, (c_len, LANES), 1) < GLA_DK
    st_row = lax.broadcasted_iota(jnp.int32, (2 * GLA_DV, LANES), 0)
    st_col = lax.broadcasted_iota(jnp.int32, (2 * GLA_DV, LANES), 1)
    blockdiag = (st_row < GLA_DV) == (st_col < GLA_DK)

    z = _dot(lr_ref[...].astype(BF16), wz_ref[...]) + bz_ref[...]
    la_ref[...] = (jnp.minimum(z, 0.0) - jnp.log(1.0 + jnp.exp(-jnp.abs(z)))) * (1.0 / GLA_TAU)

    def chunk_rows(c):
        return pl.ds(pl.multiple_of(c * c_len, c_len), c_len)

    def stack_heads(x):
        zero = jnp.zeros_like(x)
        return jnp.concatenate([jnp.where(head_a, x, zero), jnp.where(head_a, zero, x)], axis=0).astype(BF16)

    def local(c, carry):
        rows = chunk_rows(c)
        la = la_ref[rows, :]
        hi = la.astype(BF16)
        lo = (la - hi.astype(F32)).astype(BF16)
        binc = _dot(cum, jnp.concatenate([hi, lo], axis=0))
        btot = jnp.sum(la, axis=0, keepdims=True)
        b_f = binc[:, :LANES]
        b_b = btot[:, LANES:] - binc[:, LANES:] + la[:, LANES:]
        e_tot = jnp.exp(btot)
        q = q_ref[rows, :].astype(F32) * (GLA_DK ** -0.5)
        k = k_ref[rows, :].astype(F32)
        qt_f = q * jnp.exp(b_f)
        qt_b = q * jnp.exp(b_b)
        kt_f = k * jnp.exp(-b_f)
        kt_b = k * jnp.exp(-b_b)
        a = (jnp.where(mask_f, _dot_nt(stack_heads(qt_f), kt_f.astype(BF16)), 0.0)
             + jnp.where(mask_b, _dot_nt(stack_heads(qt_b), kt_b.astype(BF16)), 0.0))
        oi = _dot(a.astype(BF16), v_ref[rows, :])
        acc_ref[rows, :] = jnp.concatenate([oi[0:c_len, 0:GLA_DV], oi[c_len:, GLA_DV:]], axis=1)
        qt_ref[rows, :] = jnp.concatenate([qt_f, qt_b], axis=1).astype(BF16)
        ke_ref[rows, :] = jnp.concatenate([kt_f * e_tot[:, :LANES], kt_b * e_tot[:, LANES:]], axis=1).astype(BF16)
        dec_ref[pl.ds(pl.multiple_of(c * SUBLANES, SUBLANES), SUBLANES), :] = jnp.broadcast_to(e_tot, (SUBLANES, 2 * LANES))
        return carry

    def carried(c, s_ref, half):
        rows = chunk_rows(c)
        lanes = slice(half * LANES, (half + 1) * LANES)
        st = s_ref[...]
        acc_ref[rows, :] += _dot_nt(qt_ref[rows, lanes], st.astype(BF16))
        upd = _dot_tn(v_ref[rows, :], ke_ref[rows, lanes])
        dec = dec_ref[pl.ds(pl.multiple_of(c * SUBLANES, SUBLANES), 1), lanes]
        s_ref[...] = st * dec + jnp.where(blockdiag, upd, 0.0)

    def scan(c, carry):
        carried(c, sf_ref, 0)
        carried(n - 1 - c, sb_ref, 1)
        return carry

    def finalize(c, carry):
        rows = chunk_rows(c)
        o = acc_ref[rows, :]
        halves = []
        for hh in range(2):
            oh = o[:, hh * GLA_DV:(hh + 1) * GLA_DV]
            ms = jnp.mean(oh * oh, axis=-1, keepdims=True)
            halves.append(oh * lax.rsqrt(ms + EPS) * ng_ref[...])
        y = jnp.concatenate(halves, axis=1)
        g = g_ref[rows, :].astype(F32)
        o_ref[rows, :] = (y * (g * jax.nn.sigmoid(g))).astype(o_ref.dtype)
        return carry

    sf_ref[...] = jnp.zeros_like(sf_ref)
    sb_ref[...] = jnp.zeros_like(sb_ref)
    lax.fori_loop(0, n, local, 0, unroll=GLA_UNROLL)
    lax.fori_loop(0, n, scan, 0, unroll=GLA_UNROLL)
    lax.fori_loop(0, n, finalize, 0, unroll=GLA_UNROLL)


def _gla(pg, plr, wz, bz, norm_g, *, seq, batches, block0):
    pair_w = 2 * GLA_DK
    pair_v = 2 * GLA_DV
    k0 = GLA_QK // pair_w
    v0 = 2 * GLA_QK // pair_v
    g0 = (2 * GLA_QK + GLA_V) // pair_v
    n_chunks = seq // GLA_CHUNK
    in_specs = [pl.BlockSpec((seq, pair_w), lambda b, p: (block0 + b, p)),
                pl.BlockSpec((seq, pair_w), lambda b, p: (block0 + b, k0 + p)),
                pl.BlockSpec((seq, pair_v), lambda b, p: (block0 + b, v0 + p)),
                pl.BlockSpec((seq, pair_v), lambda b, p: (block0 + b, g0 + p)),
                pl.BlockSpec((seq, LANES), lambda b, p: (block0 + b, 0)),
                pl.BlockSpec((LANES, 2 * pair_w), lambda b, p: (0, p)),
                pl.BlockSpec((1, 2 * pair_w), lambda b, p: (0, p)),
                pl.BlockSpec((1, GLA_DV), lambda b, p: (0, 0))]
    return pl.pallas_call(
        functools.partial(_gla_kernel, seq=seq),
        grid=(batches, GLA_HEADS // 2),
        in_specs=in_specs,
        out_specs=pl.BlockSpec((seq, pair_v), lambda b, p: (b, p)),
        out_shape=jax.ShapeDtypeStruct((batches * seq, GLA_V), BF16),
        scratch_shapes=[pltpu.VMEM((seq, 2 * pair_w), F32),
                        pltpu.VMEM((seq, pair_v), F32),
                        pltpu.VMEM((seq, 2 * pair_w), BF16),
                        pltpu.VMEM((seq, 2 * pair_w), BF16),
                        pltpu.VMEM((n_chunks * SUBLANES, 2 * pair_w), F32),
                        pltpu.VMEM((pair_v, pair_w), F32),
                        pltpu.VMEM((pair_v, pair_w), F32)],
        compiler_params=_cparams(("parallel", "parallel")),
    )(pg, pg, pg, pg, plr, wz, bz, norm_g)


def _nat_kernel(q_ref, k_ref, v_ref, tab_ref, o_ref, s_ref, *, rows):
    w = GRID_W
    head = lax.broadcasted_iota(jnp.int32, (w, LANES), 1) // NAT_HD
    win = NAT_KH * w

    def window(i):
        r0 = jnp.clip(i - NAT_KH // 2, 0, rows - NAT_KH)
        return r0 - i + NAT_KH - 1, pl.ds(pl.multiple_of(r0 * w, w), win)

    def scores(i, slot):
        base, krows = window(i)
        q = q_ref[pl.ds(pl.multiple_of(i * w, w), w), :]
        zero = jnp.zeros_like(q)
        q_stack = jnp.concatenate([jnp.where(head == h, q, zero) for h in range(NAT_GROUP)], axis=0)
        s_ref[slot] = _dot_nt(q_stack, k_ref[krows, :]) + tab_ref[0, base]

    def attend(i, slot):
        _, krows = window(i)
        s = s_ref[slot]
        m = jnp.max(s, axis=-1, keepdims=True)
        e = jnp.exp2(s - m)
        l = jnp.sum(e, axis=-1, keepdims=True)
        o = _dot(e.astype(BF16), v_ref[krows, :]) / l
        out = jnp.zeros((w, LANES), F32)
        for h in range(NAT_GROUP):
            out = out + jnp.where(head == h, o[h * w:(h + 1) * w, :], 0.0)
        o_ref[pl.ds(pl.multiple_of(i * w, w), w), :] = out.astype(o_ref.dtype)

    scores(0, 0)

    def body(j, carry):
        for u in range(2):
            i = 2 * j + u
            scores(jnp.minimum(i + 1, rows - 1), 1 - u)
            attend(i, u)
        return carry

    lax.fori_loop(0, rows // 2, body, 0, unroll=NAT_UNROLL // 2)


def _nat(pn, tab, *, seq, batches, block0):
    rows = seq // GRID_W
    groups = NAT_HEADS // NAT_GROUP
    in_specs = [pl.BlockSpec((seq, LANES), lambda g, b: (block0 + b, g)),
                pl.BlockSpec((seq, LANES), lambda g, b: (block0 + b, groups + g)),
                pl.BlockSpec((seq, LANES), lambda g, b: (block0 + b, 2 * groups + g)),
                pl.BlockSpec((1,) + tab.shape[1:], lambda g, b: (g, 0, 0, 0))]
    return pl.pallas_call(
        functools.partial(_nat_kernel, rows=rows),
        grid=(groups, batches),
        in_specs=in_specs,
        out_specs=pl.BlockSpec((seq, LANES), lambda g, b: (b, g)),
        out_shape=jax.ShapeDtypeStruct((batches * seq, NAT_W), BF16),
        scratch_shapes=[pltpu.VMEM((2, NAT_GROUP * GRID_W, NAT_KH * GRID_W), F32)],
        compiler_params=_cparams(("parallel", "parallel")),
    )(pn, pn, pn, tab)


def _nat_bias_table(rpb):
    w = GRID_W
    jq = np.arange(w)[:, None]
    jk = np.arange(w)[None, :]
    c0 = np.clip(jq - NAT_KW // 2, 0, w - NAT_KW)
    valid = (jk >= c0) & (jk < c0 + NAT_KW)
    dcol = np.clip(jk - jq + NAT_KW - 1, 0, 2 * NAT_KW - 2)
    select = (dcol[None] == np.arange(2 * NAT_KW - 1)[:, None, None]).astype(np.float32)
    t = jnp.einsum('hrc,cqk->hrqk', rpb.astype(F32) * LOG2E, select, precision=HIGHEST)
    t = jnp.where(valid[None, None], t, -jnp.inf)
    t = jnp.stack([t[:, b:b + NAT_KH] for b in range(NAT_KH)], axis=1)
    t = t.transpose(0, 1, 3, 2, 4).reshape(NAT_HEADS, NAT_KH, w, NAT_KH * w)
    t = t.reshape(NAT_HEADS // NAT_GROUP, NAT_GROUP, NAT_KH, w, NAT_KH * w).transpose(0, 2, 1, 3, 4)
    return t.reshape(NAT_HEADS // NAT_GROUP, NAT_KH, NAT_GROUP * w, NAT_KH * w)


def _out_kernel(ogp_ref, ogs_ref, onp_ref, ons_ref, xp_ref, xs_ref, mod_ref, wg_ref, wn_ref, n2_ref, rw_ref, rb_ref,
                x1_ref, h2_ref, ti_ref, gt_ref, rk_ref, cnt_ref, tcnt_ref, tcar_ref, carry_ref, *, n_prompt_tiles):
    tm = x1_ref.shape[0]

    @pl.when(pl.program_id(0) == 0)
    def _():
        carry_ref[...] = jnp.zeros_like(carry_ref)

    def residual(stream):
        og_ref, on_ref, x_ref = ((ogp_ref, onp_ref, xp_ref), (ogs_ref, ons_ref, xs_ref))[stream]
        mix = _dot(og_ref[...], wg_ref[...]) + _dot(on_ref[...], wn_ref[...])
        x1_ref[...] = x_ref[...] + mod_ref[0, 2:3, :] * mix

    _per_stream(n_prompt_tiles, residual)
    x1 = x1_ref[...]
    ms = jnp.mean(x1 * x1, axis=-1, keepdims=True)
    h2 = x1 * lax.rsqrt(ms + EPS) * n2_ref[...]
    h2 = h2 * (1.0 + mod_ref[0, 4:5, :]) + mod_ref[0, 3:4, :]
    _store_rows(h2_ref, h2)

    logits = _dot_nt(rw_ref[...], h2, precision=HIGHEST) + rb_ref[...]
    eidx = lax.broadcasted_iota(jnp.int32, logits.shape, 0)
    vals = logits
    top_v, top_i = [], []
    for _ in range(TOP_K):
        m = jnp.max(vals, axis=0, keepdims=True)
        idx = jnp.min(jnp.where(vals == m, eidx, N_EXPERTS), axis=0, keepdims=True)
        top_v.append(m)
        top_i.append(idx)
        vals = jnp.where(eidx == idx, -jnp.inf, vals)
    ev = [jnp.exp(v - top_v[0]) for v in top_v]
    den = ev[0] + ev[1] + ev[2] + ev[3]
    ti_ref[0] = jnp.concatenate(top_i, axis=0)
    gt_ref[0] = jnp.concatenate([e / den for e in ev], axis=0)

    onehot = jnp.zeros(logits.shape, F32)
    for idx in top_i:
        onehot = onehot + (eidx == idx).astype(F32)
    r = lax.broadcasted_iota(jnp.int32, (tm, tm), 0)
    c = lax.broadcasted_iota(jnp.int32, (tm, tm), 1)
    earlier = (r < c).astype(BF16)
    before = _dot(onehot.astype(BF16), earlier)
    rk_ref[0] = jnp.concatenate(
        [jnp.sum(jnp.where(eidx == idx, before, 0.0), axis=0, keepdims=True) for idx in top_i], axis=0).astype(jnp.int32)
    tile_counts = jnp.sum(onehot, axis=1, keepdims=True)
    tcar_ref[0] = jnp.broadcast_to(carry_ref[...], tcar_ref.shape[1:])
    tcnt_ref[0] = jnp.broadcast_to(tile_counts, tcnt_ref.shape[1:])
    carry_ref[...] = carry_ref[...] + tile_counts
    cnt_ref[...] = jnp.broadcast_to(carry_ref[...], cnt_ref.shape)


def _out_proj(og, on, x, mod3, w_g, w_n, norm_g, rw_t, rb, batch_of_tile):
    d = x[0].shape[1]
    n = x[0].shape[0] + x[1].shape[0]
    tm = TOKEN_TILE
    n_prompt_tiles = x[0].shape[0] // tm
    tok = lambda i: (i, 0)
    const = lambda i: (0, 0)
    return pl.pallas_call(
        functools.partial(_out_kernel, n_prompt_tiles=n_prompt_tiles),
        grid=(n // tm,),
        in_specs=[*_stream_specs((tm, GLA_V), n_prompt_tiles),
                  *_stream_specs((tm, NAT_W), n_prompt_tiles),
                  *_stream_specs((tm, d), n_prompt_tiles),
                  pl.BlockSpec((1, 6, d), lambda i: (batch_of_tile(i), 0, 0)),
                  pl.BlockSpec(w_g.shape, const),
                  pl.BlockSpec(w_n.shape, const),
                  pl.BlockSpec((1, d), const),
                  pl.BlockSpec(rw_t.shape, const),
                  pl.BlockSpec((N_EXPERTS, 1), const)],
        out_specs=[pl.BlockSpec((tm, d), tok),
                   pl.BlockSpec(_row_table_shape(tm), tok),
                   pl.BlockSpec((1, TOP_K, tm), lambda i: (i, 0, 0)),
                   pl.BlockSpec((1, TOP_K, tm), lambda i: (i, 0, 0)),
                   pl.BlockSpec((1, TOP_K, tm), lambda i: (i, 0, 0)),
                   pl.BlockSpec((N_EXPERTS, LANES), const),
                   pl.BlockSpec((1, N_EXPERTS, LANES), lambda i: (i, 0, 0)),
                   pl.BlockSpec((1, N_EXPERTS, LANES), lambda i: (i, 0, 0))],
        out_shape=[jax.ShapeDtypeStruct((n, d), F32),
                   jax.ShapeDtypeStruct(_row_table_shape(n), F32),
                   jax.ShapeDtypeStruct((n // tm, TOP_K, tm), jnp.int32),
                   jax.ShapeDtypeStruct((n // tm, TOP_K, tm), F32),
                   jax.ShapeDtypeStruct((n // tm, TOP_K, tm), jnp.int32),
                   jax.ShapeDtypeStruct((N_EXPERTS, LANES), F32),
                   jax.ShapeDtypeStruct((n // tm, N_EXPERTS, LANES), F32),
                   jax.ShapeDtypeStruct((n // tm, N_EXPERTS, LANES), F32)],
        scratch_shapes=[pltpu.VMEM((N_EXPERTS, 1), F32)],
        compiler_params=_cparams(("arbitrary",)),
    )(*og, *on, *x, mod3, w_g, w_n, norm_g.reshape(1, d), rw_t, rb.reshape(N_EXPERTS, 1))


def _row_copy(src, dst, sem):
    return pltpu.make_async_copy(src, dst, sem)


def _segment_copies(cnt_ref, far_ref, near_ref, tile, copy):
    def per_expert(e, carry):
        idx = tile * N_EXPERTS + e
        c, far, near = cnt_ref[idx], far_ref[idx], near_ref[idx]

        def whole(i, carry2):
            copy(near + i * SEGMENT_CHUNK, far + i * SEGMENT_CHUNK, SEGMENT_CHUNK).start()
            return carry2

        lax.fori_loop(0, c // SEGMENT_CHUNK, whole, 0)
        for bit in reversed(range(SEGMENT_CHUNK.bit_length() - 1)):
            size = 1 << bit
            done = c & -(2 * size)

            @pl.when((c & size) != 0)
            def _(done=done, size=size):
                copy(near + done, far + done, size).start()

        return carry

    lax.fori_loop(0, N_EXPERTS, per_expert, 0)


def _dispatch_kernel(cnt_ref, far_ref, near_ref, pos_ref, h_ref, xs_hbm, buf_ref, sem):
    pairs = pos_ref.shape[2]
    tt = pairs // TOP_K
    i = pl.program_id(0)
    buf = i % 2
    base = buf * pairs

    def place(t, carry):
        row = h_ref[_row_slice(t), :]
        for k in range(TOP_K):
            buf_ref[_row_slice(base + pos_ref[0, 0, k * tt + t]), :] = row
        return carry

    lax.fori_loop(0, tt, place, 0, unroll=ROW_LOOP_UNROLL)

    def copy(near, far, rows):
        return _row_copy(_rows_of(buf_ref, base + near, rows), _rows_of(xs_hbm, far, rows), sem.at[buf])

    _segment_copies(cnt_ref, far_ref, near_ref, i, copy)

    def drain(b):
        _row_copy(_rows_of(buf_ref, b * pairs, pairs), _rows_of(xs_hbm, 0, pairs), sem.at[b]).wait()

    pl.when(i > 0)(lambda: drain(1 - buf))
    pl.when(i == pl.num_programs(0) - 1)(lambda: drain(buf))


def _dispatch(seg, pos, h2):
    n = h2.shape[0] // LANE_TILES
    tt = TOKEN_TILE
    grid_spec = pltpu.PrefetchScalarGridSpec(
        num_scalar_prefetch=3,
        grid=(n // tt,),
        in_specs=[pl.BlockSpec((1, 1, TOP_K * tt), lambda i, *_: (i, 0, 0), memory_space=pltpu.SMEM),
                  pl.BlockSpec(_row_table_shape(tt), lambda i, *_: (i, 0))],
        out_specs=pl.BlockSpec(memory_space=pl.ANY),
        scratch_shapes=[pltpu.VMEM(_row_table_shape(2 * TOP_K * tt), F32),
                        pltpu.SemaphoreType.DMA((2,))],
    )
    return pl.pallas_call(
        _dispatch_kernel,
        grid_spec=grid_spec,
        out_shape=jax.ShapeDtypeStruct(_row_table_shape(n * TOP_K), F32),
        compiler_params=_cparams(("arbitrary",)),
    )(*seg, pos, h2)


def _swiglu_col(feature0, up):
    group, within = divmod(feature0, SWIGLU_GROUP)
    return (2 * group + up) * SWIGLU_GROUP + within


def _deinterleave_kernel(w_ref, o_ref):
    f2 = w_ref.shape[2]
    slab = 2 * LANES
    r = lax.broadcasted_iota(jnp.int32, (slab, slab), 0)
    c = lax.broadcasted_iota(jnp.int32, (slab, slab), 1)
    perm = (r == jnp.where(c < LANES, 2 * c, 2 * (c - LANES) + 1)).astype(BF16)
    for j in range(f2 // slab):
        y = _dot(w_ref[0, :, j * slab:(j + 1) * slab].astype(BF16), perm)
        for up in range(2):
            c0 = _swiglu_col(j * LANES, up)
            o_ref[0, :, c0:c0 + LANES] = y[:, up * LANES:(up + 1) * LANES].astype(BF16)


def _deinterleave(w):
    e, d, f2 = w.shape
    return pl.pallas_call(
        _deinterleave_kernel,
        grid=(e,),
        in_specs=[pl.BlockSpec((1, d, f2), lambda i: (i, 0, 0))],
        out_specs=pl.BlockSpec((1, d, f2), lambda i: (i, 0, 0)),
        out_shape=jax.ShapeDtypeStruct((e, d, f2), BF16),
        compiler_params=_cparams(("parallel",)),
    )(w)


def _expert_kernel(blk_ref, exp_ref, lo_ref, hi_ref, xs_ref, wgu_ref, bgu_ref, wd_ref, bd_ref, ys_ref):
    del blk_ref, exp_ref
    rows = xs_ref.shape[0] // LANE_TILES
    f = wd_ref.shape[1]
    i = pl.program_id(0)
    lo = lo_ref[i]
    hi = hi_ref[i]

    def mlp(row0, nrows):
        g = SWIGLU_GROUP
        x = _load_rows(xs_ref, row0, nrows).astype(BF16)
        acts = []
        for j in range(f // g):
            gu = _dot(x, wgu_ref[0, :, 2 * j * g:2 * (j + 1) * g]) + bgu_ref[0, :, 2 * j * g:2 * (j + 1) * g]
            gate = jnp.minimum(gu[:, :g], SWIGLU_LIMIT)
            up = jnp.clip(gu[:, g:], -SWIGLU_LIMIT, SWIGLU_LIMIT)
            acts.append(((up + 1.0) * gate * jax.nn.sigmoid(SWIGLU_ALPHA * gate)).astype(BF16))
        return _dot(jnp.concatenate(acts, axis=1), wd_ref[0]) + bd_ref[0]

    whole = jnp.logical_and(lo == 0, hi == rows)

    @pl.when(whole)
    def _():
        _store_rows(ys_ref, mlp(0, rows))

    @pl.when(jnp.logical_and(hi > lo, jnp.logical_not(whole)))
    def _():
        @pl.when(lo == 0)
        def _():
            ys_ref[...] = jnp.zeros_like(ys_ref)

        part = rows // EXPERT_PARTS
        for p in range(EXPERT_PARTS):
            @pl.when(jnp.logical_and(lo < (p + 1) * part, hi > p * part))
            def _(p=p):
                r = lax.broadcasted_iota(jnp.int32, (part, LANE_TILES * LANES), 0) + p * part
                mine = (r >= lo) & (r < hi)
                _store_rows(ys_ref, jnp.where(mine, mlp(p * part, part), _load_rows(ys_ref, p * part, part)), p * part)


def _experts(items, xs2, w_gu, b_gu, w_down, b_down):
    item_blk, item_exp, item_lo, item_hi = items
    d, f2 = w_gu.shape[1], w_gu.shape[2]
    wmap = lambda i, blk, exp, lo, hi: (exp[i], 0, 0)
    xmap = lambda i, blk, exp, lo, hi: (blk[i], 0)
    grid_spec = pltpu.PrefetchScalarGridSpec(
        num_scalar_prefetch=4,
        grid=(item_blk.shape[0],),
        in_specs=[pl.BlockSpec(_row_table_shape(MOE_ROWS), xmap),
                  pl.BlockSpec((1, d, f2), wmap),
                  pl.BlockSpec((1, 1, f2), wmap),
                  pl.BlockSpec((1, f2 // 2, d), wmap),
                  pl.BlockSpec((1, 1, d), wmap)],
        out_specs=pl.BlockSpec(_row_table_shape(MOE_ROWS), xmap),
    )
    return pl.pallas_call(
        _expert_kernel,
        grid_spec=grid_spec,
        out_shape=jax.ShapeDtypeStruct(xs2.shape, F32),
        compiler_params=_cparams(("arbitrary",)),
    )(item_blk, item_exp, item_lo, item_hi, xs2, w_gu, b_gu, w_down, b_down)


def _expert_work_items(counts, n_rows):
    ends = jnp.cumsum(counts)
    starts = ends - counts
    n_blk = n_rows // MOE_ROWS
    cuts = jnp.sort(jnp.concatenate([jnp.arange(n_blk, dtype=jnp.int32) * MOE_ROWS, starts[1:]]))
    nxt = jnp.concatenate([cuts[1:], jnp.full((1,), n_rows, jnp.int32)])
    blk = jnp.minimum(cuts // MOE_ROWS, n_blk - 1)
    exp = jnp.minimum(jnp.sum(ends[None, :] <= cuts[:, None], axis=1), N_EXPERTS - 1).astype(jnp.int32)
    lo = cuts - blk * MOE_ROWS
    hi = nxt - blk * MOE_ROWS
    return (blk.astype(jnp.int32), exp, lo.astype(jnp.int32), hi.astype(jnp.int32)), starts


def _combine_kernel(cnt_ref, far_ref, near_ref, pos_ref, gates_ref, x1_ref, mod_ref, fg_ref, ys_hbm,
                    op_ref, os_ref, buf_ref, y_ref, sem, *, n_prompt_tiles):
    tt = x1_ref.shape[0]
    pairs = TOP_K * tt
    i = pl.program_id(0)
    buf = i % 2
    base = buf * pairs

    def fetch(tile, b):
        def copy(near, far, rows):
            return _row_copy(_rows_of(ys_hbm, far, rows), _rows_of(buf_ref, b * pairs + near, rows), sem.at[b])

        _segment_copies(cnt_ref, far_ref, near_ref, tile, copy)

    pl.when(i == 0)(lambda: fetch(0, 0))
    pl.when(i + 1 < pl.num_programs(0))(lambda: fetch(i + 1, 1 - buf))
    _row_copy(_rows_of(ys_hbm, 0, pairs), _rows_of(buf_ref, base, pairs), sem.at[buf]).wait()

    def weighted_sum(t, carry):
        acc = None
        for k in range(TOP_K):
            term = gates_ref[0, 0, k * tt + t] * buf_ref[_row_slice(base + pos_ref[0, 0, k * tt + t]), :]
            acc = term if acc is None else acc + term
        y_ref[_row_slice(t), :] = acc
        return carry

    lax.fori_loop(0, tt, weighted_sum, 0, unroll=ROW_LOOP_UNROLL)

    x2 = x1_ref[...] + mod_ref[0, 5:6, :] * _load_rows(y_ref, 0, tt)
    ms = jnp.mean(x2 * x2, axis=-1, keepdims=True)
    out = x2 * lax.rsqrt(ms + EPS) * fg_ref[...]

    def write(stream):
        (op_ref, os_ref)[stream][...] = out

    _per_stream(n_prompt_tiles, write)


def _combine(seg, pos, gates, x1, mod3, final_g, ys, batch_of_tile, n_prompt):
    n, d = x1.shape
    tt = TOKEN_TILE
    n_prompt_tiles = n_prompt // tt
    grid_spec = pltpu.PrefetchScalarGridSpec(
        num_scalar_prefetch=3,
        grid=(n // tt,),
        in_specs=[pl.BlockSpec((1, 1, TOP_K * tt), lambda i, *_: (i, 0, 0), memory_space=pltpu.SMEM),
                  pl.BlockSpec((1, 1, TOP_K * tt), lambda i, *_: (i, 0, 0), memory_space=pltpu.SMEM),
                  pl.BlockSpec((tt, d), lambda i, *_: (i, 0)),
                  pl.BlockSpec((1, 6, d), lambda i, *_: (batch_of_tile(i), 0, 0)),
                  pl.BlockSpec((1, d), lambda i, *_: (0, 0)),
                  pl.BlockSpec(memory_space=pl.ANY)],
        out_specs=list(_stream_specs((tt, d), n_prompt_tiles)),
        scratch_shapes=[pltpu.VMEM(_row_table_shape(2 * TOP_K * tt), F32),
                        pltpu.VMEM(_row_table_shape(tt), F32),
                        pltpu.SemaphoreType.DMA((2,))],
    )
    return pl.pallas_call(
        functools.partial(_combine_kernel, n_prompt_tiles=n_prompt_tiles),
        grid_spec=grid_spec,
        out_shape=[jax.ShapeDtypeStruct((n_prompt, d), F32),
                   jax.ShapeDtypeStruct((n - n_prompt, d), F32)],
        compiler_params=_cparams(("arbitrary",)),
    )(*seg, pos, gates, x1, mod3, final_g.reshape(1, d), ys)


def _batch_of_tile_fn(tile, bp, tp, ts):
    n_prompt = bp * tp

    def batch_of_tile(i):
        t = i * tile
        return jnp.where(t < n_prompt, t // tp, bp + (t - n_prompt) // ts)

    return batch_of_tile


def kernel(x_prompt, x_sample, c_prompt, c_sample, norm1_g, w_ada, b_ada, w_in, gla_w2_fwd, gla_b2_fwd, gla_w2_bwd, gla_b2_bwd, gla_norm_g, nat_rpb, w_out, norm2_g, router_w, router_b, w_gate_up, b_gate_up, w_down, b_down, final_norm_g):
    assert w_ada.shape[0] == 1, "single-layer encoder"
    bp, tp, d = x_prompt.shape
    bs, ts, _ = x_sample.shape
    n_prompt, n_sample = bp * tp, bs * ts
    n = n_prompt + n_sample
    assert tp % TOKEN_TILE == 0 and ts % TOKEN_TILE == 0 and n_prompt % ts == 0
    assert d == LANE_TILES * LANES

    x = (x_prompt.reshape(n_prompt, d), x_sample.reshape(n_sample, d))
    c = jnp.concatenate([c_prompt, c_sample], axis=0)

    sizes = (GLA_QK, GLA_QK, GLA_V, GLA_V, GLA_RANK, GLA_RANK, NAT_W, NAT_W, NAT_W)
    offs = np.concatenate([[0], np.cumsum(sizes)])
    w_in0 = w_in[0]
    seg = lambda j: w_in0[:, offs[j]:offs[j + 1]]
    w_all = jnp.concatenate(
        [seg(0), seg(1), seg(2), seg(3), seg(6) * (NAT_HD ** -0.5 * LOG2E), seg(7), seg(8), seg(4), seg(5),
         jnp.zeros((d, LANES - 2 * GLA_RANK), F32)], axis=1).astype(BF16)
    pairs = GLA_HEADS // 2
    w2 = jnp.zeros((LANES, pairs, 2, 2 * GLA_DK), F32)
    w2 = w2.at[0:GLA_RANK, :, 0].set(gla_w2_fwd[0].reshape(GLA_RANK, pairs, 2 * GLA_DK))
    w2 = w2.at[GLA_RANK:2 * GLA_RANK, :, 1].set(gla_w2_bwd[0].reshape(GLA_RANK, pairs, 2 * GLA_DK))
    wz = w2.reshape(LANES, pairs * 4 * GLA_DK).astype(BF16)
    bz = jnp.stack([gla_b2_fwd[0].reshape(pairs, 2 * GLA_DK), gla_b2_bwd[0].reshape(pairs, 2 * GLA_DK)],
                   axis=1).reshape(1, pairs * 4 * GLA_DK)
    tab = _nat_bias_table(nat_rpb[0])
    w_og = w_out[0, :GLA_V].astype(BF16)
    w_on = w_out[0, GLA_V:].astype(BF16)
    rw_t = router_w[0].T
    w_gu = _deinterleave(w_gate_up[0])
    n_e, f2 = b_gate_up.shape[1:]
    b_gu = (b_gate_up[0].reshape(n_e, f2 // (2 * SWIGLU_GROUP), SWIGLU_GROUP, 2)
            .transpose(0, 1, 3, 2).reshape(n_e, 1, f2))
    w_dn = w_down[0].astype(BF16)
    b_dn = b_down[0, :, None, :]

    mod3 = _ada(c, w_ada[0], b_ada[0]).reshape(bp + bs, 6, d)
    bot_tok = _batch_of_tile_fn(TOKEN_TILE, bp, tp, ts)

    pg, pn, plr = _in_proj(*x, mod3, norm1_g[0], w_all, bot_tok)

    gla_args = (pg, plr, wz, bz, gla_norm_g[0].reshape(1, -1))
    og = (_gla(*gla_args, seq=tp, batches=bp, block0=0),
          _gla(*gla_args, seq=ts, batches=bs, block0=n_prompt // ts))
    on = (_nat(pn, tab, seq=tp, batches=bp, block0=0),
          _nat(pn, tab, seq=ts, batches=bs, block0=n_prompt // ts))

    x1, h2, top_i, gates, rank, cnt, tile_cnt, tile_carry = _out_proj(
        og, on, x, mod3, w_og, w_on, norm2_g[0], rw_t, router_b[0], bot_tok)

    n_rows = n * TOP_K
    n_tiles = n // TOKEN_TILE
    assert n_rows % MOE_ROWS == 0
    counts = cnt[:, 0].astype(jnp.int32)
    items, starts = _expert_work_items(counts, n_rows)
    seg_cnt = tile_cnt[:, :, 0].astype(jnp.int32)
    seg_far = starts[None, :] + tile_carry[:, :, 0].astype(jnp.int32)
    seg_near = jnp.cumsum(seg_cnt, axis=1) - seg_cnt
    seg = tuple(a.reshape(n_tiles * N_EXPERTS) for a in (seg_cnt, seg_far, seg_near))
    chosen = top_i[None] == jnp.arange(N_EXPERTS, dtype=jnp.int32)[:, None, None, None]
    pos = rank + jnp.sum(jnp.where(chosen, seg_near.T[:, :, None, None], 0), axis=0)
    per_tile = lambda a: a.reshape(n_tiles, 1, TOP_K * TOKEN_TILE)
    pos, gates = per_tile(pos), per_tile(gates)

    xs = _dispatch(seg, pos, h2)
    ys = _experts(items, xs, w_gu, b_gu, w_dn, b_dn)
    y_prompt, y_sample = _combine(seg, pos, gates, x1, mod3, final_norm_g, ys, bot_tok, n_prompt)

    return (y_prompt.reshape(bp, tp, d), y_sample.reshape(bs, ts, d))
```

```python
import functools

import numpy as np
import jax
import jax.numpy as jnp
from jax import lax
from jax.experimental import pallas as pl
from jax.experimental.pallas import tpu as pltpu

F32 = jnp.float32
BF16 = jnp.bfloat16
HIGHEST = lax.Precision.HIGHEST

EPS = 1e-5
GRID_W = 64
GLA_HEADS = 4
GLA_DK = 64
GLA_DV = 128
GLA_RANK = 16
GLA_TAU = 16.0
GLA_CHUNK = 64
GLA_QK = GLA_HEADS * GLA_DK
GLA_V = GLA_HEADS * GLA_DV
NAT_HEADS = 16
NAT_HD = 32
NAT_W = NAT_HEADS * NAT_HD
NAT_KH = 8
NAT_KW = 16
NAT_GROUP = 4
N_EXPERTS = 32
TOP_K = 4
SWIGLU_LIMIT = 7.0
SWIGLU_ALPHA = 1.702

LANES = 128
SUBLANES = 8
TOKEN_TILE = 512
MOE_ROWS = 512
EXPERT_PARTS = 4
ADA_COLS = 1024
GLA_UNROLL = 16
NAT_UNROLL = 16
ROW_LOOP_UNROLL = 8
SEGMENT_CHUNK = 32
LOG2E = 1.4426950408889634
VMEM_LIMIT = 56 * 1024 * 1024


def _cparams(sem, vmem=VMEM_LIMIT):
    return pltpu.CompilerParams(dimension_semantics=sem, vmem_limit_bytes=vmem)


def _dot(a, b):
    return jnp.dot(a, b, preferred_element_type=F32)


def _dot_nt(a, b, precision=None):
    return lax.dot_general(a, b, (((1,), (1,)), ((), ())), preferred_element_type=F32, precision=precision)


LANE_TILES = 8


def _row_table_shape(rows):
    return (rows * LANE_TILES, LANES)


def _store_rows(ref, x, row0=0):
    rows = x.shape[0]
    for s in range(LANE_TILES):
        ref[pl.ds(row0 * LANE_TILES + s, rows, stride=LANE_TILES), :] = x[:, s * LANES:(s + 1) * LANES]


def _load_rows(ref, row0, rows):
    return jnp.concatenate(
        [ref[pl.ds(row0 * LANE_TILES + s, rows, stride=LANE_TILES), :] for s in range(LANE_TILES)], axis=1)


def _row_slice(row0, rows=1):
    return pl.ds(pl.multiple_of(row0 * LANE_TILES, LANE_TILES), rows * LANE_TILES)


def _rows_of(ref, row0, rows):
    return ref.at[_row_slice(row0, rows), :]


def _dot_tn(a, b):
    return lax.dot_general(a, b, (((0,), (0,)), ((), ())), preferred_element_type=F32)


def _ada_kernel(c_ref, w_ref, b_ref, o_ref):
    c = c_ref[...]
    s = c * jax.nn.sigmoid(c)
    o_ref[...] = jnp.dot(s, w_ref[...], preferred_element_type=F32, precision=HIGHEST) + b_ref[...]


def _ada(c, w, b):
    nb, d = c.shape
    cols = w.shape[1]
    blk = ADA_COLS
    return pl.pallas_call(
        _ada_kernel,
        grid=(cols // blk,),
        in_specs=[pl.BlockSpec((nb, d), lambda j: (0, 0)),
                  pl.BlockSpec((d, blk), lambda j: (0, j)),
                  pl.BlockSpec((1, blk), lambda j: (0, j))],
        out_specs=pl.BlockSpec((nb, blk), lambda j: (0, j)),
        out_shape=jax.ShapeDtypeStruct((nb, cols), F32),
        compiler_params=_cparams(("arbitrary",)),
    )(c, w, b.reshape(1, cols))


def _stream_specs(block, n_prompt_tiles):
    return (pl.BlockSpec(block, lambda i, *_: (jnp.minimum(i, n_prompt_tiles - 1), 0)),
            pl.BlockSpec(block, lambda i, *_: (jnp.maximum(i - n_prompt_tiles, 0), 0)))


def _per_stream(n_prompt_tiles, body):
    i = pl.program_id(0)
    pl.when(i < n_prompt_tiles)(functools.partial(body, 0))
    pl.when(i >= n_prompt_tiles)(functools.partial(body, 1))


def _in_kernel(xp_ref, xs_ref, mod_ref, g_ref, w_ref, pg_ref, pn_ref, plr_ref, *, n_prompt_tiles):
    def body(stream):
        x = (xp_ref, xs_ref)[stream][...]
        ms = jnp.mean(x * x, axis=-1, keepdims=True)
        y = x * lax.rsqrt(ms + EPS) * g_ref[...]
        h = y * (1.0 + mod_ref[0, 1:2, :]) + mod_ref[0, 0:1, :]
        hb = h.astype(BF16)
        wg = GLA_QK * 2 + GLA_V * 2
        wn = 3 * NAT_W
        pg_ref[...] = _dot(hb, w_ref[:, 0:wg]).astype(BF16)
        pn_ref[...] = _dot(hb, w_ref[:, wg:wg + wn]).astype(BF16)
        plr_ref[...] = _dot(hb, w_ref[:, wg + wn:wg + wn + LANES])

    _per_stream(n_prompt_tiles, body)


def _in_proj(xp, xs, mod3, norm_g, w_all, batch_of_tile):
    d = xp.shape[1]
    n = xp.shape[0] + xs.shape[0]
    wg = GLA_QK * 2 + GLA_V * 2
    wn = 3 * NAT_W
    tm = TOKEN_TILE
    n_prompt_tiles = xp.shape[0] // tm
    return pl.pallas_call(
        functools.partial(_in_kernel, n_prompt_tiles=n_prompt_tiles),
        grid=(n // tm,),
        in_specs=[*_stream_specs((tm, d), n_prompt_tiles),
                  pl.BlockSpec((1, 6, d), lambda i: (batch_of_tile(i), 0, 0)),
                  pl.BlockSpec((1, d), lambda i: (0, 0)),
                  pl.BlockSpec(w_all.shape, lambda i: (0, 0))],
        out_specs=[pl.BlockSpec((tm, wg), lambda i: (i, 0)),
                   pl.BlockSpec((tm, wn), lambda i: (i, 0)),
                   pl.BlockSpec((tm, LANES), lambda i: (i, 0))],
        out_shape=[jax.ShapeDtypeStruct((n, wg), BF16),
                   jax.ShapeDtypeStruct((n, wn), BF16),
                   jax.ShapeDtypeStruct((n, LANES), F32)],
        compiler_params=_cparams(("arbitrary",)),
    )(xp, xs, mod3, norm_g.reshape(1, d), w_all)


def _gla_kernel(q_ref, k_ref, v_ref, g_ref, lr_ref, wz_ref, bz_ref, ng_ref,
                o_ref, la_ref, acc_ref, qt_ref, ke_ref, dec_ref, sf_ref, sb_ref, *, seq):
    c_len = GLA_CHUNK
    n = seq // c_len
    row = lax.broadcasted_iota(jnp.int32, (c_len, c_len), 0)
    col = lax.broadcasted_iota(jnp.int32, (c_len, c_len), 1)
    tril = col <= row
    triu = col >= row
    cum_row = lax.broadcasted_iota(jnp.int32, (c_len, 2 * c_len), 0)
    cum_col = lax.broadcasted_iota(jnp.int32, (c_len, 2 * c_len), 1) % c_len
    cum = (cum_col <= cum_row).astype(F32).astype(BF16)
    mask_f = jnp.concatenate([tril, tril], axis=0)
    mask_b = jnp.concatenate([triu, triu], axis=0)
    head_a = lax.broadcasted_iota(jnp.int32, (c_len, LANES), 1) < GLA_DK
    st_row = lax.broadcasted_iota(jnp.int32, (2 * GLA_DV, LANES), 0)
    st_col = lax.broadcasted_iota(jnp.int32, (2 * GLA_DV, LANES), 1)
    blockdiag = (st_row < GLA_DV) == (st_col < GLA_DK)

    z = _dot(lr_ref[...].astype(BF16), wz_ref[...]) + bz_ref[...]
    la_ref[...] = (jnp.minimum(z, 0.0) - jnp.log(1.0 + jnp.exp(-jnp.abs(z)))) * (1.0 / GLA_TAU)

    def chunk_rows(c):
        return pl.ds(pl.multiple_of(c * c_len, c_len), c_len)

    def stack_heads(x):
        zero = jnp.zeros_like(x)
        return jnp.concatenate([jnp.where(head_a, x, zero), jnp.where(head_a, zero, x)], axis=0).astype(BF16)

    def local(c, carry):
        rows = chunk_rows(c)
        la = la_ref[rows, :]
        hi = la.astype(BF16)
        lo = (la - hi.astype(F32)).astype(BF16)
        binc = _dot(cum, jnp.concatenate([hi, lo], axis=0))
        btot = jnp.sum(la, axis=0, keepdims=True)
        b_f = binc[:, :LANES]
        b_b = btot[:, LANES:] - binc[:, LANES:] + la[:, LANES:]
        e_tot = jnp.exp(btot)
        q = q_ref[rows, :].astype(F32) * (GLA_DK ** -0.5)
        k = k_ref[rows, :].astype(F32)
        qt_f = q * jnp.exp(b_f)
        qt_b = q * jnp.exp(b_b)
        kt_f = k * jnp.exp(-b_f)
        kt_b = k * jnp.exp(-b_b)
        a = (jnp.where(mask_f, _dot_nt(stack_heads(qt_f), kt_f.astype(BF16)), 0.0)
             + jnp.where(mask_b, _dot_nt(stack_heads(qt_b), kt_b.astype(BF16)), 0.0))
        oi = _dot(a.astype(BF16), v_ref[rows, :])
        acc_ref[rows, :] = jnp.concatenate([oi[0:c_len, 0:GLA_DV], oi[c_len:, GLA_DV:]], axis=1)
        qt_ref[rows, :] = jnp.concatenate([qt_f, qt_b], axis=1).astype(BF16)
        ke_ref[rows, :] = jnp.concatenate([kt_f * e_tot[:, :LANES], kt_b * e_tot[:, LANES:]], axis=1).astype(BF16)
        dec_ref[pl.ds(pl.multiple_of(c * SUBLANES, SUBLANES), SUBLANES), :] = jnp.broadcast_to(e_tot, (SUBLANES, 2 * LANES))
        return carry

    def carried(c, s_ref, half):
        rows = chunk_rows(c)
        lanes = slice(half * LANES, (half + 1) * LANES)
        st = s_ref[...]
        acc_ref[rows, :] += _dot_nt(qt_ref[rows, lanes], st.astype(BF16))
        upd = _dot_tn(v_ref[rows, :], ke_ref[rows, lanes])
        dec = dec_ref[pl.ds(pl.multiple_of(c * SUBLANES, SUBLANES), 1), lanes]
        s_ref[...] = st * dec + jnp.where(blockdiag, upd, 0.0)

    def scan(c, carry):
        carried(c, sf_ref, 0)
        carried(n - 1 - c, sb_ref, 1)
        return carry

    def finalize(c, carry):
        rows = chunk_rows(c)
        o = acc_ref[rows, :]
        halves = []
        for hh in range(2):
            oh = o[:, hh * GLA_DV:(hh + 1) * GLA_DV]
            ms = jnp.mean(oh * oh, axis=-1, keepdims=True)
            halves.append(oh * lax.rsqrt(ms + EPS) * ng_ref[...])
        y = jnp.concatenate(halves, axis=1)
        g = g_ref[rows, :].astype(F32)
        o_ref[rows, :] = (y * (g * jax.nn.sigmoid(g))).astype(o_ref.dtype)
        return carry

    sf_ref[...] = jnp.zeros_like(sf_ref)
    sb_ref[...] = jnp.zeros_like(sb_ref)
    lax.fori_loop(0, n, local, 0, unroll=GLA_UNROLL)
    lax.fori_loop(0, n, scan, 0, unroll=GLA_UNROLL)
    lax.fori_loop(0, n, finalize, 0, unroll=GLA_UNROLL)


def _gla(pg, plr, wz, bz, norm_g, *, seq, batches, block0):
    pair_w = 2 * GLA_DK
    pair_v = 2 * GLA_DV
    k0 = GLA_QK // pair_w
    v0 = 2 * GLA_QK // pair_v
    g0 = (2 * GLA_QK + GLA_V) // pair_v
    n_chunks = seq // GLA_CHUNK
    in_specs = [pl.BlockSpec((seq, pair_w), lambda b, p: (block0 + b, p)),
                pl.BlockSpec((seq, pair_w), lambda b, p: (block0 + b, k0 + p)),
                pl.BlockSpec((seq, pair_v), lambda b, p: (block0 + b, v0 + p)),
                pl.BlockSpec((seq, pair_v), lambda b, p: (block0 + b, g0 + p)),
                pl.BlockSpec((seq, LANES), lambda b, p: (block0 + b, 0)),
                pl.BlockSpec((LANES, 2 * pair_w), lambda b, p: (0, p)),
                pl.BlockSpec((1, 2 * pair_w), lambda b, p: (0, p)),
                pl.BlockSpec((1, GLA_DV), lambda b, p: (0, 0))]
    return pl.pallas_call(
        functools.partial(_gla_kernel, seq=seq),
        grid=(batches, GLA_HEADS // 2),
        in_specs=in_specs,
        out_specs=pl.BlockSpec((seq, pair_v), lambda b, p: (b, p)),
        out_shape=jax.ShapeDtypeStruct((batches * seq, GLA_V), BF16),
        scratch_shapes=[pltpu.VMEM((seq, 2 * pair_w), F32),
                        pltpu.VMEM((seq, pair_v), F32),
                        pltpu.VMEM((seq, 2 * pair_w), BF16),
                        pltpu.VMEM((seq, 2 * pair_w), BF16),
                        pltpu.VMEM((n_chunks * SUBLANES, 2 * pair_w), F32),
                        pltpu.VMEM((pair_v, pair_w), F32),
                        pltpu.VMEM((pair_v, pair_w), F32)],
        compiler_params=_cparams(("parallel", "parallel")),
    )(pg, pg, pg, pg, plr, wz, bz, norm_g)


def _nat_kernel(q_ref, k_ref, v_ref, tab_ref, o_ref, s_ref, *, rows):
    w = GRID_W
    head = lax.broadcasted_iota(jnp.int32, (w, LANES), 1) // NAT_HD
    win = NAT_KH * w

    def window(i):
        r0 = jnp.clip(i - NAT_KH // 2, 0, rows - NAT_KH)
        return r0 - i + NAT_KH - 1, pl.ds(pl.multiple_of(r0 * w, w), win)

    def scores(i, slot):
        base, krows = window(i)
        q = q_ref[pl.ds(pl.multiple_of(i * w, w), w), :]
        zero = jnp.zeros_like(q)
        q_stack = jnp.concatenate([jnp.where(head == h, q, zero) for h in range(NAT_GROUP)], axis=0)
        s_ref[slot] = _dot_nt(q_stack, k_ref[krows, :]) + tab_ref[0, base]

    def attend(i, slot):
        _, krows = window(i)
        s = s_ref[slot]
        m = jnp.max(s, axis=-1, keepdims=True)
        e = jnp.exp2(s - m)
        l = jnp.sum(e, axis=-1, keepdims=True)
        o = _dot(e.astype(BF16), v_ref[krows, :]) / l
        out = jnp.zeros((w, LANES), F32)
        for h in range(NAT_GROUP):
            out = out + jnp.where(head == h, o[h * w:(h + 1) * w, :], 0.0)
        o_ref[pl.ds(pl.multiple_of(i * w, w), w), :] = out.astype(o_ref.dtype)

    scores(0, 0)

    def body(j, carry):
        for u in range(2):
            i = 2 * j + u
            scores(jnp.minimum(i + 1, rows - 1), 1 - u)
            attend(i, u)
        return carry

    lax.fori_loop(0, rows // 2, body, 0, unroll=NAT_UNROLL // 2)


def _nat(pn, tab, *, seq, batches, block0):
    rows = seq // GRID_W
    groups = NAT_HEADS // NAT_GROUP
    in_specs = [pl.BlockSpec((seq, LANES), lambda g, b: (block0 + b, g)),
                pl.BlockSpec((seq, LANES), lambda g, b: (block0 + b, groups + g)),
                pl.BlockSpec((seq, LANES), lambda g, b: (block0 + b, 2 * groups + g)),
                pl.BlockSpec((1,) + tab.shape[1:], lambda g, b: (g, 0, 0, 0))]
    return pl.pallas_call(
        functools.partial(_nat_kernel, rows=rows),
        grid=(groups, batches),
        in_specs=in_specs,
        out_specs=pl.BlockSpec((seq, LANES), lambda g, b: (b, g)),
        out_shape=jax.ShapeDtypeStruct((batches * seq, NAT_W), BF16),
        scratch_shapes=[pltpu.VMEM((2, NAT_GROUP * GRID_W, NAT_KH * GRID_W), F32)],
        compiler_params=_cparams(("parallel", "parallel")),
    )(pn, pn, pn, tab)


def _nat_bias_table(rpb):
    w = GRID_W
    jq = np.arange(w)[:, None]
    jk = np.arange(w)[None, :]
    c0 = np.clip(jq - NAT_KW // 2, 0, w - NAT_KW)
    valid = (jk >= c0) & (jk < c0 + NAT_KW)
    dcol = np.clip(jk - jq + NAT_KW - 1, 0, 2 * NAT_KW - 2)
    select = (dcol[None] == np.arange(2 * NAT_KW - 1)[:, None, None]).astype(np.float32)
    t = jnp.einsum('hrc,cqk->hrqk', rpb.astype(F32) * LOG2E, select, precision=HIGHEST)
    t = jnp.where(valid[None, None], t, -jnp.inf)
    t = jnp.stack([t[:, b:b + NAT_KH] for b in range(NAT_KH)], axis=1)
    t = t.transpose(0, 1, 3, 2, 4).reshape(NAT_HEADS, NAT_KH, w, NAT_KH * w)
    t = t.reshape(NAT_HEADS // NAT_GROUP, NAT_GROUP, NAT_KH, w, NAT_KH * w).transpose(0, 2, 1, 3, 4)
    return t.reshape(NAT_HEADS // NAT_GROUP, NAT_KH, NAT_GROUP * w, NAT_KH * w)


def _out_kernel(ogp_ref, ogs_ref, onp_ref, ons_ref, xp_ref, xs_ref, mod_ref, wg_ref, wn_ref, n2_ref, rw_ref, rb_ref,
                x1_ref, h2_ref, ti_ref, gt_ref, rk_ref, cnt_ref, tcnt_ref, tcar_ref, carry_ref, *, n_prompt_tiles):
    tm = x1_ref.shape[0]

    @pl.when(pl.program_id(0) == 0)
    def _():
        carry_ref[...] = jnp.zeros_like(carry_ref)

    def residual(stream):
        og_ref, on_ref, x_ref = ((ogp_ref, onp_ref, xp_ref), (ogs_ref, ons_ref, xs_ref))[stream]
        mix = _dot(og_ref[...], wg_ref[...]) + _dot(on_ref[...], wn_ref[...])
        x1_ref[...] = x_ref[...] + mod_ref[0, 2:3, :] * mix

    _per_stream(n_prompt_tiles, residual)
    x1 = x1_ref[...]
    ms = jnp.mean(x1 * x1, axis=-1, keepdims=True)
    h2 = x1 * lax.rsqrt(ms + EPS) * n2_ref[...]
    h2 = h2 * (1.0 + mod_ref[0, 4:5, :]) + mod_ref[0, 3:4, :]
    _store_rows(h2_ref, h2)

    logits = _dot_nt(rw_ref[...], h2, precision=HIGHEST) + rb_ref[...]
    eidx = lax.broadcasted_iota(jnp.int32, logits.shape, 0)
    vals = logits
    top_v, top_i = [], []
    for _ in range(TOP_K):
        m = jnp.max(vals, axis=0, keepdims=True)
        idx = jnp.min(jnp.where(vals == m, eidx, N_EXPERTS), axis=0, keepdims=True)
        top_v.append(m)
        top_i.append(idx)
        vals = jnp.where(eidx == idx, -jnp.inf, vals)
    ev = [jnp.exp(v - top_v[0]) for v in top_v]
    den = ev[0] + ev[1] + ev[2] + ev[3]
    ti_ref[0] = jnp.concatenate(top_i, axis=0)
    gt_ref[0] = jnp.concatenate([e / den for e in ev], axis=0)

    onehot = jnp.zeros(logits.shape, F32)
    for idx in top_i:
        onehot = onehot + (eidx == idx).astype(F32)
    r = lax.broadcasted_iota(jnp.int32, (tm, tm), 0)
    c = lax.broadcasted_iota(jnp.int32, (tm, tm), 1)
    earlier = (r < c).astype(BF16)
    before = _dot(onehot.astype(BF16), earlier)
    rk_ref[0] = jnp.concatenate(
        [jnp.sum(jnp.where(eidx == idx, before, 0.0), axis=0, keepdims=True) for idx in top_i], axis=0).astype(jnp.int32)
    tile_counts = jnp.sum(onehot, axis=1, keepdims=True)
    tcar_ref[0] = jnp.broadcast_to(carry_ref[...], tcar_ref.shape[1:])
    tcnt_ref[0] = jnp.broadcast_to(tile_counts, tcnt_ref.shape[1:])
    carry_ref[...] = carry_ref[...] + tile_counts
    cnt_ref[...] = jnp.broadcast_to(carry_ref[...], cnt_ref.shape)


def _out_proj(og, on, x, mod3, w_g, w_n, norm_g, rw_t, rb, batch_of_tile):
    d = x[0].shape[1]
    n = x[0].shape[0] + x[1].shape[0]
    tm = TOKEN_TILE
    n_prompt_tiles = x[0].shape[0] // tm
    tok = lambda i: (i, 0)
    const = lambda i: (0, 0)
    return pl.pallas_call(
        functools.partial(_out_kernel, n_prompt_tiles=n_prompt_tiles),
        grid=(n // tm,),
        in_specs=[*_stream_specs((tm, GLA_V), n_prompt_tiles),
                  *_stream_specs((tm, NAT_W), n_prompt_tiles),
                  *_stream_specs((tm, d), n_prompt_tiles),
                  pl.BlockSpec((1, 6, d), lambda i: (batch_of_tile(i), 0, 0)),
                  pl.BlockSpec(w_g.shape, const),
                  pl.BlockSpec(w_n.shape, const),
                  pl.BlockSpec((1, d), const),
                  pl.BlockSpec(rw_t.shape, const),
                  pl.BlockSpec((N_EXPERTS, 1), const)],
        out_specs=[pl.BlockSpec((tm, d), tok),
                   pl.BlockSpec(_row_table_shape(tm), tok),
                   pl.BlockSpec((1, TOP_K, tm), lambda i: (i, 0, 0)),
                   pl.BlockSpec((1, TOP_K, tm), lambda i: (i, 0, 0)),
                   pl.BlockSpec((1, TOP_K, tm), lambda i: (i, 0, 0)),
                   pl.BlockSpec((N_EXPERTS, LANES), const),
                   pl.BlockSpec((1, N_EXPERTS, LANES), lambda i: (i, 0, 0)),
                   pl.BlockSpec((1, N_EXPERTS, LANES), lambda i: (i, 0, 0))],
        out_shape=[jax.ShapeDtypeStruct((n, d), F32),
                   jax.ShapeDtypeStruct(_row_table_shape(n), F32),
                   jax.ShapeDtypeStruct((n // tm, TOP_K, tm), jnp.int32),
                   jax.ShapeDtypeStruct((n // tm, TOP_K, tm), F32),
                   jax.ShapeDtypeStruct((n // tm, TOP_K, tm), jnp.int32),
                   jax.ShapeDtypeStruct((N_EXPERTS, LANES), F32),
                   jax.ShapeDtypeStruct((n // tm, N_EXPERTS, LANES), F32),
                   jax.ShapeDtypeStruct((n // tm, N_EXPERTS, LANES), F32)],
        scratch_shapes=[pltpu.VMEM((N_EXPERTS, 1), F32)],
        compiler_params=_cparams(("arbitrary",)),
    )(*og, *on, *x, mod3, w_g, w_n, norm_g.reshape(1, d), rw_t, rb.reshape(N_EXPERTS, 1))


def _row_copy(src, dst, sem):
    return pltpu.make_async_copy(src, dst, sem)


def _segment_copies(cnt_ref, far_ref, near_ref, tile, copy):
    def per_expert(e, carry):
        idx = tile * N_EXPERTS + e
        c, far, near = cnt_ref[idx], far_ref[idx], near_ref[idx]

        def whole(i, carry2):
            copy(near + i * SEGMENT_CHUNK, far + i * SEGMENT_CHUNK, SEGMENT_CHUNK).start()
            return carry2

        lax.fori_loop(0, c // SEGMENT_CHUNK, whole, 0)
        for bit in reversed(range(SEGMENT_CHUNK.bit_length() - 1)):
            size = 1 << bit
            done = c & -(2 * size)

            @pl.when((c & size) != 0)
            def _(done=done, size=size):
                copy(near + done, far + done, size).start()

        return carry

    lax.fori_loop(0, N_EXPERTS, per_expert, 0)


def _dispatch_kernel(cnt_ref, far_ref, near_ref, pos_ref, h_ref, xs_hbm, buf_ref, sem):
    pairs = pos_ref.shape[2]
    tt = pairs // TOP_K
    i = pl.program_id(0)
    buf = i % 2
    base = buf * pairs

    def place(t, carry):
        row = h_ref[_row_slice(t), :]
        for k in range(TOP_K):
            buf_ref[_row_slice(base + pos_ref[0, 0, k * tt + t]), :] = row
        return carry

    lax.fori_loop(0, tt, place, 0, unroll=ROW_LOOP_UNROLL)

    def copy(near, far, rows):
        return _row_copy(_rows_of(buf_ref, base + near, rows), _rows_of(xs_hbm, far, rows), sem.at[buf])

    _segment_copies(cnt_ref, far_ref, near_ref, i, copy)

    def drain(b):
        _row_copy(_rows_of(buf_ref, b * pairs, pairs), _rows_of(xs_hbm, 0, pairs), sem.at[b]).wait()

    pl.when(i > 0)(lambda: drain(1 - buf))
    pl.when(i == pl.num_programs(0) - 1)(lambda: drain(buf))


def _dispatch(seg, pos, h2):
    n = h2.shape[0] // LANE_TILES
    tt = TOKEN_TILE
    grid_spec = pltpu.PrefetchScalarGridSpec(
        num_scalar_prefetch=3,
        grid=(n // tt,),
        in_specs=[pl.BlockSpec((1, 1, TOP_K * tt), lambda i, *_: (i, 0, 0), memory_space=pltpu.SMEM),
                  pl.BlockSpec(_row_table_shape(tt), lambda i, *_: (i, 0))],
        out_specs=pl.BlockSpec(memory_space=pl.ANY),
        scratch_shapes=[pltpu.VMEM(_row_table_shape(2 * TOP_K * tt), F32),
                        pltpu.SemaphoreType.DMA((2,))],
    )
    return pl.pallas_call(
        _dispatch_kernel,
        grid_spec=grid_spec,
        out_shape=jax.ShapeDtypeStruct(_row_table_shape(n * TOP_K), F32),
        compiler_params=_cparams(("arbitrary",)),
    )(*seg, pos, h2)


def _deinterleave_kernel(w_ref, o_ref):
    f2 = w_ref.shape[2]
    slab = 2 * LANES
    r = lax.broadcasted_iota(jnp.int32, (slab, slab), 0)
    c = lax.broadcasted_iota(jnp.int32, (slab, slab), 1)
    perm = (r == jnp.where(c < LANES, 2 * c, 2 * (c - LANES) + 1)).astype(BF16)
    for j in range(f2 // slab):
        y = _dot(w_ref[0, :, j * slab:(j + 1) * slab].astype(BF16), perm)
        o_ref[0, :, j * LANES:(j + 1) * LANES] = y[:, :LANES].astype(BF16)
        o_ref[0, :, f2 // 2 + j * LANES:f2 // 2 + (j + 1) * LANES] = y[:, LANES:].astype(BF16)


def _deinterleave(w):
    e, d, f2 = w.shape
    return pl.pallas_call(
        _deinterleave_kernel,
        grid=(e,),
        in_specs=[pl.BlockSpec((1, d, f2), lambda i: (i, 0, 0))],
        out_specs=pl.BlockSpec((1, d, f2), lambda i: (i, 0, 0)),
        out_shape=jax.ShapeDtypeStruct((e, d, f2), BF16),
        compiler_params=_cparams(("parallel",)),
    )(w)


def _expert_kernel(blk_ref, exp_ref, lo_ref, hi_ref, xs_ref, wgu_ref, bgu_ref, wd_ref, bd_ref, ys_ref):
    del blk_ref, exp_ref
    rows = xs_ref.shape[0] // LANE_TILES
    f = wd_ref.shape[1]
    i = pl.program_id(0)
    lo = lo_ref[i]
    hi = hi_ref[i]

    def mlp(row0, nrows):
        x = _load_rows(xs_ref, row0, nrows).astype(BF16)
        gu = _dot(x, wgu_ref[0]) + bgu_ref[0]
        gate = jnp.minimum(gu[:, :f], SWIGLU_LIMIT)
        up = jnp.clip(gu[:, f:], -SWIGLU_LIMIT, SWIGLU_LIMIT)
        act = (up + 1.0) * gate * jax.nn.sigmoid(SWIGLU_ALPHA * gate)
        return _dot(act.astype(BF16), wd_ref[0]) + bd_ref[0]

    whole = jnp.logical_and(lo == 0, hi == rows)

    @pl.when(whole)
    def _():
        _store_rows(ys_ref, mlp(0, rows))

    @pl.when(jnp.logical_and(hi > lo, jnp.logical_not(whole)))
    def _():
        @pl.when(lo == 0)
        def _():
            ys_ref[...] = jnp.zeros_like(ys_ref)

        part = rows // EXPERT_PARTS
        for p in range(EXPERT_PARTS):
            @pl.when(jnp.logical_and(lo < (p + 1) * part, hi > p * part))
            def _(p=p):
                r = lax.broadcasted_iota(jnp.int32, (part, LANE_TILES * LANES), 0) + p * part
                mine = (r >= lo) & (r < hi)
                _store_rows(ys_ref, jnp.where(mine, mlp(p * part, part), _load_rows(ys_ref, p * part, part)), p * part)


def _experts(items, xs2, w_gu, b_gu, w_down, b_down):
    item_blk, item_exp, item_lo, item_hi = items
    d, f2 = w_gu.shape[1], w_gu.shape[2]
    wmap = lambda i, blk, exp, lo, hi: (exp[i], 0, 0)
    xmap = lambda i, blk, exp, lo, hi: (blk[i], 0)
    grid_spec = pltpu.PrefetchScalarGridSpec(
        num_scalar_prefetch=4,
        grid=(item_blk.shape[0],),
        in_specs=[pl.BlockSpec(_row_table_shape(MOE_ROWS), xmap),
                  pl.BlockSpec((1, d, f2), wmap),
                  pl.BlockSpec((1, 1, f2), wmap),
                  pl.BlockSpec((1, f2 // 2, d), wmap),
                  pl.BlockSpec((1, 1, d), wmap)],
        out_specs=pl.BlockSpec(_row_table_shape(MOE_ROWS), xmap),
    )
    return pl.pallas_call(
        _expert_kernel,
        grid_spec=grid_spec,
        out_shape=jax.ShapeDtypeStruct(xs2.shape, F32),
        compiler_params=_cparams(("arbitrary",)),
    )(item_blk, item_exp, item_lo, item_hi, xs2, w_gu, b_gu, w_down, b_down)


def _expert_work_items(counts, n_rows):
    ends = jnp.cumsum(counts)
    starts = ends - counts
    n_blk = n_rows // MOE_ROWS
    cuts = jnp.sort(jnp.concatenate([jnp.arange(n_blk, dtype=jnp.int32) * MOE_ROWS, starts[1:]]))
    nxt = jnp.concatenate([cuts[1:], jnp.full((1,), n_rows, jnp.int32)])
    blk = jnp.minimum(cuts // MOE_ROWS, n_blk - 1)
    exp = jnp.minimum(jnp.sum(ends[None, :] <= cuts[:, None], axis=1), N_EXPERTS - 1).astype(jnp.int32)
    lo = cuts - blk * MOE_ROWS
    hi = nxt - blk * MOE_ROWS
    return (blk.astype(jnp.int32), exp, lo.astype(jnp.int32), hi.astype(jnp.int32)), starts


def _combine_kernel(cnt_ref, far_ref, near_ref, pos_ref, gates_ref, x1_ref, mod_ref, fg_ref, ys_hbm,
                    op_ref, os_ref, buf_ref, y_ref, sem, *, n_prompt_tiles):
    tt = x1_ref.shape[0]
    pairs = TOP_K * tt
    i = pl.program_id(0)
    buf = i % 2
    base = buf * pairs

    def fetch(tile, b):
        def copy(near, far, rows):
            return _row_copy(_rows_of(ys_hbm, far, rows), _rows_of(buf_ref, b * pairs + near, rows), sem.at[b])

        _segment_copies(cnt_ref, far_ref, near_ref, tile, copy)

    pl.when(i == 0)(lambda: fetch(0, 0))
    pl.when(i + 1 < pl.num_programs(0))(lambda: fetch(i + 1, 1 - buf))
    _row_copy(_rows_of(ys_hbm, 0, pairs), _rows_of(buf_ref, base, pairs), sem.at[buf]).wait()

    def weighted_sum(t, carry):
        acc = None
        for k in range(TOP_K):
            term = gates_ref[0, 0, k * tt + t] * buf_ref[_row_slice(base + pos_ref[0, 0, k * tt + t]), :]
            acc = term if acc is None else acc + term
        y_ref[_row_slice(t), :] = acc
        return carry

    lax.fori_loop(0, tt, weighted_sum, 0, unroll=ROW_LOOP_UNROLL)

    x2 = x1_ref[...] + mod_ref[0, 5:6, :] * _load_rows(y_ref, 0, tt)
    ms = jnp.mean(x2 * x2, axis=-1, keepdims=True)
    out = x2 * lax.rsqrt(ms + EPS) * fg_ref[...]

    def write(stream):
        (op_ref, os_ref)[stream][...] = out

    _per_stream(n_prompt_tiles, write)


def _combine(seg, pos, gates, x1, mod3, final_g, ys, batch_of_tile, n_prompt):
    n, d = x1.shape
    tt = TOKEN_TILE
    n_prompt_tiles = n_prompt // tt
    grid_spec = pltpu.PrefetchScalarGridSpec(
        num_scalar_prefetch=3,
        grid=(n // tt,),
        in_specs=[pl.BlockSpec((1, 1, TOP_K * tt), lambda i, *_: (i, 0, 0), memory_space=pltpu.SMEM),
                  pl.BlockSpec((1, 1, TOP_K * tt), lambda i, *_: (i, 0, 0), memory_space=pltpu.SMEM),
                  pl.BlockSpec((tt, d), lambda i, *_: (i, 0)),
                  pl.BlockSpec((1, 6, d), lambda i, *_: (batch_of_tile(i), 0, 0)),
                  pl.BlockSpec((1, d), lambda i, *_: (0, 0)),
                  pl.BlockSpec(memory_space=pl.ANY)],
        out_specs=list(_stream_specs((tt, d), n_prompt_tiles)),
        scratch_shapes=[pltpu.VMEM(_row_table_shape(2 * TOP_K * tt), F32),
                        pltpu.VMEM(_row_table_shape(tt), F32),
                        pltpu.SemaphoreType.DMA((2,))],
    )
    return pl.pallas_call(
        functools.partial(_combine_kernel, n_prompt_tiles=n_prompt_tiles),
        grid_spec=grid_spec,
        out_shape=[jax.ShapeDtypeStruct((n_prompt, d), F32),
                   jax.ShapeDtypeStruct((n - n_prompt, d), F32)],
        compiler_params=_cparams(("arbitrary",)),
    )(*seg, pos, gates, x1, mod3, final_g.reshape(1, d), ys)


def _batch_of_tile_fn(tile, bp, tp, ts):
    n_prompt = bp * tp

    def batch_of_tile(i):
        t = i * tile
        return jnp.where(t < n_prompt, t // tp, bp + (t - n_prompt) // ts)

    return batch_of_tile


def kernel(x_prompt, x_sample, c_prompt, c_sample, norm1_g, w_ada, b_ada, w_in, gla_w2_fwd, gla_b2_fwd, gla_w2_bwd, gla_b2_bwd, gla_norm_g, nat_rpb, w_out, norm2_g, router_w, router_b, w_gate_up, b_gate_up, w_down, b_down, final_norm_g):
    assert w_ada.shape[0] == 1, "single-layer encoder"
    bp, tp, d = x_prompt.shape
    bs, ts, _ = x_sample.shape
    n_prompt, n_sample = bp * tp, bs * ts
    n = n_prompt + n_sample
    assert tp % TOKEN_TILE == 0 and ts % TOKEN_TILE == 0 and n_prompt % ts == 0
    assert d == LANE_TILES * LANES

    x = (x_prompt.reshape(n_prompt, d), x_sample.reshape(n_sample, d))
    c = jnp.concatenate([c_prompt, c_sample], axis=0)

    sizes = (GLA_QK, GLA_QK, GLA_V, GLA_V, GLA_RANK, GLA_RANK, NAT_W, NAT_W, NAT_W)
    offs = np.concatenate([[0], np.cumsum(sizes)])
    w_in0 = w_in[0]
    seg = lambda j: w_in0[:, offs[j]:offs[j + 1]]
    w_all = jnp.concatenate(
        [seg(0), seg(1), seg(2), seg(3), seg(6) * (NAT_HD ** -0.5 * LOG2E), seg(7), seg(8), seg(4), seg(5),
         jnp.zeros((d, LANES - 2 * GLA_RANK), F32)], axis=1).astype(BF16)
    pairs = GLA_HEADS // 2
    w2 = jnp.zeros((LANES, pairs, 2, 2 * GLA_DK), F32)
    w2 = w2.at[0:GLA_RANK, :, 0].set(gla_w2_fwd[0].reshape(GLA_RANK, pairs, 2 * GLA_DK))
    w2 = w2.at[GLA_RANK:2 * GLA_RANK, :, 1].set(gla_w2_bwd[0].reshape(GLA_RANK, pairs, 2 * GLA_DK))
    wz = w2.reshape(LANES, pairs * 4 * GLA_DK).astype(BF16)
    bz = jnp.stack([gla_b2_fwd[0].reshape(pairs, 2 * GLA_DK), gla_b2_bwd[0].reshape(pairs, 2 * GLA_DK)],
                   axis=1).reshape(1, pairs * 4 * GLA_DK)
    tab = _nat_bias_table(nat_rpb[0])
    w_og = w_out[0, :GLA_V].astype(BF16)
    w_on = w_out[0, GLA_V:].astype(BF16)
    rw_t = router_w[0].T
    w_gu = _deinterleave(w_gate_up[0])
    b_gu = jnp.concatenate([b_gate_up[0, :, None, 0::2], b_gate_up[0, :, None, 1::2]], axis=-1)
    w_dn = w_down[0].astype(BF16)
    b_dn = b_down[0, :, None, :]

    mod3 = _ada(c, w_ada[0], b_ada[0]).reshape(bp + bs, 6, d)
    bot_tok = _batch_of_tile_fn(TOKEN_TILE, bp, tp, ts)

    pg, pn, plr = _in_proj(*x, mod3, norm1_g[0], w_all, bot_tok)

    gla_args = (pg, plr, wz, bz, gla_norm_g[0].reshape(1, -1))
    og = (_gla(*gla_args, seq=tp, batches=bp, block0=0),
          _gla(*gla_args, seq=ts, batches=bs, block0=n_prompt // ts))
    on = (_nat(pn, tab, seq=tp, batches=bp, block0=0),
          _nat(pn, tab, seq=ts, batches=bs, block0=n_prompt // ts))

    x1, h2, top_i, gates, rank, cnt, tile_cnt, tile_carry = _out_proj(
        og, on, x, mod3, w_og, w_on, norm2_g[0], rw_t, router_b[0], bot_tok)

    n_rows = n * TOP_K
    n_tiles = n // TOKEN_TILE
    assert n_rows % MOE_ROWS == 0
    counts = cnt[:, 0].astype(jnp.int32)
    items, starts = _expert_work_items(counts, n_rows)
    seg_cnt = tile_cnt[:, :, 0].astype(jnp.int32)
    seg_far = starts[None, :] + tile_carry[:, :, 0].astype(jnp.int32)
    seg_near = jnp.cumsum(seg_cnt, axis=1) - seg_cnt
    seg = tuple(a.reshape(n_tiles * N_EXPERTS) for a in (seg_cnt, seg_far, seg_near))
    chosen = top_i[None] == jnp.arange(N_EXPERTS, dtype=jnp.int32)[:, None, None, None]
    pos = rank + jnp.sum(jnp.where(chosen, seg_near.T[:, :, None, None], 0), axis=0)
    per_tile = lambda a: a.reshape(n_tiles, 1, TOP_K * TOKEN_TILE)
    pos, gates = per_tile(pos), per_tile(gates)

    xs = _dispatch(seg, pos, h2)
    ys = _experts(items, xs, w_gu, b_gu, w_dn, b_dn)
    y_prompt, y_sample = _combine(seg, pos, gates, x1, mod3, final_norm_g, ys, bot_tok, n_prompt)

    return (y_prompt.reshape(bp, tp, d), y_sample.reshape(bs, ts, d))
```

```python
import functools

import numpy as np
import jax
import jax.numpy as jnp
from jax import lax
from jax.experimental import pallas as pl
from jax.experimental.pallas import tpu as pltpu

F32 = jnp.float32
BF16 = jnp.bfloat16
HIGHEST = lax.Precision.HIGHEST

EPS = 1e-5
GRID_W = 64
GLA_HEADS = 4
GLA_DK = 64
GLA_DV = 128
GLA_RANK = 16
GLA_TAU = 16.0
GLA_CHUNK = 64
GLA_QK = GLA_HEADS * GLA_DK
GLA_V = GLA_HEADS * GLA_DV
NAT_HEADS = 16
NAT_HD = 32
NAT_W = NAT_HEADS * NAT_HD
NAT_KH = 8
NAT_KW = 16
NAT_GROUP = 4
N_EXPERTS = 32
TOP_K = 4
SWIGLU_LIMIT = 7.0
SWIGLU_ALPHA = 1.702

LANES = 128
SUBLANES = 8
TOKEN_TILE = 512
MOE_ROWS = 1024
EXPERT_PARTS = 8
ADA_COLS = 1024
GLA_UNROLL = 16
NAT_UNROLL = 16
ROW_LOOP_UNROLL = 8
SEGMENT_CHUNK = 32
LOG2E = 1.4426950408889634
VMEM_LIMIT = 56 * 1024 * 1024


def _cparams(sem, vmem=VMEM_LIMIT):
    return pltpu.CompilerParams(dimension_semantics=sem, vmem_limit_bytes=vmem)


def _dot(a, b):
    return jnp.dot(a, b, preferred_element_type=F32)


def _dot_nt(a, b, precision=None):
    return lax.dot_general(a, b, (((1,), (1,)), ((), ())), preferred_element_type=F32, precision=precision)


LANE_TILES = 8


def _row_table_shape(rows):
    return (rows * LANE_TILES, LANES)


def _store_rows(ref, x, row0=0):
    rows = x.shape[0]
    for s in range(LANE_TILES):
        ref[pl.ds(row0 * LANE_TILES + s, rows, stride=LANE_TILES), :] = x[:, s * LANES:(s + 1) * LANES]


def _load_rows(ref, row0, rows):
    return jnp.concatenate(
        [ref[pl.ds(row0 * LANE_TILES + s, rows, stride=LANE_TILES), :] for s in range(LANE_TILES)], axis=1)


def _row_slice(row0, rows=1):
    return pl.ds(pl.multiple_of(row0 * LANE_TILES, LANE_TILES), rows * LANE_TILES)


def _rows_of(ref, row0, rows):
    return ref.at[_row_slice(row0, rows), :]


def _dot_tn(a, b):
    return lax.dot_general(a, b, (((0,), (0,)), ((), ())), preferred_element_type=F32)


def _ada_kernel(c_ref, w_ref, b_ref, o_ref):
    c = c_ref[...]
    s = c * jax.nn.sigmoid(c)
    o_ref[...] = jnp.dot(s, w_ref[...], preferred_element_type=F32, precision=HIGHEST) + b_ref[...]


def _ada(c, w, b):
    nb, d = c.shape
    cols = w.shape[1]
    blk = ADA_COLS
    return pl.pallas_call(
        _ada_kernel,
        grid=(cols // blk,),
        in_specs=[pl.BlockSpec((nb, d), lambda j: (0, 0)),
                  pl.BlockSpec((d, blk), lambda j: (0, j)),
                  pl.BlockSpec((1, blk), lambda j: (0, j))],
        out_specs=pl.BlockSpec((nb, blk), lambda j: (0, j)),
        out_shape=jax.ShapeDtypeStruct((nb, cols), F32),
        compiler_params=_cparams(("arbitrary",)),
    )(c, w, b.reshape(1, cols))


def _stream_specs(block, n_prompt_tiles):
    return (pl.BlockSpec(block, lambda i, *_: (jnp.minimum(i, n_prompt_tiles - 1), 0)),
            pl.BlockSpec(block, lambda i, *_: (jnp.maximum(i - n_prompt_tiles, 0), 0)))


def _per_stream(n_prompt_tiles, body):
    i = pl.program_id(0)
    pl.when(i < n_prompt_tiles)(functools.partial(body, 0))
    pl.when(i >= n_prompt_tiles)(functools.partial(body, 1))


def _in_kernel(xp_ref, xs_ref, mod_ref, g_ref, w_ref, pg_ref, pn_ref, plr_ref, *, n_prompt_tiles):
    def body(stream):
        x = (xp_ref, xs_ref)[stream][...]
        ms = jnp.mean(x * x, axis=-1, keepdims=True)
        y = x * lax.rsqrt(ms + EPS) * g_ref[...]
        h = y * (1.0 + mod_ref[0, 1:2, :]) + mod_ref[0, 0:1, :]
        hb = h.astype(BF16)
        wg = GLA_QK * 2 + GLA_V * 2
        wn = 3 * NAT_W
        pg_ref[...] = _dot(hb, w_ref[:, 0:wg]).astype(BF16)
        pn_ref[...] = _dot(hb, w_ref[:, wg:wg + wn]).astype(BF16)
        plr_ref[...] = _dot(hb, w_ref[:, wg + wn:wg + wn + LANES])

    _per_stream(n_prompt_tiles, body)


def _in_proj(xp, xs, mod3, norm_g, w_all, batch_of_tile):
    d = xp.shape[1]
    n = xp.shape[0] + xs.shape[0]
    wg = GLA_QK * 2 + GLA_V * 2
    wn = 3 * NAT_W
    tm = TOKEN_TILE
    n_prompt_tiles = xp.shape[0] // tm
    return pl.pallas_call(
        functools.partial(_in_kernel, n_prompt_tiles=n_prompt_tiles),
        grid=(n // tm,),
        in_specs=[*_stream_specs((tm, d), n_prompt_tiles),
                  pl.BlockSpec((1, 6, d), lambda i: (batch_of_tile(i), 0, 0)),
                  pl.BlockSpec((1, d), lambda i: (0, 0)),
                  pl.BlockSpec(w_all.shape, lambda i: (0, 0))],
        out_specs=[pl.BlockSpec((tm, wg), lambda i: (i, 0)),
                   pl.BlockSpec((tm, wn), lambda i: (i, 0)),
                   pl.BlockSpec((tm, LANES), lambda i: (i, 0))],
        out_shape=[jax.ShapeDtypeStruct((n, wg), BF16),
                   jax.ShapeDtypeStruct((n, wn), BF16),
                   jax.ShapeDtypeStruct((n, LANES), F32)],
        compiler_params=_cparams(("arbitrary",)),
    )(xp, xs, mod3, norm_g.reshape(1, d), w_all)


def _gla_kernel(q_ref, k_ref, v_ref, g_ref, lr_ref, wz_ref, bz_ref, ng_ref,
                o_ref, la_ref, acc_ref, qt_ref, ke_ref, dec_ref, sf_ref, sb_ref, *, seq):
    c_len = GLA_CHUNK
    n = seq // c_len
    row = lax.broadcasted_iota(jnp.int32, (c_len, c_len), 0)
    col = lax.broadcasted_iota(jnp.int32, (c_len, c_len), 1)
    tril = col <= row
    triu = col >= row
    cum_row = lax.broadcasted_iota(jnp.int32, (c_len, 2 * c_len), 0)
    cum_col = lax.broadcasted_iota(jnp.int32, (c_len, 2 * c_len), 1) % c_len
    cum = (cum_col <= cum_row).astype(F32).astype(BF16)
    mask_f = jnp.concatenate([tril, tril], axis=0)
    mask_b = jnp.concatenate([triu, triu], axis=0)
    head_a = lax.broadcasted_iota(jnp.int32, (c_len, LANES), 1) < GLA_DK
    st_row = lax.broadcasted_iota(jnp.int32, (2 * GLA_DV, LANES), 0)
    st_col = lax.broadcasted_iota(jnp.int32, (2 * GLA_DV, LANES), 1)
    blockdiag = (st_row < GLA_DV) == (st_col < GLA_DK)

    z = _dot(lr_ref[...].astype(BF16), wz_ref[...]) + bz_ref[...]
    la_ref[...] = (jnp.minimum(z, 0.0) - jnp.log(1.0 + jnp.exp(-jnp.abs(z)))) * (1.0 / GLA_TAU)

    def chunk_rows(c):
        return pl.ds(pl.multiple_of(c * c_len, c_len), c_len)

    def stack_heads(x):
        zero = jnp.zeros_like(x)
        return jnp.concatenate([jnp.where(head_a, x, zero), jnp.where(head_a, zero, x)], axis=0).astype(BF16)

    def local(c, carry):
        rows = chunk_rows(c)
        la = la_ref[rows, :]
        hi = la.astype(BF16)
        lo = (la - hi.astype(F32)).astype(BF16)
        binc = _dot(cum, jnp.concatenate([hi, lo], axis=0))
        btot = jnp.sum(la, axis=0, keepdims=True)
        b_f = binc[:, :LANES]
        b_b = btot[:, LANES:] - binc[:, LANES:] + la[:, LANES:]
        e_tot = jnp.exp(btot)
        q = q_ref[rows, :].astype(F32) * (GLA_DK ** -0.5)
        k = k_ref[rows, :].astype(F32)
        qt_f = q * jnp.exp(b_f)
        qt_b = q * jnp.exp(b_b)
        kt_f = k * jnp.exp(-b_f)
        kt_b = k * jnp.exp(-b_b)
        a = (jnp.where(mask_f, _dot_nt(stack_heads(qt_f), kt_f.astype(BF16)), 0.0)
             + jnp.where(mask_b, _dot_nt(stack_heads(qt_b), kt_b.astype(BF16)), 0.0))
        oi = _dot(a.astype(BF16), v_ref[rows, :])
        acc_ref[rows, :] = jnp.concatenate([oi[0:c_len, 0:GLA_DV], oi[c_len:, GLA_DV:]], axis=1)
        qt_ref[rows, :] = jnp.concatenate([qt_f, qt_b], axis=1).astype(BF16)
        ke_ref[rows, :] = jnp.concatenate([kt_f * e_tot[:, :LANES], kt_b * e_tot[:, LANES:]], axis=1).astype(BF16)
        dec_ref[pl.ds(pl.multiple_of(c * SUBLANES, SUBLANES), SUBLANES), :] = jnp.broadcast_to(e_tot, (SUBLANES, 2 * LANES))
        return carry

    def carried(c, s_ref, half):
        rows = chunk_rows(c)
        lanes = slice(half * LANES, (half + 1) * LANES)
        st = s_ref[...]
        acc_ref[rows, :] += _dot_nt(qt_ref[rows, lanes], st.astype(BF16))
        upd = _dot_tn(v_ref[rows, :], ke_ref[rows, lanes])
        dec = dec_ref[pl.ds(pl.multiple_of(c * SUBLANES, SUBLANES), 1), lanes]
        s_ref[...] = st * dec + jnp.where(blockdiag, upd, 0.0)

    def scan(c, carry):
        carried(c, sf_ref, 0)
        carried(n - 1 - c, sb_ref, 1)
        return carry

    def finalize(c, carry):
        rows = chunk_rows(c)
        o = acc_ref[rows, :]
        halves = []
        for hh in range(2):
            oh = o[:, hh * GLA_DV:(hh + 1) * GLA_DV]
            ms = jnp.mean(oh * oh, axis=-1, keepdims=True)
            halves.append(oh * lax.rsqrt(ms + EPS) * ng_ref[...])
        y = jnp.concatenate(halves, axis=1)
        g = g_ref[rows, :].astype(F32)
        o_ref[rows, :] = (y * (g * jax.nn.sigmoid(g))).astype(o_ref.dtype)
        return carry

    sf_ref[...] = jnp.zeros_like(sf_ref)
    sb_ref[...] = jnp.zeros_like(sb_ref)
    lax.fori_loop(0, n, local, 0, unroll=GLA_UNROLL)
    lax.fori_loop(0, n, scan, 0, unroll=GLA_UNROLL)
    lax.fori_loop(0, n, finalize, 0, unroll=GLA_UNROLL)


def _gla(pg, plr, wz, bz, norm_g, *, seq, batches, block0):
    pair_w = 2 * GLA_DK
    pair_v = 2 * GLA_DV
    k0 = GLA_QK // pair_w
    v0 = 2 * GLA_QK // pair_v
    g0 = (2 * GLA_QK + GLA_V) // pair_v
    n_chunks = seq // GLA_CHUNK
    in_specs = [pl.BlockSpec((seq, pair_w), lambda b, p: (block0 + b, p)),
                pl.BlockSpec((seq, pair_w), lambda b, p: (block0 + b, k0 + p)),
                pl.BlockSpec((seq, pair_v), lambda b, p: (block0 + b, v0 + p)),
                pl.BlockSpec((seq, pair_v), lambda b, p: (block0 + b, g0 + p)),
                pl.BlockSpec((seq, LANES), lambda b, p: (block0 + b, 0)),
                pl.BlockSpec((LANES, 2 * pair_w), lambda b, p: (0, p)),
                pl.BlockSpec((1, 2 * pair_w), lambda b, p: (0, p)),
                pl.BlockSpec((1, GLA_DV), lambda b, p: (0, 0))]
    return pl.pallas_call(
        functools.partial(_gla_kernel, seq=seq),
        grid=(batches, GLA_HEADS // 2),
        in_specs=in_specs,
        out_specs=pl.BlockSpec((seq, pair_v), lambda b, p: (b, p)),
        out_shape=jax.ShapeDtypeStruct((batches * seq, GLA_V), BF16),
        scratch_shapes=[pltpu.VMEM((seq, 2 * pair_w), F32),
                        pltpu.VMEM((seq, pair_v), F32),
                        pltpu.VMEM((seq, 2 * pair_w), BF16),
                        pltpu.VMEM((seq, 2 * pair_w), BF16),
                        pltpu.VMEM((n_chunks * SUBLANES, 2 * pair_w), F32),
                        pltpu.VMEM((pair_v, pair_w), F32),
                        pltpu.VMEM((pair_v, pair_w), F32)],
        compiler_params=_cparams(("parallel", "parallel")),
    )(pg, pg, pg, pg, plr, wz, bz, norm_g)


def _nat_kernel(q_ref, k_ref, v_ref, tab_ref, o_ref, s_ref, *, rows):
    w = GRID_W
    head = lax.broadcasted_iota(jnp.int32, (w, LANES), 1) // NAT_HD
    win = NAT_KH * w

    def window(i):
        r0 = jnp.clip(i - NAT_KH // 2, 0, rows - NAT_KH)
        return r0 - i + NAT_KH - 1, pl.ds(pl.multiple_of(r0 * w, w), win)

    def scores(i, slot):
        base, krows = window(i)
        q = q_ref[pl.ds(pl.multiple_of(i * w, w), w), :]
        zero = jnp.zeros_like(q)
        q_stack = jnp.concatenate([jnp.where(head == h, q, zero) for h in range(NAT_GROUP)], axis=0)
        s_ref[slot] = _dot_nt(q_stack, k_ref[krows, :]) + tab_ref[0, base]

    def attend(i, slot):
        _, krows = window(i)
        s = s_ref[slot]
        m = jnp.max(s, axis=-1, keepdims=True)
        e = jnp.exp2(s - m)
        l = jnp.sum(e, axis=-1, keepdims=True)
        o = _dot(e.astype(BF16), v_ref[krows, :]) / l
        out = jnp.zeros((w, LANES), F32)
        for h in range(NAT_GROUP):
            out = out + jnp.where(head == h, o[h * w:(h + 1) * w, :], 0.0)
        o_ref[pl.ds(pl.multiple_of(i * w, w), w), :] = out.astype(o_ref.dtype)

    scores(0, 0)

    def body(j, carry):
        for u in range(2):
            i = 2 * j + u
            scores(jnp.minimum(i + 1, rows - 1), 1 - u)
            attend(i, u)
        return carry

    lax.fori_loop(0, rows // 2, body, 0, unroll=NAT_UNROLL // 2)


def _nat(pn, tab, *, seq, batches, block0):
    rows = seq // GRID_W
    groups = NAT_HEADS // NAT_GROUP
    in_specs = [pl.BlockSpec((seq, LANES), lambda g, b: (block0 + b, g)),
                pl.BlockSpec((seq, LANES), lambda g, b: (block0 + b, groups + g)),
                pl.BlockSpec((seq, LANES), lambda g, b: (block0 + b, 2 * groups + g)),
                pl.BlockSpec((1,) + tab.shape[1:], lambda g, b: (g, 0, 0, 0))]
    return pl.pallas_call(
        functools.partial(_nat_kernel, rows=rows),
        grid=(groups, batches),
        in_specs=in_specs,
        out_specs=pl.BlockSpec((seq, LANES), lambda g, b: (b, g)),
        out_shape=jax.ShapeDtypeStruct((batches * seq, NAT_W), BF16),
        scratch_shapes=[pltpu.VMEM((2, NAT_GROUP * GRID_W, NAT_KH * GRID_W), F32)],
        compiler_params=_cparams(("parallel", "parallel")),
    )(pn, pn, pn, tab)


def _nat_bias_table(rpb):
    w = GRID_W
    jq = np.arange(w)[:, None]
    jk = np.arange(w)[None, :]
    c0 = np.clip(jq - NAT_KW // 2, 0, w - NAT_KW)
    valid = (jk >= c0) & (jk < c0 + NAT_KW)
    dcol = np.clip(jk - jq + NAT_KW - 1, 0, 2 * NAT_KW - 2)
    select = (dcol[None] == np.arange(2 * NAT_KW - 1)[:, None, None]).astype(np.float32)
    t = jnp.einsum('hrc,cqk->hrqk', rpb.astype(F32) * LOG2E, select, precision=HIGHEST)
    t = jnp.where(valid[None, None], t, -jnp.inf)
    t = jnp.stack([t[:, b:b + NAT_KH] for b in range(NAT_KH)], axis=1)
    t = t.transpose(0, 1, 3, 2, 4).reshape(NAT_HEADS, NAT_KH, w, NAT_KH * w)
    t = t.reshape(NAT_HEADS // NAT_GROUP, NAT_GROUP, NAT_KH, w, NAT_KH * w).transpose(0, 2, 1, 3, 4)
    return t.reshape(NAT_HEADS // NAT_GROUP, NAT_KH, NAT_GROUP * w, NAT_KH * w)


def _out_kernel(ogp_ref, ogs_ref, onp_ref, ons_ref, xp_ref, xs_ref, mod_ref, wg_ref, wn_ref, n2_ref, rw_ref, rb_ref,
                x1_ref, h2_ref, ti_ref, gt_ref, rk_ref, cnt_ref, tcnt_ref, tcar_ref, carry_ref, *, n_prompt_tiles):
    tm = x1_ref.shape[0]

    @pl.when(pl.program_id(0) == 0)
    def _():
        carry_ref[...] = jnp.zeros_like(carry_ref)

    def residual(stream):
        og_ref, on_ref, x_ref = ((ogp_ref, onp_ref, xp_ref), (ogs_ref, ons_ref, xs_ref))[stream]
        mix = _dot(og_ref[...], wg_ref[...]) + _dot(on_ref[...], wn_ref[...])
        x1_ref[...] = x_ref[...] + mod_ref[0, 2:3, :] * mix

    _per_stream(n_prompt_tiles, residual)
    x1 = x1_ref[...]
    ms = jnp.mean(x1 * x1, axis=-1, keepdims=True)
    h2 = x1 * lax.rsqrt(ms + EPS) * n2_ref[...]
    h2 = h2 * (1.0 + mod_ref[0, 4:5, :]) + mod_ref[0, 3:4, :]
    _store_rows(h2_ref, h2)

    logits = _dot_nt(rw_ref[...], h2, precision=HIGHEST) + rb_ref[...]
    eidx = lax.broadcasted_iota(jnp.int32, logits.shape, 0)
    vals = logits
    top_v, top_i = [], []
    for _ in range(TOP_K):
        m = jnp.max(vals, axis=0, keepdims=True)
        idx = jnp.min(jnp.where(vals == m, eidx, N_EXPERTS), axis=0, keepdims=True)
        top_v.append(m)
        top_i.append(idx)
        vals = jnp.where(eidx == idx, -jnp.inf, vals)
    ev = [jnp.exp(v - top_v[0]) for v in top_v]
    den = ev[0] + ev[1] + ev[2] + ev[3]
    ti_ref[0] = jnp.concatenate(top_i, axis=0)
    gt_ref[0] = jnp.concatenate([e / den for e in ev], axis=0)

    onehot = jnp.zeros(logits.shape, F32)
    for idx in top_i:
        onehot = onehot + (eidx == idx).astype(F32)
    r = lax.broadcasted_iota(jnp.int32, (tm, tm), 0)
    c = lax.broadcasted_iota(jnp.int32, (tm, tm), 1)
    earlier = (r < c).astype(BF16)
    before = _dot(onehot.astype(BF16), earlier)
    rk_ref[0] = jnp.concatenate(
        [jnp.sum(jnp.where(eidx == idx, before, 0.0), axis=0, keepdims=True) for idx in top_i], axis=0).astype(jnp.int32)
    tile_counts = jnp.sum(onehot, axis=1, keepdims=True)
    tcar_ref[0] = jnp.broadcast_to(carry_ref[...], tcar_ref.shape[1:])
    tcnt_ref[0] = jnp.broadcast_to(tile_counts, tcnt_ref.shape[1:])
    carry_ref[...] = carry_ref[...] + tile_counts
    cnt_ref[...] = jnp.broadcast_to(carry_ref[...], cnt_ref.shape)


def _out_proj(og, on, x, mod3, w_g, w_n, norm_g, rw_t, rb, batch_of_tile):
    d = x[0].shape[1]
    n = x[0].shape[0] + x[1].shape[0]
    tm = TOKEN_TILE
    n_prompt_tiles = x[0].shape[0] // tm
    tok = lambda i: (i, 0)
    const = lambda i: (0, 0)
    return pl.pallas_call(
        functools.partial(_out_kernel, n_prompt_tiles=n_prompt_tiles),
        grid=(n // tm,),
        in_specs=[*_stream_specs((tm, GLA_V), n_prompt_tiles),
                  *_stream_specs((tm, NAT_W), n_prompt_tiles),
                  *_stream_specs((tm, d), n_prompt_tiles),
                  pl.BlockSpec((1, 6, d), lambda i: (batch_of_tile(i), 0, 0)),
                  pl.BlockSpec(w_g.shape, const),
                  pl.BlockSpec(w_n.shape, const),
                  pl.BlockSpec((1, d), const),
                  pl.BlockSpec(rw_t.shape, const),
                  pl.BlockSpec((N_EXPERTS, 1), const)],
        out_specs=[pl.BlockSpec((tm, d), tok),
                   pl.BlockSpec(_row_table_shape(tm), tok),
                   pl.BlockSpec((1, TOP_K, tm), lambda i: (i, 0, 0)),
                   pl.BlockSpec((1, TOP_K, tm), lambda i: (i, 0, 0)),
                   pl.BlockSpec((1, TOP_K, tm), lambda i: (i, 0, 0)),
                   pl.BlockSpec((N_EXPERTS, LANES), const),
                   pl.BlockSpec((1, N_EXPERTS, LANES), lambda i: (i, 0, 0)),
                   pl.BlockSpec((1, N_EXPERTS, LANES), lambda i: (i, 0, 0))],
        out_shape=[jax.ShapeDtypeStruct((n, d), F32),
                   jax.ShapeDtypeStruct(_row_table_shape(n), F32),
                   jax.ShapeDtypeStruct((n // tm, TOP_K, tm), jnp.int32),
                   jax.ShapeDtypeStruct((n // tm, TOP_K, tm), F32),
                   jax.ShapeDtypeStruct((n // tm, TOP_K, tm), jnp.int32),
                   jax.ShapeDtypeStruct((N_EXPERTS, LANES), F32),
                   jax.ShapeDtypeStruct((n // tm, N_EXPERTS, LANES), F32),
                   jax.ShapeDtypeStruct((n // tm, N_EXPERTS, LANES), F32)],
        scratch_shapes=[pltpu.VMEM((N_EXPERTS, 1), F32)],
        compiler_params=_cparams(("arbitrary",)),
    )(*og, *on, *x, mod3, w_g, w_n, norm_g.reshape(1, d), rw_t, rb.reshape(N_EXPERTS, 1))


def _row_copy(src, dst, sem):
    return pltpu.make_async_copy(src, dst, sem)


def _segment_copies(cnt_ref, far_ref, near_ref, tile, copy):
    def per_expert(e, carry):
        idx = tile * N_EXPERTS + e
        c, far, near = cnt_ref[idx], far_ref[idx], near_ref[idx]

        def whole(i, carry2):
            copy(near + i * SEGMENT_CHUNK, far + i * SEGMENT_CHUNK, SEGMENT_CHUNK).start()
            return carry2

        lax.fori_loop(0, c // SEGMENT_CHUNK, whole, 0)
        for bit in reversed(range(SEGMENT_CHUNK.bit_length() - 1)):
            size = 1 << bit
            done = c & -(2 * size)

            @pl.when((c & size) != 0)
            def _(done=done, size=size):
                copy(near + done, far + done, size).start()

        return carry

    lax.fori_loop(0, N_EXPERTS, per_expert, 0)


def _dispatch_kernel(cnt_ref, far_ref, near_ref, pos_ref, h_ref, xs_hbm, buf_ref, sem):
    pairs = pos_ref.shape[2]
    tt = pairs // TOP_K
    i = pl.program_id(0)
    buf = i % 2
    base = buf * pairs

    def place(t, carry):
        row = h_ref[_row_slice(t), :]
        for k in range(TOP_K):
            buf_ref[_row_slice(base + pos_ref[0, 0, k * tt + t]), :] = row
        return carry

    lax.fori_loop(0, tt, place, 0, unroll=ROW_LOOP_UNROLL)

    def copy(near, far, rows):
        return _row_copy(_rows_of(buf_ref, base + near, rows), _rows_of(xs_hbm, far, rows), sem.at[buf])

    _segment_copies(cnt_ref, far_ref, near_ref, i, copy)

    def drain(b):
        _row_copy(_rows_of(buf_ref, b * pairs, pairs), _rows_of(xs_hbm, 0, pairs), sem.at[b]).wait()

    pl.when(i > 0)(lambda: drain(1 - buf))
    pl.when(i == pl.num_programs(0) - 1)(lambda: drain(buf))


def _dispatch(seg, pos, h2):
    n = h2.shape[0] // LANE_TILES
    tt = TOKEN_TILE
    grid_spec = pltpu.PrefetchScalarGridSpec(
        num_scalar_prefetch=3,
        grid=(n // tt,),
        in_specs=[pl.BlockSpec((1, 1, TOP_K * tt), lambda i, *_: (i, 0, 0), memory_space=pltpu.SMEM),
                  pl.BlockSpec(_row_table_shape(tt), lambda i, *_: (i, 0))],
        out_specs=pl.BlockSpec(memory_space=pl.ANY),
        scratch_shapes=[pltpu.VMEM(_row_table_shape(2 * TOP_K * tt), F32),
                        pltpu.SemaphoreType.DMA((2,))],
    )
    return pl.pallas_call(
        _dispatch_kernel,
        grid_spec=grid_spec,
        out_shape=jax.ShapeDtypeStruct(_row_table_shape(n * TOP_K), F32),
        compiler_params=_cparams(("arbitrary",)),
    )(*seg, pos, h2)


def _deinterleave_kernel(w_ref, o_ref):
    f2 = w_ref.shape[2]
    slab = 2 * LANES
    r = lax.broadcasted_iota(jnp.int32, (slab, slab), 0)
    c = lax.broadcasted_iota(jnp.int32, (slab, slab), 1)
    perm = (r == jnp.where(c < LANES, 2 * c, 2 * (c - LANES) + 1)).astype(BF16)
    for j in range(f2 // slab):
        y = _dot(w_ref[0, :, j * slab:(j + 1) * slab].astype(BF16), perm)
        o_ref[0, :, j * LANES:(j + 1) * LANES] = y[:, :LANES].astype(BF16)
        o_ref[0, :, f2 // 2 + j * LANES:f2 // 2 + (j + 1) * LANES] = y[:, LANES:].astype(BF16)


def _deinterleave(w):
    e, d, f2 = w.shape
    return pl.pallas_call(
        _deinterleave_kernel,
        grid=(e,),
        in_specs=[pl.BlockSpec((1, d, f2), lambda i: (i, 0, 0))],
        out_specs=pl.BlockSpec((1, d, f2), lambda i: (i, 0, 0)),
        out_shape=jax.ShapeDtypeStruct((e, d, f2), BF16),
        compiler_params=_cparams(("parallel",)),
    )(w)


def _expert_kernel(blk_ref, exp_ref, lo_ref, hi_ref, xs_ref, wgu_ref, bgu_ref, wd_ref, bd_ref, ys_ref):
    del blk_ref, exp_ref
    rows = xs_ref.shape[0] // LANE_TILES
    f = wd_ref.shape[1]
    i = pl.program_id(0)
    lo = lo_ref[i]
    hi = hi_ref[i]

    def mlp(row0, nrows):
        x = _load_rows(xs_ref, row0, nrows).astype(BF16)
        gu = _dot(x, wgu_ref[0]) + bgu_ref[0]
        gate = jnp.minimum(gu[:, :f], SWIGLU_LIMIT)
        up = jnp.clip(gu[:, f:], -SWIGLU_LIMIT, SWIGLU_LIMIT)
        act = (up + 1.0) * gate * jax.nn.sigmoid(SWIGLU_ALPHA * gate)
        return _dot(act.astype(BF16), wd_ref[0]) + bd_ref[0]

    whole = jnp.logical_and(lo == 0, hi == rows)

    @pl.when(whole)
    def _():
        _store_rows(ys_ref, mlp(0, rows))

    @pl.when(jnp.logical_and(hi > lo, jnp.logical_not(whole)))
    def _():
        @pl.when(lo == 0)
        def _():
            ys_ref[...] = jnp.zeros_like(ys_ref)

        part = rows // EXPERT_PARTS
        for p in range(EXPERT_PARTS):
            @pl.when(jnp.logical_and(lo < (p + 1) * part, hi > p * part))
            def _(p=p):
                r = lax.broadcasted_iota(jnp.int32, (part, LANE_TILES * LANES), 0) + p * part
                mine = (r >= lo) & (r < hi)
                _store_rows(ys_ref, jnp.where(mine, mlp(p * part, part), _load_rows(ys_ref, p * part, part)), p * part)


def _experts(items, xs2, w_gu, b_gu, w_down, b_down):
    item_blk, item_exp, item_lo, item_hi = items
    d, f2 = w_gu.shape[1], w_gu.shape[2]
    wmap = lambda i, blk, exp, lo, hi: (exp[i], 0, 0)
    xmap = lambda i, blk, exp, lo, hi: (blk[i], 0)
    grid_spec = pltpu.PrefetchScalarGridSpec(
        num_scalar_prefetch=4,
        grid=(item_blk.shape[0],),
        in_specs=[pl.BlockSpec(_row_table_shape(MOE_ROWS), xmap),
                  pl.BlockSpec((1, d, f2), wmap),
                  pl.BlockSpec((1, 1, f2), wmap),
                  pl.BlockSpec((1, f2 // 2, d), wmap),
                  pl.BlockSpec((1, 1, d), wmap)],
        out_specs=pl.BlockSpec(_row_table_shape(MOE_ROWS), xmap),
    )
    return pl.pallas_call(
        _expert_kernel,
        grid_spec=grid_spec,
        out_shape=jax.ShapeDtypeStruct(xs2.shape, F32),
        compiler_params=_cparams(("arbitrary",)),
    )(item_blk, item_exp, item_lo, item_hi, xs2, w_gu, b_gu, w_down, b_down)


def _expert_work_items(counts, n_rows):
    ends = jnp.cumsum(counts)
    starts = ends - counts
    n_blk = n_rows // MOE_ROWS
    cuts = jnp.sort(jnp.concatenate([jnp.arange(n_blk, dtype=jnp.int32) * MOE_ROWS, starts[1:]]))
    nxt = jnp.concatenate([cuts[1:], jnp.full((1,), n_rows, jnp.int32)])
    blk = jnp.minimum(cuts // MOE_ROWS, n_blk - 1)
    exp = jnp.minimum(jnp.sum(ends[None, :] <= cuts[:, None], axis=1), N_EXPERTS - 1).astype(jnp.int32)
    lo = cuts - blk * MOE_ROWS
    hi = nxt - blk * MOE_ROWS
    return (blk.astype(jnp.int32), exp, lo.astype(jnp.int32), hi.astype(jnp.int32)), starts


def _combine_kernel(cnt_ref, far_ref, near_ref, pos_ref, gates_ref, x1_ref, mod_ref, fg_ref, ys_hbm,
                    op_ref, os_ref, buf_ref, y_ref, sem, *, n_prompt_tiles):
    tt = x1_ref.shape[0]
    pairs = TOP_K * tt
    i = pl.program_id(0)
    buf = i % 2
    base = buf * pairs

    def fetch(tile, b):
        def copy(near, far, rows):
            return _row_copy(_rows_of(ys_hbm, far, rows), _rows_of(buf_ref, b * pairs + near, rows), sem.at[b])

        _segment_copies(cnt_ref, far_ref, near_ref, tile, copy)

    pl.when(i == 0)(lambda: fetch(0, 0))
    pl.when(i + 1 < pl.num_programs(0))(lambda: fetch(i + 1, 1 - buf))
    _row_copy(_rows_of(ys_hbm, 0, pairs), _rows_of(buf_ref, base, pairs), sem.at[buf]).wait()

    def weighted_sum(t, carry):
        acc = None
        for k in range(TOP_K):
            term = gates_ref[0, 0, k * tt + t] * buf_ref[_row_slice(base + pos_ref[0, 0, k * tt + t]), :]
            acc = term if acc is None else acc + term
        y_ref[_row_slice(t), :] = acc
        return carry

    lax.fori_loop(0, tt, weighted_sum, 0, unroll=ROW_LOOP_UNROLL)

    x2 = x1_ref[...] + mod_ref[0, 5:6, :] * _load_rows(y_ref, 0, tt)
    ms = jnp.mean(x2 * x2, axis=-1, keepdims=True)
    out = x2 * lax.rsqrt(ms + EPS) * fg_ref[...]

    def write(stream):
        (op_ref, os_ref)[stream][...] = out

    _per_stream(n_prompt_tiles, write)


def _combine(seg, pos, gates, x1, mod3, final_g, ys, batch_of_tile, n_prompt):
    n, d = x1.shape
    tt = TOKEN_TILE
    n_prompt_tiles = n_prompt // tt
    grid_spec = pltpu.PrefetchScalarGridSpec(
        num_scalar_prefetch=3,
        grid=(n // tt,),
        in_specs=[pl.BlockSpec((1, 1, TOP_K * tt), lambda i, *_: (i, 0, 0), memory_space=pltpu.SMEM),
                  pl.BlockSpec((1, 1, TOP_K * tt), lambda i, *_: (i, 0, 0), memory_space=pltpu.SMEM),
                  pl.BlockSpec((tt, d), lambda i, *_: (i, 0)),
                  pl.BlockSpec((1, 6, d), lambda i, *_: (batch_of_tile(i), 0, 0)),
                  pl.BlockSpec((1, d), lambda i, *_: (0, 0)),
                  pl.BlockSpec(memory_space=pl.ANY)],
        out_specs=list(_stream_specs((tt, d), n_prompt_tiles)),
        scratch_shapes=[pltpu.VMEM(_row_table_shape(2 * TOP_K * tt), F32),
                        pltpu.VMEM(_row_table_shape(tt), F32),
                        pltpu.SemaphoreType.DMA((2,))],
    )
    return pl.pallas_call(
        functools.partial(_combine_kernel, n_prompt_tiles=n_prompt_tiles),
        grid_spec=grid_spec,
        out_shape=[jax.ShapeDtypeStruct((n_prompt, d), F32),
                   jax.ShapeDtypeStruct((n - n_prompt, d), F32)],
        compiler_params=_cparams(("arbitrary",)),
    )(*seg, pos, gates, x1, mod3, final_g.reshape(1, d), ys)


def _batch_of_tile_fn(tile, bp, tp, ts):
    n_prompt = bp * tp

    def batch_of_tile(i):
        t = i * tile
        return jnp.where(t < n_prompt, t // tp, bp + (t - n_prompt) // ts)

    return batch_of_tile


def kernel(x_prompt, x_sample, c_prompt, c_sample, norm1_g, w_ada, b_ada, w_in, gla_w2_fwd, gla_b2_fwd, gla_w2_bwd, gla_b2_bwd, gla_norm_g, nat_rpb, w_out, norm2_g, router_w, router_b, w_gate_up, b_gate_up, w_down, b_down, final_norm_g):
    assert w_ada.shape[0] == 1, "single-layer encoder"
    bp, tp, d = x_prompt.shape
    bs, ts, _ = x_sample.shape
    n_prompt, n_sample = bp * tp, bs * ts
    n = n_prompt + n_sample
    assert tp % TOKEN_TILE == 0 and ts % TOKEN_TILE == 0 and n_prompt % ts == 0
    assert d == LANE_TILES * LANES

    x = (x_prompt.reshape(n_prompt, d), x_sample.reshape(n_sample, d))
    c = jnp.concatenate([c_prompt, c_sample], axis=0)

    sizes = (GLA_QK, GLA_QK, GLA_V, GLA_V, GLA_RANK, GLA_RANK, NAT_W, NAT_W, NAT_W)
    offs = np.concatenate([[0], np.cumsum(sizes)])
    w_in0 = w_in[0]
    seg = lambda j: w_in0[:, offs[j]:offs[j + 1]]
    w_all = jnp.concatenate(
        [seg(0), seg(1), seg(2), seg(3), seg(6) * (NAT_HD ** -0.5 * LOG2E), seg(7), seg(8), seg(4), seg(5),
         jnp.zeros((d, LANES - 2 * GLA_RANK), F32)], axis=1).astype(BF16)
    pairs = GLA_HEADS // 2
    w2 = jnp.zeros((LANES, pairs, 2, 2 * GLA_DK), F32)
    w2 = w2.at[0:GLA_RANK, :, 0].set(gla_w2_fwd[0].reshape(GLA_RANK, pairs, 2 * GLA_DK))
    w2 = w2.at[GLA_RANK:2 * GLA_RANK, :, 1].set(gla_w2_bwd[0].reshape(GLA_RANK, pairs, 2 * GLA_DK))
    wz = w2.reshape(LANES, pairs * 4 * GLA_DK).astype(BF16)
    bz = jnp.stack([gla_b2_fwd[0].reshape(pairs, 2 * GLA_DK), gla_b2_bwd[0].reshape(pairs, 2 * GLA_DK)],
                   axis=1).reshape(1, pairs * 4 * GLA_DK)
    tab = _nat_bias_table(nat_rpb[0])
    w_og = w_out[0, :GLA_V].astype(BF16)
    w_on = w_out[0, GLA_V:].astype(BF16)
    rw_t = router_w[0].T
    w_gu = _deinterleave(w_gate_up[0])
    b_gu = jnp.concatenate([b_gate_up[0, :, None, 0::2], b_gate_up[0, :, None, 1::2]], axis=-1)
    w_dn = w_down[0].astype(BF16)
    b_dn = b_down[0, :, None, :]

    mod3 = _ada(c, w_ada[0], b_ada[0]).reshape(bp + bs, 6, d)
    bot_tok = _batch_of_tile_fn(TOKEN_TILE, bp, tp, ts)

    pg, pn, plr = _in_proj(*x, mod3, norm1_g[0], w_all, bot_tok)

    gla_args = (pg, plr, wz, bz, gla_norm_g[0].reshape(1, -1))
    og = (_gla(*gla_args, seq=tp, batches=bp, block0=0),
          _gla(*gla_args, seq=ts, batches=bs, block0=n_prompt // ts))
    on = (_nat(pn, tab, seq=tp, batches=bp, block0=0),
          _nat(pn, tab, seq=ts, batches=bs, block0=n_prompt // ts))

    x1, h2, top_i, gates, rank, cnt, tile_cnt, tile_carry = _out_proj(
        og, on, x, mod3, w_og, w_on, norm2_g[0], rw_t, router_b[0], bot_tok)

    n_rows = n * TOP_K
    n_tiles = n // TOKEN_TILE
    assert n_rows % MOE_ROWS == 0
    counts = cnt[:, 0].astype(jnp.int32)
    items, starts = _expert_work_items(counts, n_rows)
    seg_cnt = tile_cnt[:, :, 0].astype(jnp.int32)
    seg_far = starts[None, :] + tile_carry[:, :, 0].astype(jnp.int32)
    seg_near = jnp.cumsum(seg_cnt, axis=1) - seg_cnt
    seg = tuple(a.reshape(n_tiles * N_EXPERTS) for a in (seg_cnt, seg_far, seg_near))
    chosen = top_i[None] == jnp.arange(N_EXPERTS, dtype=jnp.int32)[:, None, None, None]
    pos = rank + jnp.sum(jnp.where(chosen, seg_near.T[:, :, None, None], 0), axis=0)
    per_tile = lambda a: a.reshape(n_tiles, 1, TOP_K * TOKEN_TILE)
    pos, gates = per_tile(pos), per_tile(gates)

    xs = _dispatch(seg, pos, h2)
    ys = _experts(items, xs, w_gu, b_gu, w_dn, b_dn)
    y_prompt, y_sample = _combine(seg, pos, gates, x1, mod3, final_norm_g, ys, bot_tok, n_prompt)

    return (y_prompt.reshape(bp, tp, d), y_sample.reshape(bs, ts, d))
```

```python
import functools

import numpy as np
import jax
import jax.numpy as jnp
from jax import lax
from jax.experimental import pallas as pl
from jax.experimental.pallas import tpu as pltpu

F32 = jnp.float32
BF16 = jnp.bfloat16
HIGHEST = lax.Precision.HIGHEST

EPS = 1e-5
GRID_W = 64
GLA_HEADS = 4
GLA_DK = 64
GLA_DV = 128
GLA_RANK = 16
GLA_TAU = 16.0
GLA_CHUNK = 64
GLA_QK = GLA_HEADS * GLA_DK
GLA_V = GLA_HEADS * GLA_DV
NAT_HEADS = 16
NAT_HD = 32
NAT_W = NAT_HEADS * NAT_HD
NAT_KH = 8
NAT_KW = 16
NAT_GROUP = 4
N_EXPERTS = 32
TOP_K = 4
SWIGLU_LIMIT = 7.0
SWIGLU_ALPHA = 1.702

LANES = 128
SUBLANES = 8
TOKEN_TILE = 512
MOE_ROWS = 1024
EXPERT_PARTS = 8
ADA_COLS = 1024
GLA_UNROLL = 16
NAT_UNROLL = 16
ROW_LOOP_UNROLL = 8
SEGMENT_CHUNK = 32
LOG2E = 1.4426950408889634
VMEM_LIMIT = 56 * 1024 * 1024


def _cparams(sem, vmem=VMEM_LIMIT):
    return pltpu.CompilerParams(dimension_semantics=sem, vmem_limit_bytes=vmem)


def _dot(a, b):
    return jnp.dot(a, b, preferred_element_type=F32)


def _dot_nt(a, b, precision=None):
    return lax.dot_general(a, b, (((1,), (1,)), ((), ())), preferred_element_type=F32, precision=precision)


LANE_TILES = 8


def _row_table_shape(rows):
    return (rows * LANE_TILES, LANES)


def _store_rows(ref, x, row0=0):
    rows = x.shape[0]
    for s in range(LANE_TILES):
        ref[pl.ds(row0 * LANE_TILES + s, rows, stride=LANE_TILES), :] = x[:, s * LANES:(s + 1) * LANES]


def _load_rows(ref, row0, rows):
    return jnp.concatenate(
        [ref[pl.ds(row0 * LANE_TILES + s, rows, stride=LANE_TILES), :] for s in range(LANE_TILES)], axis=1)


def _row_slice(row0, rows=1):
    return pl.ds(pl.multiple_of(row0 * LANE_TILES, LANE_TILES), rows * LANE_TILES)


def _sublane_slice(sublane0):
    return pl.ds(pl.multiple_of(sublane0, LANE_TILES), LANE_TILES)


def _rows_of(ref, row0, rows):
    return ref.at[_row_slice(row0, rows), :]


def _dot_tn(a, b):
    return lax.dot_general(a, b, (((0,), (0,)), ((), ())), preferred_element_type=F32)


def _ada_kernel(c_ref, w_ref, b_ref, o_ref):
    c = c_ref[...]
    s = c * jax.nn.sigmoid(c)
    o_ref[...] = jnp.dot(s, w_ref[...], preferred_element_type=F32, precision=HIGHEST) + b_ref[...]


def _ada(c, w, b):
    nb, d = c.shape
    cols = w.shape[1]
    blk = ADA_COLS
    return pl.pallas_call(
        _ada_kernel,
        grid=(cols // blk,),
        in_specs=[pl.BlockSpec((nb, d), lambda j: (0, 0)),
                  pl.BlockSpec((d, blk), lambda j: (0, j)),
                  pl.BlockSpec((1, blk), lambda j: (0, j))],
        out_specs=pl.BlockSpec((nb, blk), lambda j: (0, j)),
        out_shape=jax.ShapeDtypeStruct((nb, cols), F32),
        compiler_params=_cparams(("arbitrary",)),
    )(c, w, b.reshape(1, cols))


def _stream_specs(block, n_prompt_tiles):
    return (pl.BlockSpec(block, lambda i, *_: (jnp.minimum(i, n_prompt_tiles - 1), 0)),
            pl.BlockSpec(block, lambda i, *_: (jnp.maximum(i - n_prompt_tiles, 0), 0)))


def _per_stream(n_prompt_tiles, body):
    i = pl.program_id(0)
    pl.when(i < n_prompt_tiles)(functools.partial(body, 0))
    pl.when(i >= n_prompt_tiles)(functools.partial(body, 1))


def _in_kernel(xp_ref, xs_ref, mod_ref, g_ref, w_ref, pg_ref, pn_ref, plr_ref, *, n_prompt_tiles):
    def body(stream):
        x = (xp_ref, xs_ref)[stream][...]
        ms = jnp.mean(x * x, axis=-1, keepdims=True)
        y = x * lax.rsqrt(ms + EPS) * g_ref[...]
        h = y * (1.0 + mod_ref[0, 1:2, :]) + mod_ref[0, 0:1, :]
        hb = h.astype(BF16)
        wg = GLA_QK * 2 + GLA_V * 2
        wn = 3 * NAT_W
        pg_ref[...] = _dot(hb, w_ref[:, 0:wg]).astype(BF16)
        pn_ref[...] = _dot(hb, w_ref[:, wg:wg + wn]).astype(BF16)
        plr_ref[...] = _dot(hb, w_ref[:, wg + wn:wg + wn + LANES])

    _per_stream(n_prompt_tiles, body)


def _in_proj(xp, xs, mod3, norm_g, w_all, batch_of_tile):
    d = xp.shape[1]
    n = xp.shape[0] + xs.shape[0]
    wg = GLA_QK * 2 + GLA_V * 2
    wn = 3 * NAT_W
    tm = TOKEN_TILE
    n_prompt_tiles = xp.shape[0] // tm
    return pl.pallas_call(
        functools.partial(_in_kernel, n_prompt_tiles=n_prompt_tiles),
        grid=(n // tm,),
        in_specs=[*_stream_specs((tm, d), n_prompt_tiles),
                  pl.BlockSpec((1, 6, d), lambda i: (batch_of_tile(i), 0, 0)),
                  pl.BlockSpec((1, d), lambda i: (0, 0)),
                  pl.BlockSpec(w_all.shape, lambda i: (0, 0))],
        out_specs=[pl.BlockSpec((tm, wg), lambda i: (i, 0)),
                   pl.BlockSpec((tm, wn), lambda i: (i, 0)),
                   pl.BlockSpec((tm, LANES), lambda i: (i, 0))],
        out_shape=[jax.ShapeDtypeStruct((n, wg), BF16),
                   jax.ShapeDtypeStruct((n, wn), BF16),
                   jax.ShapeDtypeStruct((n, LANES), F32)],
        compiler_params=_cparams(("arbitrary",)),
    )(xp, xs, mod3, norm_g.reshape(1, d), w_all)


def _gla_kernel(q_ref, k_ref, v_ref, g_ref, lr_ref, wz_ref, bz_ref, ng_ref,
                o_ref, la_ref, acc_ref, qt_ref, ke_ref, dec_ref, sf_ref, sb_ref, *, seq):
    c_len = GLA_CHUNK
    n = seq // c_len
    row = lax.broadcasted_iota(jnp.int32, (c_len, c_len), 0)
    col = lax.broadcasted_iota(jnp.int32, (c_len, c_len), 1)
    tril = col <= row
    triu = col >= row
    cum_row = lax.broadcasted_iota(jnp.int32, (c_len, 2 * c_len), 0)
    cum_col = lax.broadcasted_iota(jnp.int32, (c_len, 2 * c_len), 1) % c_len
    cum = (cum_col <= cum_row).astype(F32).astype(BF16)
    mask_f = jnp.concatenate([tril, tril], axis=0)
    mask_b = jnp.concatenate([triu, triu], axis=0)
    head_a = lax.broadcasted_iota(jnp.int32, (c_len, LANES), 1) < GLA_DK
    st_row = lax.broadcasted_iota(jnp.int32, (2 * GLA_DV, LANES), 0)
    st_col = lax.broadcasted_iota(jnp.int32, (2 * GLA_DV, LANES), 1)
    blockdiag = (st_row < GLA_DV) == (st_col < GLA_DK)

    z = _dot(lr_ref[...].astype(BF16), wz_ref[...]) + bz_ref[...]
    la_ref[...] = (jnp.minimum(z, 0.0) - jnp.log(1.0 + jnp.exp(-jnp.abs(z)))) * (1.0 / GLA_TAU)

    def chunk_rows(c):
        return pl.ds(pl.multiple_of(c * c_len, c_len), c_len)

    def stack_heads(x):
        zero = jnp.zeros_like(x)
        return jnp.concatenate([jnp.where(head_a, x, zero), jnp.where(head_a, zero, x)], axis=0).astype(BF16)

    def local(c, carry):
        rows = chunk_rows(c)
        la = la_ref[rows, :]
        hi = la.astype(BF16)
        lo = (la - hi.astype(F32)).astype(BF16)
        binc = _dot(cum, jnp.concatenate([hi, lo], axis=0))
        btot = jnp.sum(la, axis=0, keepdims=True)
        b_f = binc[:, :LANES]
        b_b = btot[:, LANES:] - binc[:, LANES:] + la[:, LANES:]
        e_tot = jnp.exp(btot)
        q = q_ref[rows, :].astype(F32) * (GLA_DK ** -0.5)
        k = k_ref[rows, :].astype(F32)
        qt_f = q * jnp.exp(b_f)
        qt_b = q * jnp.exp(b_b)
        kt_f = k * jnp.exp(-b_f)
        kt_b = k * jnp.exp(-b_b)
        a = (jnp.where(mask_f, _dot_nt(stack_heads(qt_f), kt_f.astype(BF16)), 0.0)
             + jnp.where(mask_b, _dot_nt(stack_heads(qt_b), kt_b.astype(BF16)), 0.0))
        oi = _dot(a.astype(BF16), v_ref[rows, :])
        acc_ref[rows, :] = jnp.concatenate([oi[0:c_len, 0:GLA_DV], oi[c_len:, GLA_DV:]], axis=1)
        qt_ref[rows, :] = jnp.concatenate([qt_f, qt_b], axis=1).astype(BF16)
        ke_ref[rows, :] = jnp.concatenate([kt_f * e_tot[:, :LANES], kt_b * e_tot[:, LANES:]], axis=1).astype(BF16)
        dec_ref[pl.ds(pl.multiple_of(c * SUBLANES, SUBLANES), SUBLANES), :] = jnp.broadcast_to(e_tot, (SUBLANES, 2 * LANES))
        return carry

    def carried(c, s_ref, half):
        rows = chunk_rows(c)
        lanes = slice(half * LANES, (half + 1) * LANES)
        st = s_ref[...]
        acc_ref[rows, :] += _dot_nt(qt_ref[rows, lanes], st.astype(BF16))
        upd = _dot_tn(v_ref[rows, :], ke_ref[rows, lanes])
        dec = dec_ref[pl.ds(pl.multiple_of(c * SUBLANES, SUBLANES), 1), lanes]
        s_ref[...] = st * dec + jnp.where(blockdiag, upd, 0.0)

    def scan(c, carry):
        carried(c, sf_ref, 0)
        carried(n - 1 - c, sb_ref, 1)
        return carry

    def finalize(c, carry):
        rows = chunk_rows(c)
        o = acc_ref[rows, :]
        halves = []
        for hh in range(2):
            oh = o[:, hh * GLA_DV:(hh + 1) * GLA_DV]
            ms = jnp.mean(oh * oh, axis=-1, keepdims=True)
            halves.append(oh * lax.rsqrt(ms + EPS) * ng_ref[...])
        y = jnp.concatenate(halves, axis=1)
        g = g_ref[rows, :].astype(F32)
        o_ref[rows, :] = (y * (g * jax.nn.sigmoid(g))).astype(o_ref.dtype)
        return carry

    sf_ref[...] = jnp.zeros_like(sf_ref)
    sb_ref[...] = jnp.zeros_like(sb_ref)
    lax.fori_loop(0, n, local, 0, unroll=GLA_UNROLL)
    lax.fori_loop(0, n, scan, 0, unroll=GLA_UNROLL)
    lax.fori_loop(0, n, finalize, 0, unroll=GLA_UNROLL)


def _gla(pg, plr, wz, bz, norm_g, *, seq, batches, block0):
    pair_w = 2 * GLA_DK
    pair_v = 2 * GLA_DV
    k0 = GLA_QK // pair_w
    v0 = 2 * GLA_QK // pair_v
    g0 = (2 * GLA_QK + GLA_V) // pair_v
    n_chunks = seq // GLA_CHUNK
    in_specs = [pl.BlockSpec((seq, pair_w), lambda b, p: (block0 + b, p)),
                pl.BlockSpec((seq, pair_w), lambda b, p: (block0 + b, k0 + p)),
                pl.BlockSpec((seq, pair_v), lambda b, p: (block0 + b, v0 + p)),
                pl.BlockSpec((seq, pair_v), lambda b, p: (block0 + b, g0 + p)),
                pl.BlockSpec((seq, LANES), lambda b, p: (block0 + b, 0)),
                pl.BlockSpec((LANES, 2 * pair_w), lambda b, p: (0, p)),
                pl.BlockSpec((1, 2 * pair_w), lambda b, p: (0, p)),
                pl.BlockSpec((1, GLA_DV), lambda b, p: (0, 0))]
    return pl.pallas_call(
        functools.partial(_gla_kernel, seq=seq),
        grid=(batches, GLA_HEADS // 2),
        in_specs=in_specs,
        out_specs=pl.BlockSpec((seq, pair_v), lambda b, p: (b, p)),
        out_shape=jax.ShapeDtypeStruct((batches * seq, GLA_V), BF16),
        scratch_shapes=[pltpu.VMEM((seq, 2 * pair_w), F32),
                        pltpu.VMEM((seq, pair_v), F32),
                        pltpu.VMEM((seq, 2 * pair_w), BF16),
                        pltpu.VMEM((seq, 2 * pair_w), BF16),
                        pltpu.VMEM((n_chunks * SUBLANES, 2 * pair_w), F32),
                        pltpu.VMEM((pair_v, pair_w), F32),
                        pltpu.VMEM((pair_v, pair_w), F32)],
        compiler_params=_cparams(("parallel", "parallel")),
    )(pg, pg, pg, pg, plr, wz, bz, norm_g)


def _nat_kernel(q_ref, k_ref, v_ref, tab_ref, o_ref, s_ref, *, rows):
    w = GRID_W
    head = lax.broadcasted_iota(jnp.int32, (w, LANES), 1) // NAT_HD
    win = NAT_KH * w

    def window(i):
        r0 = jnp.clip(i - NAT_KH // 2, 0, rows - NAT_KH)
        return r0 - i + NAT_KH - 1, pl.ds(pl.multiple_of(r0 * w, w), win)

    def scores(i, slot):
        base, krows = window(i)
        q = q_ref[pl.ds(pl.multiple_of(i * w, w), w), :]
        zero = jnp.zeros_like(q)
        q_stack = jnp.concatenate([jnp.where(head == h, q, zero) for h in range(NAT_GROUP)], axis=0)
        s_ref[slot] = _dot_nt(q_stack, k_ref[krows, :]) + tab_ref[0, base]

    def attend(i, slot):
        _, krows = window(i)
        s = s_ref[slot]
        m = jnp.max(s, axis=-1, keepdims=True)
        e = jnp.exp2(s - m)
        l = jnp.sum(e, axis=-1, keepdims=True)
        o = _dot(e.astype(BF16), v_ref[krows, :]) / l
        out = jnp.zeros((w, LANES), F32)
        for h in range(NAT_GROUP):
            out = out + jnp.where(head == h, o[h * w:(h + 1) * w, :], 0.0)
        o_ref[pl.ds(pl.multiple_of(i * w, w), w), :] = out.astype(o_ref.dtype)

    scores(0, 0)

    def body(j, carry):
        for u in range(2):
            i = 2 * j + u
            scores(jnp.minimum(i + 1, rows - 1), 1 - u)
            attend(i, u)
        return carry

    lax.fori_loop(0, rows // 2, body, 0, unroll=NAT_UNROLL // 2)


def _nat(pn, tab, *, seq, batches, block0):
    rows = seq // GRID_W
    groups = NAT_HEADS // NAT_GROUP
    in_specs = [pl.BlockSpec((seq, LANES), lambda g, b: (block0 + b, g)),
                pl.BlockSpec((seq, LANES), lambda g, b: (block0 + b, groups + g)),
                pl.BlockSpec((seq, LANES), lambda g, b: (block0 + b, 2 * groups + g)),
                pl.BlockSpec((1,) + tab.shape[1:], lambda g, b: (g, 0, 0, 0))]
    return pl.pallas_call(
        functools.partial(_nat_kernel, rows=rows),
        grid=(groups, batches),
        in_specs=in_specs,
        out_specs=pl.BlockSpec((seq, LANES), lambda g, b: (b, g)),
        out_shape=jax.ShapeDtypeStruct((batches * seq, NAT_W), BF16),
        scratch_shapes=[pltpu.VMEM((2, NAT_GROUP * GRID_W, NAT_KH * GRID_W), F32)],
        compiler_params=_cparams(("parallel", "parallel")),
    )(pn, pn, pn, tab)


def _nat_bias_table(rpb):
    w = GRID_W
    jq = np.arange(w)[:, None]
    jk = np.arange(w)[None, :]
    c0 = np.clip(jq - NAT_KW // 2, 0, w - NAT_KW)
    valid = (jk >= c0) & (jk < c0 + NAT_KW)
    dcol = np.clip(jk - jq + NAT_KW - 1, 0, 2 * NAT_KW - 2)
    select = (dcol[None] == np.arange(2 * NAT_KW - 1)[:, None, None]).astype(np.float32)
    t = jnp.einsum('hrc,cqk->hrqk', rpb.astype(F32) * LOG2E, select, precision=HIGHEST)
    t = jnp.where(valid[None, None], t, -jnp.inf)
    t = jnp.stack([t[:, b:b + NAT_KH] for b in range(NAT_KH)], axis=1)
    t = t.transpose(0, 1, 3, 2, 4).reshape(NAT_HEADS, NAT_KH, w, NAT_KH * w)
    t = t.reshape(NAT_HEADS // NAT_GROUP, NAT_GROUP, NAT_KH, w, NAT_KH * w).transpose(0, 2, 1, 3, 4)
    return t.reshape(NAT_HEADS // NAT_GROUP, NAT_KH, NAT_GROUP * w, NAT_KH * w)


def _out_kernel(ogp_ref, ogs_ref, onp_ref, ons_ref, xp_ref, xs_ref, mod_ref, wg_ref, wn_ref, n2_ref, rw_ref, rb_ref,
                x1_ref, h2_ref, ti_ref, gt_ref, rk_ref, cnt_ref, tcnt_ref, tcar_ref, carry_ref, *, n_prompt_tiles):
    tm = x1_ref.shape[0]

    @pl.when(pl.program_id(0) == 0)
    def _():
        carry_ref[...] = jnp.zeros_like(carry_ref)

    def residual(stream):
        og_ref, on_ref, x_ref = ((ogp_ref, onp_ref, xp_ref), (ogs_ref, ons_ref, xs_ref))[stream]
        mix = _dot(og_ref[...], wg_ref[...]) + _dot(on_ref[...], wn_ref[...])
        x1_ref[...] = x_ref[...] + mod_ref[0, 2:3, :] * mix

    _per_stream(n_prompt_tiles, residual)
    x1 = x1_ref[...]
    ms = jnp.mean(x1 * x1, axis=-1, keepdims=True)
    h2 = x1 * lax.rsqrt(ms + EPS) * n2_ref[...]
    h2 = h2 * (1.0 + mod_ref[0, 4:5, :]) + mod_ref[0, 3:4, :]
    _store_rows(h2_ref, h2)

    logits = _dot_nt(rw_ref[...], h2, precision=HIGHEST) + rb_ref[...]
    eidx = lax.broadcasted_iota(jnp.int32, logits.shape, 0)
    vals = logits
    top_v, top_i = [], []
    for _ in range(TOP_K):
        m = jnp.max(vals, axis=0, keepdims=True)
        idx = jnp.min(jnp.where(vals == m, eidx, N_EXPERTS), axis=0, keepdims=True)
        top_v.append(m)
        top_i.append(idx)
        vals = jnp.where(eidx == idx, -jnp.inf, vals)
    ev = [jnp.exp(v - top_v[0]) for v in top_v]
    den = ev[0] + ev[1] + ev[2] + ev[3]
    ti_ref[0] = jnp.concatenate(top_i, axis=0)
    gt_ref[0] = jnp.concatenate([e / den for e in ev], axis=0)

    onehot = jnp.zeros(logits.shape, F32)
    for idx in top_i:
        onehot = onehot + (eidx == idx).astype(F32)
    r = lax.broadcasted_iota(jnp.int32, (tm, tm), 0)
    c = lax.broadcasted_iota(jnp.int32, (tm, tm), 1)
    earlier = (r < c).astype(BF16)
    before = _dot(onehot.astype(BF16), earlier)
    rk_ref[0] = jnp.concatenate(
        [jnp.sum(jnp.where(eidx == idx, before, 0.0), axis=0, keepdims=True) for idx in top_i], axis=0).astype(jnp.int32)
    tile_counts = jnp.sum(onehot, axis=1, keepdims=True)
    tcar_ref[0] = jnp.broadcast_to(carry_ref[...], tcar_ref.shape[1:])
    tcnt_ref[0] = jnp.broadcast_to(tile_counts, tcnt_ref.shape[1:])
    carry_ref[...] = carry_ref[...] + tile_counts
    cnt_ref[...] = jnp.broadcast_to(carry_ref[...], cnt_ref.shape)


def _out_proj(og, on, x, mod3, w_g, w_n, norm_g, rw_t, rb, batch_of_tile):
    d = x[0].shape[1]
    n = x[0].shape[0] + x[1].shape[0]
    tm = TOKEN_TILE
    n_prompt_tiles = x[0].shape[0] // tm
    tok = lambda i: (i, 0)
    const = lambda i: (0, 0)
    return pl.pallas_call(
        functools.partial(_out_kernel, n_prompt_tiles=n_prompt_tiles),
        grid=(n // tm,),
        in_specs=[*_stream_specs((tm, GLA_V), n_prompt_tiles),
                  *_stream_specs((tm, NAT_W), n_prompt_tiles),
                  *_stream_specs((tm, d), n_prompt_tiles),
                  pl.BlockSpec((1, 6, d), lambda i: (batch_of_tile(i), 0, 0)),
                  pl.BlockSpec(w_g.shape, const),
                  pl.BlockSpec(w_n.shape, const),
                  pl.BlockSpec((1, d), const),
                  pl.BlockSpec(rw_t.shape, const),
                  pl.BlockSpec((N_EXPERTS, 1), const)],
        out_specs=[pl.BlockSpec((tm, d), tok),
                   pl.BlockSpec(_row_table_shape(tm), tok),
                   pl.BlockSpec((1, TOP_K, tm), lambda i: (i, 0, 0)),
                   pl.BlockSpec((1, TOP_K, tm), lambda i: (i, 0, 0)),
                   pl.BlockSpec((1, TOP_K, tm), lambda i: (i, 0, 0)),
                   pl.BlockSpec((N_EXPERTS, LANES), const),
                   pl.BlockSpec((1, N_EXPERTS, LANES), lambda i: (i, 0, 0)),
                   pl.BlockSpec((1, N_EXPERTS, LANES), lambda i: (i, 0, 0))],
        out_shape=[jax.ShapeDtypeStruct((n, d), F32),
                   jax.ShapeDtypeStruct(_row_table_shape(n), F32),
                   jax.ShapeDtypeStruct((n // tm, TOP_K, tm), jnp.int32),
                   jax.ShapeDtypeStruct((n // tm, TOP_K, tm), F32),
                   jax.ShapeDtypeStruct((n // tm, TOP_K, tm), jnp.int32),
                   jax.ShapeDtypeStruct((N_EXPERTS, LANES), F32),
                   jax.ShapeDtypeStruct((n // tm, N_EXPERTS, LANES), F32),
                   jax.ShapeDtypeStruct((n // tm, N_EXPERTS, LANES), F32)],
        scratch_shapes=[pltpu.VMEM((N_EXPERTS, 1), F32)],
        compiler_params=_cparams(("arbitrary",)),
    )(*og, *on, *x, mod3, w_g, w_n, norm_g.reshape(1, d), rw_t, rb.reshape(N_EXPERTS, 1))


def _row_copy(src, dst, sem):
    return pltpu.make_async_copy(src, dst, sem)


def _segment_copies(cnt_ref, far_ref, near_ref, tile, copy):
    def per_expert(e, carry):
        idx = tile * N_EXPERTS + e
        c, far, near = cnt_ref[idx], far_ref[idx], near_ref[idx]

        def whole(i, carry2):
            copy(near + i * SEGMENT_CHUNK, far + i * SEGMENT_CHUNK, SEGMENT_CHUNK).start()
            return carry2

        lax.fori_loop(0, c // SEGMENT_CHUNK, whole, 0)
        for bit in reversed(range(SEGMENT_CHUNK.bit_length() - 1)):
            size = 1 << bit
            done = c & -(2 * size)

            @pl.when((c & size) != 0)
            def _(done=done, size=size):
                copy(near + done, far + done, size).start()

        return carry

    lax.fori_loop(0, N_EXPERTS, per_expert, 0)


def _dispatch_kernel(cnt_ref, far_ref, near_ref, pos_ref, h_ref, xs_hbm, buf_ref, sem):
    pairs = pos_ref.shape[2]
    tt = pairs // TOP_K
    i = pl.program_id(0)
    buf = i % 2
    base = buf * pairs

    def place(t, carry):
        row = h_ref[_row_slice(t), :]
        for k in range(TOP_K):
            buf_ref[_sublane_slice(base * LANE_TILES + pos_ref[0, 0, k * tt + t]), :] = row
        return carry

    lax.fori_loop(0, tt, place, 0, unroll=ROW_LOOP_UNROLL)

    def copy(near, far, rows):
        return _row_copy(_rows_of(buf_ref, base + near, rows), _rows_of(xs_hbm, far, rows), sem.at[buf])

    _segment_copies(cnt_ref, far_ref, near_ref, i, copy)

    def drain(b):
        _row_copy(_rows_of(buf_ref, b * pairs, pairs), _rows_of(xs_hbm, 0, pairs), sem.at[b]).wait()

    pl.when(i > 0)(lambda: drain(1 - buf))
    pl.when(i == pl.num_programs(0) - 1)(lambda: drain(buf))


def _dispatch(seg, pos, h2):
    n = h2.shape[0] // LANE_TILES
    tt = TOKEN_TILE
    grid_spec = pltpu.PrefetchScalarGridSpec(
        num_scalar_prefetch=3,
        grid=(n // tt,),
        in_specs=[pl.BlockSpec((1, 1, TOP_K * tt), lambda i, *_: (i, 0, 0), memory_space=pltpu.SMEM),
                  pl.BlockSpec(_row_table_shape(tt), lambda i, *_: (i, 0))],
        out_specs=pl.BlockSpec(memory_space=pl.ANY),
        scratch_shapes=[pltpu.VMEM(_row_table_shape(2 * TOP_K * tt), F32),
                        pltpu.SemaphoreType.DMA((2,))],
    )
    return pl.pallas_call(
        _dispatch_kernel,
        grid_spec=grid_spec,
        out_shape=jax.ShapeDtypeStruct(_row_table_shape(n * TOP_K), F32),
        compiler_params=_cparams(("arbitrary",)),
    )(*seg, pos, h2)


def _deinterleave_kernel(w_ref, o_ref):
    f2 = w_ref.shape[2]
    slab = 2 * LANES
    r = lax.broadcasted_iota(jnp.int32, (slab, slab), 0)
    c = lax.broadcasted_iota(jnp.int32, (slab, slab), 1)
    perm = (r == jnp.where(c < LANES, 2 * c, 2 * (c - LANES) + 1)).astype(BF16)
    for j in range(f2 // slab):
        y = _dot(w_ref[0, :, j * slab:(j + 1) * slab].astype(BF16), perm)
        o_ref[0, :, j * LANES:(j + 1) * LANES] = y[:, :LANES].astype(BF16)
        o_ref[0, :, f2 // 2 + j * LANES:f2 // 2 + (j + 1) * LANES] = y[:, LANES:].astype(BF16)


def _deinterleave(w):
    e, d, f2 = w.shape
    return pl.pallas_call(
        _deinterleave_kernel,
        grid=(e,),
        in_specs=[pl.BlockSpec((1, d, f2), lambda i: (i, 0, 0))],
        out_specs=pl.BlockSpec((1, d, f2), lambda i: (i, 0, 0)),
        out_shape=jax.ShapeDtypeStruct((e, d, f2), BF16),
        compiler_params=_cparams(("parallel",)),
    )(w)


def _expert_kernel(blk_ref, exp_ref, lo_ref, hi_ref, xs_ref, wgu_ref, bgu_ref, wd_ref, bd_ref, ys_ref):
    del blk_ref, exp_ref
    rows = xs_ref.shape[0] // LANE_TILES
    f = wd_ref.shape[1]
    i = pl.program_id(0)
    lo = lo_ref[i]
    hi = hi_ref[i]

    def mlp(row0, nrows):
        x = _load_rows(xs_ref, row0, nrows).astype(BF16)
        gu = _dot(x, wgu_ref[0]) + bgu_ref[0]
        gate = jnp.minimum(gu[:, :f], SWIGLU_LIMIT)
        up = jnp.clip(gu[:, f:], -SWIGLU_LIMIT, SWIGLU_LIMIT)
        act = (up + 1.0) * gate * jax.nn.sigmoid(SWIGLU_ALPHA * gate)
        return _dot(act.astype(BF16), wd_ref[0]) + bd_ref[0]

    whole = jnp.logical_and(lo == 0, hi == rows)

    @pl.when(whole)
    def _():
        _store_rows(ys_ref, mlp(0, rows))

    @pl.when(jnp.logical_and(hi > lo, jnp.logical_not(whole)))
    def _():
        @pl.when(lo == 0)
        def _():
            ys_ref[...] = jnp.zeros_like(ys_ref)

        part = rows // EXPERT_PARTS
        for p in range(EXPERT_PARTS):
            @pl.when(jnp.logical_and(lo < (p + 1) * part, hi > p * part))
            def _(p=p):
                r = lax.broadcasted_iota(jnp.int32, (part, LANE_TILES * LANES), 0) + p * part
                mine = (r >= lo) & (r < hi)
                _store_rows(ys_ref, jnp.where(mine, mlp(p * part, part), _load_rows(ys_ref, p * part, part)), p * part)


def _experts(items, xs2, w_gu, b_gu, w_down, b_down):
    item_blk, item_exp, item_lo, item_hi = items
    d, f2 = w_gu.shape[1], w_gu.shape[2]
    wmap = lambda i, blk, exp, lo, hi: (exp[i], 0, 0)
    xmap = lambda i, blk, exp, lo, hi: (blk[i], 0)
    grid_spec = pltpu.PrefetchScalarGridSpec(
        num_scalar_prefetch=4,
        grid=(item_blk.shape[0],),
        in_specs=[pl.BlockSpec(_row_table_shape(MOE_ROWS), xmap),
                  pl.BlockSpec((1, d, f2), wmap),
                  pl.BlockSpec((1, 1, f2), wmap),
                  pl.BlockSpec((1, f2 // 2, d), wmap),
                  pl.BlockSpec((1, 1, d), wmap)],
        out_specs=pl.BlockSpec(_row_table_shape(MOE_ROWS), xmap),
    )
    return pl.pallas_call(
        _expert_kernel,
        grid_spec=grid_spec,
        out_shape=jax.ShapeDtypeStruct(xs2.shape, F32),
        compiler_params=_cparams(("arbitrary",)),
    )(item_blk, item_exp, item_lo, item_hi, xs2, w_gu, b_gu, w_down, b_down)


def _expert_work_items(counts, n_rows):
    ends = jnp.cumsum(counts)
    starts = ends - counts
    n_blk = n_rows // MOE_ROWS
    cuts = jnp.sort(jnp.concatenate([jnp.arange(n_blk, dtype=jnp.int32) * MOE_ROWS, starts[1:]]))
    nxt = jnp.concatenate([cuts[1:], jnp.full((1,), n_rows, jnp.int32)])
    blk = jnp.minimum(cuts // MOE_ROWS, n_blk - 1)
    exp = jnp.minimum(jnp.sum(ends[None, :] <= cuts[:, None], axis=1), N_EXPERTS - 1).astype(jnp.int32)
    lo = cuts - blk * MOE_ROWS
    hi = nxt - blk * MOE_ROWS
    return (blk.astype(jnp.int32), exp, lo.astype(jnp.int32), hi.astype(jnp.int32)), starts


def _combine_kernel(cnt_ref, far_ref, near_ref, pos_ref, gates_ref, x1_ref, mod_ref, fg_ref, ys_hbm,
                    op_ref, os_ref, buf_ref, y_ref, sem, *, n_prompt_tiles):
    tt = x1_ref.shape[0]
    pairs = TOP_K * tt
    i = pl.program_id(0)
    buf = i % 2
    base = buf * pairs

    def fetch(tile, b):
        def copy(near, far, rows):
            return _row_copy(_rows_of(ys_hbm, far, rows), _rows_of(buf_ref, b * pairs + near, rows), sem.at[b])

        _segment_copies(cnt_ref, far_ref, near_ref, tile, copy)

    pl.when(i == 0)(lambda: fetch(0, 0))
    pl.when(i + 1 < pl.num_programs(0))(lambda: fetch(i + 1, 1 - buf))
    _row_copy(_rows_of(ys_hbm, 0, pairs), _rows_of(buf_ref, base, pairs), sem.at[buf]).wait()

    def weighted_sum(t, carry):
        acc = None
        for k in range(TOP_K):
            row = buf_ref[_sublane_slice(base * LANE_TILES + pos_ref[0, 0, k * tt + t]), :]
            term = gates_ref[0, 0, k * tt + t] * row
            acc = term if acc is None else acc + term
        y_ref[_row_slice(t), :] = acc
        return carry

    lax.fori_loop(0, tt, weighted_sum, 0, unroll=ROW_LOOP_UNROLL)

    x2 = x1_ref[...] + mod_ref[0, 5:6, :] * _load_rows(y_ref, 0, tt)
    ms = jnp.mean(x2 * x2, axis=-1, keepdims=True)
    out = x2 * lax.rsqrt(ms + EPS) * fg_ref[...]

    def write(stream):
        (op_ref, os_ref)[stream][...] = out

    _per_stream(n_prompt_tiles, write)


def _combine(seg, pos, gates, x1, mod3, final_g, ys, batch_of_tile, n_prompt):
    n, d = x1.shape
    tt = TOKEN_TILE
    n_prompt_tiles = n_prompt // tt
    grid_spec = pltpu.PrefetchScalarGridSpec(
        num_scalar_prefetch=3,
        grid=(n // tt,),
        in_specs=[pl.BlockSpec((1, 1, TOP_K * tt), lambda i, *_: (i, 0, 0), memory_space=pltpu.SMEM),
                  pl.BlockSpec((1, 1, TOP_K * tt), lambda i, *_: (i, 0, 0), memory_space=pltpu.SMEM),
                  pl.BlockSpec((tt, d), lambda i, *_: (i, 0)),
                  pl.BlockSpec((1, 6, d), lambda i, *_: (batch_of_tile(i), 0, 0)),
                  pl.BlockSpec((1, d), lambda i, *_: (0, 0)),
                  pl.BlockSpec(memory_space=pl.ANY)],
        out_specs=list(_stream_specs((tt, d), n_prompt_tiles)),
        scratch_shapes=[pltpu.VMEM(_row_table_shape(2 * TOP_K * tt), F32),
                        pltpu.VMEM(_row_table_shape(tt), F32),
                        pltpu.SemaphoreType.DMA((2,))],
    )
    return pl.pallas_call(
        functools.partial(_combine_kernel, n_prompt_tiles=n_prompt_tiles),
        grid_spec=grid_spec,
        out_shape=[jax.ShapeDtypeStruct((n_prompt, d), F32),
                   jax.ShapeDtypeStruct((n - n_prompt, d), F32)],
        compiler_params=_cparams(("arbitrary",)),
    )(*seg, pos, gates, x1, mod3, final_g.reshape(1, d), ys)


def _batch_of_tile_fn(tile, bp, tp, ts):
    n_prompt = bp * tp

    def batch_of_tile(i):
        t = i * tile
        return jnp.where(t < n_prompt, t // tp, bp + (t - n_prompt) // ts)

    return batch_of_tile


def kernel(x_prompt, x_sample, c_prompt, c_sample, norm1_g, w_ada, b_ada, w_in, gla_w2_fwd, gla_b2_fwd, gla_w2_bwd, gla_b2_bwd, gla_norm_g, nat_rpb, w_out, norm2_g, router_w, router_b, w_gate_up, b_gate_up, w_down, b_down, final_norm_g):
    assert w_ada.shape[0] == 1, "single-layer encoder"
    bp, tp, d = x_prompt.shape
    bs, ts, _ = x_sample.shape
    n_prompt, n_sample = bp * tp, bs * ts
    n = n_prompt + n_sample
    assert tp % TOKEN_TILE == 0 and ts % TOKEN_TILE == 0 and n_prompt % ts == 0
    assert d == LANE_TILES * LANES

    x = (x_prompt.reshape(n_prompt, d), x_sample.reshape(n_sample, d))
    c = jnp.concatenate([c_prompt, c_sample], axis=0)

    sizes = (GLA_QK, GLA_QK, GLA_V, GLA_V, GLA_RANK, GLA_RANK, NAT_W, NAT_W, NAT_W)
    offs = np.concatenate([[0], np.cumsum(sizes)])
    w_in0 = w_in[0]
    seg = lambda j: w_in0[:, offs[j]:offs[j + 1]]
    w_all = jnp.concatenate(
        [seg(0), seg(1), seg(2), seg(3), seg(6) * (NAT_HD ** -0.5 * LOG2E), seg(7), seg(8), seg(4), seg(5),
         jnp.zeros((d, LANES - 2 * GLA_RANK), F32)], axis=1).astype(BF16)
    pairs = GLA_HEADS // 2
    w2 = jnp.zeros((LANES, pairs, 2, 2 * GLA_DK), F32)
    w2 = w2.at[0:GLA_RANK, :, 0].set(gla_w2_fwd[0].reshape(GLA_RANK, pairs, 2 * GLA_DK))
    w2 = w2.at[GLA_RANK:2 * GLA_RANK, :, 1].set(gla_w2_bwd[0].reshape(GLA_RANK, pairs, 2 * GLA_DK))
    wz = w2.reshape(LANES, pairs * 4 * GLA_DK).astype(BF16)
    bz = jnp.stack([gla_b2_fwd[0].reshape(pairs, 2 * GLA_DK), gla_b2_bwd[0].reshape(pairs, 2 * GLA_DK)],
                   axis=1).reshape(1, pairs * 4 * GLA_DK)
    tab = _nat_bias_table(nat_rpb[0])
    w_og = w_out[0, :GLA_V].astype(BF16)
    w_on = w_out[0, GLA_V:].astype(BF16)
    rw_t = router_w[0].T
    w_gu = _deinterleave(w_gate_up[0])
    b_gu = jnp.concatenate([b_gate_up[0, :, None, 0::2], b_gate_up[0, :, None, 1::2]], axis=-1)
    w_dn = w_down[0].astype(BF16)
    b_dn = b_down[0, :, None, :]

    mod3 = _ada(c, w_ada[0], b_ada[0]).reshape(bp + bs, 6, d)
    bot_tok = _batch_of_tile_fn(TOKEN_TILE, bp, tp, ts)

    pg, pn, plr = _in_proj(*x, mod3, norm1_g[0], w_all, bot_tok)

    gla_args = (pg, plr, wz, bz, gla_norm_g[0].reshape(1, -1))
    og = (_gla(*gla_args, seq=tp, batches=bp, block0=0),
          _gla(*gla_args, seq=ts, batches=bs, block0=n_prompt // ts))
    on = (_nat(pn, tab, seq=tp, batches=bp, block0=0),
          _nat(pn, tab, seq=ts, batches=bs, block0=n_prompt // ts))

    x1, h2, top_i, gates, rank, cnt, tile_cnt, tile_carry = _out_proj(
        og, on, x, mod3, w_og, w_on, norm2_g[0], rw_t, router_b[0], bot_tok)

    n_rows = n * TOP_K
    n_tiles = n // TOKEN_TILE
    assert n_rows % MOE_ROWS == 0
    counts = cnt[:, 0].astype(jnp.int32)
    items, starts = _expert_work_items(counts, n_rows)
    seg_cnt = tile_cnt[:, :, 0].astype(jnp.int32)
    seg_far = starts[None, :] + tile_carry[:, :, 0].astype(jnp.int32)
    seg_near = jnp.cumsum(seg_cnt, axis=1) - seg_cnt
    seg = tuple(a.reshape(n_tiles * N_EXPERTS) for a in (seg_cnt, seg_far, seg_near))
    chosen = top_i[None] == jnp.arange(N_EXPERTS, dtype=jnp.int32)[:, None, None, None]
    pos = (rank + jnp.sum(jnp.where(chosen, seg_near.T[:, :, None, None], 0), axis=0)) * LANE_TILES
    per_tile = lambda a: a.reshape(n_tiles, 1, TOP_K * TOKEN_TILE)
    pos, gates = per_tile(pos), per_tile(gates)

    xs = _dispatch(seg, pos, h2)
    ys = _experts(items, xs, w_gu, b_gu, w_dn, b_dn)
    y_prompt, y_sample = _combine(seg, pos, gates, x1, mod3, final_norm_g, ys, bot_tok, n_prompt)

    return (y_prompt.reshape(bp, tp, d), y_sample.reshape(bs, ts, d))
```

```python
import functools

import numpy as np
import jax
import jax.numpy as jnp
from jax import lax
from jax.experimental import pallas as pl
from jax.experimental.pallas import tpu as pltpu

F32 = jnp.float32
BF16 = jnp.bfloat16
HIGHEST = lax.Precision.HIGHEST

EPS = 1e-5
GRID_W = 64
GLA_HEADS = 4
GLA_DK = 64
GLA_DV = 128
GLA_RANK = 16
GLA_TAU = 16.0
GLA_CHUNK = 64
GLA_QK = GLA_HEADS * GLA_DK
GLA_V = GLA_HEADS * GLA_DV
NAT_HEADS = 16
NAT_HD = 32
NAT_W = NAT_HEADS * NAT_HD
NAT_KH = 8
NAT_KW = 16
NAT_GROUP = 4
N_EXPERTS = 32
TOP_K = 4
SWIGLU_LIMIT = 7.0
SWIGLU_ALPHA = 1.702

LANES = 128
SUBLANES = 8
TOKEN_TILE = 512
IN_TILE = 1024
MOE_ROWS = 1024
EXPERT_PARTS = 8
ADA_COLS = 1024
GLA_UNROLL = 16
NAT_UNROLL = 16
ROW_LOOP_UNROLL = 8
SEGMENT_CHUNK = 32
LOG2E = 1.4426950408889634
VMEM_LIMIT = 56 * 1024 * 1024


def _cparams(sem, vmem=VMEM_LIMIT):
    return pltpu.CompilerParams(dimension_semantics=sem, vmem_limit_bytes=vmem)


def _dot(a, b):
    return jnp.dot(a, b, preferred_element_type=F32)


def _dot_nt(a, b, precision=None):
    return lax.dot_general(a, b, (((1,), (1,)), ((), ())), preferred_element_type=F32, precision=precision)


LANE_TILES = 8


def _row_table_shape(rows):
    return (rows * LANE_TILES, LANES)


def _store_rows(ref, x, row0=0):
    rows = x.shape[0]
    for s in range(LANE_TILES):
        ref[pl.ds(row0 * LANE_TILES + s, rows, stride=LANE_TILES), :] = x[:, s * LANES:(s + 1) * LANES]


def _load_rows(ref, row0, rows):
    return jnp.concatenate(
        [ref[pl.ds(row0 * LANE_TILES + s, rows, stride=LANE_TILES), :] for s in range(LANE_TILES)], axis=1)


def _row_slice(row0, rows=1):
    return pl.ds(pl.multiple_of(row0 * LANE_TILES, LANE_TILES), rows * LANE_TILES)


def _sublane_slice(sublane0):
    return pl.ds(pl.multiple_of(sublane0, LANE_TILES), LANE_TILES)


def _rows_of(ref, row0, rows):
    return ref.at[_row_slice(row0, rows), :]


def _dot_tn(a, b):
    return lax.dot_general(a, b, (((0,), (0,)), ((), ())), preferred_element_type=F32)


def _ada_kernel(c_ref, w_ref, b_ref, o_ref):
    c = c_ref[...]
    s = c * jax.nn.sigmoid(c)
    o_ref[...] = jnp.dot(s, w_ref[...], preferred_element_type=F32, precision=HIGHEST) + b_ref[...]


def _ada(c, w, b):
    nb, d = c.shape
    cols = w.shape[1]
    blk = ADA_COLS
    return pl.pallas_call(
        _ada_kernel,
        grid=(cols // blk,),
        in_specs=[pl.BlockSpec((nb, d), lambda j: (0, 0)),
                  pl.BlockSpec((d, blk), lambda j: (0, j)),
                  pl.BlockSpec((1, blk), lambda j: (0, j))],
        out_specs=pl.BlockSpec((nb, blk), lambda j: (0, j)),
        out_shape=jax.ShapeDtypeStruct((nb, cols), F32),
        compiler_params=_cparams(("arbitrary",)),
    )(c, w, b.reshape(1, cols))


def _stream_specs(block, n_prompt_tiles):
    return (pl.BlockSpec(block, lambda i, *_: (jnp.minimum(i, n_prompt_tiles - 1), 0)),
            pl.BlockSpec(block, lambda i, *_: (jnp.maximum(i - n_prompt_tiles, 0), 0)))


def _per_stream(n_prompt_tiles, body):
    i = pl.program_id(0)
    pl.when(i < n_prompt_tiles)(functools.partial(body, 0))
    pl.when(i >= n_prompt_tiles)(functools.partial(body, 1))


def _in_kernel(xp_ref, xs_ref, mod_ref, g_ref, w_ref, pg_ref, pn_ref, plr_ref, *, n_prompt_tiles):
    def body(stream):
        x = (xp_ref, xs_ref)[stream][...]
        ms = jnp.mean(x * x, axis=-1, keepdims=True)
        y = x * lax.rsqrt(ms + EPS) * g_ref[...]
        h = y * (1.0 + mod_ref[0, 1:2, :]) + mod_ref[0, 0:1, :]
        hb = h.astype(BF16)
        wg = GLA_QK * 2 + GLA_V * 2
        wn = 3 * NAT_W
        pg_ref[...] = _dot(hb, w_ref[:, 0:wg]).astype(BF16)
        pn_ref[...] = _dot(hb, w_ref[:, wg:wg + wn]).astype(BF16)
        plr_ref[...] = _dot(hb, w_ref[:, wg + wn:wg + wn + LANES])

    _per_stream(n_prompt_tiles, body)


def _in_proj(xp, xs, mod3, norm_g, w_all, batch_of_tile):
    d = xp.shape[1]
    n = xp.shape[0] + xs.shape[0]
    wg = GLA_QK * 2 + GLA_V * 2
    wn = 3 * NAT_W
    tm = IN_TILE
    n_prompt_tiles = xp.shape[0] // tm
    return pl.pallas_call(
        functools.partial(_in_kernel, n_prompt_tiles=n_prompt_tiles),
        grid=(n // tm,),
        in_specs=[*_stream_specs((tm, d), n_prompt_tiles),
                  pl.BlockSpec((1, 6, d), lambda i: (batch_of_tile(i), 0, 0)),
                  pl.BlockSpec((1, d), lambda i: (0, 0)),
                  pl.BlockSpec(w_all.shape, lambda i: (0, 0))],
        out_specs=[pl.BlockSpec((tm, wg), lambda i: (i, 0)),
                   pl.BlockSpec((tm, wn), lambda i: (i, 0)),
                   pl.BlockSpec((tm, LANES), lambda i: (i, 0))],
        out_shape=[jax.ShapeDtypeStruct((n, wg), BF16),
                   jax.ShapeDtypeStruct((n, wn), BF16),
                   jax.ShapeDtypeStruct((n, LANES), F32)],
        compiler_params=_cparams(("arbitrary",)),
    )(xp, xs, mod3, norm_g.reshape(1, d), w_all)


def _gla_kernel(q_ref, k_ref, v_ref, g_ref, lr_ref, wz_ref, bz_ref, ng_ref,
                o_ref, la_ref, acc_ref, qt_ref, ke_ref, dec_ref, sf_ref, sb_ref, *, seq):
    c_len = GLA_CHUNK
    n = seq // c_len
    row = lax.broadcasted_iota(jnp.int32, (c_len, c_len), 0)
    col = lax.broadcasted_iota(jnp.int32, (c_len, c_len), 1)
    tril = col <= row
    triu = col >= row
    cum_row = lax.broadcasted_iota(jnp.int32, (c_len, 2 * c_len), 0)
    cum_col = lax.broadcasted_iota(jnp.int32, (c_len, 2 * c_len), 1) % c_len
    cum = (cum_col <= cum_row).astype(F32).astype(BF16)
    mask_f = jnp.concatenate([tril, tril], axis=0)
    mask_b = jnp.concatenate([triu, triu], axis=0)
    head_a = lax.broadcasted_iota(jnp.int32, (c_len, LANES), 1) < GLA_DK
    st_row = lax.broadcasted_iota(jnp.int32, (2 * GLA_DV, LANES), 0)
    st_col = lax.broadcasted_iota(jnp.int32, (2 * GLA_DV, LANES), 1)
    blockdiag = (st_row < GLA_DV) == (st_col < GLA_DK)

    z = _dot(lr_ref[...].astype(BF16), wz_ref[...]) + bz_ref[...]
    la_ref[...] = (jnp.minimum(z, 0.0) - jnp.log(1.0 + jnp.exp(-jnp.abs(z)))) * (1.0 / GLA_TAU)

    def chunk_rows(c):
        return pl.ds(pl.multiple_of(c * c_len, c_len), c_len)

    def stack_heads(x):
        zero = jnp.zeros_like(x)
        return jnp.concatenate([jnp.where(head_a, x, zero), jnp.where(head_a, zero, x)], axis=0).astype(BF16)

    def local(c, carry):
        rows = chunk_rows(c)
        la = la_ref[rows, :]
        hi = la.astype(BF16)
        lo = (la - hi.astype(F32)).astype(BF16)
        binc = _dot(cum, jnp.concatenate([hi, lo], axis=0))
        btot = jnp.sum(la, axis=0, keepdims=True)
        b_f = binc[:, :LANES]
        b_b = btot[:, LANES:] - binc[:, LANES:] + la[:, LANES:]
        e_tot = jnp.exp(btot)
        q = q_ref[rows, :].astype(F32) * (GLA_DK ** -0.5)
        k = k_ref[rows, :].astype(F32)
        qt_f = q * jnp.exp(b_f)
        qt_b = q * jnp.exp(b_b)
        kt_f = k * jnp.exp(-b_f)
        kt_b = k * jnp.exp(-b_b)
        a = (jnp.where(mask_f, _dot_nt(stack_heads(qt_f), kt_f.astype(BF16)), 0.0)
             + jnp.where(mask_b, _dot_nt(stack_heads(qt_b), kt_b.astype(BF16)), 0.0))
        oi = _dot(a.astype(BF16), v_ref[rows, :])
        acc_ref[rows, :] = jnp.concatenate([oi[0:c_len, 0:GLA_DV], oi[c_len:, GLA_DV:]], axis=1)
        qt_ref[rows, :] = jnp.concatenate([qt_f, qt_b], axis=1).astype(BF16)
        ke_ref[rows, :] = jnp.concatenate([kt_f * e_tot[:, :LANES], kt_b * e_tot[:, LANES:]], axis=1).astype(BF16)
        dec_ref[pl.ds(pl.multiple_of(c * SUBLANES, SUBLANES), SUBLANES), :] = jnp.broadcast_to(e_tot, (SUBLANES, 2 * LANES))
        return carry

    def carried(c, s_ref, half):
        rows = chunk_rows(c)
        lanes = slice(half * LANES, (half + 1) * LANES)
        st = s_ref[...]
        acc_ref[rows, :] += _dot_nt(qt_ref[rows, lanes], st.astype(BF16))
        upd = _dot_tn(v_ref[rows, :], ke_ref[rows, lanes])
        dec = dec_ref[pl.ds(pl.multiple_of(c * SUBLANES, SUBLANES), 1), lanes]
        s_ref[...] = st * dec + jnp.where(blockdiag, upd, 0.0)

    def scan(c, carry):
        carried(c, sf_ref, 0)
        carried(n - 1 - c, sb_ref, 1)
        return carry

    def finalize(c, carry):
        rows = chunk_rows(c)
        o = acc_ref[rows, :]
        halves = []
        for hh in range(2):
            oh = o[:, hh * GLA_DV:(hh + 1) * GLA_DV]
            ms = jnp.mean(oh * oh, axis=-1, keepdims=True)
            halves.append(oh * lax.rsqrt(ms + EPS) * ng_ref[...])
        y = jnp.concatenate(halves, axis=1)
        g = g_ref[rows, :].astype(F32)
        o_ref[rows, :] = (y * (g * jax.nn.sigmoid(g))).astype(o_ref.dtype)
        return carry

    sf_ref[...] = jnp.zeros_like(sf_ref)
    sb_ref[...] = jnp.zeros_like(sb_ref)
    lax.fori_loop(0, n, local, 0, unroll=GLA_UNROLL)
    lax.fori_loop(0, n, scan, 0, unroll=GLA_UNROLL)
    lax.fori_loop(0, n, finalize, 0, unroll=GLA_UNROLL)


def _gla(pg, plr, wz, bz, norm_g, *, seq, batches, block0):
    pair_w = 2 * GLA_DK
    pair_v = 2 * GLA_DV
    k0 = GLA_QK // pair_w
    v0 = 2 * GLA_QK // pair_v
    g0 = (2 * GLA_QK + GLA_V) // pair_v
    n_chunks = seq // GLA_CHUNK
    in_specs = [pl.BlockSpec((seq, pair_w), lambda b, p: (block0 + b, p)),
                pl.BlockSpec((seq, pair_w), lambda b, p: (block0 + b, k0 + p)),
                pl.BlockSpec((seq, pair_v), lambda b, p: (block0 + b, v0 + p)),
                pl.BlockSpec((seq, pair_v), lambda b, p: (block0 + b, g0 + p)),
                pl.BlockSpec((seq, LANES), lambda b, p: (block0 + b, 0)),
                pl.BlockSpec((LANES, 2 * pair_w), lambda b, p: (0, p)),
                pl.BlockSpec((1, 2 * pair_w), lambda b, p: (0, p)),
                pl.BlockSpec((1, GLA_DV), lambda b, p: (0, 0))]
    return pl.pallas_call(
        functools.partial(_gla_kernel, seq=seq),
        grid=(batches, GLA_HEADS // 2),
        in_specs=in_specs,
        out_specs=pl.BlockSpec((seq, pair_v), lambda b, p: (b, p)),
        out_shape=jax.ShapeDtypeStruct((batches * seq, GLA_V), BF16),
        scratch_shapes=[pltpu.VMEM((seq, 2 * pair_w), F32),
                        pltpu.VMEM((seq, pair_v), F32),
                        pltpu.VMEM((seq, 2 * pair_w), BF16),
                        pltpu.VMEM((seq, 2 * pair_w), BF16),
                        pltpu.VMEM((n_chunks * SUBLANES, 2 * pair_w), F32),
                        pltpu.VMEM((pair_v, pair_w), F32),
                        pltpu.VMEM((pair_v, pair_w), F32)],
        compiler_params=_cparams(("parallel", "parallel")),
    )(pg, pg, pg, pg, plr, wz, bz, norm_g)


def _nat_kernel(q_ref, k_ref, v_ref, tab_ref, o_ref, s_ref, *, rows):
    w = GRID_W
    head = lax.broadcasted_iota(jnp.int32, (w, LANES), 1) // NAT_HD
    win = NAT_KH * w

    def window(i):
        r0 = jnp.clip(i - NAT_KH // 2, 0, rows - NAT_KH)
        return r0 - i + NAT_KH - 1, pl.ds(pl.multiple_of(r0 * w, w), win)

    def scores(i, slot):
        base, krows = window(i)
        q = q_ref[pl.ds(pl.multiple_of(i * w, w), w), :]
        zero = jnp.zeros_like(q)
        q_stack = jnp.concatenate([jnp.where(head == h, q, zero) for h in range(NAT_GROUP)], axis=0)
        s_ref[slot] = _dot_nt(q_stack, k_ref[krows, :]) + tab_ref[0, base]

    def attend(i, slot):
        _, krows = window(i)
        s = s_ref[slot]
        m = jnp.max(s, axis=-1, keepdims=True)
        e = jnp.exp2(s - m)
        l = jnp.sum(e, axis=-1, keepdims=True)
        o = _dot(e.astype(BF16), v_ref[krows, :]) / l
        out = jnp.zeros((w, LANES), F32)
        for h in range(NAT_GROUP):
            out = out + jnp.where(head == h, o[h * w:(h + 1) * w, :], 0.0)
        o_ref[pl.ds(pl.multiple_of(i * w, w), w), :] = out.astype(o_ref.dtype)

    scores(0, 0)

    def body(j, carry):
        for u in range(2):
            i = 2 * j + u
            scores(jnp.minimum(i + 1, rows - 1), 1 - u)
            attend(i, u)
        return carry

    lax.fori_loop(0, rows // 2, body, 0, unroll=NAT_UNROLL // 2)


def _nat(pn, tab, *, seq, batches, block0):
    rows = seq // GRID_W
    groups = NAT_HEADS // NAT_GROUP
    in_specs = [pl.BlockSpec((seq, LANES), lambda g, b: (block0 + b, g)),
                pl.BlockSpec((seq, LANES), lambda g, b: (block0 + b, groups + g)),
                pl.BlockSpec((seq, LANES), lambda g, b: (block0 + b, 2 * groups + g)),
                pl.BlockSpec((1,) + tab.shape[1:], lambda g, b: (g, 0, 0, 0))]
    return pl.pallas_call(
        functools.partial(_nat_kernel, rows=rows),
        grid=(groups, batches),
        in_specs=in_specs,
        out_specs=pl.BlockSpec((seq, LANES), lambda g, b: (b, g)),
        out_shape=jax.ShapeDtypeStruct((batches * seq, NAT_W), BF16),
        scratch_shapes=[pltpu.VMEM((2, NAT_GROUP * GRID_W, NAT_KH * GRID_W), F32)],
        compiler_params=_cparams(("parallel", "parallel")),
    )(pn, pn, pn, tab)


def _nat_bias_table(rpb):
    w = GRID_W
    jq = np.arange(w)[:, None]
    jk = np.arange(w)[None, :]
    c0 = np.clip(jq - NAT_KW // 2, 0, w - NAT_KW)
    valid = (jk >= c0) & (jk < c0 + NAT_KW)
    dcol = np.clip(jk - jq + NAT_KW - 1, 0, 2 * NAT_KW - 2)
    select = (dcol[None] == np.arange(2 * NAT_KW - 1)[:, None, None]).astype(np.float32)
    t = jnp.einsum('hrc,cqk->hrqk', rpb.astype(F32) * LOG2E, select, precision=HIGHEST)
    t = jnp.where(valid[None, None], t, -jnp.inf)
    t = jnp.stack([t[:, b:b + NAT_KH] for b in range(NAT_KH)], axis=1)
    t = t.transpose(0, 1, 3, 2, 4).reshape(NAT_HEADS, NAT_KH, w, NAT_KH * w)
    t = t.reshape(NAT_HEADS // NAT_GROUP, NAT_GROUP, NAT_KH, w, NAT_KH * w).transpose(0, 2, 1, 3, 4)
    return t.reshape(NAT_HEADS // NAT_GROUP, NAT_KH, NAT_GROUP * w, NAT_KH * w)


def _out_kernel(ogp_ref, ogs_ref, onp_ref, ons_ref, xp_ref, xs_ref, mod_ref, wg_ref, wn_ref, n2_ref, rw_ref, rb_ref,
                x1_ref, h2_ref, ti_ref, gt_ref, rk_ref, cnt_ref, tcnt_ref, tcar_ref, carry_ref, *, n_prompt_tiles):
    tm = x1_ref.shape[0]

    @pl.when(pl.program_id(0) == 0)
    def _():
        carry_ref[...] = jnp.zeros_like(carry_ref)

    def residual(stream):
        og_ref, on_ref, x_ref = ((ogp_ref, onp_ref, xp_ref), (ogs_ref, ons_ref, xs_ref))[stream]
        mix = _dot(og_ref[...], wg_ref[...]) + _dot(on_ref[...], wn_ref[...])
        x1_ref[...] = x_ref[...] + mod_ref[0, 2:3, :] * mix

    _per_stream(n_prompt_tiles, residual)
    x1 = x1_ref[...]
    ms = jnp.mean(x1 * x1, axis=-1, keepdims=True)
    h2 = x1 * lax.rsqrt(ms + EPS) * n2_ref[...]
    h2 = h2 * (1.0 + mod_ref[0, 4:5, :]) + mod_ref[0, 3:4, :]
    _store_rows(h2_ref, h2)

    logits = _dot_nt(rw_ref[...], h2, precision=HIGHEST) + rb_ref[...]
    eidx = lax.broadcasted_iota(jnp.int32, logits.shape, 0)
    vals = logits
    top_v, top_i = [], []
    for _ in range(TOP_K):
        m = jnp.max(vals, axis=0, keepdims=True)
        idx = jnp.min(jnp.where(vals == m, eidx, N_EXPERTS), axis=0, keepdims=True)
        top_v.append(m)
        top_i.append(idx)
        vals = jnp.where(eidx == idx, -jnp.inf, vals)
    ev = [jnp.exp(v - top_v[0]) for v in top_v]
    den = ev[0] + ev[1] + ev[2] + ev[3]
    ti_ref[0] = jnp.concatenate(top_i, axis=0)
    gt_ref[0] = jnp.concatenate([e / den for e in ev], axis=0)

    onehot = jnp.zeros(logits.shape, F32)
    for idx in top_i:
        onehot = onehot + (eidx == idx).astype(F32)
    r = lax.broadcasted_iota(jnp.int32, (tm, tm), 0)
    c = lax.broadcasted_iota(jnp.int32, (tm, tm), 1)
    earlier = (r < c).astype(BF16)
    before = _dot(onehot.astype(BF16), earlier)
    rk_ref[0] = jnp.concatenate(
        [jnp.sum(jnp.where(eidx == idx, before, 0.0), axis=0, keepdims=True) for idx in top_i], axis=0).astype(jnp.int32)
    tile_counts = jnp.sum(onehot, axis=1, keepdims=True)
    tcar_ref[0] = jnp.broadcast_to(carry_ref[...], tcar_ref.shape[1:])
    tcnt_ref[0] = jnp.broadcast_to(tile_counts, tcnt_ref.shape[1:])
    carry_ref[...] = carry_ref[...] + tile_counts
    cnt_ref[...] = jnp.broadcast_to(carry_ref[...], cnt_ref.shape)


def _out_proj(og, on, x, mod3, w_g, w_n, norm_g, rw_t, rb, batch_of_tile):
    d = x[0].shape[1]
    n = x[0].shape[0] + x[1].shape[0]
    tm = TOKEN_TILE
    n_prompt_tiles = x[0].shape[0] // tm
    tok = lambda i: (i, 0)
    const = lambda i: (0, 0)
    return pl.pallas_call(
        functools.partial(_out_kernel, n_prompt_tiles=n_prompt_tiles),
        grid=(n // tm,),
        in_specs=[*_stream_specs((tm, GLA_V), n_prompt_tiles),
                  *_stream_specs((tm, NAT_W), n_prompt_tiles),
                  *_stream_specs((tm, d), n_prompt_tiles),
                  pl.BlockSpec((1, 6, d), lambda i: (batch_of_tile(i), 0, 0)),
                  pl.BlockSpec(w_g.shape, const),
                  pl.BlockSpec(w_n.shape, const),
                  pl.BlockSpec((1, d), const),
                  pl.BlockSpec(rw_t.shape, const),
                  pl.BlockSpec((N_EXPERTS, 1), const)],
        out_specs=[pl.BlockSpec((tm, d), tok),
                   pl.BlockSpec(_row_table_shape(tm), tok),
                   pl.BlockSpec((1, TOP_K, tm), lambda i: (i, 0, 0)),
                   pl.BlockSpec((1, TOP_K, tm), lambda i: (i, 0, 0)),
                   pl.BlockSpec((1, TOP_K, tm), lambda i: (i, 0, 0)),
                   pl.BlockSpec((N_EXPERTS, LANES), const),
                   pl.BlockSpec((1, N_EXPERTS, LANES), lambda i: (i, 0, 0)),
                   pl.BlockSpec((1, N_EXPERTS, LANES), lambda i: (i, 0, 0))],
        out_shape=[jax.ShapeDtypeStruct((n, d), F32),
                   jax.ShapeDtypeStruct(_row_table_shape(n), F32),
                   jax.ShapeDtypeStruct((n // tm, TOP_K, tm), jnp.int32),
                   jax.ShapeDtypeStruct((n // tm, TOP_K, tm), F32),
                   jax.ShapeDtypeStruct((n // tm, TOP_K, tm), jnp.int32),
                   jax.ShapeDtypeStruct((N_EXPERTS, LANES), F32),
                   jax.ShapeDtypeStruct((n // tm, N_EXPERTS, LANES), F32),
                   jax.ShapeDtypeStruct((n // tm, N_EXPERTS, LANES), F32)],
        scratch_shapes=[pltpu.VMEM((N_EXPERTS, 1), F32)],
        compiler_params=_cparams(("arbitrary",)),
    )(*og, *on, *x, mod3, w_g, w_n, norm_g.reshape(1, d), rw_t, rb.reshape(N_EXPERTS, 1))


def _row_copy(src, dst, sem):
    return pltpu.make_async_copy(src, dst, sem)


def _segment_copies(cnt_ref, far_ref, near_ref, tile, copy):
    def per_expert(e, carry):
        idx = tile * N_EXPERTS + e
        c, far, near = cnt_ref[idx], far_ref[idx], near_ref[idx]

        def whole(i, carry2):
            copy(near + i * SEGMENT_CHUNK, far + i * SEGMENT_CHUNK, SEGMENT_CHUNK).start()
            return carry2

        lax.fori_loop(0, c // SEGMENT_CHUNK, whole, 0)
        for bit in reversed(range(SEGMENT_CHUNK.bit_length() - 1)):
            size = 1 << bit
            done = c & -(2 * size)

            @pl.when((c & size) != 0)
            def _(done=done, size=size):
                copy(near + done, far + done, size).start()

        return carry

    lax.fori_loop(0, N_EXPERTS, per_expert, 0)


def _dispatch_kernel(cnt_ref, far_ref, near_ref, pos_ref, h_ref, xs_hbm, buf_ref, sem):
    pairs = pos_ref.shape[2]
    tt = pairs // TOP_K
    i = pl.program_id(0)
    buf = i % 2
    base = buf * pairs

    def place(t, carry):
        row = h_ref[_row_slice(t), :]
        for k in range(TOP_K):
            buf_ref[_sublane_slice(base * LANE_TILES + pos_ref[0, 0, k * tt + t]), :] = row
        return carry

    lax.fori_loop(0, tt, place, 0, unroll=ROW_LOOP_UNROLL)

    def copy(near, far, rows):
        return _row_copy(_rows_of(buf_ref, base + near, rows), _rows_of(xs_hbm, far, rows), sem.at[buf])

    _segment_copies(cnt_ref, far_ref, near_ref, i, copy)

    def drain(b):
        _row_copy(_rows_of(buf_ref, b * pairs, pairs), _rows_of(xs_hbm, 0, pairs), sem.at[b]).wait()

    pl.when(i > 0)(lambda: drain(1 - buf))
    pl.when(i == pl.num_programs(0) - 1)(lambda: drain(buf))


def _dispatch(seg, pos, h2):
    n = h2.shape[0] // LANE_TILES
    tt = TOKEN_TILE
    grid_spec = pltpu.PrefetchScalarGridSpec(
        num_scalar_prefetch=3,
        grid=(n // tt,),
        in_specs=[pl.BlockSpec((1, 1, TOP_K * tt), lambda i, *_: (i, 0, 0), memory_space=pltpu.SMEM),
                  pl.BlockSpec(_row_table_shape(tt), lambda i, *_: (i, 0))],
        out_specs=pl.BlockSpec(memory_space=pl.ANY),
        scratch_shapes=[pltpu.VMEM(_row_table_shape(2 * TOP_K * tt), F32),
                        pltpu.SemaphoreType.DMA((2,))],
    )
    return pl.pallas_call(
        _dispatch_kernel,
        grid_spec=grid_spec,
        out_shape=jax.ShapeDtypeStruct(_row_table_shape(n * TOP_K), F32),
        compiler_params=_cparams(("arbitrary",)),
    )(*seg, pos, h2)


def _deinterleave_kernel(w_ref, o_ref):
    f2 = w_ref.shape[2]
    slab = 2 * LANES
    r = lax.broadcasted_iota(jnp.int32, (slab, slab), 0)
    c = lax.broadcasted_iota(jnp.int32, (slab, slab), 1)
    perm = (r == jnp.where(c < LANES, 2 * c, 2 * (c - LANES) + 1)).astype(BF16)
    for j in range(f2 // slab):
        y = _dot(w_ref[0, :, j * slab:(j + 1) * slab].astype(BF16), perm)
        o_ref[0, :, j * LANES:(j + 1) * LANES] = y[:, :LANES].astype(BF16)
        o_ref[0, :, f2 // 2 + j * LANES:f2 // 2 + (j + 1) * LANES] = y[:, LANES:].astype(BF16)


def _deinterleave(w):
    e, d, f2 = w.shape
    return pl.pallas_call(
        _deinterleave_kernel,
        grid=(e,),
        in_specs=[pl.BlockSpec((1, d, f2), lambda i: (i, 0, 0))],
        out_specs=pl.BlockSpec((1, d, f2), lambda i: (i, 0, 0)),
        out_shape=jax.ShapeDtypeStruct((e, d, f2), BF16),
        compiler_params=_cparams(("parallel",)),
    )(w)


def _expert_kernel(blk_ref, exp_ref, lo_ref, hi_ref, xs_ref, wgu_ref, bgu_ref, wd_ref, bd_ref, ys_ref):
    del blk_ref, exp_ref
    rows = xs_ref.shape[0] // LANE_TILES
    f = wd_ref.shape[1]
    i = pl.program_id(0)
    lo = lo_ref[i]
    hi = hi_ref[i]

    def mlp(row0, nrows):
        x = _load_rows(xs_ref, row0, nrows).astype(BF16)
        gu = _dot(x, wgu_ref[0]) + bgu_ref[0]
        gate = jnp.minimum(gu[:, :f], SWIGLU_LIMIT)
        up = jnp.clip(gu[:, f:], -SWIGLU_LIMIT, SWIGLU_LIMIT)
        act = (up + 1.0) * gate * jax.nn.sigmoid(SWIGLU_ALPHA * gate)
        return _dot(act.astype(BF16), wd_ref[0]) + bd_ref[0]

    whole = jnp.logical_and(lo == 0, hi == rows)

    @pl.when(whole)
    def _():
        _store_rows(ys_ref, mlp(0, rows))

    @pl.when(jnp.logical_and(hi > lo, jnp.logical_not(whole)))
    def _():
        @pl.when(lo == 0)
        def _():
            ys_ref[...] = jnp.zeros_like(ys_ref)

        part = rows // EXPERT_PARTS
        for p in range(EXPERT_PARTS):
            @pl.when(jnp.logical_and(lo < (p + 1) * part, hi > p * part))
            def _(p=p):
                r = lax.broadcasted_iota(jnp.int32, (part, LANE_TILES * LANES), 0) + p * part
                mine = (r >= lo) & (r < hi)
                _store_rows(ys_ref, jnp.where(mine, mlp(p * part, part), _load_rows(ys_ref, p * part, part)), p * part)


def _experts(items, xs2, w_gu, b_gu, w_down, b_down):
    item_blk, item_exp, item_lo, item_hi = items
    d, f2 = w_gu.shape[1], w_gu.shape[2]
    wmap = lambda i, blk, exp, lo, hi: (exp[i], 0, 0)
    xmap = lambda i, blk, exp, lo, hi: (blk[i], 0)
    grid_spec = pltpu.PrefetchScalarGridSpec(
        num_scalar_prefetch=4,
        grid=(item_blk.shape[0],),
        in_specs=[pl.BlockSpec(_row_table_shape(MOE_ROWS), xmap),
                  pl.BlockSpec((1, d, f2), wmap),
                  pl.BlockSpec((1, 1, f2), wmap),
                  pl.BlockSpec((1, f2 // 2, d), wmap),
                  pl.BlockSpec((1, 1, d), wmap)],
        out_specs=pl.BlockSpec(_row_table_shape(MOE_ROWS), xmap),
    )
    return pl.pallas_call(
        _expert_kernel,
        grid_spec=grid_spec,
        out_shape=jax.ShapeDtypeStruct(xs2.shape, F32),
        compiler_params=_cparams(("arbitrary",)),
    )(item_blk, item_exp, item_lo, item_hi, xs2, w_gu, b_gu, w_down, b_down)


def _expert_work_items(counts, n_rows):
    ends = jnp.cumsum(counts)
    starts = ends - counts
    n_blk = n_rows // MOE_ROWS
    cuts = jnp.sort(jnp.concatenate([jnp.arange(n_blk, dtype=jnp.int32) * MOE_ROWS, starts[1:]]))
    nxt = jnp.concatenate([cuts[1:], jnp.full((1,), n_rows, jnp.int32)])
    blk = jnp.minimum(cuts // MOE_ROWS, n_blk - 1)
    exp = jnp.minimum(jnp.sum(ends[None, :] <= cuts[:, None], axis=1), N_EXPERTS - 1).astype(jnp.int32)
    lo = cuts - blk * MOE_ROWS
    hi = nxt - blk * MOE_ROWS
    return (blk.astype(jnp.int32), exp, lo.astype(jnp.int32), hi.astype(jnp.int32)), starts


def _combine_kernel(cnt_ref, far_ref, near_ref, pos_ref, gates_ref, x1_ref, mod_ref, fg_ref, ys_hbm,
                    op_ref, os_ref, buf_ref, y_ref, sem, *, n_prompt_tiles):
    tt = x1_ref.shape[0]
    pairs = TOP_K * tt
    i = pl.program_id(0)
    buf = i % 2
    base = buf * pairs

    def fetch(tile, b):
        def copy(near, far, rows):
            return _row_copy(_rows_of(ys_hbm, far, rows), _rows_of(buf_ref, b * pairs + near, rows), sem.at[b])

        _segment_copies(cnt_ref, far_ref, near_ref, tile, copy)

    pl.when(i == 0)(lambda: fetch(0, 0))
    pl.when(i + 1 < pl.num_programs(0))(lambda: fetch(i + 1, 1 - buf))
    _row_copy(_rows_of(ys_hbm, 0, pairs), _rows_of(buf_ref, base, pairs), sem.at[buf]).wait()

    def weighted_sum(t, carry):
        acc = None
        for k in range(TOP_K):
            row = buf_ref[_sublane_slice(base * LANE_TILES + pos_ref[0, 0, k * tt + t]), :]
            term = gates_ref[0, 0, k * tt + t] * row
            acc = term if acc is None else acc + term
        y_ref[_row_slice(t), :] = acc
        return carry

    lax.fori_loop(0, tt, weighted_sum, 0, unroll=ROW_LOOP_UNROLL)

    x2 = x1_ref[...] + mod_ref[0, 5:6, :] * _load_rows(y_ref, 0, tt)
    ms = jnp.mean(x2 * x2, axis=-1, keepdims=True)
    out = x2 * lax.rsqrt(ms + EPS) * fg_ref[...]

    def write(stream):
        (op_ref, os_ref)[stream][...] = out

    _per_stream(n_prompt_tiles, write)


def _combine(seg, pos, gates, x1, mod3, final_g, ys, batch_of_tile, n_prompt):
    n, d = x1.shape
    tt = TOKEN_TILE
    n_prompt_tiles = n_prompt // tt
    grid_spec = pltpu.PrefetchScalarGridSpec(
        num_scalar_prefetch=3,
        grid=(n // tt,),
        in_specs=[pl.BlockSpec((1, 1, TOP_K * tt), lambda i, *_: (i, 0, 0), memory_space=pltpu.SMEM),
                  pl.BlockSpec((1, 1, TOP_K * tt), lambda i, *_: (i, 0, 0), memory_space=pltpu.SMEM),
                  pl.BlockSpec((tt, d), lambda i, *_: (i, 0)),
                  pl.BlockSpec((1, 6, d), lambda i, *_: (batch_of_tile(i), 0, 0)),
                  pl.BlockSpec((1, d), lambda i, *_: (0, 0)),
                  pl.BlockSpec(memory_space=pl.ANY)],
        out_specs=list(_stream_specs((tt, d), n_prompt_tiles)),
        scratch_shapes=[pltpu.VMEM(_row_table_shape(2 * TOP_K * tt), F32),
                        pltpu.VMEM(_row_table_shape(tt), F32),
                        pltpu.SemaphoreType.DMA((2,))],
    )
    return pl.pallas_call(
        functools.partial(_combine_kernel, n_prompt_tiles=n_prompt_tiles),
        grid_spec=grid_spec,
        out_shape=[jax.ShapeDtypeStruct((n_prompt, d), F32),
                   jax.ShapeDtypeStruct((n - n_prompt, d), F32)],
        compiler_params=_cparams(("arbitrary",)),
    )(*seg, pos, gates, x1, mod3, final_g.reshape(1, d), ys)


def _batch_of_tile_fn(tile, bp, tp, ts):
    n_prompt = bp * tp

    def batch_of_tile(i):
        t = i * tile
        return jnp.where(t < n_prompt, t // tp, bp + (t - n_prompt) // ts)

    return batch_of_tile


def kernel(x_prompt, x_sample, c_prompt, c_sample, norm1_g, w_ada, b_ada, w_in, gla_w2_fwd, gla_b2_fwd, gla_w2_bwd, gla_b2_bwd, gla_norm_g, nat_rpb, w_out, norm2_g, router_w, router_b, w_gate_up, b_gate_up, w_down, b_down, final_norm_g):
    assert w_ada.shape[0] == 1, "single-layer encoder"
    bp, tp, d = x_prompt.shape
    bs, ts, _ = x_sample.shape
    n_prompt, n_sample = bp * tp, bs * ts
    n = n_prompt + n_sample
    assert tp % IN_TILE == 0 and ts % IN_TILE == 0 and IN_TILE % TOKEN_TILE == 0 and n_prompt % ts == 0
    assert d == LANE_TILES * LANES

    x = (x_prompt.reshape(n_prompt, d), x_sample.reshape(n_sample, d))
    c = jnp.concatenate([c_prompt, c_sample], axis=0)

    sizes = (GLA_QK, GLA_QK, GLA_V, GLA_V, GLA_RANK, GLA_RANK, NAT_W, NAT_W, NAT_W)
    offs = np.concatenate([[0], np.cumsum(sizes)])
    w_in0 = w_in[0]
    seg = lambda j: w_in0[:, offs[j]:offs[j + 1]]
    w_all = jnp.concatenate(
        [seg(0), seg(1), seg(2), seg(3), seg(6) * (NAT_HD ** -0.5 * LOG2E), seg(7), seg(8), seg(4), seg(5),
         jnp.zeros((d, LANES - 2 * GLA_RANK), F32)], axis=1).astype(BF16)
    pairs = GLA_HEADS // 2
    w2 = jnp.zeros((LANES, pairs, 2, 2 * GLA_DK), F32)
    w2 = w2.at[0:GLA_RANK, :, 0].set(gla_w2_fwd[0].reshape(GLA_RANK, pairs, 2 * GLA_DK))
    w2 = w2.at[GLA_RANK:2 * GLA_RANK, :, 1].set(gla_w2_bwd[0].reshape(GLA_RANK, pairs, 2 * GLA_DK))
    wz = w2.reshape(LANES, pairs * 4 * GLA_DK).astype(BF16)
    bz = jnp.stack([gla_b2_fwd[0].reshape(pairs, 2 * GLA_DK), gla_b2_bwd[0].reshape(pairs, 2 * GLA_DK)],
                   axis=1).reshape(1, pairs * 4 * GLA_DK)
    tab = _nat_bias_table(nat_rpb[0])
    w_og = w_out[0, :GLA_V].astype(BF16)
    w_on = w_out[0, GLA_V:].astype(BF16)
    rw_t = router_w[0].T
    w_gu = _deinterleave(w_gate_up[0])
    b_gu = jnp.concatenate([b_gate_up[0, :, None, 0::2], b_gate_up[0, :, None, 1::2]], axis=-1)
    w_dn = w_down[0].astype(BF16)
    b_dn = b_down[0, :, None, :]

    mod3 = _ada(c, w_ada[0], b_ada[0]).reshape(bp + bs, 6, d)
    bot_tok = _batch_of_tile_fn(TOKEN_TILE, bp, tp, ts)

    pg, pn, plr = _in_proj(*x, mod3, norm1_g[0], w_all, _batch_of_tile_fn(IN_TILE, bp, tp, ts))

    gla_args = (pg, plr, wz, bz, gla_norm_g[0].reshape(1, -1))
    og = (_gla(*gla_args, seq=tp, batches=bp, block0=0),
          _gla(*gla_args, seq=ts, batches=bs, block0=n_prompt // ts))
    on = (_nat(pn, tab, seq=tp, batches=bp, block0=0),
          _nat(pn, tab, seq=ts, batches=bs, block0=n_prompt // ts))

    x1, h2, top_i, gates, rank, cnt, tile_cnt, tile_carry = _out_proj(
        og, on, x, mod3, w_og, w_on, norm2_g[0], rw_t, router_b[0], bot_tok)

    n_rows = n * TOP_K
    n_tiles = n // TOKEN_TILE
    assert n_rows % MOE_ROWS == 0
    counts = cnt[:, 0].astype(jnp.int32)
    items, starts = _expert_work_items(counts, n_rows)
    seg_cnt = tile_cnt[:, :, 0].astype(jnp.int32)
    seg_far = starts[None, :] + tile_carry[:, :, 0].astype(jnp.int32)
    seg_near = jnp.cumsum(seg_cnt, axis=1) - seg_cnt
    seg = tuple(a.reshape(n_tiles * N_EXPERTS) for a in (seg_cnt, seg_far, seg_near))
    chosen = top_i[None] == jnp.arange(N_EXPERTS, dtype=jnp.int32)[:, None, None, None]
    pos = (rank + jnp.sum(jnp.where(chosen, seg_near.T[:, :, None, None], 0), axis=0)) * LANE_TILES
    per_tile = lambda a: a.reshape(n_tiles, 1, TOP_K * TOKEN_TILE)
    pos, gates = per_tile(pos), per_tile(gates)

    xs = _dispatch(seg, pos, h2)
    ys = _experts(items, xs, w_gu, b_gu, w_dn, b_dn)
    y_prompt, y_sample = _combine(seg, pos, gates, x1, mod3, final_norm_g, ys, bot_tok, n_prompt)

    return (y_prompt.reshape(bp, tp, d), y_sample.reshape(bs, ts, d))
```
